```python
import jax, jax.numpy as jnp
from jax import lax
import numpy as np

D_MODEL = 1024
BATCH = 8
SEQ = 2048
DEPTH = 1
DEC_BATCH = 128
DEC_SEQ = 1
PAST_LEN = 16384
PAGE_SIZE = 128

D_MIX = D_MODEL
D_A = D_MIX // 2
HEAD_DIM_A = 64
N_HEADS_A = D_A // HEAD_DIM_A
D_B = D_MIX - D_A
N_BLOCKS_B = 8
BLOCK_B = D_B // N_BLOCKS_B
CONV_W = 4
LRU_C = 8.0
R_DECAY = 64
R_AAA = 64
R_GATE = 128
N_COLS_A = 3 * D_A + R_DECAY + R_AAA + R_GATE
N_COLS_B = 2 * D_B
N_COLS = N_COLS_A + N_COLS_B
N_EXPERTS = 32
TOP_K = 4
D_FF = D_MODEL
SWIGLU_LIMIT = 7.0
SWIGLU_ALPHA = 1.702
MOE_BLOCK = 128
RMS_EPS = 1e-6
LN_X_EPS = 64e-5
N_MOD = 6

kernel_name = "hymba_rwkv7_rglru_moe_adaln_step"

F32 = jnp.float32


def rmsnorm(x, g):
    xf = x.astype(F32)
    y = xf * lax.rsqrt(jnp.mean(xf * xf, axis=-1, keepdims=True) + RMS_EPS)
    return y * g.astype(F32)


def wkv_scan(S0, r, w, k, v, kk, a):
    xs = tuple(jnp.moveaxis(t, 1, 0) for t in (r, w, k, v, kk, a))

    def step(S, inp):
        r_t, w_t, k_t, v_t, kk_t, a_t = inp
        sa = jnp.einsum('bhij,bhj->bhi', S, -kk_t)
        S = (S * w_t[:, :, None, :] + sa[..., None] * (kk_t * a_t)[:, :, None, :]
             + v_t[..., None] * k_t[:, :, None, :])
        y = jnp.einsum('bhij,bhj->bhi', S, r_t)
        return S, y

    S, ys = lax.scan(step, S0, xs)
    return S, jnp.moveaxis(ys, 0, 1)


def rwkv7_group(pa, shift0, wkv0, p):
    B, T, _ = pa.shape
    paf = pa.astype(F32)
    prev = jnp.concatenate([shift0.astype(F32)[:, None, :], paf[:, :-1]], axis=1)
    z = paf + (prev - paf) * p['rk_mu'].astype(F32)
    new_shift = paf[:, -1]
    idx = [D_A, 2 * D_A, 3 * D_A, 3 * D_A + R_DECAY, 3 * D_A + R_DECAY + R_AAA]
    r, k, v, wd, ad, gd = jnp.split(z, idx, axis=-1)
    w_log = -jax.nn.softplus(-(p['rk_w0'] + jnp.tanh(wd) @ p['rk_w_up'])) - 0.5
    decay = jnp.exp(-jnp.exp(w_log))
    a = jax.nn.sigmoid(p['rk_a0'] + ad @ p['rk_a_up'])
    g = jax.nn.sigmoid(gd) @ p['rk_g_up']
    kk = k * p['rk_k_k']
    hd = lambda t: t.reshape(B, T, N_HEADS_A, HEAD_DIM_A)
    kk_h = hd(kk)
    kk_h = kk_h / jnp.maximum(jnp.sqrt(jnp.sum(kk_h * kk_h, axis=-1, keepdims=True)), 1e-12)
    k = k * (1.0 + (a - 1.0) * p['rk_k_a'])
    r_h, k_h, v_h, w_h, a_h = hd(r), hd(k), hd(v), hd(decay), hd(a)
    S, y = wkv_scan(wkv0.astype(F32), r_h, w_h, k_h, v_h, kk_h, a_h)
    mu = jnp.mean(y, axis=-1, keepdims=True)
    var = jnp.mean((y - mu) ** 2, axis=-1, keepdims=True)
    yn = ((y - mu) * lax.rsqrt(var + LN_X_EPS)).reshape(B, T, D_A) * p['rk_lnx_w'] + p['rk_lnx_b']
    bonus = (jnp.sum(r_h * k_h * p['rk_r_k'], axis=-1, keepdims=True) * v_h).reshape(B, T, D_A)
    out = (yn + bonus) * g
    return out, new_shift, S


def rglru_group(pb, conv0, h0, p):
    B, T, _ = pb.shape
    pbf = pb.astype(F32)
    yb, xb = jnp.split(pbf, 2, axis=-1)
    xpad = jnp.concatenate([conv0.astype(F32), xb], axis=1)
    cw = p['lru_conv_w'].astype(F32)
    xc = p['lru_conv_b'] + sum(cw[j] * xpad[:, j:j + T] for j in range(CONV_W))
    new_conv = xpad[:, -(CONV_W - 1):]
    xblk = xc.reshape(B, T, N_BLOCKS_B, BLOCK_B)
    gate_r = jax.nn.sigmoid(jnp.einsum('btnc,ncd->btnd', xblk, p['lru_w_r']).reshape(B, T, D_B) + p['lru_b_r'])
    gate_i = jax.nn.sigmoid(jnp.einsum('btnc,ncd->btnd', xblk, p['lru_w_i']).reshape(B, T, D_B) + p['lru_b_i'])
    log_a = -LRU_C * gate_r * jax.nn.softplus(-p['lru_lambda'].astype(F32))
    a_t = jnp.exp(log_a)
    b_t = jnp.sqrt(-jnp.expm1(2.0 * log_a)) * (gate_i * xc)
    b_t = b_t.at[:, 0].add(a_t[:, 0] * h0.astype(F32))

    def combine(e1, e2):
        a1, x1 = e1
        a2, x2 = e2
        return a1 * a2, a2 * x1 + x2

    _, h = lax.associative_scan(combine, (a_t, b_t), axis=1)
    out = rmsnorm(h * jax.nn.gelu(yb), p['lru_norm_g'])
    return out, new_conv, h[:, -1]


def moe_ffn(h, p):
    N = h.shape[0]
    logits = h.astype(F32) @ p['router_w'].astype(F32) + p['router_b'].astype(F32)
    top_v, top_i = lax.top_k(logits, TOP_K)
    gates = jax.nn.softmax(top_v, axis=-1)
    NK = N * TOP_K
    flat_e = top_i.reshape(NK)
    flat_tok = jnp.arange(NK, dtype=jnp.int32) // TOP_K
    flat_g = gates.reshape(NK)
    order = jnp.argsort(flat_e)
    e_sorted = flat_e[order]
    counts = jnp.bincount(flat_e, length=N_EXPERTS)
    padded = (counts + MOE_BLOCK - 1) // MOE_BLOCK * MOE_BLOCK
    pad_end = jnp.cumsum(padded)
    pad_start = pad_end - padded
    start = jnp.cumsum(counts) - counts
    dest = pad_start[e_sorted] + (jnp.arange(NK) - start[e_sorted])
    n_blocks = -(-NK // MOE_BLOCK) + N_EXPERTS
    n_rows = n_blocks * MOE_BLOCK
    row_tok = jnp.zeros((n_rows,), jnp.int32).at[dest].set(flat_tok[order])
    row_gate = jnp.zeros((n_rows,), F32).at[dest].set(flat_g[order])
    block_e = jnp.minimum(jnp.searchsorted(pad_end, jnp.arange(n_blocks) * MOE_BLOCK, side='right'),
                          N_EXPERTS - 1)

    def expert_block(args):
        e, toks = args
        xb = h[toks].astype(F32)
        gt = xb @ p['w_gate'][e] + p['b_gate'][e]
        up = xb @ p['w_up'][e] + p['b_up'][e]
        gt = jnp.minimum(gt, SWIGLU_LIMIT)
        up = jnp.clip(up, -SWIGLU_LIMIT, SWIGLU_LIMIT)
        glu = gt * jax.nn.sigmoid(gt * SWIGLU_ALPHA)
        return ((up + 1.0) * glu) @ p['w_down'][e] + p['b_down'][e]

    out = lax.map(expert_block, (block_e, row_tok.reshape(n_blocks, MOE_BLOCK)))
    y = jnp.zeros((N, D_MODEL), F32).at[row_tok].add(out.reshape(n_rows, D_MODEL).astype(F32) * row_gate[:, None])
    return y


def decoder_layer(x, c, st_wkv, st_shift, st_conv, st_lru, p):
    B, T, D = x.shape
    mod = (jax.nn.silu(c.astype(F32)) @ p['w_ada'] + p['b_ada']).reshape(B, N_MOD, D)
    shift1, scale1, gate1, shift2, scale2, gate2 = (mod[:, i, None, :] for i in range(N_MOD))
    h = rmsnorm(x, p['norm1_g']) * (1.0 + scale1) + shift1
    proj = h @ p['w_in']
    ya, new_shift, new_wkv = rwkv7_group(proj[..., :N_COLS_A], st_shift, st_wkv, p)
    yb, new_conv, new_lru = rglru_group(proj[..., N_COLS_A:], st_conv, st_lru, p)
    mixed = jnp.concatenate([ya, yb], axis=-1) @ p['w_out']
    x = (x.astype(F32) + gate1 * mixed).astype(x.dtype)
    h2 = rmsnorm(x, p['norm2_g']) * (1.0 + scale2) + shift2
    ff = moe_ffn(h2.reshape(B * T, D), p).reshape(B, T, D)
    x = (x.astype(F32) + gate2 * ff).astype(x.dtype)
    dt = x.dtype
    return x, (new_wkv.astype(dt), new_shift.astype(dt), new_conv.astype(dt), new_lru.astype(dt))


def setup_inputs(seed: int = 0) -> dict:
    key = jax.random.key(seed)
    ks = iter(jax.random.split(key, 64))

    def nrm(shape, scale):
        return scale * jax.random.normal(next(ks), shape, F32)

    def unif(shape, lo, hi):
        return jax.random.uniform(next(ks), shape, F32, lo, hi)

    L = DEPTH
    u = unif((L, D_B), 0.9, 0.999) ** (1.0 / LRU_C)
    return {
        'x_prompt': nrm((BATCH, SEQ, D_MODEL), 1.0),
        'x_sample': nrm((DEC_BATCH, DEC_SEQ, D_MODEL), 1.0),
        'c_prompt': nrm((BATCH, D_MODEL), 1.0),
        'c_sample': nrm((DEC_BATCH, D_MODEL), 1.0),
        'state_wkv': nrm((L, DEC_BATCH, N_HEADS_A, HEAD_DIM_A, HEAD_DIM_A), 0.3),
        'state_shift': nrm((L, DEC_BATCH, N_COLS_A), 1.0),
        'state_conv': nrm((L, DEC_BATCH, CONV_W - 1, D_B), 1.0),
        'state_lru': nrm((L, DEC_BATCH, D_B), 0.5),
        'w_ada': nrm((L, D_MODEL, N_MOD * D_MODEL), 0.5 * D_MODEL ** -0.5),
        'b_ada': nrm((L, N_MOD * D_MODEL), 0.02),
        'norm1_g': 1.0 + nrm((L, D_MODEL), 0.05),
        'norm2_g': 1.0 + nrm((L, D_MODEL), 0.05),
        'w_in': nrm((L, D_MODEL, N_COLS), D_MODEL ** -0.5),
        'rk_mu': unif((L, N_COLS_A), 0.0, 1.0),
        'rk_w0': unif((L, D_A), -6.0, 1.0),
        'rk_w_up': nrm((L, R_DECAY, D_A), 0.1 * R_DECAY ** -0.5),
        'rk_a0': nrm((L, D_A), 0.1),
        'rk_a_up': nrm((L, R_AAA, D_A), 0.1 * R_AAA ** -0.5),
        'rk_g_up': nrm((L, R_GATE, D_A), R_GATE ** -0.5),
        'rk_k_k': 0.85 + nrm((L, D_A), 0.05),
        'rk_k_a': 1.0 + nrm((L, D_A), 0.05),
        'rk_r_k': nrm((L, N_HEADS_A, HEAD_DIM_A), 0.1),
        'rk_lnx_w': 1.0 + nrm((L, D_A), 0.05),
        'rk_lnx_b': nrm((L, D_A), 0.02),
        'lru_conv_w': nrm((L, CONV_W, D_B), 0.5),
        'lru_conv_b': nrm((L, D_B), 0.02),
        'lru_w_r': nrm((L, N_BLOCKS_B, BLOCK_B, BLOCK_B), BLOCK_B ** -0.5),
        'lru_b_r': nrm((L, D_B), 0.02),
        'lru_w_i': nrm((L, N_BLOCKS_B, BLOCK_B, BLOCK_B), BLOCK_B ** -0.5),
        'lru_b_i': nrm((L, D_B), 0.02),
        'lru_lambda': jnp.log(u) - jnp.log1p(-u),
        'lru_norm_g': 1.0 + nrm((L, D_B), 0.05),
        'w_out': nrm((L, D_MIX, D_MODEL), D_MIX ** -0.5),
        'router_w': nrm((L, D_MODEL, N_EXPERTS), D_MODEL ** -0.5),
        'router_b': nrm((L, N_EXPERTS), 0.01),
        'w_gate': nrm((L, N_EXPERTS, D_MODEL, D_FF), D_MODEL ** -0.5),
        'b_gate': nrm((L, N_EXPERTS, D_FF), 0.02),
        'w_up': nrm((L, N_EXPERTS, D_MODEL, D_FF), D_MODEL ** -0.5),
        'b_up': nrm((L, N_EXPERTS, D_FF), 0.02),
        'w_down': nrm((L, N_EXPERTS, D_FF, D_MODEL), D_FF ** -0.5),
        'b_down': nrm((L, N_EXPERTS, D_MODEL), 0.02),
        'final_g': 1.0 + nrm((D_MODEL,), 0.05),
    }


def reference(x_prompt, x_sample, c_prompt, c_sample, state_wkv, state_shift, state_conv, state_lru,
              w_ada, b_ada, norm1_g, norm2_g, w_in, rk_mu, rk_w0, rk_w_up, rk_a0, rk_a_up, rk_g_up,
              rk_k_k, rk_k_a, rk_r_k, rk_lnx_w, rk_lnx_b, lru_conv_w, lru_conv_b, lru_w_r, lru_b_r,
              lru_w_i, lru_b_i, lru_lambda, lru_norm_g, w_out, router_w, router_b, w_gate, b_gate,
              w_up, b_up, w_down, b_down, final_g):
    yp, ys = x_prompt, x_sample
    Bp = x_prompt.shape[0]
    dt = x_prompt.dtype
    wkv_p, shift_p, conv_p, lru_p = [], [], [], []
    wkv_s, shift_s, conv_s, lru_s = [], [], [], []
    for l in range(DEPTH):
        lp = dict(w_ada=w_ada[l], b_ada=b_ada[l], norm1_g=norm1_g[l], norm2_g=norm2_g[l], w_in=w_in[l],
                  rk_mu=rk_mu[l], rk_w0=rk_w0[l], rk_w_up=rk_w_up[l], rk_a0=rk_a0[l], rk_a_up=rk_a_up[l],
                  rk_g_up=rk_g_up[l], rk_k_k=rk_k_k[l], rk_k_a=rk_k_a[l], rk_r_k=rk_r_k[l],
                  rk_lnx_w=rk_lnx_w[l], rk_lnx_b=rk_lnx_b[l], lru_conv_w=lru_conv_w[l],
                  lru_conv_b=lru_conv_b[l], lru_w_r=lru_w_r[l], lru_b_r=lru_b_r[l], lru_w_i=lru_w_i[l],
                  lru_b_i=lru_b_i[l], lru_lambda=lru_lambda[l], lru_norm_g=lru_norm_g[l], w_out=w_out[l],
                  router_w=router_w[l], router_b=router_b[l], w_gate=w_gate[l], b_gate=b_gate[l],
                  w_up=w_up[l], b_up=b_up[l], w_down=w_down[l], b_down=b_down[l])
        z_wkv = jnp.zeros((Bp, N_HEADS_A, HEAD_DIM_A, HEAD_DIM_A), dt)
        z_shift = jnp.zeros((Bp, N_COLS_A), dt)
        z_conv = jnp.zeros((Bp, CONV_W - 1, D_B), dt)
        z_lru = jnp.zeros((Bp, D_B), dt)
        yp, sp = decoder_layer(yp, c_prompt, z_wkv, z_shift, z_conv, z_lru, lp)
        ys, ss = decoder_layer(ys, c_sample, state_wkv[l], state_shift[l], state_conv[l], state_lru[l], lp)
        wkv_p.append(sp[0]); shift_p.append(sp[1]); conv_p.append(sp[2]); lru_p.append(sp[3])
        wkv_s.append(ss[0]); shift_s.append(ss[1]); conv_s.append(ss[2]); lru_s.append(ss[3])
    y_prompt = rmsnorm(yp, final_g).astype(dt)
    y_sample = rmsnorm(ys, final_g).astype(x_sample.dtype)
    return (y_prompt, y_sample,
            jnp.stack(wkv_p), jnp.stack(shift_p), jnp.stack(conv_p), jnp.stack(lru_p),
            jnp.stack(wkv_s), jnp.stack(shift_s), jnp.stack(conv_s), jnp.stack(lru_s))
```

```python
import functools

import jax
import jax.numpy as jnp
from jax import lax
from jax.experimental import pallas as pl
from jax.experimental.pallas import tpu as pltpu

F32 = jnp.float32
BF16 = jnp.bfloat16

D_MODEL = 1024
D_A = 512
HEAD_DIM = 64
N_HEADS = 8
D_B = 512
N_BLOCKS_B = 8
CONV_W = 4
LRU_C = 8.0
R_DECAY = 64
R_AAA = 64
R_GATE = 128
N_COLS_A = 3 * D_A + R_DECAY + R_AAA + R_GATE
N_COLS_B = 2 * D_B
N_EXPERTS = 32
TOP_K = 4
D_FF = 1024
SWIGLU_LIMIT = 7.0
SWIGLU_ALPHA = 1.702
RMS_EPS = 1e-6
LN_X_EPS = 64e-5
N_MOD = 6

LANES = 128
SUBLANES = 8
GROUP_BATCH = 8
HALF_ROWS = HEAD_DIM // 2
MOE_ROWS = 256
VMEM_LIMIT = 56 * 1024 * 1024


def _cparams(sem):
    return pltpu.CompilerParams(dimension_semantics=sem, vmem_limit_bytes=VMEM_LIMIT)


def _dot(a, b):
    return jnp.dot(a.astype(BF16), b.astype(BF16), preferred_element_type=F32)


def _split(a):
    hi = a.astype(BF16)
    lo = (a - hi.astype(F32)).astype(BF16)
    return hi, lo


def _dot3(a, b):
    ah, al = _split(a)
    bh, bl = _split(b)
    return (jnp.dot(ah, bh, preferred_element_type=F32)
            + (jnp.dot(al, bh, preferred_element_type=F32) + jnp.dot(ah, bl, preferred_element_type=F32)))


def _dot3_nt(a, b):
    dn = (((1,), (1,)), ((), ()))
    ah, al = _split(a)
    bh, bl = _split(b)
    d = lambda x, y: lax.dot_general(x, y, dn, preferred_element_type=F32)
    return d(ah, bh) + (d(al, bh) + d(ah, bl))


def _softplus(x):
    return jnp.maximum(x, 0.0) + jnp.log1p(jnp.exp(-jnp.abs(x)))


def _sigmoid(x):
    return 1.0 / (1.0 + jnp.exp(-x))


def _rms(x, g):
    ms = jnp.mean(x * x, axis=-1, keepdims=True)
    return x * lax.rsqrt(ms + RMS_EPS) * g


def _ada_kernel(c_ref, w_ref, b_ref, o_ref):
    c = c_ref[...]
    s = c * _sigmoid(c)
    o_ref[...] = _dot3(s, w_ref[...]) + b_ref[...]


def _ada(c, w_ada, b_ada):
    rows = c.shape[0]
    ncol = w_ada.shape[1]
    tn = D_MODEL
    return pl.pallas_call(
        _ada_kernel,
        grid=(ncol // tn,),
        in_specs=[pl.BlockSpec((rows, D_MODEL), lambda j: (0, 0)),
                  pl.BlockSpec((D_MODEL, tn), lambda j: (0, j)),
                  pl.BlockSpec((1, tn), lambda j: (0, j))],
        out_specs=pl.BlockSpec((rows, tn), lambda j: (0, j)),
        out_shape=jax.ShapeDtypeStruct((rows, ncol), F32),
        compiler_params=_cparams(("arbitrary",)),
        name="ada_mod",
    )(c, w_ada, b_ada.reshape(1, ncol))


def _inproj_kernel(x_ref, shift_ref, scale_ref, g_ref, wa_ref, wb_ref, pa_ref, pb_ref, *, precise):
    x = x_ref[0]
    h = _rms(x, g_ref[...]) * (1.0 + scale_ref[0]) + shift_ref[0]
    dot = _dot3 if precise else _dot
    pa_ref[0] = dot(h, wa_ref[...])
    pb_ref[0] = dot(h, wb_ref[...])


def _inproj(x, shift, scale, mod_map, g, wa, wb, tm, precise):
    nb, t, _ = x.shape
    mod_block = (1,) + shift.shape[1:]
    return pl.pallas_call(
        functools.partial(_inproj_kernel, precise=precise),
        grid=(nb, t // tm),
        in_specs=[pl.BlockSpec((1, tm, D_MODEL), lambda b, i: (b, i, 0)),
                  pl.BlockSpec(mod_block, functools.partial(mod_map, 0)),
                  pl.BlockSpec(mod_block, functools.partial(mod_map, 1)),
                  pl.BlockSpec((1, D_MODEL), lambda b, i: (0, 0)),
                  pl.BlockSpec((D_MODEL, N_COLS_A), lambda b, i: (0, 0)),
                  pl.BlockSpec((D_MODEL, N_COLS_B), lambda b, i: (0, 0))],
        out_specs=[pl.BlockSpec((1, tm, N_COLS_A), lambda b, i: (b, i, 0)),
                   pl.BlockSpec((1, tm, N_COLS_B), lambda b, i: (b, i, 0))],
        out_shape=[jax.ShapeDtypeStruct((nb, t, N_COLS_A), F32),
                   jax.ShapeDtypeStruct((nb, t, N_COLS_B), F32)],
        compiler_params=_cparams(("arbitrary", "arbitrary")),
        name="norm1_inproj",
    )(x, shift, scale, g.reshape(1, D_MODEL), wa, wb)


def _prep_kernel(pa_ref, prev_ref, mu_ref, w0_ref, wup_ref, a0_ref, gup_ref, kk_ref, ka_ref,
                 r_out, w_out, k_out, v_out, kk_out, b_out, g_out, carry_ref, *, seq):
    pa = pa_ref[0]
    if seq:
        @pl.when(pl.program_id(1) == 0)
        def _():
            carry_ref[...] = jnp.zeros_like(carry_ref)
        rolled = pltpu.roll(pa, 1, axis=0)
        row = lax.broadcasted_iota(jnp.int32, pa.shape, 0)
        prev = jnp.where(row == 0, carry_ref[...], rolled)
        carry_ref[...] = pa[pa.shape[0] - 1:, :]
    else:
        prev = prev_ref[0]
    z = pa + (prev - pa) * mu_ref[...]
    r = z[:, 0:D_A]
    k = z[:, D_A:2 * D_A]
    v = z[:, 2 * D_A:3 * D_A]
    lo = 3 * D_A
    za = z[:, lo:lo + R_DECAY + R_AAA]
    lane = lax.broadcasted_iota(jnp.int32, za.shape, 1)
    za = jnp.where(lane < R_DECAY, jnp.tanh(za), za)
    lw = _dot3(za, wup_ref[...])
    w_log = -_softplus(-(w0_ref[...] + lw[:, :D_A])) - 0.5
    decay = jnp.exp(-jnp.exp(w_log))
    a = _sigmoid(a0_ref[...] + lw[:, D_A:])
    gd = z[:, lo + R_DECAY + R_AAA:]
    g = _dot3(_sigmoid(gd), gup_ref[...])
    kk = k * kk_ref[...]
    r_out[0] = r
    w_out[0] = decay
    k_out[0] = k * (1.0 + (a - 1.0) * ka_ref[...])
    v_out[0] = v
    kk_out[0] = kk
    b_out[0] = kk * a
    g_out[0] = g


def _prep(pa, prev, p, tm, seq):
    nb, t, _ = pa.shape
    wup = jnp.zeros((R_DECAY + R_AAA, 2 * D_A), F32)
    wup = wup.at[:R_DECAY, :D_A].set(p['rk_w_up']).at[R_DECAY:, D_A:].set(p['rk_a_up'])
    vec = lambda a: a.reshape(1, -1)
    row_spec = lambda n: pl.BlockSpec((1, n), lambda b, i: (0, 0))
    tile = pl.BlockSpec((1, tm, D_A), lambda b, i: (b, i, 0))
    if prev is None:
        prev = jnp.zeros((1, SUBLANES, N_COLS_A), F32)
        prev_spec = pl.BlockSpec((1, SUBLANES, N_COLS_A), lambda b, i: (0, 0, 0))
    else:
        prev_spec = pl.BlockSpec((1, tm, N_COLS_A), lambda b, i: (b, i, 0))
    return pl.pallas_call(
        functools.partial(_prep_kernel, seq=seq),
        grid=(nb, t // tm),
        in_specs=[pl.BlockSpec((1, tm, N_COLS_A), lambda b, i: (b, i, 0)),
                  prev_spec,
                  row_spec(N_COLS_A), row_spec(D_A),
                  pl.BlockSpec((R_DECAY + R_AAA, 2 * D_A), lambda b, i: (0, 0)),
                  row_spec(D_A),
                  pl.BlockSpec((R_GATE, D_A), lambda b, i: (0, 0)),
                  row_spec(D_A), row_spec(D_A)],
        out_specs=[tile] * 7,
        out_shape=[jax.ShapeDtypeStruct((nb, t, D_A), F32)] * 7,
        scratch_shapes=[pltpu.VMEM((1, N_COLS_A), F32)],
        compiler_params=_cparams(("arbitrary", "arbitrary")),
        name="rwkv_prep",
    )(pa, prev, vec(p['rk_mu']), vec(p['rk_w0']), wup, vec(p['rk_a0']), p['rk_g_up'],
      vec(p['rk_k_k']), vec(p['rk_k_a']))


def _scan_kernel(w_ref, kkn_ref, b_ref, k_ref, r_ref, v_ref, kk0_ref, s0_ref, lnw_ref, lnb_ref, rk_ref,
                 out_ref, sfin_ref, s_ref, sa_ref, inv_ref, *, tt):
    ti = pl.program_id(1)

    def inv_norm2(kk_rows):
        s2 = jnp.sum(kk_rows * kk_rows, axis=0, keepdims=True)
        return 1.0 / jnp.maximum(s2, 1e-24)

    @pl.when(ti == 0)
    def _():
        s_ref[...] = s0_ref[0]
        acc = jnp.zeros((HALF_ROWS, LANES), F32)
        for j in range(HEAD_DIM):
            acc = acc + s0_ref[0, j] * kk0_ref[0, 0, j:j + 1, :]
        sa_ref[...] = acc
        inv_ref[...] = inv_norm2(kk0_ref[0, 0])

    def step(t, carry):
        sa, inv2 = carry
        sae = sa * (-inv2)
        v = v_ref[0, t]
        acc_y = jnp.zeros((HALF_ROWS, LANES), F32)
        acc_s = jnp.zeros((HALF_ROWS, LANES), F32)
        for j in range(HEAD_DIM):
            row = pl.ds(j, 1)
            s_new = (s_ref[j] * w_ref[0, t, row, :] + sae * b_ref[0, t, row, :]
                     + v * k_ref[0, t, row, :])
            s_ref[j] = s_new
            acc_y = acc_y + s_new * r_ref[0, t, row, :]
            acc_s = acc_s + s_new * kkn_ref[0, t, row, :]
        tot = jnp.sum(acc_y, axis=0, keepdims=True)
        tot = tot + pltpu.roll(tot, LANES // 2, axis=1)
        d = acc_y - tot * (1.0 / HEAD_DIM)
        sq = jnp.sum(d * d, axis=0, keepdims=True)
        sq = sq + pltpu.roll(sq, LANES // 2, axis=1)
        yn = d * lax.rsqrt(sq * (1.0 / HEAD_DIM) + LN_X_EPS)
        c = jnp.sum(r_ref[0, t] * k_ref[0, t] * rk_ref[...], axis=0, keepdims=True)
        out_ref[0, t] = yn * lnw_ref[...] + lnb_ref[...] + c * v
        return acc_s, inv_norm2(kkn_ref[0, t])

    sa, inv2 = lax.fori_loop(0, tt, step, (sa_ref[...], inv_ref[...]))
    sa_ref[...] = sa
    inv_ref[...] = inv2

    @pl.when(ti == pl.num_programs(1) - 1)
    def _():
        sfin_ref[0] = s_ref[...]


def _to_scan_keys(x):
    nb, t, _ = x.shape
    g = nb // GROUP_BATCH
    y = x.reshape(g, GROUP_BATCH, t, N_HEADS, HEAD_DIM).transpose(0, 2, 4, 1, 3)
    y = y.reshape(g, t, HEAD_DIM, GROUP_BATCH * N_HEADS)
    return jnp.concatenate([y, y], axis=-1)


def _to_scan_rows(x):
    nb, t, _ = x.shape
    g = nb // GROUP_BATCH
    y = x.reshape(g, GROUP_BATCH, t, N_HEADS, 2, HALF_ROWS).transpose(0, 2, 5, 4, 1, 3)
    return y.reshape(g, t, HALF_ROWS, LANES)


def _from_scan_rows(y):
    g, t = y.shape[:2]
    x = y.reshape(g, t, HALF_ROWS, 2, GROUP_BATCH, N_HEADS).transpose(0, 4, 1, 5, 3, 2)
    return x.reshape(g * GROUP_BATCH, t, D_A)


def _state_to_scan(s):
    g = s.shape[0] // GROUP_BATCH
    y = s.reshape(g, GROUP_BATCH, N_HEADS, 2, HALF_ROWS, HEAD_DIM).transpose(0, 5, 4, 3, 1, 2)
    return y.reshape(g, HEAD_DIM, HALF_ROWS, LANES)


def _state_from_scan(y):
    g = y.shape[0]
    s = y.reshape(g, HEAD_DIM, HALF_ROWS, 2, GROUP_BATCH, N_HEADS).transpose(0, 4, 5, 3, 2, 1)
    return s.reshape(g * GROUP_BATCH, N_HEADS, HEAD_DIM, HEAD_DIM)


def _head_rows(x):
    y = x.reshape(N_HEADS, 2, HALF_ROWS).transpose(2, 1, 0)
    y = jnp.broadcast_to(y[:, :, None, :], (HALF_ROWS, 2, GROUP_BATCH, N_HEADS))
    return y.reshape(HALF_ROWS, LANES)


def _head_keys(x):
    y = jnp.broadcast_to(x.T[:, None, None, :], (HEAD_DIM, 2, GROUP_BATCH, N_HEADS))
    return y.reshape(HEAD_DIM, LANES)


def _wkv_scan(r, w, k, v, kk, b, s0, p, tt):
    t = r.shape[1]
    wk, bk, kkey, rkey, kks = (_to_scan_keys(a) for a in (w, b, k, r, kk))
    kkn = jnp.concatenate([kks[:, 1:], jnp.zeros_like(kks[:, :1])], axis=1)
    kk0 = kks[:, :1]
    vr = _to_scan_rows(v)
    g = wk.shape[0]
    key_tile = pl.BlockSpec((1, tt, HEAD_DIM, LANES), lambda gi, i: (gi, i, 0, 0))
    row_tile = pl.BlockSpec((1, tt, HALF_ROWS, LANES), lambda gi, i: (gi, i, 0, 0))
    state = pl.BlockSpec((1, HEAD_DIM, HALF_ROWS, LANES), lambda gi, i: (gi, 0, 0, 0))
    const = lambda n: pl.BlockSpec((n, LANES), lambda gi, i: (0, 0))
    out, sfin = pl.pallas_call(
        functools.partial(_scan_kernel, tt=tt),
        grid=(g, t // tt),
        in_specs=[key_tile] * 5 + [row_tile,
                  pl.BlockSpec((1, 1, HEAD_DIM, LANES), lambda gi, i: (gi, 0, 0, 0)),
                  state, const(HALF_ROWS), const(HALF_ROWS), const(HEAD_DIM)],
        out_specs=[row_tile, state],
        out_shape=[jax.ShapeDtypeStruct((g, t, HALF_ROWS, LANES), F32),
                   jax.ShapeDtypeStruct((g, HEAD_DIM, HALF_ROWS, LANES), F32)],
        scratch_shapes=[pltpu.VMEM((HEAD_DIM, HALF_ROWS, LANES), F32),
                        pltpu.VMEM((HALF_ROWS, LANES), F32),
                        pltpu.VMEM((1, LANES), F32)],
        compiler_params=_cparams(("arbitrary", "arbitrary")),
        name="wkv_scan",
    )(wk, kkn, bk, kkey, rkey, vr, kk0, _state_to_scan(s0),
      _head_rows(p['rk_lnx_w']), _head_rows(p['rk_lnx_b']), _head_keys(p['rk_r_k']))
    return _from_scan_rows(out), _state_from_scan(sfin)


def _gelu(x):
    return 0.5 * x * (1.0 + jnp.tanh(0.7978845608028654 * (x + 0.044715 * (x * x * x))))


def _lru_gates(xc, wri_ref, bri_ref, nsl_ref, precise):
    dot = _dot3 if precise else _dot
    gates = _sigmoid(dot(xc, wri_ref[...]) + bri_ref[...])
    gate_r = gates[:, :D_B]
    gate_i = gates[:, D_B:]
    log_a = gate_r * nsl_ref[...]
    a = jnp.exp(log_a)
    th = jnp.tanh(log_a)
    one_minus_a2 = -2.0 * th / (1.0 - th)
    bt = jnp.sqrt(one_minus_a2) * (gate_i * xc)
    return a, bt


def _lru_seq_kernel(pb_ref, cw_ref, cb_ref, wri_ref, bri_ref, nsl_ref, ng_ref,
                    out_ref, hlast_ref, xprev_ref, h_ref):
    tm = pb_ref.shape[1]

    @pl.when(pl.program_id(1) == 0)
    def _():
        xprev_ref[...] = jnp.zeros_like(xprev_ref)
        h_ref[...] = jnp.zeros_like(h_ref)

    pb = pb_ref[0]
    yb = pb[:, :D_B]
    xb = pb[:, D_B:]
    xprev = xprev_ref[...]
    row8 = lax.broadcasted_iota(jnp.int32, (SUBLANES, D_B), 0)

    def shifted(d):
        rolled = pltpu.roll(xb, d, axis=0)
        top = jnp.where(row8 < d, pltpu.roll(xprev, d, axis=0), rolled[:SUBLANES])
        return jnp.concatenate([top, rolled[SUBLANES:]], axis=0)

    xc = cb_ref[...] + cw_ref[3:4, :] * xb
    for d in range(1, CONV_W):
        xc = xc + cw_ref[3 - d:4 - d, :] * shifted(d)
    xprev_ref[...] = xb[tm - SUBLANES:, :]

    a, x = _lru_gates(xc, wri_ref, bri_ref, nsl_ref, False)
    row = lax.broadcasted_iota(jnp.int32, (tm, D_B), 0)
    d = 1
    while d < tm:
        keep = row >= d
        a_s = jnp.where(keep, pltpu.roll(a, d, axis=0), 1.0)
        x_s = jnp.where(keep, pltpu.roll(x, d, axis=0), 0.0)
        x = a * x_s + x
        a = a * a_s
        d *= 2
    h = a * h_ref[...] + x
    h_ref[...] = h[tm - 1:, :]
    hlast_ref[0] = h[tm - 1:, :]
    out_ref[0] = _rms(h * _gelu(yb), ng_ref[...]).astype(out_ref.dtype)


def _lru_params(p):
    eye = jnp.eye(N_BLOCKS_B, dtype=F32)
    blk = D_B // N_BLOCKS_B
    bd = lambda w: (eye[:, None, :, None] * w[:, :, None, :]).reshape(D_B, D_B)
    wri = jnp.concatenate([bd(p['lru_w_r']), bd(p['lru_w_i'])], axis=1)
    bri = jnp.concatenate([p['lru_b_r'], p['lru_b_i']]).reshape(1, 2 * D_B)
    nsl = (-LRU_C * jax.nn.softplus(-p['lru_lambda'])).reshape(1, D_B)
    del blk
    return wri, bri, nsl


def _lru_seq(pb, p, tm):
    nb, t, _ = pb.shape
    wri, bri, nsl = _lru_params(p)
    row_spec = lambda n: pl.BlockSpec((1, n), lambda b, i: (0, 0))
    return pl.pallas_call(
        _lru_seq_kernel,
        grid=(nb, t // tm),
        in_specs=[pl.BlockSpec((1, tm, N_COLS_B), lambda b, i: (b, i, 0)),
                  pl.BlockSpec((CONV_W, D_B), lambda b, i: (0, 0)),
                  row_spec(D_B),
                  pl.BlockSpec((D_B, 2 * D_B), lambda b, i: (0, 0)),
                  row_spec(2 * D_B), row_spec(D_B), row_spec(D_B)],
        out_specs=[pl.BlockSpec((1, tm, D_B), lambda b, i: (b, i, 0)),
                   pl.BlockSpec((1, 1, D_B), lambda b, i: (b, 0, 0))],
        out_shape=[jax.ShapeDtypeStruct((nb, t, D_B), BF16),
                   jax.ShapeDtypeStruct((nb, 1, D_B), F32)],
        scratch_shapes=[pltpu.VMEM((SUBLANES, D_B), F32), pltpu.VMEM((1, D_B), F32)],
        compiler_params=_cparams(("arbitrary", "arbitrary")),
        name="rglru_seq",
    )(pb, p['lru_conv_w'], p['lru_conv_b'].reshape(1, D_B), wri.astype(BF16), bri, nsl,
      p['lru_norm_g'].reshape(1, D_B))


def _lru_step_kernel(pb_ref, conv_ref, h0_ref, cw_ref, cb_ref, wri_ref, bri_ref, nsl_ref, ng_ref,
                     out_ref, hnew_ref):
    pb = pb_ref[...]
    yb = pb[:, :D_B]
    xb = pb[:, D_B:]
    xc = cb_ref[...] + cw_ref[3:4, :] * xb
    for j in range(CONV_W - 1):
        xc = xc + cw_ref[j:j + 1, :] * conv_ref[j]
    a, x = _lru_gates(xc, wri_ref, bri_ref, nsl_ref, True)
    h = a * h0_ref[...] + x
    hnew_ref[...] = h
    out_ref[...] = _rms(h * _gelu(yb), ng_ref[...]).astype(out_ref.dtype)


def _lru_step(pb, conv0, h0, p):
    n = pb.shape[0]
    wri, bri, nsl = _lru_params(p)
    return pl.pallas_call(
        _lru_step_kernel,
        out_shape=[jax.ShapeDtypeStruct((n, D_B), BF16), jax.ShapeDtypeStruct((n, D_B), F32)],
        compiler_params=pltpu.CompilerParams(vmem_limit_bytes=VMEM_LIMIT),
        name="rglru_step",
    )(pb, conv0, h0, p['lru_conv_w'], p['lru_conv_b'].reshape(1, D_B), wri, bri, nsl,
      p['lru_norm_g'].reshape(1, D_B))


def _post_kernel(x_ref, wkv_ref, g_ref, yb_ref, gate1_ref, shift2_ref, scale2_ref, n2_ref,
                 wo_ref, rw_ref, rb_ref, x1_ref, h2_ref, ti_ref, tg_ref):
    ya = (wkv_ref[0] * g_ref[0]).astype(BF16)
    mixed = (jnp.dot(ya, wo_ref[:D_A, :], preferred_element_type=F32)
             + jnp.dot(yb_ref[0], wo_ref[D_A:, :], preferred_element_type=F32))
    x1 = x_ref[0] + gate1_ref[0] * mixed
    x1_ref[0] = x1
    h2 = _rms(x1, n2_ref[...]) * (1.0 + scale2_ref[0]) + shift2_ref[0]
    h2_ref[0] = h2
    logits = _dot3_nt(rw_ref[...], h2) + rb_ref[...]
    eidx = lax.broadcasted_iota(jnp.int32, logits.shape, 0)
    vals, idxs = [], []
    cur = logits
    for _ in range(TOP_K):
        m = jnp.max(cur, axis=0, keepdims=True)
        i = jnp.min(jnp.where(cur == m, eidx, N_EXPERTS), axis=0, keepdims=True)
        vals.append(m)
        idxs.append(i)
        cur = jnp.where(eidx == i, -jnp.inf, cur)
    ex = [jnp.exp(v - vals[0]) for v in vals]
    den = ex[0] + ex[1] + ex[2] + ex[3]
    ti_ref[0] = jnp.concatenate(idxs, axis=0)
    tg_ref[0] = jnp.concatenate([e / den for e in ex], axis=0)


def _post(x, wkv, g, yb, gate1, shift2, scale2, mod_map, p, tm):
    nb, t, _ = x.shape
    mod_block = (1,) + gate1.shape[1:]
    tile = lambda n: pl.BlockSpec((1, tm, n), lambda b, i: (b, i, 0))
    full = lambda a: pl.BlockSpec(a.shape, lambda b, i: (0,) * a.ndim)
    mspec = lambda j: pl.BlockSpec(mod_block, functools.partial(mod_map, j))
    n2 = p['norm2_g'].reshape(1, D_MODEL)
    wo = p['w_out'].astype(BF16)
    rw = p['router_w'].T
    rb = p['router_b'].reshape(N_EXPERTS, 1)
    nt = t // tm
    topk = pl.BlockSpec((1, TOP_K, tm), lambda b, i: (b * nt + i, 0, 0))
    x1, h2, ti, tg = pl.pallas_call(
        _post_kernel,
        grid=(nb, nt),
        in_specs=[tile(D_MODEL), tile(D_A), tile(D_A), tile(D_B), mspec(2), mspec(3), mspec(4),
                  full(n2), full(wo), full(rw), full(rb)],
        out_specs=[tile(D_MODEL), tile(D_MODEL), topk, topk],
        out_shape=[jax.ShapeDtypeStruct((nb, t, D_MODEL), F32),
                   jax.ShapeDtypeStruct((nb, t, D_MODEL), F32),
                   jax.ShapeDtypeStruct((nb * nt, TOP_K, tm), jnp.int32),
                   jax.ShapeDtypeStruct((nb * nt, TOP_K, tm), F32)],
        compiler_params=_cparams(("arbitrary", "arbitrary")),
        name="outproj_router",
    )(x, wkv, g, yb, gate1, shift2, scale2, n2, wo, rw, rb)
    n = nb * t
    ti = ti.transpose(1, 0, 2).reshape(TOP_K, n)
    tg = tg.transpose(1, 0, 2).reshape(TOP_K, n)
    return x1.reshape(n, D_MODEL), h2.reshape(n, D_MODEL), ti, tg


def _moe_plan(top_i, n):
    onehot = jnp.any(top_i[:, :, None] == jnp.arange(N_EXPERTS, dtype=jnp.int32), axis=0).astype(jnp.int32)
    csum = jnp.cumsum(onehot, axis=0)
    counts = csum[-1]
    rank = csum - onehot
    padded = (counts + MOE_ROWS - 1) // MOE_ROWS * MOE_ROWS
    pad_end = jnp.cumsum(padded)
    pad_start = pad_end - padded
    dest = pad_start[top_i] + jnp.take_along_axis(rank, top_i.T, axis=1).T
    n_blocks = -(-n * TOP_K // MOE_ROWS) + N_EXPERTS
    block_row0 = jnp.arange(n_blocks, dtype=jnp.int32) * MOE_ROWS
    block_e = jnp.minimum(jnp.sum(pad_end[None, :] <= block_row0[:, None], axis=1), N_EXPERTS - 1).astype(jnp.int32)
    n_used = (pad_end[-1] // MOE_ROWS).astype(jnp.int32).reshape(1)
    return dest.astype(jnp.int32), block_e, n_used, pad_end.astype(jnp.int32), padded.astype(jnp.int32), n_blocks


def _row_copy(src, src_row, dst, dst_row, sem):
    return pltpu.make_async_copy(src.at[pl.ds(src_row, 1), :], dst.at[pl.ds(dst_row, 1), :], sem)


def _dispatch_kernel(pad_end_ref, padded_ref, h2_ref, dest_hbm, xs_hbm, idx_ref, zero_ref, sem, isem):
    i = pl.program_id(0)
    tm = h2_ref.shape[0]
    idx_copy = pltpu.make_async_copy(dest_hbm.at[i], idx_ref, isem)
    idx_copy.start()

    @pl.when(i == 0)
    def _():
        zero_ref[...] = jnp.zeros_like(zero_ref)

        def fill(e, n_started):
            has = padded_ref[e] > 0
            start = pl.multiple_of(jnp.maximum(pad_end_ref[e] - MOE_ROWS, 0), MOE_ROWS)

            @pl.when(has)
            def _():
                pltpu.make_async_copy(zero_ref, xs_hbm.at[pl.ds(start, MOE_ROWS), :], sem).start()
            return n_started + has.astype(jnp.int32)

        n_started = lax.fori_loop(0, N_EXPERTS, fill, jnp.int32(0))

        n_used = pad_end_ref[N_EXPERTS - 1] // MOE_ROWS
        n_blocks = xs_hbm.shape[0] // MOE_ROWS

        def fill_tail(blk, c):
            start = pl.multiple_of(blk * MOE_ROWS, MOE_ROWS)
            pltpu.make_async_copy(zero_ref, xs_hbm.at[pl.ds(start, MOE_ROWS), :], sem).start()
            return c
        lax.fori_loop(n_used, n_blocks, fill_tail, 0)
        n_started = n_started + (n_blocks - n_used)

        def drain(e, c):
            pltpu.make_async_copy(zero_ref, xs_hbm.at[pl.ds(0, MOE_ROWS), :], sem).wait()
            return c
        lax.fori_loop(0, n_started, drain, 0)

    idx_copy.wait()

    def issue(r, c):
        for k in range(TOP_K):
            _row_copy(h2_ref, r, xs_hbm, idx_ref[k * tm + r], sem).start()
        return c
    lax.fori_loop(0, tm, issue, 0)

    def drain_rows(r, c):
        _row_copy(h2_ref, 0, xs_hbm, 0, sem).wait()
        return c
    lax.fori_loop(0, tm * TOP_K, drain_rows, 0)


def _dispatch(h2, dest, pad_end, padded, n_rows, tm):
    n = h2.shape[0]
    nt = n // tm
    dest_t = dest.reshape(TOP_K, nt, tm).transpose(1, 0, 2).reshape(nt, TOP_K * tm)
    grid_spec = pltpu.PrefetchScalarGridSpec(
        num_scalar_prefetch=2,
        grid=(nt,),
        in_specs=[pl.BlockSpec((tm, D_MODEL), lambda i, *_: (i, 0)),
                  pl.BlockSpec(memory_space=pl.ANY)],
        out_specs=pl.BlockSpec(memory_space=pl.ANY),
        scratch_shapes=[pltpu.SMEM((TOP_K * tm,), jnp.int32),
                        pltpu.VMEM((MOE_ROWS, D_MODEL), F32),
                        pltpu.SemaphoreType.DMA, pltpu.SemaphoreType.DMA],
    )
    return pl.pallas_call(
        _dispatch_kernel,
        grid_spec=grid_spec,
        out_shape=jax.ShapeDtypeStruct((n_rows, D_MODEL), F32),
        compiler_params=_cparams(("arbitrary",)),
        name="moe_dispatch",
    )(pad_end, padded, h2, dest_t)


def _experts_kernel(be_ref, nu_ref, xs_ref, wg_ref, bg_ref, wu_ref, bu_ref, wd_ref, bd_ref, out_ref):
    @pl.when(pl.program_id(0) < nu_ref[0])
    def _():
        x = xs_ref[...].astype(BF16)
        gt = jnp.dot(x, wg_ref[0], preferred_element_type=F32) + bg_ref[0]
        up = jnp.dot(x, wu_ref[0], preferred_element_type=F32) + bu_ref[0]
        gt = jnp.minimum(gt, SWIGLU_LIMIT)
        up = jnp.clip(up, -SWIGLU_LIMIT, SWIGLU_LIMIT)
        glu = gt * _sigmoid(gt * SWIGLU_ALPHA)
        mid = ((up + 1.0) * glu).astype(BF16)
        out_ref[...] = jnp.dot(mid, wd_ref[0], preferred_element_type=F32) + bd_ref[0]

    @pl.when(pl.program_id(0) >= nu_ref[0])
    def _():
        out_ref[...] = jnp.zeros_like(out_ref)


def _experts(xs, block_e, n_used, n_blocks, wts):
    wg, bg, wu, bu, wd, bd = wts
    row_map = lambda i, be, nu: (i, 0)
    w_map = lambda i, be, nu: (be[i], 0, 0)
    wspec = pl.BlockSpec((1, D_MODEL, D_FF), w_map)
    bspec = pl.BlockSpec((1, 1, D_FF), w_map)
    grid_spec = pltpu.PrefetchScalarGridSpec(
        num_scalar_prefetch=2,
        grid=(n_blocks,),
        in_specs=[pl.BlockSpec((MOE_ROWS, D_MODEL), row_map), wspec, bspec, wspec, bspec, wspec, bspec],
        out_specs=pl.BlockSpec((MOE_ROWS, D_MODEL), row_map),
    )
    return pl.pallas_call(
        _experts_kernel,
        grid_spec=grid_spec,
        out_shape=jax.ShapeDtypeStruct(xs.shape, F32),
        compiler_params=_cparams(("arbitrary",)),
        name="moe_experts",
    )(block_e, n_used, xs, wg, bg, wu, bu, wd, bd)


def _combine_kernel(x1_ref, tg_ref, gate2_ref, fg_ref, dest_hbm, rows_hbm, y_ref, idx_ref, buf_ref, sem, isem):
    i = pl.program_id(1) + pl.program_id(0) * pl.num_programs(1)
    tm = x1_ref.shape[1]
    idx_copy = pltpu.make_async_copy(dest_hbm.at[i], idx_ref, isem)
    idx_copy.start()
    idx_copy.wait()

    def issue(r, c):
        for k in range(TOP_K):
            _row_copy(rows_hbm, idx_ref[k * tm + r], buf_ref.at[k], r, sem).start()
        return c
    lax.fori_loop(0, tm, issue, 0)

    def drain(r, c):
        _row_copy(rows_hbm, 0, buf_ref.at[0], 0, sem).wait()
        return c
    lax.fori_loop(0, tm * TOP_K, drain, 0)

    krow = lax.broadcasted_iota(jnp.int32, (LANES, tm), 0)
    gates = jnp.zeros((LANES, tm), F32)
    for k in range(TOP_K):
        gates = jnp.where(krow == k, tg_ref[0, k:k + 1, :], gates)
    gates = gates.T
    ff = buf_ref[0] * gates[:, 0:1]
    for k in range(1, TOP_K):
        ff = ff + buf_ref[k] * gates[:, k:k + 1]
    x2 = x1_ref[0] + gate2_ref[0] * ff
    y_ref[0] = _rms(x2, fg_ref[...])


def _combine(x1, tg, gate2, mod_map, final_g, dest, rows, nb, t, tm):
    nt = t // tm
    n = nb * t
    dest_t = dest.reshape(TOP_K, nb * nt, tm).transpose(1, 0, 2).reshape(nb * nt, TOP_K * tm)
    tg_t = tg.reshape(TOP_K, nb * nt, tm).transpose(1, 0, 2)
    mod_block = (1,) + gate2.shape[1:]
    return pl.pallas_call(
        _combine_kernel,
        grid=(nb, nt),
        in_specs=[pl.BlockSpec((1, tm, D_MODEL), lambda b, i: (b, i, 0)),
                  pl.BlockSpec((1, TOP_K, tm), lambda b, i: (b * nt + i, 0, 0)),
                  pl.BlockSpec(mod_block, functools.partial(mod_map, 5)),
                  pl.BlockSpec((1, D_MODEL), lambda b, i: (0, 0)),
                  pl.BlockSpec(memory_space=pl.ANY),
                  pl.BlockSpec(memory_space=pl.ANY)],
        out_specs=pl.BlockSpec((1, tm, D_MODEL), lambda b, i: (b, i, 0)),
        out_shape=jax.ShapeDtypeStruct((nb, t, D_MODEL), F32),
        scratch_shapes=[pltpu.SMEM((TOP_K * tm,), jnp.int32),
                        pltpu.VMEM((TOP_K, tm, D_MODEL), F32),
                        pltpu.SemaphoreType.DMA, pltpu.SemaphoreType.DMA],
        compiler_params=_cparams(("arbitrary", "arbitrary")),
        name="moe_combine",
    )(x1.reshape(nb, t, D_MODEL), tg_t, gate2, final_g.reshape(1, D_MODEL), dest_t, rows)


def _moe(x1, h2, ti, tg, gate2, mod_map, final_g, wts, nb, t, tm):
    n = nb * t
    dest, block_e, n_used, pad_end, padded, n_blocks = _moe_plan(ti, n)
    xs = _dispatch(h2, dest, pad_end, padded, n_blocks * MOE_ROWS, tm)
    rows = _experts(xs, block_e, n_used, n_blocks, wts)
    return _combine(x1, tg, gate2, mod_map, final_g, dest, rows, nb, t, tm)


def _forward(x_prompt, x_sample, c_prompt, c_sample, state_wkv, state_shift, state_conv, state_lru, p, final_g):
    bp, tp, _ = x_prompt.shape
    bs = x_sample.shape[0]
    tm = min(512, tp)
    tt = min(32, tp)

    mod = _ada(jnp.concatenate([c_prompt, c_sample], axis=0), p['w_ada'], p['b_ada'])
    mod_p = mod[:bp].reshape(bp * N_MOD, 1, D_MODEL)
    mod_s = mod[bp:].reshape(bs, N_MOD, D_MODEL).transpose(1, 0, 2)
    map_p = lambda j, b, i: (b * N_MOD + j, 0, 0)
    map_s = lambda j, b, i: (j, 0, 0)

    wa = p['w_in'][:, :N_COLS_A]
    wb = p['w_in'][:, N_COLS_A:]
    wts = (p['w_gate'].astype(BF16), p['b_gate'].reshape(N_EXPERTS, 1, D_FF),
           p['w_up'].astype(BF16), p['b_up'].reshape(N_EXPERTS, 1, D_FF),
           p['w_down'].astype(BF16), p['b_down'].reshape(N_EXPERTS, 1, D_MODEL))

    pa, pb = _inproj(x_prompt, mod_p, mod_p, map_p, p['norm1_g'], wa.astype(BF16), wb.astype(BF16), tm, False)
    r, w, k, v, kk, b, g = _prep(pa, None, p, tm, True)
    s0 = jnp.zeros((bp, N_HEADS, HEAD_DIM, HEAD_DIM), F32)
    wkv_out, wkv_p = _wkv_scan(r, w, k, v, kk, b, s0, p, tt)
    yb, lru_p = _lru_seq(pb, p, tm)
    x1, h2, ti, tg = _post(x_prompt, wkv_out, g, yb, mod_p, mod_p, mod_p, map_p, p, tm)
    y_prompt = _moe(x1, h2, ti, tg, mod_p, map_p, final_g, wts, bp, tp, min(256, tp))
    shift_p = pa[:, -1, :]
    conv_p = pb[:, tp - (CONV_W - 1):, D_B:]

    xs = x_sample.reshape(1, bs, D_MODEL)
    pa_s, pb_s = _inproj(xs, mod_s, mod_s, map_s, p['norm1_g'], wa, wb, bs, True)
    r, w, k, v, kk, b, g = _prep(pa_s, state_shift.reshape(1, bs, N_COLS_A), p, bs, False)
    as_seq = lambda a: a.reshape(bs, 1, D_A)
    wkv_out, wkv_s = _wkv_scan(as_seq(r), as_seq(w), as_seq(k), as_seq(v), as_seq(kk), as_seq(b),
                               state_wkv, p, 1)
    conv0 = state_conv.transpose(1, 0, 2)
    yb, lru_s = _lru_step(pb_s[0], conv0, state_lru, p)
    x1, h2, ti, tg = _post(xs, wkv_out.reshape(1, bs, D_A), g, yb.reshape(1, bs, D_B), mod_s, mod_s, mod_s,
                           map_s, p, bs)
    y_sample = _moe(x1, h2, ti, tg, mod_s, map_s, final_g, wts, 1, bs, bs)
    shift_s = pa_s[0]
    conv_s = jnp.concatenate([state_conv[:, 1:], pb_s[0][:, None, D_B:]], axis=1)

    return (y_prompt, y_sample.reshape(bs, 1, D_MODEL),
            wkv_p[None], shift_p[None], conv_p[None], lru_p.reshape(bp, D_B)[None],
            wkv_s[None], shift_s[None], conv_s[None], lru_s[None])


def kernel(x_prompt, x_sample, c_prompt, c_sample, state_wkv, state_shift, state_conv, state_lru, w_ada, b_ada, norm1_g, norm2_g, w_in, rk_mu, rk_w0, rk_w_up, rk_a0, rk_a_up, rk_g_up, rk_k_k, rk_k_a, rk_r_k, rk_lnx_w, rk_lnx_b, lru_conv_w, lru_conv_b, lru_w_r, lru_b_r, lru_w_i, lru_b_i, lru_lambda, lru_norm_g, w_out, router_w, router_b, w_gate, b_gate, w_up, b_up, w_down, b_down, final_g):
    assert w_ada.shape[0] == 1, "single-layer trunk"
    p = dict(w_ada=w_ada[0], b_ada=b_ada[0], norm1_g=norm1_g[0], norm2_g=norm2_g[0], w_in=w_in[0],
             rk_mu=rk_mu[0], rk_w0=rk_w0[0], rk_w_up=rk_w_up[0], rk_a0=rk_a0[0], rk_a_up=rk_a_up[0],
             rk_g_up=rk_g_up[0], rk_k_k=rk_k_k[0], rk_k_a=rk_k_a[0], rk_r_k=rk_r_k[0],
             rk_lnx_w=rk_lnx_w[0], rk_lnx_b=rk_lnx_b[0], lru_conv_w=lru_conv_w[0],
             lru_conv_b=lru_conv_b[0], lru_w_r=lru_w_r[0], lru_b_r=lru_b_r[0], lru_w_i=lru_w_i[0],
             lru_b_i=lru_b_i[0], lru_lambda=lru_lambda[0], lru_norm_g=lru_norm_g[0], w_out=w_out[0],
             router_w=router_w[0], router_b=router_b[0], w_gate=w_gate[0], b_gate=b_gate[0],
             w_up=w_up[0], b_up=b_up[0], w_down=w_down[0], b_down=b_down[0])
    return _forward(x_prompt, x_sample, c_prompt, c_sample, state_wkv[0], state_shift[0], state_conv[0],
                    state_lru[0], p, final_g)
```

```python
import functools

import jax
import jax.numpy as jnp
from jax import lax
from jax.experimental import pallas as pl
from jax.experimental.pallas import tpu as pltpu

F32 = jnp.float32
BF16 = jnp.bfloat16

D_MODEL = 1024
D_A = 512
HEAD_DIM = 64
N_HEADS = 8
D_B = 512
N_BLOCKS_B = 8
CONV_W = 4
LRU_C = 8.0
R_DECAY = 64
R_AAA = 64
R_GATE = 128
N_COLS_A = 3 * D_A + R_DECAY + R_AAA + R_GATE
N_COLS_B = 2 * D_B
N_EXPERTS = 32
TOP_K = 4
D_FF = 1024
SWIGLU_LIMIT = 7.0
SWIGLU_ALPHA = 1.702
RMS_EPS = 1e-6
LN_X_EPS = 64e-5
N_MOD = 6

LANES = 128
SUBLANES = 8
GROUP_BATCH = 8
HALF_ROWS = HEAD_DIM // 2
MOE_ROWS = 256
TILE_CHUNKS = D_MODEL // LANES
VMEM_LIMIT = 56 * 1024 * 1024


def _cparams(sem):
    return pltpu.CompilerParams(dimension_semantics=sem, vmem_limit_bytes=VMEM_LIMIT)


def _dot(a, b):
    return jnp.dot(a.astype(BF16), b.astype(BF16), preferred_element_type=F32)


def _split(a):
    hi = a.astype(BF16)
    lo = (a - hi.astype(F32)).astype(BF16)
    return hi, lo


def _dot3(a, b):
    ah, al = _split(a)
    bh, bl = _split(b)
    return (jnp.dot(ah, bh, preferred_element_type=F32)
            + (jnp.dot(al, bh, preferred_element_type=F32) + jnp.dot(ah, bl, preferred_element_type=F32)))


def _dot3_nt(a, b):
    dn = (((1,), (1,)), ((), ()))
    ah, al = _split(a)
    bh, bl = _split(b)
    d = lambda x, y: lax.dot_general(x, y, dn, preferred_element_type=F32)
    return d(ah, bh) + (d(al, bh) + d(ah, bl))


def _softplus(x):
    return jnp.maximum(x, 0.0) + jnp.log1p(jnp.exp(-jnp.abs(x)))


def _sigmoid(x):
    return 1.0 / (1.0 + jnp.exp(-x))


def _rms(x, g):
    ms = jnp.mean(x * x, axis=-1, keepdims=True)
    return x * lax.rsqrt(ms + RMS_EPS) * g


def _ada_kernel(c_ref, w_ref, b_ref, o_ref):
    c = c_ref[...]
    s = c * _sigmoid(c)
    o_ref[...] = _dot3(s, w_ref[...]) + b_ref[...]


def _ada(c, w_ada, b_ada):
    rows = c.shape[0]
    ncol = w_ada.shape[1]
    tn = D_MODEL
    return pl.pallas_call(
        _ada_kernel,
        grid=(ncol // tn,),
        in_specs=[pl.BlockSpec((rows, D_MODEL), lambda j: (0, 0)),
                  pl.BlockSpec((D_MODEL, tn), lambda j: (0, j)),
                  pl.BlockSpec((1, tn), lambda j: (0, j))],
        out_specs=pl.BlockSpec((rows, tn), lambda j: (0, j)),
        out_shape=jax.ShapeDtypeStruct((rows, ncol), F32),
        compiler_params=_cparams(("arbitrary",)),
        name="ada_mod",
    )(c, w_ada, b_ada.reshape(1, ncol))


def _inproj_kernel(x_ref, shift_ref, scale_ref, g_ref, wa_ref, wb_ref, pa_ref, pb_ref, *, precise):
    x = x_ref[0]
    h = _rms(x, g_ref[...]) * (1.0 + scale_ref[0]) + shift_ref[0]
    dot = _dot3 if precise else _dot
    pa_ref[0] = dot(h, wa_ref[...])
    pb_ref[0] = dot(h, wb_ref[...])


def _inproj(x, shift, scale, mod_map, g, wa, wb, tm, precise):
    nb, t, _ = x.shape
    mod_block = (1,) + shift.shape[1:]
    return pl.pallas_call(
        functools.partial(_inproj_kernel, precise=precise),
        grid=(nb, t // tm),
        in_specs=[pl.BlockSpec((1, tm, D_MODEL), lambda b, i: (b, i, 0)),
                  pl.BlockSpec(mod_block, functools.partial(mod_map, 0)),
                  pl.BlockSpec(mod_block, functools.partial(mod_map, 1)),
                  pl.BlockSpec((1, D_MODEL), lambda b, i: (0, 0)),
                  pl.BlockSpec((D_MODEL, N_COLS_A), lambda b, i: (0, 0)),
                  pl.BlockSpec((D_MODEL, N_COLS_B), lambda b, i: (0, 0))],
        out_specs=[pl.BlockSpec((1, tm, N_COLS_A), lambda b, i: (b, i, 0)),
                   pl.BlockSpec((1, tm, N_COLS_B), lambda b, i: (b, i, 0))],
        out_shape=[jax.ShapeDtypeStruct((nb, t, N_COLS_A), F32),
                   jax.ShapeDtypeStruct((nb, t, N_COLS_B), F32)],
        compiler_params=_cparams(("arbitrary", "arbitrary")),
        name="norm1_inproj",
    )(x, shift, scale, g.reshape(1, D_MODEL), wa, wb)


def _prep_kernel(pa_ref, prev_ref, mu_ref, w0_ref, wup_ref, a0_ref, gup_ref, kk_ref, ka_ref,
                 r_out, w_out, k_out, v_out, kk_out, b_out, g_out, carry_ref, *, seq):
    pa = pa_ref[0]
    if seq:
        @pl.when(pl.program_id(1) == 0)
        def _():
            carry_ref[...] = jnp.zeros_like(carry_ref)
        rolled = pltpu.roll(pa, 1, axis=0)
        row = lax.broadcasted_iota(jnp.int32, pa.shape, 0)
        prev = jnp.where(row == 0, carry_ref[...], rolled)
        carry_ref[...] = pa[pa.shape[0] - 1:, :]
    else:
        prev = prev_ref[0]
    z = pa + (prev - pa) * mu_ref[...]
    r = z[:, 0:D_A]
    k = z[:, D_A:2 * D_A]
    v = z[:, 2 * D_A:3 * D_A]
    lo = 3 * D_A
    za = z[:, lo:lo + R_DECAY + R_AAA]
    lane = lax.broadcasted_iota(jnp.int32, za.shape, 1)
    za = jnp.where(lane < R_DECAY, jnp.tanh(za), za)
    lw = _dot3(za, wup_ref[...])
    w_log = -_softplus(-(w0_ref[...] + lw[:, :D_A])) - 0.5
    decay = jnp.exp(-jnp.exp(w_log))
    a = _sigmoid(a0_ref[...] + lw[:, D_A:])
    gd = z[:, lo + R_DECAY + R_AAA:]
    g = _dot3(_sigmoid(gd), gup_ref[...])
    kk = k * kk_ref[...]
    r_out[0] = r
    w_out[0] = decay
    k_out[0] = k * (1.0 + (a - 1.0) * ka_ref[...])
    v_out[0] = v
    kk_out[0] = kk
    b_out[0] = kk * a
    g_out[0] = g


def _prep(pa, prev, p, tm, seq):
    nb, t, _ = pa.shape
    wup = jnp.zeros((R_DECAY + R_AAA, 2 * D_A), F32)
    wup = wup.at[:R_DECAY, :D_A].set(p['rk_w_up']).at[R_DECAY:, D_A:].set(p['rk_a_up'])
    vec = lambda a: a.reshape(1, -1)
    row_spec = lambda n: pl.BlockSpec((1, n), lambda b, i: (0, 0))
    tile = pl.BlockSpec((1, tm, D_A), lambda b, i: (b, i, 0))
    if prev is None:
        prev = jnp.zeros((1, SUBLANES, N_COLS_A), F32)
        prev_spec = pl.BlockSpec((1, SUBLANES, N_COLS_A), lambda b, i: (0, 0, 0))
    else:
        prev_spec = pl.BlockSpec((1, tm, N_COLS_A), lambda b, i: (b, i, 0))
    return pl.pallas_call(
        functools.partial(_prep_kernel, seq=seq),
        grid=(nb, t // tm),
        in_specs=[pl.BlockSpec((1, tm, N_COLS_A), lambda b, i: (b, i, 0)),
                  prev_spec,
                  row_spec(N_COLS_A), row_spec(D_A),
                  pl.BlockSpec((R_DECAY + R_AAA, 2 * D_A), lambda b, i: (0, 0)),
                  row_spec(D_A),
                  pl.BlockSpec((R_GATE, D_A), lambda b, i: (0, 0)),
                  row_spec(D_A), row_spec(D_A)],
        out_specs=[tile] * 7,
        out_shape=[jax.ShapeDtypeStruct((nb, t, D_A), F32)] * 7,
        scratch_shapes=[pltpu.VMEM((1, N_COLS_A), F32)],
        compiler_params=_cparams(("arbitrary", "arbitrary")),
        name="rwkv_prep",
    )(pa, prev, vec(p['rk_mu']), vec(p['rk_w0']), wup, vec(p['rk_a0']), p['rk_g_up'],
      vec(p['rk_k_k']), vec(p['rk_k_a']))


def _scan_kernel(w_ref, kk_ref, kknext_ref, b_ref, k_ref, r_ref, v_ref, s0_ref, lnw_ref, lnb_ref, rk_ref,
                 out_ref, sfin_ref, s_ref, sa_ref, inv_ref, wd_ref, kkd_ref, bd_ref, kd_ref, rd_ref, *, tt):
    ti = pl.program_id(1)

    def both_halves(x):
        return jnp.concatenate([x, x], axis=-1)

    wd_ref[...] = both_halves(w_ref[0])
    bd_ref[...] = both_halves(b_ref[0])
    kd_ref[...] = both_halves(k_ref[0])
    rd_ref[...] = both_halves(r_ref[0])
    kkd_ref[0:tt] = both_halves(kk_ref[0])
    kkd_ref[tt] = both_halves(kknext_ref[0, 0])

    def inv_norm2(kk_rows):
        s2 = jnp.sum(kk_rows * kk_rows, axis=0, keepdims=True)
        return 1.0 / jnp.maximum(s2, 1e-24)

    @pl.when(ti == 0)
    def _():
        s_ref[...] = s0_ref[0]
        acc = jnp.zeros((HALF_ROWS, LANES), F32)
        for j in range(HEAD_DIM):
            acc = acc + s0_ref[0, j] * kkd_ref[0, j:j + 1, :]
        sa_ref[...] = acc
        inv_ref[...] = inv_norm2(kkd_ref[0])

    def finish(y, cv):
        tot = jnp.sum(y, axis=0, keepdims=True)
        tot = tot + pltpu.roll(tot, LANES // 2, axis=1)
        d = y - tot * (1.0 / HEAD_DIM)
        sq = jnp.sum(d * d, axis=0, keepdims=True)
        sq = sq + pltpu.roll(sq, LANES // 2, axis=1)
        yn = d * lax.rsqrt(sq * (1.0 / HEAD_DIM) + LN_X_EPS)
        return yn * lnw_ref[...] + lnb_ref[...] + cv

    def step(t, carry):
        sa, inv2, y_prev, cv_prev = carry
        out_ref[0, jnp.maximum(t - 1, 0)] = finish(y_prev, cv_prev)
        sae = sa * (-inv2)
        v = v_ref[0, t]
        acc_y = jnp.zeros((HALF_ROWS, LANES), F32)
        acc_s = jnp.zeros((HALF_ROWS, LANES), F32)
        for j in range(HEAD_DIM):
            row = pl.ds(j, 1)
            s_new = s_ref[j] * wd_ref[t, row, :] + sae * bd_ref[t, row, :] + v * kd_ref[t, row, :]
            s_ref[j] = s_new
            acc_y = acc_y + s_new * rd_ref[t, row, :]
            acc_s = acc_s + s_new * kkd_ref[t + 1, row, :]
        c = jnp.sum(rd_ref[t] * kd_ref[t] * rk_ref[...], axis=0, keepdims=True)
        return acc_s, inv_norm2(kkd_ref[t + 1]), acc_y, c * v

    zeros = jnp.zeros((HALF_ROWS, LANES), F32)
    sa, inv2, y_last, cv_last = lax.fori_loop(0, tt, step, (sa_ref[...], inv_ref[...], zeros, zeros))
    out_ref[0, tt - 1] = finish(y_last, cv_last)
    sa_ref[...] = sa
    inv_ref[...] = inv2

    @pl.when(ti == pl.num_programs(1) - 1)
    def _():
        sfin_ref[0] = s_ref[...]


def _to_scan_keys(x):
    nb, t, _ = x.shape
    g = nb // GROUP_BATCH
    y = x.reshape(g, GROUP_BATCH, t, N_HEADS, HEAD_DIM).transpose(0, 2, 4, 1, 3)
    return y.reshape(g, t, HEAD_DIM, GROUP_BATCH * N_HEADS)


def _to_scan_rows(x):
    nb, t, _ = x.shape
    g = nb // GROUP_BATCH
    y = x.reshape(g, GROUP_BATCH, t, N_HEADS, 2, HALF_ROWS).transpose(0, 2, 5, 4, 1, 3)
    return y.reshape(g, t, HALF_ROWS, LANES)


def _from_scan_rows(y):
    g, t = y.shape[:2]
    x = y.reshape(g, t, HALF_ROWS, 2, GROUP_BATCH, N_HEADS).transpose(0, 4, 1, 5, 3, 2)
    return x.reshape(g * GROUP_BATCH, t, D_A)


def _state_to_scan(s):
    g = s.shape[0] // GROUP_BATCH
    y = s.reshape(g, GROUP_BATCH, N_HEADS, 2, HALF_ROWS, HEAD_DIM).transpose(0, 5, 4, 3, 1, 2)
    return y.reshape(g, HEAD_DIM, HALF_ROWS, LANES)


def _state_from_scan(y):
    g = y.shape[0]
    s = y.reshape(g, HEAD_DIM, HALF_ROWS, 2, GROUP_BATCH, N_HEADS).transpose(0, 4, 5, 3, 2, 1)
    return s.reshape(g * GROUP_BATCH, N_HEADS, HEAD_DIM, HEAD_DIM)


def _head_rows(x):
    y = x.reshape(N_HEADS, 2, HALF_ROWS).transpose(2, 1, 0)
    y = jnp.broadcast_to(y[:, :, None, :], (HALF_ROWS, 2, GROUP_BATCH, N_HEADS))
    return y.reshape(HALF_ROWS, LANES)


def _head_keys(x):
    y = jnp.broadcast_to(x.T[:, None, None, :], (HEAD_DIM, 2, GROUP_BATCH, N_HEADS))
    return y.reshape(HEAD_DIM, LANES)


def _wkv_scan(r, w, k, v, kk, b, s0, p, tt):
    t = r.shape[1]
    wk, bk, kkey, rkey, kks = (_to_scan_keys(a) for a in (w, b, k, r, kk))
    vr = _to_scan_rows(v)
    g = wk.shape[0]
    half = LANES // 2
    key_tile = pl.BlockSpec((1, tt, HEAD_DIM, half), lambda gi, i: (gi, i, 0, 0))
    next_row = pl.BlockSpec((1, 1, HEAD_DIM, half), lambda gi, i: (gi, jnp.minimum((i + 1) * tt, t - 1), 0, 0))
    row_tile = pl.BlockSpec((1, tt, HALF_ROWS, LANES), lambda gi, i: (gi, i, 0, 0))
    state = pl.BlockSpec((1, HEAD_DIM, HALF_ROWS, LANES), lambda gi, i: (gi, 0, 0, 0))
    const = lambda n: pl.BlockSpec((n, LANES), lambda gi, i: (0, 0))
    dup = pltpu.VMEM((tt, HEAD_DIM, LANES), F32)
    out, sfin = pl.pallas_call(
        functools.partial(_scan_kernel, tt=tt),
        grid=(g, t // tt),
        in_specs=[key_tile, key_tile, next_row, key_tile, key_tile, key_tile, row_tile,
                  state, const(HALF_ROWS), const(HALF_ROWS), const(HEAD_DIM)],
        out_specs=[row_tile, state],
        out_shape=[jax.ShapeDtypeStruct((g, t, HALF_ROWS, LANES), F32),
                   jax.ShapeDtypeStruct((g, HEAD_DIM, HALF_ROWS, LANES), F32)],
        scratch_shapes=[pltpu.VMEM((HEAD_DIM, HALF_ROWS, LANES), F32),
                        pltpu.VMEM((HALF_ROWS, LANES), F32),
                        pltpu.VMEM((1, LANES), F32),
                        dup, pltpu.VMEM((tt + 1, HEAD_DIM, LANES), F32), dup, dup, dup],
        compiler_params=_cparams(("arbitrary", "arbitrary")),
        name="wkv_scan",
    )(wk, kks, kks, bk, kkey, rkey, vr, _state_to_scan(s0),
      _head_rows(p['rk_lnx_w']), _head_rows(p['rk_lnx_b']), _head_keys(p['rk_r_k']))
    return _from_scan_rows(out), _state_from_scan(sfin)


def _gelu(x):
    return 0.5 * x * (1.0 + jnp.tanh(0.7978845608028654 * (x + 0.044715 * (x * x * x))))


def _lru_gates(xc, wri_ref, bri_ref, nsl_ref, precise):
    dot = _dot3 if precise else _dot
    gates = _sigmoid(dot(xc, wri_ref[...]) + bri_ref[...])
    gate_r = gates[:, :D_B]
    gate_i = gates[:, D_B:]
    log_a = gate_r * nsl_ref[...]
    a = jnp.exp(log_a)
    th = jnp.tanh(log_a)
    one_minus_a2 = -2.0 * th / (1.0 - th)
    bt = jnp.sqrt(one_minus_a2) * (gate_i * xc)
    return a, bt


def _lru_seq_kernel(pb_ref, cw_ref, cb_ref, wri_ref, bri_ref, nsl_ref, ng_ref,
                    out_ref, hlast_ref, xprev_ref, h_ref):
    tm = pb_ref.shape[1]

    @pl.when(pl.program_id(1) == 0)
    def _():
        xprev_ref[...] = jnp.zeros_like(xprev_ref)
        h_ref[...] = jnp.zeros_like(h_ref)

    pb = pb_ref[0]
    yb = pb[:, :D_B]
    xb = pb[:, D_B:]
    xprev = xprev_ref[...]
    row8 = lax.broadcasted_iota(jnp.int32, (SUBLANES, D_B), 0)

    def shifted(d):
        rolled = pltpu.roll(xb, d, axis=0)
        top = jnp.where(row8 < d, pltpu.roll(xprev, d, axis=0), rolled[:SUBLANES])
        return jnp.concatenate([top, rolled[SUBLANES:]], axis=0)

    xc = cb_ref[...] + cw_ref[3:4, :] * xb
    for d in range(1, CONV_W):
        xc = xc + cw_ref[3 - d:4 - d, :] * shifted(d)
    xprev_ref[...] = xb[tm - SUBLANES:, :]

    a, x = _lru_gates(xc, wri_ref, bri_ref, nsl_ref, False)
    row = lax.broadcasted_iota(jnp.int32, (tm, D_B), 0)
    d = 1
    while d < tm:
        keep = row >= d
        a_s = jnp.where(keep, pltpu.roll(a, d, axis=0), 1.0)
        x_s = jnp.where(keep, pltpu.roll(x, d, axis=0), 0.0)
        x = a * x_s + x
        a = a * a_s
        d *= 2
    h = a * h_ref[...] + x
    h_ref[...] = h[tm - 1:, :]
    hlast_ref[0] = h[tm - 1:, :]
    out_ref[0] = _rms(h * _gelu(yb), ng_ref[...]).astype(out_ref.dtype)


def _lru_params(p):
    eye = jnp.eye(N_BLOCKS_B, dtype=F32)
    blk = D_B // N_BLOCKS_B
    bd = lambda w: (eye[:, None, :, None] * w[:, :, None, :]).reshape(D_B, D_B)
    wri = jnp.concatenate([bd(p['lru_w_r']), bd(p['lru_w_i'])], axis=1)
    bri = jnp.concatenate([p['lru_b_r'], p['lru_b_i']]).reshape(1, 2 * D_B)
    nsl = (-LRU_C * jax.nn.softplus(-p['lru_lambda'])).reshape(1, D_B)
    del blk
    return wri, bri, nsl


def _lru_seq(pb, p, tm):
    nb, t, _ = pb.shape
    wri, bri, nsl = _lru_params(p)
    row_spec = lambda n: pl.BlockSpec((1, n), lambda b, i: (0, 0))
    return pl.pallas_call(
        _lru_seq_kernel,
        grid=(nb, t // tm),
        in_specs=[pl.BlockSpec((1, tm, N_COLS_B), lambda b, i: (b, i, 0)),
                  pl.BlockSpec((CONV_W, D_B), lambda b, i: (0, 0)),
                  row_spec(D_B),
                  pl.BlockSpec((D_B, 2 * D_B), lambda b, i: (0, 0)),
                  row_spec(2 * D_B), row_spec(D_B), row_spec(D_B)],
        out_specs=[pl.BlockSpec((1, tm, D_B), lambda b, i: (b, i, 0)),
                   pl.BlockSpec((1, 1, D_B), lambda b, i: (b, 0, 0))],
        out_shape=[jax.ShapeDtypeStruct((nb, t, D_B), BF16),
                   jax.ShapeDtypeStruct((nb, 1, D_B), F32)],
        scratch_shapes=[pltpu.VMEM((SUBLANES, D_B), F32), pltpu.VMEM((1, D_B), F32)],
        compiler_params=_cparams(("arbitrary", "arbitrary")),
        name="rglru_seq",
    )(pb, p['lru_conv_w'], p['lru_conv_b'].reshape(1, D_B), wri.astype(BF16), bri, nsl,
      p['lru_norm_g'].reshape(1, D_B))


def _lru_step_kernel(pb_ref, conv_ref, h0_ref, cw_ref, cb_ref, wri_ref, bri_ref, nsl_ref, ng_ref,
                     out_ref, hnew_ref):
    pb = pb_ref[...]
    yb = pb[:, :D_B]
    xb = pb[:, D_B:]
    xc = cb_ref[...] + cw_ref[3:4, :] * xb
    for j in range(CONV_W - 1):
        xc = xc + cw_ref[j:j + 1, :] * conv_ref[j]
    a, x = _lru_gates(xc, wri_ref, bri_ref, nsl_ref, True)
    h = a * h0_ref[...] + x
    hnew_ref[...] = h
    out_ref[...] = _rms(h * _gelu(yb), ng_ref[...]).astype(out_ref.dtype)


def _lru_step(pb, conv0, h0, p):
    n = pb.shape[0]
    wri, bri, nsl = _lru_params(p)
    return pl.pallas_call(
        _lru_step_kernel,
        out_shape=[jax.ShapeDtypeStruct((n, D_B), BF16), jax.ShapeDtypeStruct((n, D_B), F32)],
        compiler_params=pltpu.CompilerParams(vmem_limit_bytes=VMEM_LIMIT),
        name="rglru_step",
    )(pb, conv0, h0, p['lru_conv_w'], p['lru_conv_b'].reshape(1, D_B), wri, bri, nsl,
      p['lru_norm_g'].reshape(1, D_B))


def _to_token_tiles(ref, x):
    rows = x.shape[0]
    for c in range(TILE_CHUNKS):
        ref[pl.ds(c, rows, stride=TILE_CHUNKS), :] = x[:, c * LANES:(c + 1) * LANES]


def _from_token_tiles(ref, row0, rows):
    return jnp.concatenate(
        [ref[pl.ds(row0 + c, rows, stride=TILE_CHUNKS), :] for c in range(TILE_CHUNKS)], axis=1)


def _post_kernel(x_ref, wkv_ref, g_ref, yb_ref, gate1_ref, shift2_ref, scale2_ref, n2_ref,
                 wo_ref, rw_ref, rb_ref, tri_ref, x1_ref, h2_ref, ti_ref, rk_ref, tg_ref, cnt_ref, run_ref):
    @pl.when((pl.program_id(0) == 0) & (pl.program_id(1) == 0))
    def _():
        run_ref[...] = jnp.zeros_like(run_ref)

    ya = (wkv_ref[0] * g_ref[0]).astype(BF16)
    mixed = (jnp.dot(ya, wo_ref[:D_A, :], preferred_element_type=F32)
             + jnp.dot(yb_ref[0], wo_ref[D_A:, :], preferred_element_type=F32))
    x1 = x_ref[0] + gate1_ref[0] * mixed
    x1_ref[0] = x1
    h2 = _rms(x1, n2_ref[...]) * (1.0 + scale2_ref[0]) + shift2_ref[0]
    _to_token_tiles(h2_ref, h2)
    logits = _dot3_nt(rw_ref[...], h2) + rb_ref[...]
    eidx = lax.broadcasted_iota(jnp.int32, logits.shape, 0)
    vals, idxs = [], []
    cur = logits
    for _ in range(TOP_K):
        m = jnp.max(cur, axis=0, keepdims=True)
        i = jnp.min(jnp.where(cur == m, eidx, N_EXPERTS), axis=0, keepdims=True)
        vals.append(m)
        idxs.append(i)
        cur = jnp.where(eidx == i, -jnp.inf, cur)
    ex = [jnp.exp(v - vals[0]) for v in vals]
    den = ex[0] + ex[1] + ex[2] + ex[3]
    sel = [eidx == i for i in idxs]
    onehot = (sel[0] | sel[1] | sel[2] | sel[3]).astype(F32)
    incl = jnp.dot(onehot.astype(BF16), tri_ref[...], preferred_element_type=F32)
    rank = run_ref[...] + (incl - onehot)
    run_ref[...] = run_ref[...] + jnp.sum(onehot, axis=1, keepdims=True)
    cnt_ref[...] = jnp.broadcast_to(run_ref[...], cnt_ref.shape)
    for k in range(TOP_K):
        ti_ref[0, k:k + 1, :] = idxs[k]
        rk_ref[0, k:k + 1, :] = jnp.sum(jnp.where(sel[k], rank, 0.0), axis=0, keepdims=True).astype(jnp.int32)
        tg_ref[0, k:k + 1, :] = ex[k] / den


def _post(x, wkv, g, yb, gate1, shift2, scale2, mod_map, p, tm):
    nb, t, _ = x.shape
    mod_block = (1,) + gate1.shape[1:]
    tile = lambda n: pl.BlockSpec((1, tm, n), lambda b, i: (b, i, 0))
    full = lambda a: pl.BlockSpec(a.shape, lambda b, i: (0,) * a.ndim)
    mspec = lambda j: pl.BlockSpec(mod_block, functools.partial(mod_map, j))
    n2 = p['norm2_g'].reshape(1, D_MODEL)
    wo = p['w_out'].astype(BF16)
    rw = p['router_w'].T
    rb = p['router_b'].reshape(N_EXPERTS, 1)
    tri = jnp.triu(jnp.ones((tm, tm), BF16))
    nt = t // tm
    n = nb * t
    topk = pl.BlockSpec((1, TOP_K, tm), lambda b, i: (b * nt + i, 0, 0))
    topk_i = jax.ShapeDtypeStruct((nb * nt, TOP_K, tm), jnp.int32)
    x1, h2, ti, rk, tg, cnt = pl.pallas_call(
        _post_kernel,
        grid=(nb, nt),
        in_specs=[tile(D_MODEL), tile(D_A), tile(D_A), tile(D_B), mspec(2), mspec(3), mspec(4),
                  full(n2), full(wo), full(rw), full(rb), full(tri)],
        out_specs=[tile(D_MODEL),
                   pl.BlockSpec((tm * TILE_CHUNKS, LANES), lambda b, i: (b * nt + i, 0)),
                   topk, topk, topk,
                   pl.BlockSpec((N_EXPERTS, LANES), lambda b, i: (0, 0))],
        out_shape=[jax.ShapeDtypeStruct((nb, t, D_MODEL), F32),
                   jax.ShapeDtypeStruct((n * TILE_CHUNKS, LANES), F32),
                   topk_i, topk_i,
                   jax.ShapeDtypeStruct((nb * nt, TOP_K, tm), F32),
                   jax.ShapeDtypeStruct((N_EXPERTS, LANES), F32)],
        scratch_shapes=[pltpu.VMEM((N_EXPERTS, 1), F32)],
        compiler_params=_cparams(("arbitrary", "arbitrary")),
        name="outproj_router",
    )(x, wkv, g, yb, gate1, shift2, scale2, n2, wo, rw, rb, tri)
    return x1, h2, ti, rk, tg, cnt[:, 0].astype(jnp.int32)


def _moe_plan(top_i, n):
    onehot = jnp.any(top_i[:, :, None] == jnp.arange(N_EXPERTS, dtype=jnp.int32), axis=0).astype(jnp.int32)
    csum = jnp.cumsum(onehot, axis=0)
    counts = csum[-1]
    rank = csum - onehot
    padded = (counts + MOE_ROWS - 1) // MOE_ROWS * MOE_ROWS
    pad_end = jnp.cumsum(padded)
    pad_start = pad_end - padded
    dest = pad_start[top_i] + jnp.take_along_axis(rank, top_i.T, axis=1).T
    n_blocks = -(-n * TOP_K // MOE_ROWS) + N_EXPERTS
    block_row0 = jnp.arange(n_blocks, dtype=jnp.int32) * MOE_ROWS
    block_e = jnp.minimum(jnp.sum(pad_end[None, :] <= block_row0[:, None], axis=1), N_EXPERTS - 1).astype(jnp.int32)
    n_used = (pad_end[-1] // MOE_ROWS).astype(jnp.int32).reshape(1)
    return dest.astype(jnp.int32), block_e, n_used, pad_end.astype(jnp.int32), padded.astype(jnp.int32), n_blocks


def _row_copy(src, src_row, dst, dst_row, sem):
    return pltpu.make_async_copy(src.at[pl.ds(src_row, 1), :], dst.at[pl.ds(dst_row, 1), :], sem)


def _dispatch_kernel(pad_end_ref, padded_ref, h2_ref, dest_hbm, xs_hbm, idx_ref, zero_ref, sem, isem):
    i = pl.program_id(0)
    tm = h2_ref.shape[0]
    idx_copy = pltpu.make_async_copy(dest_hbm.at[i], idx_ref, isem)
    idx_copy.start()

    @pl.when(i == 0)
    def _():
        zero_ref[...] = jnp.zeros_like(zero_ref)

        def fill(e, n_started):
            has = padded_ref[e] > 0
            start = pl.multiple_of(jnp.maximum(pad_end_ref[e] - MOE_ROWS, 0), MOE_ROWS)

            @pl.when(has)
            def _():
                pltpu.make_async_copy(zero_ref, xs_hbm.at[pl.ds(start, MOE_ROWS), :], sem).start()
            return n_started + has.astype(jnp.int32)

        n_started = lax.fori_loop(0, N_EXPERTS, fill, jnp.int32(0))

        n_used = pad_end_ref[N_EXPERTS - 1] // MOE_ROWS
        n_blocks = xs_hbm.shape[0] // MOE_ROWS

        def fill_tail(blk, c):
            start = pl.multiple_of(blk * MOE_ROWS, MOE_ROWS)
            pltpu.make_async_copy(zero_ref, xs_hbm.at[pl.ds(start, MOE_ROWS), :], sem).start()
            return c
        lax.fori_loop(n_used, n_blocks, fill_tail, 0)
        n_started = n_started + (n_blocks - n_used)

        def drain(e, c):
            pltpu.make_async_copy(zero_ref, xs_hbm.at[pl.ds(0, MOE_ROWS), :], sem).wait()
            return c
        lax.fori_loop(0, n_started, drain, 0)

    idx_copy.wait()

    def issue(r, c):
        for k in range(TOP_K):
            _row_copy(h2_ref, r, xs_hbm, idx_ref[k * tm + r], sem).start()
        return c
    lax.fori_loop(0, tm, issue, 0)

    def drain_rows(r, c):
        _row_copy(h2_ref, 0, xs_hbm, 0, sem).wait()
        return c
    lax.fori_loop(0, tm * TOP_K, drain_rows, 0)


def _dispatch(h2, dest, pad_end, padded, n_rows, tm):
    n = h2.shape[0]
    nt = n // tm
    dest_t = dest.reshape(TOP_K, nt, tm).transpose(1, 0, 2).reshape(nt, TOP_K * tm)
    grid_spec = pltpu.PrefetchScalarGridSpec(
        num_scalar_prefetch=2,
        grid=(nt,),
        in_specs=[pl.BlockSpec((tm, D_MODEL), lambda i, *_: (i, 0)),
                  pl.BlockSpec(memory_space=pl.ANY)],
        out_specs=pl.BlockSpec(memory_space=pl.ANY),
        scratch_shapes=[pltpu.SMEM((TOP_K * tm,), jnp.int32),
                        pltpu.VMEM((MOE_ROWS, D_MODEL), F32),
                        pltpu.SemaphoreType.DMA, pltpu.SemaphoreType.DMA],
    )
    return pl.pallas_call(
        _dispatch_kernel,
        grid_spec=grid_spec,
        out_shape=jax.ShapeDtypeStruct((n_rows, D_MODEL), F32),
        compiler_params=_cparams(("arbitrary",)),
        name="moe_dispatch",
    )(pad_end, padded, h2, dest_t)


def _experts_kernel(be_ref, nu_ref, xs_ref, wg_ref, bg_ref, wu_ref, bu_ref, wd_ref, bd_ref, out_ref):
    @pl.when(pl.program_id(0) < nu_ref[0])
    def _():
        x = xs_ref[...].astype(BF16)
        gt = jnp.dot(x, wg_ref[0], preferred_element_type=F32) + bg_ref[0]
        up = jnp.dot(x, wu_ref[0], preferred_element_type=F32) + bu_ref[0]
        gt = jnp.minimum(gt, SWIGLU_LIMIT)
        up = jnp.clip(up, -SWIGLU_LIMIT, SWIGLU_LIMIT)
        glu = gt * _sigmoid(gt * SWIGLU_ALPHA)
        mid = ((up + 1.0) * glu).astype(BF16)
        out_ref[...] = jnp.dot(mid, wd_ref[0], preferred_element_type=F32) + bd_ref[0]

    @pl.when(pl.program_id(0) >= nu_ref[0])
    def _():
        out_ref[...] = jnp.zeros_like(out_ref)


def _experts(xs, block_e, n_used, n_blocks, wts):
    wg, bg, wu, bu, wd, bd = wts
    row_map = lambda i, be, nu: (i, 0)
    w_map = lambda i, be, nu: (be[i], 0, 0)
    wspec = pl.BlockSpec((1, D_MODEL, D_FF), w_map)
    bspec = pl.BlockSpec((1, 1, D_FF), w_map)
    grid_spec = pltpu.PrefetchScalarGridSpec(
        num_scalar_prefetch=2,
        grid=(n_blocks,),
        in_specs=[pl.BlockSpec((MOE_ROWS, D_MODEL), row_map), wspec, bspec, wspec, bspec, wspec, bspec],
        out_specs=pl.BlockSpec((MOE_ROWS, D_MODEL), row_map),
    )
    return pl.pallas_call(
        _experts_kernel,
        grid_spec=grid_spec,
        out_shape=jax.ShapeDtypeStruct(xs.shape, F32),
        compiler_params=_cparams(("arbitrary",)),
        name="moe_experts",
    )(block_e, n_used, xs, wg, bg, wu, bu, wd, bd)


def _combine_kernel(x1_ref, tg_ref, gate2_ref, fg_ref, dest_hbm, rows_hbm, y_ref, idx_ref, buf_ref, sem, isem):
    i = pl.program_id(1) + pl.program_id(0) * pl.num_programs(1)
    tm = x1_ref.shape[1]
    idx_copy = pltpu.make_async_copy(dest_hbm.at[i], idx_ref, isem)
    idx_copy.start()
    idx_copy.wait()

    def issue(r, c):
        for k in range(TOP_K):
            _row_copy(rows_hbm, idx_ref[k * tm + r], buf_ref.at[k], r, sem).start()
        return c
    lax.fori_loop(0, tm, issue, 0)

    def drain(r, c):
        _row_copy(rows_hbm, 0, buf_ref.at[0], 0, sem).wait()
        return c
    lax.fori_loop(0, tm * TOP_K, drain, 0)

    krow = lax.broadcasted_iota(jnp.int32, (LANES, tm), 0)
    gates = jnp.zeros((LANES, tm), F32)
    for k in range(TOP_K):
        gates = jnp.where(krow == k, tg_ref[0, k:k + 1, :], gates)
    gates = gates.T
    ff = buf_ref[0] * gates[:, 0:1]
    for k in range(1, TOP_K):
        ff = ff + buf_ref[k] * gates[:, k:k + 1]
    x2 = x1_ref[0] + gate2_ref[0] * ff
    y_ref[0] = _rms(x2, fg_ref[...])


def _combine(x1, tg, gate2, mod_map, final_g, dest, rows, nb, t, tm):
    nt = t // tm
    n = nb * t
    dest_t = dest.reshape(TOP_K, nb * nt, tm).transpose(1, 0, 2).reshape(nb * nt, TOP_K * tm)
    tg_t = tg.reshape(TOP_K, nb * nt, tm).transpose(1, 0, 2)
    mod_block = (1,) + gate2.shape[1:]
    return pl.pallas_call(
        _combine_kernel,
        grid=(nb, nt),
        in_specs=[pl.BlockSpec((1, tm, D_MODEL), lambda b, i: (b, i, 0)),
                  pl.BlockSpec((1, TOP_K, tm), lambda b, i: (b * nt + i, 0, 0)),
                  pl.BlockSpec(mod_block, functools.partial(mod_map, 5)),
                  pl.BlockSpec((1, D_MODEL), lambda b, i: (0, 0)),
                  pl.BlockSpec(memory_space=pl.ANY),
                  pl.BlockSpec(memory_space=pl.ANY)],
        out_specs=pl.BlockSpec((1, tm, D_MODEL), lambda b, i: (b, i, 0)),
        out_shape=jax.ShapeDtypeStruct((nb, t, D_MODEL), F32),
        scratch_shapes=[pltpu.SMEM((TOP_K * tm,), jnp.int32),
                        pltpu.VMEM((TOP_K, tm, D_MODEL), F32),
                        pltpu.SemaphoreType.DMA, pltpu.SemaphoreType.DMA],
        compiler_params=_cparams(("arbitrary", "arbitrary")),
        name="moe_combine",
    )(x1.reshape(nb, t, D_MODEL), tg_t, gate2, final_g.reshape(1, D_MODEL), dest_t, rows)


def _moe(x1, h2, ti, tg, gate2, mod_map, final_g, wts, nb, t, tm):
    n = nb * t
    dest, block_e, n_used, pad_end, padded, n_blocks = _moe_plan(ti, n)
    xs = _dispatch(h2, dest, pad_end, padded, n_blocks * MOE_ROWS, tm)
    rows = _experts(xs, block_e, n_used, n_blocks, wts)
    return _combine(x1, tg, gate2, mod_map, final_g, dest, rows, nb, t, tm)


def _plan(ti, rk, counts, n):
    padded = (counts + MOE_ROWS - 1) // MOE_ROWS * MOE_ROWS
    pad_end = jnp.cumsum(padded)
    pad_start = pad_end - padded
    experts = jnp.arange(N_EXPERTS, dtype=jnp.int32)
    dest = rk + jnp.sum(jnp.where(ti[..., None] == experts, pad_start, 0), axis=-1)
    n_blocks = -(-n * TOP_K // MOE_ROWS) + N_EXPERTS
    block_row0 = jnp.arange(n_blocks, dtype=jnp.int32) * MOE_ROWS
    block_e = jnp.minimum(jnp.sum(pad_end[None, :] <= block_row0[:, None], axis=1), N_EXPERTS - 1)
    n_used = (pad_end[-1] // MOE_ROWS).reshape(1)
    as_i32 = lambda a: a.astype(jnp.int32)
    return as_i32(dest), as_i32(block_e), as_i32(n_used), as_i32(pad_end), as_i32(padded), n_blocks


def _tile_copy(src, src_row, dst, dst_row, sem):
    s0 = pl.multiple_of(src_row * TILE_CHUNKS, TILE_CHUNKS)
    d0 = pl.multiple_of(dst_row * TILE_CHUNKS, TILE_CHUNKS)
    return pltpu.make_async_copy(src.at[pl.ds(s0, TILE_CHUNKS), :], dst.at[pl.ds(d0, TILE_CHUNKS), :], sem)


def _block_copy(src, dst, dst_block, sem):
    rows = MOE_ROWS * TILE_CHUNKS
    d0 = pl.multiple_of(dst_block * rows, rows)
    return pltpu.make_async_copy(src, dst.at[pl.ds(d0, rows), :], sem)


def _scatter_kernel(pad_end_ref, padded_ref, h2_ref, dest_hbm, xs_hbm, idx_ref, zero_ref, sem, isem):
    i = pl.program_id(0)
    tm = h2_ref.shape[0] // TILE_CHUNKS
    idx_copy = pltpu.make_async_copy(dest_hbm.at[i], idx_ref, isem)
    idx_copy.start()

    @pl.when(i == 0)
    def _():
        zero_ref[...] = jnp.zeros_like(zero_ref)
        n_used = pad_end_ref[N_EXPERTS - 1] // MOE_ROWS
        n_blocks = xs_hbm.shape[0] // (MOE_ROWS * TILE_CHUNKS)

        def fill(e, n_started):
            has = padded_ref[e] > 0

            @pl.when(has)
            def _():
                _block_copy(zero_ref, xs_hbm, pad_end_ref[e] // MOE_ROWS - 1, sem).start()
            return n_started + has.astype(jnp.int32)
        n_started = lax.fori_loop(0, N_EXPERTS, fill, jnp.int32(0))

        def fill_tail(blk, c):
            _block_copy(zero_ref, xs_hbm, blk, sem).start()
            return c
        lax.fori_loop(n_used, n_blocks, fill_tail, 0)

        def drain(e, c):
            _block_copy(zero_ref, xs_hbm, 0, sem).wait()
            return c
        lax.fori_loop(0, n_started + (n_blocks - n_used), drain, 0)

    idx_copy.wait()

    def issue(r, c):
        for k in range(TOP_K):
            _tile_copy(h2_ref, r, xs_hbm, idx_ref[k * tm + r], sem).start()
        return c
    lax.fori_loop(0, tm, issue, 0, unroll=8)
    for k in range(TOP_K):
        pltpu.make_async_copy(h2_ref, xs_hbm.at[pl.ds(0, tm * TILE_CHUNKS), :], sem).wait()


def _scatter(h2, dest, pad_end, padded, n_rows, tm):
    nt = dest.shape[0]
    grid_spec = pltpu.PrefetchScalarGridSpec(
        num_scalar_prefetch=2,
        grid=(nt,),
        in_specs=[pl.BlockSpec((tm * TILE_CHUNKS, LANES), lambda i, *_: (i, 0)),
                  pl.BlockSpec(memory_space=pl.ANY)],
        out_specs=pl.BlockSpec(memory_space=pl.ANY),
        scratch_shapes=[pltpu.SMEM((TOP_K * tm,), jnp.int32),
                        pltpu.VMEM((MOE_ROWS * TILE_CHUNKS, LANES), F32),
                        pltpu.SemaphoreType.DMA, pltpu.SemaphoreType.DMA],
    )
    return pl.pallas_call(
        _scatter_kernel,
        grid_spec=grid_spec,
        out_shape=jax.ShapeDtypeStruct((n_rows * TILE_CHUNKS, LANES), F32),
        compiler_params=_cparams(("arbitrary",)),
        name="moe_dispatch",
    )(pad_end, padded, h2, dest)


def _mlp_kernel(be_ref, nu_ref, xs_ref, wg_ref, bg_ref, wu_ref, bu_ref, wd_ref, bd_ref, out_ref,
                wg16_ref, wu16_ref, wd16_ref):
    i = pl.program_id(0)
    used = i < nu_ref[0]

    @pl.when(used & ((i == 0) | (be_ref[i] != be_ref[jnp.maximum(i - 1, 0)])))
    def _():
        wg16_ref[...] = wg_ref[0].astype(BF16)
        wu16_ref[...] = wu_ref[0].astype(BF16)
        wd16_ref[...] = wd_ref[0].astype(BF16)

    @pl.when(used)
    def _():
        x = _from_token_tiles(xs_ref, 0, MOE_ROWS).astype(BF16)
        gt = jnp.dot(x, wg16_ref[...], preferred_element_type=F32) + bg_ref[0]
        up = jnp.dot(x, wu16_ref[...], preferred_element_type=F32) + bu_ref[0]
        gt = jnp.minimum(gt, SWIGLU_LIMIT)
        up = jnp.clip(up, -SWIGLU_LIMIT, SWIGLU_LIMIT)
        glu = gt * _sigmoid(gt * SWIGLU_ALPHA)
        mid = ((up + 1.0) * glu).astype(BF16)
        _to_token_tiles(out_ref, jnp.dot(mid, wd16_ref[...], preferred_element_type=F32) + bd_ref[0])

    @pl.when(jnp.logical_not(used))
    def _():
        out_ref[...] = jnp.zeros_like(out_ref)


def _mlp(xs, block_e, n_used, n_blocks, wts):
    wg, bg, wu, bu, wd, bd = wts
    row_map = lambda i, be, nu: (i, 0)
    w_map = lambda i, be, nu: (be[i], 0, 0)
    wspec = pl.BlockSpec((1, D_MODEL, D_FF), w_map)
    bspec = pl.BlockSpec((1, 1, D_FF), w_map)
    rows = pl.BlockSpec((MOE_ROWS * TILE_CHUNKS, LANES), row_map)
    grid_spec = pltpu.PrefetchScalarGridSpec(
        num_scalar_prefetch=2,
        grid=(n_blocks,),
        in_specs=[rows, wspec, bspec, wspec, bspec, wspec, bspec],
        out_specs=rows,
        scratch_shapes=[pltpu.VMEM((D_MODEL, D_FF), BF16)] * 3,
    )
    return pl.pallas_call(
        _mlp_kernel,
        grid_spec=grid_spec,
        out_shape=jax.ShapeDtypeStruct(xs.shape, F32),
        compiler_params=_cparams(("arbitrary",)),
        name="moe_experts",
    )(block_e, n_used, xs, wg, bg, wu, bu, wd, bd)


def _gather_kernel(x1_ref, tg_ref, gate2_ref, fg_ref, dest_hbm, rows_hbm, y_ref, idx_ref, buf_ref, sem, isem):
    i = pl.program_id(1) + pl.program_id(0) * pl.num_programs(1)
    tm = x1_ref.shape[1]
    idx_copy = pltpu.make_async_copy(dest_hbm.at[i], idx_ref, isem)
    idx_copy.start()
    idx_copy.wait()

    def issue(r, c):
        for k in range(TOP_K):
            _tile_copy(rows_hbm, idx_ref[k * tm + r], buf_ref, k * tm + r, sem).start()
        return c
    lax.fori_loop(0, tm, issue, 0, unroll=8)
    pltpu.make_async_copy(rows_hbm.at[pl.ds(0, TOP_K * tm * TILE_CHUNKS), :], buf_ref, sem).wait()

    krow = lax.broadcasted_iota(jnp.int32, (LANES, tm), 0)
    gates = jnp.zeros((LANES, tm), F32)
    for k in range(TOP_K):
        gates = jnp.where(krow == k, tg_ref[0, k:k + 1, :], gates)
    gates = gates.T
    ff = _from_token_tiles(buf_ref, 0, tm) * gates[:, 0:1]
    for k in range(1, TOP_K):
        ff = ff + _from_token_tiles(buf_ref, k * tm * TILE_CHUNKS, tm) * gates[:, k:k + 1]
    x2 = x1_ref[0] + gate2_ref[0] * ff
    y_ref[0] = _rms(x2, fg_ref[...])


def _gather(x1, tg, gate2, mod_map, final_g, dest, rows, tm):
    nb, t, _ = x1.shape
    nt = t // tm
    mod_block = (1,) + gate2.shape[1:]
    return pl.pallas_call(
        _gather_kernel,
        grid=(nb, nt),
        in_specs=[pl.BlockSpec((1, tm, D_MODEL), lambda b, i: (b, i, 0)),
                  pl.BlockSpec((1, TOP_K, tm), lambda b, i: (b * nt + i, 0, 0)),
                  pl.BlockSpec(mod_block, functools.partial(mod_map, 5)),
                  pl.BlockSpec((1, D_MODEL), lambda b, i: (0, 0)),
                  pl.BlockSpec(memory_space=pl.ANY),
                  pl.BlockSpec(memory_space=pl.ANY)],
        out_specs=pl.BlockSpec((1, tm, D_MODEL), lambda b, i: (b, i, 0)),
        out_shape=jax.ShapeDtypeStruct((nb, t, D_MODEL), F32),
        scratch_shapes=[pltpu.SMEM((TOP_K * tm,), jnp.int32),
                        pltpu.VMEM((TOP_K * tm * TILE_CHUNKS, LANES), F32),
                        pltpu.SemaphoreType.DMA, pltpu.SemaphoreType.DMA],
        compiler_params=_cparams(("arbitrary", "arbitrary")),
        name="moe_combine",
    )(x1, tg, gate2, final_g.reshape(1, D_MODEL), dest, rows)


def _moe_tiles(x1, h2, ti, rk, tg, counts, gate2, mod_map, final_g, wts, tm):
    nb, t, _ = x1.shape
    dest, block_e, n_used, pad_end, padded, n_blocks = _plan(ti, rk, counts, nb * t)
    dest = dest.reshape(dest.shape[0], TOP_K * tm)
    xs = _scatter(h2, dest, pad_end, padded, n_blocks * MOE_ROWS, tm)
    rows = _mlp(xs, block_e, n_used, n_blocks, wts)
    return _gather(x1, tg, gate2, mod_map, final_g, dest, rows, tm)


def _forward(x_prompt, x_sample, c_prompt, c_sample, state_wkv, state_shift, state_conv, state_lru, p, final_g):
    bp, tp, _ = x_prompt.shape
    bs = x_sample.shape[0]
    tm = min(512, tp)
    tt = min(32, tp)

    mod = _ada(jnp.concatenate([c_prompt, c_sample], axis=0), p['w_ada'], p['b_ada'])
    mod_p = mod[:bp].reshape(bp * N_MOD, 1, D_MODEL)
    mod_s = mod[bp:].reshape(bs, N_MOD, D_MODEL).transpose(1, 0, 2)
    map_p = lambda j, b, i: (b * N_MOD + j, 0, 0)
    map_s = lambda j, b, i: (j, 0, 0)

    wa = p['w_in'][:, :N_COLS_A]
    wb = p['w_in'][:, N_COLS_A:]
    wts = (p['w_gate'], p['b_gate'].reshape(N_EXPERTS, 1, D_FF),
           p['w_up'], p['b_up'].reshape(N_EXPERTS, 1, D_FF),
           p['w_down'], p['b_down'].reshape(N_EXPERTS, 1, D_MODEL))

    pa, pb = _inproj(x_prompt, mod_p, mod_p, map_p, p['norm1_g'], wa.astype(BF16), wb.astype(BF16), tm, False)
    r, w, k, v, kk, b, g = _prep(pa, None, p, tm, True)
    s0 = jnp.zeros((bp, N_HEADS, HEAD_DIM, HEAD_DIM), F32)
    wkv_out, wkv_p = _wkv_scan(r, w, k, v, kk, b, s0, p, tt)
    yb, lru_p = _lru_seq(pb, p, tm)
    x1, h2, ti, rk, tg, counts = _post(x_prompt, wkv_out, g, yb, mod_p, mod_p, mod_p, map_p, p, tm)
    y_prompt = _moe_tiles(x1, h2, ti, rk, tg, counts, mod_p, map_p, final_g, wts, tm)
    shift_p = pa[:, -1, :]
    conv_p = pb[:, tp - (CONV_W - 1):, D_B:]

    xs = x_sample.reshape(1, bs, D_MODEL)
    pa_s, pb_s = _inproj(xs, mod_s, mod_s, map_s, p['norm1_g'], wa, wb, bs, True)
    r, w, k, v, kk, b, g = _prep(pa_s, state_shift.reshape(1, bs, N_COLS_A), p, bs, False)
    as_seq = lambda a: a.reshape(bs, 1, D_A)
    wkv_out, wkv_s = _wkv_scan(as_seq(r), as_seq(w), as_seq(k), as_seq(v), as_seq(kk), as_seq(b),
                               state_wkv, p, 1)
    conv0 = state_conv.transpose(1, 0, 2)
    yb, lru_s = _lru_step(pb_s[0], conv0, state_lru, p)
    x1, h2, ti, rk, tg, counts = _post(xs, wkv_out.reshape(1, bs, D_A), g, yb.reshape(1, bs, D_B),
                                       mod_s, mod_s, mod_s, map_s, p, bs)
    y_sample = _moe_tiles(x1, h2, ti, rk, tg, counts, mod_s, map_s, final_g, wts, bs)
    shift_s = pa_s[0]
    conv_s = jnp.concatenate([state_conv[:, 1:], pb_s[0][:, None, D_B:]], axis=1)

    return (y_prompt, y_sample.reshape(bs, 1, D_MODEL),
            wkv_p[None], shift_p[None], conv_p[None], lru_p.reshape(bp, D_B)[None],
            wkv_s[None], shift_s[None], conv_s[None], lru_s[None])


def kernel(x_prompt, x_sample, c_prompt, c_sample, state_wkv, state_shift, state_conv, state_lru, w_ada, b_ada, norm1_g, norm2_g, w_in, rk_mu, rk_w0, rk_w_up, rk_a0, rk_a_up, rk_g_up, rk_k_k, rk_k_a, rk_r_k, rk_lnx_w, rk_lnx_b, lru_conv_w, lru_conv_b, lru_w_r, lru_b_r, lru_w_i, lru_b_i, lru_lambda, lru_norm_g, w_out, router_w, router_b, w_gate, b_gate, w_up, b_up, w_down, b_down, final_g):
    assert w_ada.shape[0] == 1, "single-layer trunk"
    p = dict(w_ada=w_ada[0], b_ada=b_ada[0], norm1_g=norm1_g[0], norm2_g=norm2_g[0], w_in=w_in[0],
             rk_mu=rk_mu[0], rk_w0=rk_w0[0], rk_w_up=rk_w_up[0], rk_a0=rk_a0[0], rk_a_up=rk_a_up[0],
             rk_g_up=rk_g_up[0], rk_k_k=rk_k_k[0], rk_k_a=rk_k_a[0], rk_r_k=rk_r_k[0],
             rk_lnx_w=rk_lnx_w[0], rk_lnx_b=rk_lnx_b[0], lru_conv_w=lru_conv_w[0],
             lru_conv_b=lru_conv_b[0], lru_w_r=lru_w_r[0], lru_b_r=lru_b_r[0], lru_w_i=lru_w_i[0],
             lru_b_i=lru_b_i[0], lru_lambda=lru_lambda[0], lru_norm_g=lru_norm_g[0], w_out=w_out[0],
             router_w=router_w[0], router_b=router_b[0], w_gate=w_gate[0], b_gate=b_gate[0],
             w_up=w_up[0], b_up=b_up[0], w_down=w_down[0], b_down=b_down[0])
    return _forward(x_prompt, x_sample, c_prompt, c_sample, state_wkv[0], state_shift[0], state_conv[0],
                    state_lru[0], p, final_g)
```

```python
import functools

import jax
import jax.numpy as jnp
from jax import lax
from jax.experimental import pallas as pl
from jax.experimental.pallas import tpu as pltpu

F32 = jnp.float32
BF16 = jnp.bfloat16

D_MODEL = 1024
D_A = 512
HEAD_DIM = 64
N_HEADS = 8
D_B = 512
N_BLOCKS_B = 8
CONV_W = 4
LRU_C = 8.0
R_DECAY = 64
R_AAA = 64
R_GATE = 128
N_COLS_A = 3 * D_A + R_DECAY + R_AAA + R_GATE
N_COLS_B = 2 * D_B
N_EXPERTS = 32
TOP_K = 4
D_FF = 1024
SWIGLU_LIMIT = 7.0
SWIGLU_ALPHA = 1.702
RMS_EPS = 1e-6
LN_X_EPS = 64e-5
N_MOD = 6

LANES = 128
SUBLANES = 8
GROUP_BATCH = 8
HALF_ROWS = HEAD_DIM // 2
MOE_ROWS = 256
MOE_CHUNK = 16
TILE_CHUNKS = D_MODEL // LANES
VMEM_LIMIT = 56 * 1024 * 1024


def _cparams(sem):
    return pltpu.CompilerParams(dimension_semantics=sem, vmem_limit_bytes=VMEM_LIMIT)


def _dot(a, b):
    return jnp.dot(a.astype(BF16), b.astype(BF16), preferred_element_type=F32)


def _split(a):
    hi = a.astype(BF16)
    lo = (a - hi.astype(F32)).astype(BF16)
    return hi, lo


def _dot3(a, b):
    ah, al = _split(a)
    bh, bl = _split(b)
    return (jnp.dot(ah, bh, preferred_element_type=F32)
            + (jnp.dot(al, bh, preferred_element_type=F32) + jnp.dot(ah, bl, preferred_element_type=F32)))


def _dot3_nt(a, b):
    dn = (((1,), (1,)), ((), ()))
    ah, al = _split(a)
    bh, bl = _split(b)
    d = lambda x, y: lax.dot_general(x, y, dn, preferred_element_type=F32)
    return d(ah, bh) + (d(al, bh) + d(ah, bl))


def _softplus(x):
    return jnp.maximum(x, 0.0) + jnp.log1p(jnp.exp(-jnp.abs(x)))


def _sigmoid(x):
    return 1.0 / (1.0 + jnp.exp(-x))


def _rms(x, g):
    ms = jnp.mean(x * x, axis=-1, keepdims=True)
    return x * lax.rsqrt(ms + RMS_EPS) * g


def _ada_kernel(c_ref, w_ref, b_ref, o_ref):
    c = c_ref[...]
    s = c * _sigmoid(c)
    o_ref[...] = _dot3(s, w_ref[...]) + b_ref[...]


def _ada(c, w_ada, b_ada):
    rows = c.shape[0]
    ncol = w_ada.shape[1]
    tn = D_MODEL
    return pl.pallas_call(
        _ada_kernel,
        grid=(ncol // tn,),
        in_specs=[pl.BlockSpec((rows, D_MODEL), lambda j: (0, 0)),
                  pl.BlockSpec((D_MODEL, tn), lambda j: (0, j)),
                  pl.BlockSpec((1, tn), lambda j: (0, j))],
        out_specs=pl.BlockSpec((rows, tn), lambda j: (0, j)),
        out_shape=jax.ShapeDtypeStruct((rows, ncol), F32),
        compiler_params=_cparams(("arbitrary",)),
        name="ada_mod",
    )(c, w_ada, b_ada.reshape(1, ncol))


def _inproj_kernel(x_ref, shift_ref, scale_ref, g_ref, wa_ref, wb_ref, pa_ref, pb_ref, *, precise):
    x = x_ref[0]
    h = _rms(x, g_ref[...]) * (1.0 + scale_ref[0]) + shift_ref[0]
    dot = _dot3 if precise else _dot
    pa_ref[0] = dot(h, wa_ref[...])
    pb_ref[0] = dot(h, wb_ref[...])


def _inproj(x, shift, scale, mod_map, g, wa, wb, tm, precise):
    nb, t, _ = x.shape
    mod_block = (1,) + shift.shape[1:]
    return pl.pallas_call(
        functools.partial(_inproj_kernel, precise=precise),
        grid=(nb, t // tm),
        in_specs=[pl.BlockSpec((1, tm, D_MODEL), lambda b, i: (b, i, 0)),
                  pl.BlockSpec(mod_block, functools.partial(mod_map, 0)),
                  pl.BlockSpec(mod_block, functools.partial(mod_map, 1)),
                  pl.BlockSpec((1, D_MODEL), lambda b, i: (0, 0)),
                  pl.BlockSpec((D_MODEL, N_COLS_A), lambda b, i: (0, 0)),
                  pl.BlockSpec((D_MODEL, N_COLS_B), lambda b, i: (0, 0))],
        out_specs=[pl.BlockSpec((1, tm, N_COLS_A), lambda b, i: (b, i, 0)),
                   pl.BlockSpec((1, tm, N_COLS_B), lambda b, i: (b, i, 0))],
        out_shape=[jax.ShapeDtypeStruct((nb, t, N_COLS_A), F32),
                   jax.ShapeDtypeStruct((nb, t, N_COLS_B), F32)],
        compiler_params=_cparams(("arbitrary", "arbitrary")),
        name="norm1_inproj",
    )(x, shift, scale, g.reshape(1, D_MODEL), wa, wb)


def _prep_kernel(pa_ref, prev_ref, mu_ref, w0_ref, wup_ref, a0_ref, gup_ref, kk_ref, ka_ref,
                 r_out, w_out, k_out, v_out, kk_out, b_out, g_out, carry_ref, *, seq):
    pa = pa_ref[0]
    if seq:
        @pl.when(pl.program_id(1) == 0)
        def _():
            carry_ref[...] = jnp.zeros_like(carry_ref)
        rolled = pltpu.roll(pa, 1, axis=0)
        row = lax.broadcasted_iota(jnp.int32, pa.shape, 0)
        prev = jnp.where(row == 0, carry_ref[...], rolled)
        carry_ref[...] = pa[pa.shape[0] - 1:, :]
    else:
        prev = prev_ref[0]
    z = pa + (prev - pa) * mu_ref[...]
    r = z[:, 0:D_A]
    k = z[:, D_A:2 * D_A]
    v = z[:, 2 * D_A:3 * D_A]
    lo = 3 * D_A
    za = z[:, lo:lo + R_DECAY + R_AAA]
    lane = lax.broadcasted_iota(jnp.int32, za.shape, 1)
    za = jnp.where(lane < R_DECAY, jnp.tanh(za), za)
    lw = _dot3(za, wup_ref[...])
    w_log = -_softplus(-(w0_ref[...] + lw[:, :D_A])) - 0.5
    decay = jnp.exp(-jnp.exp(w_log))
    a = _sigmoid(a0_ref[...] + lw[:, D_A:])
    gd = z[:, lo + R_DECAY + R_AAA:]
    g = _dot3(_sigmoid(gd), gup_ref[...])
    kk = k * kk_ref[...]
    r_out[0] = r
    w_out[0] = decay
    k_out[0] = k * (1.0 + (a - 1.0) * ka_ref[...])
    v_out[0] = v
    kk_out[0] = kk
    b_out[0] = kk * a
    g_out[0] = g


def _prep(pa, prev, p, tm, seq):
    nb, t, _ = pa.shape
    wup = jnp.zeros((R_DECAY + R_AAA, 2 * D_A), F32)
    wup = wup.at[:R_DECAY, :D_A].set(p['rk_w_up']).at[R_DECAY:, D_A:].set(p['rk_a_up'])
    vec = lambda a: a.reshape(1, -1)
    row_spec = lambda n: pl.BlockSpec((1, n), lambda b, i: (0, 0))
    tile = pl.BlockSpec((1, tm, D_A), lambda b, i: (b, i, 0))
    if prev is None:
        prev = jnp.zeros((1, SUBLANES, N_COLS_A), F32)
        prev_spec = pl.BlockSpec((1, SUBLANES, N_COLS_A), lambda b, i: (0, 0, 0))
    else:
        prev_spec = pl.BlockSpec((1, tm, N_COLS_A), lambda b, i: (b, i, 0))
    return pl.pallas_call(
        functools.partial(_prep_kernel, seq=seq),
        grid=(nb, t // tm),
        in_specs=[pl.BlockSpec((1, tm, N_COLS_A), lambda b, i: (b, i, 0)),
                  prev_spec,
                  row_spec(N_COLS_A), row_spec(D_A),
                  pl.BlockSpec((R_DECAY + R_AAA, 2 * D_A), lambda b, i: (0, 0)),
                  row_spec(D_A),
                  pl.BlockSpec((R_GATE, D_A), lambda b, i: (0, 0)),
                  row_spec(D_A), row_spec(D_A)],
        out_specs=[tile] * 7,
        out_shape=[jax.ShapeDtypeStruct((nb, t, D_A), F32)] * 7,
        scratch_shapes=[pltpu.VMEM((1, N_COLS_A), F32)],
        compiler_params=_cparams(("arbitrary", "arbitrary")),
        name="rwkv_prep",
    )(pa, prev, vec(p['rk_mu']), vec(p['rk_w0']), wup, vec(p['rk_a0']), p['rk_g_up'],
      vec(p['rk_k_k']), vec(p['rk_k_a']))


def _scan_kernel(w_ref, kk_ref, kknext_ref, b_ref, k_ref, r_ref, v_ref, s0_ref, lnw_ref, lnb_ref, rk_ref,
                 out_ref, sfin_ref, s_ref, sa_ref, inv_ref, wd_ref, kkd_ref, bd_ref, kd_ref, rd_ref, *, tt):
    ti = pl.program_id(1)

    def both_halves(x):
        return jnp.concatenate([x, x], axis=-1)

    wd_ref[...] = both_halves(w_ref[0])
    bd_ref[...] = both_halves(b_ref[0])
    kd_ref[...] = both_halves(k_ref[0])
    rd_ref[...] = both_halves(r_ref[0])
    kkd_ref[0:tt] = both_halves(kk_ref[0])
    kkd_ref[tt] = both_halves(kknext_ref[0, 0])

    def inv_norm2(kk_rows):
        s2 = jnp.sum(kk_rows * kk_rows, axis=0, keepdims=True)
        return 1.0 / jnp.maximum(s2, 1e-24)

    @pl.when(ti == 0)
    def _():
        s_ref[...] = s0_ref[0]
        acc = jnp.zeros((HALF_ROWS, LANES), F32)
        for j in range(HEAD_DIM):
            acc = acc + s0_ref[0, j] * kkd_ref[0, j:j + 1, :]
        sa_ref[...] = acc
        inv_ref[...] = inv_norm2(kkd_ref[0])

    def finish(y, cv):
        tot = jnp.sum(y, axis=0, keepdims=True)
        tot = tot + pltpu.roll(tot, LANES // 2, axis=1)
        d = y - tot * (1.0 / HEAD_DIM)
        sq = jnp.sum(d * d, axis=0, keepdims=True)
        sq = sq + pltpu.roll(sq, LANES // 2, axis=1)
        yn = d * lax.rsqrt(sq * (1.0 / HEAD_DIM) + LN_X_EPS)
        return yn * lnw_ref[...] + lnb_ref[...] + cv

    def step(t, carry):
        sa, inv2, y_prev, cv_prev = carry
        out_ref[0, jnp.maximum(t - 1, 0)] = finish(y_prev, cv_prev)
        sae = sa * (-inv2)
        v = v_ref[0, t]
        acc_y = jnp.zeros((HALF_ROWS, LANES), F32)
        acc_s = jnp.zeros((HALF_ROWS, LANES), F32)
        for j in range(HEAD_DIM):
            row = pl.ds(j, 1)
            s_new = s_ref[j] * wd_ref[t, row, :] + sae * bd_ref[t, row, :] + v * kd_ref[t, row, :]
            s_ref[j] = s_new
            acc_y = acc_y + s_new * rd_ref[t, row, :]
            acc_s = acc_s + s_new * kkd_ref[t + 1, row, :]
        c = jnp.sum(rd_ref[t] * kd_ref[t] * rk_ref[...], axis=0, keepdims=True)
        return acc_s, inv_norm2(kkd_ref[t + 1]), acc_y, c * v

    zeros = jnp.zeros((HALF_ROWS, LANES), F32)
    sa, inv2, y_last, cv_last = lax.fori_loop(0, tt, step, (sa_ref[...], inv_ref[...], zeros, zeros))
    out_ref[0, tt - 1] = finish(y_last, cv_last)
    sa_ref[...] = sa
    inv_ref[...] = inv2

    @pl.when(ti == pl.num_programs(1) - 1)
    def _():
        sfin_ref[0] = s_ref[...]


def _to_scan_keys(x):
    nb, t, _ = x.shape
    g = nb // GROUP_BATCH
    y = x.reshape(g, GROUP_BATCH, t, N_HEADS, HEAD_DIM).transpose(0, 2, 4, 1, 3)
    return y.reshape(g, t, HEAD_DIM, GROUP_BATCH * N_HEADS)


def _to_scan_rows(x):
    nb, t, _ = x.shape
    g = nb // GROUP_BATCH
    y = x.reshape(g, GROUP_BATCH, t, N_HEADS, 2, HALF_ROWS).transpose(0, 2, 5, 4, 1, 3)
    return y.reshape(g, t, HALF_ROWS, LANES)


def _from_scan_rows(y):
    g, t = y.shape[:2]
    x = y.reshape(g, t, HALF_ROWS, 2, GROUP_BATCH, N_HEADS).transpose(0, 4, 1, 5, 3, 2)
    return x.reshape(g * GROUP_BATCH, t, D_A)


def _state_to_scan(s):
    g = s.shape[0] // GROUP_BATCH
    y = s.reshape(g, GROUP_BATCH, N_HEADS, 2, HALF_ROWS, HEAD_DIM).transpose(0, 5, 4, 3, 1, 2)
    return y.reshape(g, HEAD_DIM, HALF_ROWS, LANES)


def _state_from_scan(y):
    g = y.shape[0]
    s = y.reshape(g, HEAD_DIM, HALF_ROWS, 2, GROUP_BATCH, N_HEADS).transpose(0, 4, 5, 3, 2, 1)
    return s.reshape(g * GROUP_BATCH, N_HEADS, HEAD_DIM, HEAD_DIM)


def _head_rows(x):
    y = x.reshape(N_HEADS, 2, HALF_ROWS).transpose(2, 1, 0)
    y = jnp.broadcast_to(y[:, :, None, :], (HALF_ROWS, 2, GROUP_BATCH, N_HEADS))
    return y.reshape(HALF_ROWS, LANES)


def _head_keys(x):
    y = jnp.broadcast_to(x.T[:, None, None, :], (HEAD_DIM, 2, GROUP_BATCH, N_HEADS))
    return y.reshape(HEAD_DIM, LANES)


def _wkv_scan(r, w, k, v, kk, b, s0, p, tt):
    t = r.shape[1]
    wk, bk, kkey, rkey, kks = (_to_scan_keys(a) for a in (w, b, k, r, kk))
    vr = _to_scan_rows(v)
    g = wk.shape[0]
    half = LANES // 2
    key_tile = pl.BlockSpec((1, tt, HEAD_DIM, half), lambda gi, i: (gi, i, 0, 0))
    next_row = pl.BlockSpec((1, 1, HEAD_DIM, half), lambda gi, i: (gi, jnp.minimum((i + 1) * tt, t - 1), 0, 0))
    row_tile = pl.BlockSpec((1, tt, HALF_ROWS, LANES), lambda gi, i: (gi, i, 0, 0))
    state = pl.BlockSpec((1, HEAD_DIM, HALF_ROWS, LANES), lambda gi, i: (gi, 0, 0, 0))
    const = lambda n: pl.BlockSpec((n, LANES), lambda gi, i: (0, 0))
    dup = pltpu.VMEM((tt, HEAD_DIM, LANES), F32)
    out, sfin = pl.pallas_call(
        functools.partial(_scan_kernel, tt=tt),
        grid=(g, t // tt),
        in_specs=[key_tile, key_tile, next_row, key_tile, key_tile, key_tile, row_tile,
                  state, const(HALF_ROWS), const(HALF_ROWS), const(HEAD_DIM)],
        out_specs=[row_tile, state],
        out_shape=[jax.ShapeDtypeStruct((g, t, HALF_ROWS, LANES), F32),
                   jax.ShapeDtypeStruct((g, HEAD_DIM, HALF_ROWS, LANES), F32)],
        scratch_shapes=[pltpu.VMEM((HEAD_DIM, HALF_ROWS, LANES), F32),
                        pltpu.VMEM((HALF_ROWS, LANES), F32),
                        pltpu.VMEM((1, LANES), F32),
                        dup, pltpu.VMEM((tt + 1, HEAD_DIM, LANES), F32), dup, dup, dup],
        compiler_params=_cparams(("arbitrary", "arbitrary")),
        name="wkv_scan",
    )(wk, kks, kks, bk, kkey, rkey, vr, _state_to_scan(s0),
      _head_rows(p['rk_lnx_w']), _head_rows(p['rk_lnx_b']), _head_keys(p['rk_r_k']))
    return _from_scan_rows(out), _state_from_scan(sfin)


def _gelu(x):
    return 0.5 * x * (1.0 + jnp.tanh(0.7978845608028654 * (x + 0.044715 * (x * x * x))))


def _lru_gates(xc, wri_ref, bri_ref, nsl_ref, precise):
    dot = _dot3 if precise else _dot
    gates = _sigmoid(dot(xc, wri_ref[...]) + bri_ref[...])
    gate_r = gates[:, :D_B]
    gate_i = gates[:, D_B:]
    log_a = gate_r * nsl_ref[...]
    a = jnp.exp(log_a)
    th = jnp.tanh(log_a)
    one_minus_a2 = -2.0 * th / (1.0 - th)
    bt = jnp.sqrt(one_minus_a2) * (gate_i * xc)
    return a, bt


def _lru_seq_kernel(pb_ref, cw_ref, cb_ref, wri_ref, bri_ref, nsl_ref, ng_ref,
                    out_ref, hlast_ref, xprev_ref, h_ref):
    tm = pb_ref.shape[1]

    @pl.when(pl.program_id(1) == 0)
    def _():
        xprev_ref[...] = jnp.zeros_like(xprev_ref)
        h_ref[...] = jnp.zeros_like(h_ref)

    pb = pb_ref[0]
    yb = pb[:, :D_B]
    xb = pb[:, D_B:]
    xprev = xprev_ref[...]
    row8 = lax.broadcasted_iota(jnp.int32, (SUBLANES, D_B), 0)

    def shifted(d):
        rolled = pltpu.roll(xb, d, axis=0)
        top = jnp.where(row8 < d, pltpu.roll(xprev, d, axis=0), rolled[:SUBLANES])
        return jnp.concatenate([top, rolled[SUBLANES:]], axis=0)

    xc = cb_ref[...] + cw_ref[3:4, :] * xb
    for d in range(1, CONV_W):
        xc = xc + cw_ref[3 - d:4 - d, :] * shifted(d)
    xprev_ref[...] = xb[tm - SUBLANES:, :]

    a, x = _lru_gates(xc, wri_ref, bri_ref, nsl_ref, False)
    row = lax.broadcasted_iota(jnp.int32, (tm, D_B), 0)
    d = 1
    while d < tm:
        keep = row >= d
        a_s = jnp.where(keep, pltpu.roll(a, d, axis=0), 1.0)
        x_s = jnp.where(keep, pltpu.roll(x, d, axis=0), 0.0)
        x = a * x_s + x
        a = a * a_s
        d *= 2
    h = a * h_ref[...] + x
    h_ref[...] = h[tm - 1:, :]
    hlast_ref[0] = h[tm - 1:, :]
    out_ref[0] = _rms(h * _gelu(yb), ng_ref[...]).astype(out_ref.dtype)


def _lru_params(p):
    eye = jnp.eye(N_BLOCKS_B, dtype=F32)
    bd = lambda w: (eye[:, None, :, None] * w[:, :, None, :]).reshape(D_B, D_B)
    wri = jnp.concatenate([bd(p['lru_w_r']), bd(p['lru_w_i'])], axis=1)
    bri = jnp.concatenate([p['lru_b_r'], p['lru_b_i']]).reshape(1, 2 * D_B)
    nsl = (-LRU_C * jax.nn.softplus(-p['lru_lambda'])).reshape(1, D_B)
    return wri, bri, nsl


def _lru_seq(pb, p, tm):
    nb, t, _ = pb.shape
    wri, bri, nsl = _lru_params(p)
    row_spec = lambda n: pl.BlockSpec((1, n), lambda b, i: (0, 0))
    return pl.pallas_call(
        _lru_seq_kernel,
        grid=(nb, t // tm),
        in_specs=[pl.BlockSpec((1, tm, N_COLS_B), lambda b, i: (b, i, 0)),
                  pl.BlockSpec((CONV_W, D_B), lambda b, i: (0, 0)),
                  row_spec(D_B),
                  pl.BlockSpec((D_B, 2 * D_B), lambda b, i: (0, 0)),
                  row_spec(2 * D_B), row_spec(D_B), row_spec(D_B)],
        out_specs=[pl.BlockSpec((1, tm, D_B), lambda b, i: (b, i, 0)),
                   pl.BlockSpec((1, 1, D_B), lambda b, i: (b, 0, 0))],
        out_shape=[jax.ShapeDtypeStruct((nb, t, D_B), BF16),
                   jax.ShapeDtypeStruct((nb, 1, D_B), F32)],
        scratch_shapes=[pltpu.VMEM((SUBLANES, D_B), F32), pltpu.VMEM((1, D_B), F32)],
        compiler_params=_cparams(("arbitrary", "arbitrary")),
        name="rglru_seq",
    )(pb, p['lru_conv_w'], p['lru_conv_b'].reshape(1, D_B), wri.astype(BF16), bri, nsl,
      p['lru_norm_g'].reshape(1, D_B))


def _lru_step_kernel(pb_ref, conv_ref, h0_ref, cw_ref, cb_ref, wri_ref, bri_ref, nsl_ref, ng_ref,
                     out_ref, hnew_ref):
    pb = pb_ref[...]
    yb = pb[:, :D_B]
    xb = pb[:, D_B:]
    xc = cb_ref[...] + cw_ref[3:4, :] * xb
    for j in range(CONV_W - 1):
        xc = xc + cw_ref[j:j + 1, :] * conv_ref[j]
    a, x = _lru_gates(xc, wri_ref, bri_ref, nsl_ref, True)
    h = a * h0_ref[...] + x
    hnew_ref[...] = h
    out_ref[...] = _rms(h * _gelu(yb), ng_ref[...]).astype(out_ref.dtype)


def _lru_step(pb, conv0, h0, p):
    n = pb.shape[0]
    wri, bri, nsl = _lru_params(p)
    return pl.pallas_call(
        _lru_step_kernel,
        out_shape=[jax.ShapeDtypeStruct((n, D_B), BF16), jax.ShapeDtypeStruct((n, D_B), F32)],
        compiler_params=pltpu.CompilerParams(vmem_limit_bytes=VMEM_LIMIT),
        name="rglru_step",
    )(pb, conv0, h0, p['lru_conv_w'], p['lru_conv_b'].reshape(1, D_B), wri, bri, nsl,
      p['lru_norm_g'].reshape(1, D_B))


def _to_token_tiles(ref, x):
    rows = x.shape[0]
    for c in range(TILE_CHUNKS):
        ref[pl.ds(c, rows, stride=TILE_CHUNKS), :] = x[:, c * LANES:(c + 1) * LANES]


def _from_token_tiles(ref, row0, rows):
    return jnp.concatenate(
        [ref[pl.ds(row0 + c, rows, stride=TILE_CHUNKS), :] for c in range(TILE_CHUNKS)], axis=1)


def _post_kernel(x_ref, wkv_ref, g_ref, yb_ref, gate1_ref, shift2_ref, scale2_ref, n2_ref,
                 wo_ref, rw_ref, rb_ref, tri_ref, x1_ref, h2_ref, ti_ref, rk_ref, tg_ref, cnt_ref):
    ya = (wkv_ref[0] * g_ref[0]).astype(BF16)
    mixed = (jnp.dot(ya, wo_ref[:D_A, :], preferred_element_type=F32)
             + jnp.dot(yb_ref[0], wo_ref[D_A:, :], preferred_element_type=F32))
    x1 = x_ref[0] + gate1_ref[0] * mixed
    x1_ref[0] = x1
    h2 = _rms(x1, n2_ref[...]) * (1.0 + scale2_ref[0]) + shift2_ref[0]
    _to_token_tiles(h2_ref, h2)
    logits = _dot3_nt(rw_ref[...], h2) + rb_ref[...]
    eidx = lax.broadcasted_iota(jnp.int32, logits.shape, 0)
    vals, idxs = [], []
    cur = logits
    for _ in range(TOP_K):
        m = jnp.max(cur, axis=0, keepdims=True)
        i = jnp.min(jnp.where(cur == m, eidx, N_EXPERTS), axis=0, keepdims=True)
        vals.append(m)
        idxs.append(i)
        cur = jnp.where(eidx == i, -jnp.inf, cur)
    ex = [jnp.exp(v - vals[0]) for v in vals]
    den = ex[0] + ex[1] + ex[2] + ex[3]
    sel = [eidx == i for i in idxs]
    onehot = (sel[0] | sel[1] | sel[2] | sel[3]).astype(F32)
    incl = jnp.dot(onehot.astype(BF16), tri_ref[...], preferred_element_type=F32)
    rank = incl - onehot
    cnt_ref[0] = jnp.broadcast_to(jnp.sum(onehot, axis=1, keepdims=True), cnt_ref.shape[1:])
    for k in range(TOP_K):
        ti_ref[0, k:k + 1, :] = idxs[k]
        rk_ref[0, k:k + 1, :] = jnp.sum(jnp.where(sel[k], rank, 0.0), axis=0, keepdims=True).astype(jnp.int32)
        tg_ref[0, k:k + 1, :] = ex[k] / den


def _post(x, wkv, g, yb, gate1, shift2, scale2, mod_map, p, tm):
    nb, t, _ = x.shape
    mod_block = (1,) + gate1.shape[1:]
    tile = lambda n: pl.BlockSpec((1, tm, n), lambda b, i: (b, i, 0))
    full = lambda a: pl.BlockSpec(a.shape, lambda b, i: (0,) * a.ndim)
    mspec = lambda j: pl.BlockSpec(mod_block, functools.partial(mod_map, j))
    n2 = p['norm2_g'].reshape(1, D_MODEL)
    wo = p['w_out'].astype(BF16)
    rw = p['router_w'].T
    rb = p['router_b'].reshape(N_EXPERTS, 1)
    tri = jnp.triu(jnp.ones((tm, tm), BF16))
    nt = t // tm
    n = nb * t
    topk = pl.BlockSpec((1, TOP_K, tm), lambda b, i: (b * nt + i, 0, 0))
    topk_i = jax.ShapeDtypeStruct((nb * nt, TOP_K, tm), jnp.int32)
    x1, h2, ti, rk, tg, cnt = pl.pallas_call(
        _post_kernel,
        grid=(nb, nt),
        in_specs=[tile(D_MODEL), tile(D_A), tile(D_A), tile(D_B), mspec(2), mspec(3), mspec(4),
                  full(n2), full(wo), full(rw), full(rb), full(tri)],
        out_specs=[tile(D_MODEL),
                   pl.BlockSpec((tm * TILE_CHUNKS, LANES), lambda b, i: (b * nt + i, 0)),
                   topk, topk, topk,
                   pl.BlockSpec((1, N_EXPERTS, LANES), lambda b, i: (b * nt + i, 0, 0))],
        out_shape=[jax.ShapeDtypeStruct((nb, t, D_MODEL), F32),
                   jax.ShapeDtypeStruct((n * TILE_CHUNKS, LANES), F32),
                   topk_i, topk_i,
                   jax.ShapeDtypeStruct((nb * nt, TOP_K, tm), F32),
                   jax.ShapeDtypeStruct((nb * nt, N_EXPERTS, LANES), F32)],
        compiler_params=_cparams(("arbitrary", "arbitrary")),
        name="outproj_router",
    )(x, wkv, g, yb, gate1, shift2, scale2, n2, wo, rw, rb, tri)
    return x1, h2, ti, rk, tg, cnt[:, :, 0].astype(jnp.int32)


def _plan(routes):
    cnt = jnp.concatenate([r[2] for r in routes], axis=0)
    total = jnp.sum(cnt, axis=0)
    run_start = jnp.cumsum(cnt, axis=0) - cnt
    padded = jnp.where(total > 0, (total + MOE_CHUNK - 1 + MOE_ROWS - 1) // MOE_ROWS * MOE_ROWS, 0)
    pad_end = jnp.cumsum(padded)
    pad_start = pad_end - padded
    n_chunks = (cnt + MOE_CHUNK - 1) // MOE_CHUNK
    loc_off = (jnp.cumsum(n_chunks, axis=1) - n_chunks) * MOE_CHUNK
    dst0 = pad_start[None, :] + run_start
    n_tokens = sum(r[0].shape[0] * r[0].shape[2] for r in routes)
    n_blocks = -(-(n_tokens * TOP_K + N_EXPERTS * (MOE_CHUNK - 1)) // MOE_ROWS) + N_EXPERTS
    block_row0 = jnp.arange(n_blocks, dtype=jnp.int32) * MOE_ROWS
    block_e = jnp.minimum(jnp.sum(pad_end[None, :] <= block_row0[:, None], axis=1), N_EXPERTS - 1)
    n_used = (pad_end[-1] // MOE_ROWS).reshape(1)
    as_i32 = lambda a: a.astype(jnp.int32)
    experts = jnp.arange(N_EXPERTS, dtype=jnp.int32)
    groups, t0 = [], 0
    for ti, rk, c in routes:
        nt = ti.shape[0]
        sl = slice(t0, t0 + nt)
        off = jnp.sum(jnp.where(ti[..., None] == experts, loc_off[sl, None, None, :], 0), axis=-1)
        lpos = as_i32(rk + off).reshape(nt, -1)
        table = as_i32(jnp.stack([n_chunks[sl], loc_off[sl], dst0[sl]], axis=0).reshape(3, -1))
        groups.append((lpos, table))
        t0 += nt
    return groups, as_i32(block_e), as_i32(n_used), as_i32(pad_end), as_i32(padded), n_blocks


def _tile_rows(row):
    return pl.ds(pl.multiple_of(row * TILE_CHUNKS, TILE_CHUNKS), TILE_CHUNKS)


def _chunk_rows(row0):
    return pl.ds(pl.multiple_of(row0 * TILE_CHUNKS, TILE_CHUNKS), MOE_CHUNK * TILE_CHUNKS)


def _block_rows(block):
    rows = MOE_ROWS * TILE_CHUNKS
    return pl.ds(pl.multiple_of(block * rows, rows), rows)


def _for_each_chunk(table_ref, tile, n_tiles, fn):
    def per_expert(e, n):
        col = tile * N_EXPERTS + e
        nc = table_ref[col]
        lo = table_ref[n_tiles * N_EXPERTS + col]
        d0 = table_ref[2 * n_tiles * N_EXPERTS + col]

        def per_chunk(c, carry):
            fn(lo + c * MOE_CHUNK, d0 + c * MOE_CHUNK)
            return carry
        lax.fori_loop(0, nc, per_chunk, 0)
        return n + nc
    return lax.fori_loop(0, N_EXPERTS, per_expert, jnp.int32(0))


def _scatter_kernel(*refs, first):
    if first:
        table_ref, pad_end_ref, padded_ref, h2_ref, lpos_hbm, xs_hbm, idx_ref, loc_ref, zero_ref, sem, isem = refs
    else:
        table_ref, pad_end_ref, padded_ref, h2_ref, lpos_hbm, _, xs_hbm, idx_ref, loc_ref, zero_ref, sem, isem = refs
    i = pl.program_id(0)
    n_tiles = pl.num_programs(0)
    tm = h2_ref.shape[0] // TILE_CHUNKS
    idx_copy = pltpu.make_async_copy(lpos_hbm.at[i], idx_ref, isem)
    idx_copy.start()

    @pl.when(i == 0)
    def _():
        loc_ref[...] = jnp.zeros_like(loc_ref)

    if first:
        @pl.when(i == 0)
        def _():
            zero_ref[...] = jnp.zeros_like(zero_ref)
            n_used = pad_end_ref[N_EXPERTS - 1] // MOE_ROWS
            n_blocks = xs_hbm.shape[0] // (MOE_ROWS * TILE_CHUNKS)

            def fill(e, n_started):
                n_fill = jnp.minimum(padded_ref[e] // MOE_ROWS, 2)
                last = pad_end_ref[e] // MOE_ROWS - 1

                def one(j, c):
                    pltpu.make_async_copy(zero_ref, xs_hbm.at[_block_rows(last - j), :], sem).start()
                    return c
                lax.fori_loop(0, n_fill, one, 0)
                return n_started + n_fill
            n_started = lax.fori_loop(0, N_EXPERTS, fill, jnp.int32(0))

            def fill_tail(blk, c):
                pltpu.make_async_copy(zero_ref, xs_hbm.at[_block_rows(blk), :], sem).start()
                return c
            lax.fori_loop(n_used, n_blocks, fill_tail, 0)

            def drain(j, c):
                pltpu.make_async_copy(zero_ref, xs_hbm.at[_block_rows(0), :], sem).wait()
                return c
            lax.fori_loop(0, n_started + (n_blocks - n_used), drain, 0)

    idx_copy.wait()

    def place(r, c):
        row = h2_ref[_tile_rows(r), :]
        for k in range(TOP_K):
            loc_ref[_tile_rows(idx_ref[k * tm + r]), :] = row
        return c
    lax.fori_loop(0, tm, place, 0, unroll=8)

    def send(lo, d0):
        pltpu.make_async_copy(loc_ref.at[_chunk_rows(lo), :], xs_hbm.at[_chunk_rows(d0), :], sem).start()
    n_sent = _for_each_chunk(table_ref, i, n_tiles, send)

    def drain_chunks(j, c):
        pltpu.make_async_copy(loc_ref.at[_chunk_rows(0), :], xs_hbm.at[_chunk_rows(0), :], sem).wait()
        return c
    lax.fori_loop(0, n_sent, drain_chunks, 0)


def _scatter(h2, lpos, table, pad_end, padded, n_rows, tm, xs=None):
    nt = lpos.shape[0]
    first = xs is None
    loc_rows = (TOP_K * tm + N_EXPERTS * MOE_CHUNK) * TILE_CHUNKS
    in_specs = [pl.BlockSpec((tm * TILE_CHUNKS, LANES), lambda i, *_: (i, 0)),
                pl.BlockSpec(memory_space=pl.ANY)]
    args = [table.reshape(-1), pad_end, padded, h2, lpos]
    if not first:
        in_specs.append(pl.BlockSpec(memory_space=pl.ANY))
        args.append(xs)
    grid_spec = pltpu.PrefetchScalarGridSpec(
        num_scalar_prefetch=3,
        grid=(nt,),
        in_specs=in_specs,
        out_specs=pl.BlockSpec(memory_space=pl.ANY),
        scratch_shapes=[pltpu.SMEM((TOP_K * tm,), jnp.int32),
                        pltpu.VMEM((loc_rows, LANES), F32),
                        pltpu.VMEM((MOE_ROWS * TILE_CHUNKS, LANES), F32),
                        pltpu.SemaphoreType.DMA, pltpu.SemaphoreType.DMA],
    )
    return pl.pallas_call(
        functools.partial(_scatter_kernel, first=first),
        grid_spec=grid_spec,
        out_shape=jax.ShapeDtypeStruct((n_rows * TILE_CHUNKS, LANES), F32),
        input_output_aliases={} if first else {5: 0},
        compiler_params=_cparams(("arbitrary",)),
        name="moe_dispatch",
    )(*args)


def _mlp_kernel(be_ref, nu_ref, xs_ref, wg_ref, bg_ref, wu_ref, bu_ref, wd_ref, bd_ref, out_ref,
                wg16_ref, wu16_ref, wd16_ref):
    i = pl.program_id(0)
    used = i < nu_ref[0]

    @pl.when(used & ((i == 0) | (be_ref[i] != be_ref[jnp.maximum(i - 1, 0)])))
    def _():
        wg16_ref[...] = wg_ref[0].astype(BF16)
        wu16_ref[...] = wu_ref[0].astype(BF16)
        wd16_ref[...] = wd_ref[0].astype(BF16)

    @pl.when(used)
    def _():
        x = _from_token_tiles(xs_ref, 0, MOE_ROWS).astype(BF16)
        gt = jnp.dot(x, wg16_ref[...], preferred_element_type=F32) + bg_ref[0]
        up = jnp.dot(x, wu16_ref[...], preferred_element_type=F32) + bu_ref[0]
        gt = jnp.minimum(gt, SWIGLU_LIMIT)
        up = jnp.clip(up, -SWIGLU_LIMIT, SWIGLU_LIMIT)
        glu = gt * _sigmoid(gt * SWIGLU_ALPHA)
        mid = ((up + 1.0) * glu).astype(BF16)
        _to_token_tiles(out_ref, jnp.dot(mid, wd16_ref[...], preferred_element_type=F32) + bd_ref[0])

    @pl.when(jnp.logical_not(used))
    def _():
        out_ref[...] = jnp.zeros_like(out_ref)


def _mlp(xs, block_e, n_used, n_blocks, wts):
    wg, bg, wu, bu, wd, bd = wts
    row_map = lambda i, be, nu: (i, 0)
    w_map = lambda i, be, nu: (be[i], 0, 0)
    wspec = pl.BlockSpec((1, D_MODEL, D_FF), w_map)
    bspec = pl.BlockSpec((1, 1, D_FF), w_map)
    rows = pl.BlockSpec((MOE_ROWS * TILE_CHUNKS, LANES), row_map)
    grid_spec = pltpu.PrefetchScalarGridSpec(
        num_scalar_prefetch=2,
        grid=(n_blocks,),
        in_specs=[rows, wspec, bspec, wspec, bspec, wspec, bspec],
        out_specs=rows,
        scratch_shapes=[pltpu.VMEM((D_MODEL, D_FF), BF16)] * 3,
    )
    return pl.pallas_call(
        _mlp_kernel,
        grid_spec=grid_spec,
        out_shape=jax.ShapeDtypeStruct(xs.shape, F32),
        compiler_params=_cparams(("arbitrary",)),
        name="moe_experts",
    )(block_e, n_used, xs, wg, bg, wu, bu, wd, bd)


def _gather_kernel(table_ref, x1_ref, gate2_ref, fg_ref, lpos_hbm, tg_hbm, rows_hbm, y_ref,
                   idx_ref, gsm_ref, loc_ref, ff_ref, sem, isem):
    i = pl.program_id(1) + pl.program_id(0) * pl.num_programs(1)
    n_tiles = pl.num_programs(0) * pl.num_programs(1)
    tm = x1_ref.shape[1]
    idx_copy = pltpu.make_async_copy(lpos_hbm.at[i], idx_ref, isem)
    gate_copy = pltpu.make_async_copy(tg_hbm.at[i], gsm_ref, isem)
    idx_copy.start()
    gate_copy.start()

    def fetch(lo, d0):
        pltpu.make_async_copy(rows_hbm.at[_chunk_rows(d0), :], loc_ref.at[_chunk_rows(lo), :], sem).start()
    n_fetched = _for_each_chunk(table_ref, i, n_tiles, fetch)

    def drain_chunks(j, c):
        pltpu.make_async_copy(rows_hbm.at[_chunk_rows(0), :], loc_ref.at[_chunk_rows(0), :], sem).wait()
        return c
    lax.fori_loop(0, n_fetched, drain_chunks, 0)
    idx_copy.wait()
    gate_copy.wait()

    def mix(r, c):
        acc = gsm_ref[r] * loc_ref[_tile_rows(idx_ref[r]), :]
        for k in range(1, TOP_K):
            acc = acc + gsm_ref[k * tm + r] * loc_ref[_tile_rows(idx_ref[k * tm + r]), :]
        ff_ref[_tile_rows(r), :] = acc
        return c
    lax.fori_loop(0, tm, mix, 0, unroll=8)

    x2 = x1_ref[0] + gate2_ref[0] * _from_token_tiles(ff_ref, 0, tm)
    y_ref[0] = _rms(x2, fg_ref[...])


def _gather(x1, tg, gate2, mod_map, final_g, lpos, table, rows, tm):
    nb, t, _ = x1.shape
    nt = t // tm
    mod_block = (1,) + gate2.shape[1:]
    loc_rows = (TOP_K * tm + N_EXPERTS * MOE_CHUNK) * TILE_CHUNKS
    grid_spec = pltpu.PrefetchScalarGridSpec(
        num_scalar_prefetch=1,
        grid=(nb, nt),
        in_specs=[pl.BlockSpec((1, tm, D_MODEL), lambda b, i, *_: (b, i, 0)),
                  pl.BlockSpec(mod_block, lambda b, i, *_: mod_map(5, b, i)),
                  pl.BlockSpec((1, D_MODEL), lambda b, i, *_: (0, 0)),
                  pl.BlockSpec(memory_space=pl.ANY),
                  pl.BlockSpec(memory_space=pl.ANY),
                  pl.BlockSpec(memory_space=pl.ANY)],
        out_specs=pl.BlockSpec((1, tm, D_MODEL), lambda b, i, *_: (b, i, 0)),
        scratch_shapes=[pltpu.SMEM((TOP_K * tm,), jnp.int32),
                        pltpu.SMEM((TOP_K * tm,), F32),
                        pltpu.VMEM((loc_rows, LANES), F32),
                        pltpu.VMEM((tm * TILE_CHUNKS, LANES), F32),
                        pltpu.SemaphoreType.DMA, pltpu.SemaphoreType.DMA],
    )
    return pl.pallas_call(
        _gather_kernel,
        grid_spec=grid_spec,
        out_shape=jax.ShapeDtypeStruct((nb, t, D_MODEL), F32),
        compiler_params=_cparams(("arbitrary", "arbitrary")),
        name="moe_combine",
    )(table.reshape(-1), x1, gate2, final_g.reshape(1, D_MODEL), lpos, tg.reshape(tg.shape[0], -1), rows)


def _moe(route_p, route_s, mod_p, mod_s, final_g, wts):
    routes = [route_p, route_s]
    groups, block_e, n_used, pad_end, padded, n_blocks = _plan([(r[2], r[3], r[5]) for r in routes])
    xs = None
    for (x1, h2, ti, rk, tg, cnt), (lpos, table) in zip(routes, groups):
        xs = _scatter(h2, lpos, table, pad_end, padded, n_blocks * MOE_ROWS, ti.shape[2], xs)
    rows = _mlp(xs, block_e, n_used, n_blocks, wts)
    outs = []
    for (x1, h2, ti, rk, tg, cnt), (lpos, table), (mod, mod_map) in zip(routes, groups, (mod_p, mod_s)):
        outs.append(_gather(x1, tg, mod, mod_map, final_g, lpos, table, rows, ti.shape[2]))
    return outs


def _forward(x_prompt, x_sample, c_prompt, c_sample, state_wkv, state_shift, state_conv, state_lru, p, final_g):
    bp, tp, _ = x_prompt.shape
    bs = x_sample.shape[0]
    tm = min(512, tp)
    tt = min(32, tp)

    mod = _ada(jnp.concatenate([c_prompt, c_sample], axis=0), p['w_ada'], p['b_ada'])
    mod_p = mod[:bp].reshape(bp * N_MOD, 1, D_MODEL)
    mod_s = mod[bp:].reshape(bs, N_MOD, D_MODEL).transpose(1, 0, 2)
    map_p = lambda j, b, i: (b * N_MOD + j, 0, 0)
    map_s = lambda j, b, i: (j, 0, 0)

    wa = p['w_in'][:, :N_COLS_A]
    wb = p['w_in'][:, N_COLS_A:]
    wts = (p['w_gate'], p['b_gate'].reshape(N_EXPERTS, 1, D_FF),
           p['w_up'], p['b_up'].reshape(N_EXPERTS, 1, D_FF),
           p['w_down'], p['b_down'].reshape(N_EXPERTS, 1, D_MODEL))

    pa, pb = _inproj(x_prompt, mod_p, mod_p, map_p, p['norm1_g'], wa.astype(BF16), wb.astype(BF16), tm, False)
    r, w, k, v, kk, b, g = _prep(pa, None, p, tm, True)
    s0 = jnp.zeros((bp, N_HEADS, HEAD_DIM, HEAD_DIM), F32)
    wkv_out, wkv_p = _wkv_scan(r, w, k, v, kk, b, s0, p, tt)
    yb, lru_p = _lru_seq(pb, p, tm)
    route_p = _post(x_prompt, wkv_out, g, yb, mod_p, mod_p, mod_p, map_p, p, tm)
    shift_p = pa[:, -1, :]
    conv_p = pb[:, tp - (CONV_W - 1):, D_B:]

    xs = x_sample.reshape(1, bs, D_MODEL)
    pa_s, pb_s = _inproj(xs, mod_s, mod_s, map_s, p['norm1_g'], wa, wb, bs, True)
    r, w, k, v, kk, b, g = _prep(pa_s, state_shift.reshape(1, bs, N_COLS_A), p, bs, False)
    as_seq = lambda a: a.reshape(bs, 1, D_A)
    wkv_out, wkv_s = _wkv_scan(as_seq(r), as_seq(w), as_seq(k), as_seq(v), as_seq(kk), as_seq(b),
                               state_wkv, p, 1)
    conv0 = state_conv.transpose(1, 0, 2)
    yb, lru_s = _lru_step(pb_s[0], conv0, state_lru, p)
    route_s = _post(xs, wkv_out.reshape(1, bs, D_A), g, yb.reshape(1, bs, D_B),
                    mod_s, mod_s, mod_s, map_s, p, bs)
    y_prompt, y_sample = _moe(route_p, route_s, (mod_p, map_p), (mod_s, map_s), final_g, wts)
    shift_s = pa_s[0]
    conv_s = jnp.concatenate([state_conv[:, 1:], pb_s[0][:, None, D_B:]], axis=1)

    return (y_prompt, y_sample.reshape(bs, 1, D_MODEL),
            wkv_p[None], shift_p[None], conv_p[None], lru_p.reshape(bp, D_B)[None],
            wkv_s[None], shift_s[None], conv_s[None], lru_s[None])


def kernel(x_prompt, x_sample, c_prompt, c_sample, state_wkv, state_shift, state_conv, state_lru, w_ada, b_ada, norm1_g, norm2_g, w_in, rk_mu, rk_w0, rk_w_up, rk_a0, rk_a_up, rk_g_up, rk_k_k, rk_k_a, rk_r_k, rk_lnx_w, rk_lnx_b, lru_conv_w, lru_conv_b, lru_w_r, lru_b_r, lru_w_i, lru_b_i, lru_lambda, lru_norm_g, w_out, router_w, router_b, w_gate, b_gate, w_up, b_up, w_down, b_down, final_g):
    assert w_ada.shape[0] == 1, "single-layer trunk"
    p = dict(w_ada=w_ada[0], b_ada=b_ada[0], norm1_g=norm1_g[0], norm2_g=norm2_g[0], w_in=w_in[0],
             rk_mu=rk_mu[0], rk_w0=rk_w0[0], rk_w_up=rk_w_up[0], rk_a0=rk_a0[0], rk_a_up=rk_a_up[0],
             rk_g_up=rk_g_up[0], rk_k_k=rk_k_k[0], rk_k_a=rk_k_a[0], rk_r_k=rk_r_k[0],
             rk_lnx_w=rk_lnx_w[0], rk_lnx_b=rk_lnx_b[0], lru_conv_w=lru_conv_w[0],
             lru_conv_b=lru_conv_b[0], lru_w_r=lru_w_r[0], lru_b_r=lru_b_r[0], lru_w_i=lru_w_i[0],
             lru_b_i=lru_b_i[0], lru_lambda=lru_lambda[0], lru_norm_g=lru_norm_g[0], w_out=w_out[0],
             router_w=router_w[0], router_b=router_b[0], w_gate=w_gate[0], b_gate=b_gate[0],
             w_up=w_up[0], b_up=b_up[0], w_down=w_down[0], b_down=b_down[0])
    return _forward(x_prompt, x_sample, c_prompt, c_sample, state_wkv[0], state_shift[0], state_conv[0],
                    state_lru[0], p, final_g)
```

```python
import functools

import jax
import jax.numpy as jnp
from jax import lax
from jax.experimental import pallas as pl
from jax.experimental.pallas import tpu as pltpu

F32 = jnp.float32
BF16 = jnp.bfloat16

D_MODEL = 1024
D_A = 512
HEAD_DIM = 64
N_HEADS = 8
D_B = 512
N_BLOCKS_B = 8
CONV_W = 4
LRU_C = 8.0
R_DECAY = 64
R_AAA = 64
R_GATE = 128
N_COLS_A = 3 * D_A + R_DECAY + R_AAA + R_GATE
N_COLS_B = 2 * D_B
N_EXPERTS = 32
TOP_K = 4
D_FF = 1024
SWIGLU_LIMIT = 7.0
SWIGLU_ALPHA = 1.702
RMS_EPS = 1e-6
LN_X_EPS = 64e-5
N_MOD = 6

LANES = 128
SUBLANES = 8
GROUP_BATCH = 8
HALF_ROWS = HEAD_DIM // 2
MOE_ROWS = 256
MOE_CHUNK = 16
TILE_CHUNKS = D_MODEL // LANES
VMEM_LIMIT = 56 * 1024 * 1024


def _cparams(sem):
    return pltpu.CompilerParams(dimension_semantics=sem, vmem_limit_bytes=VMEM_LIMIT)


def _dot(a, b):
    return jnp.dot(a.astype(BF16), b.astype(BF16), preferred_element_type=F32)


def _split(a):
    hi = a.astype(BF16)
    lo = (a - hi.astype(F32)).astype(BF16)
    return hi, lo


def _dot3(a, b):
    ah, al = _split(a)
    bh, bl = _split(b)
    return (jnp.dot(ah, bh, preferred_element_type=F32)
            + (jnp.dot(al, bh, preferred_element_type=F32) + jnp.dot(ah, bl, preferred_element_type=F32)))


def _dot3_nt(a, b):
    dn = (((1,), (1,)), ((), ()))
    ah, al = _split(a)
    bh, bl = _split(b)
    d = lambda x, y: lax.dot_general(x, y, dn, preferred_element_type=F32)
    return d(ah, bh) + (d(al, bh) + d(ah, bl))


def _softplus(x):
    return jnp.maximum(x, 0.0) + jnp.log1p(jnp.exp(-jnp.abs(x)))


def _sigmoid(x):
    return 1.0 / (1.0 + jnp.exp(-x))


def _rms(x, g):
    ms = jnp.mean(x * x, axis=-1, keepdims=True)
    return x * lax.rsqrt(ms + RMS_EPS) * g


def _ada_kernel(c_ref, w_ref, b_ref, o_ref):
    c = c_ref[...]
    s = c * _sigmoid(c)
    o_ref[...] = _dot3(s, w_ref[...]) + b_ref[...]


def _ada(c, w_ada, b_ada):
    rows = c.shape[0]
    ncol = w_ada.shape[1]
    tn = D_MODEL
    return pl.pallas_call(
        _ada_kernel,
        grid=(ncol // tn,),
        in_specs=[pl.BlockSpec((rows, D_MODEL), lambda j: (0, 0)),
                  pl.BlockSpec((D_MODEL, tn), lambda j: (0, j)),
                  pl.BlockSpec((1, tn), lambda j: (0, j))],
        out_specs=pl.BlockSpec((rows, tn), lambda j: (0, j)),
        out_shape=jax.ShapeDtypeStruct((rows, ncol), F32),
        compiler_params=_cparams(("arbitrary",)),
        name="ada_mod",
    )(c, w_ada, b_ada.reshape(1, ncol))


def _inproj_kernel(x_ref, shift_ref, scale_ref, g_ref, wa_ref, wb_ref, pa_ref, pb_ref, *, precise):
    x = x_ref[0]
    h = _rms(x, g_ref[...]) * (1.0 + scale_ref[0]) + shift_ref[0]
    dot = _dot3 if precise else _dot
    pa_ref[0] = dot(h, wa_ref[...])
    pb_ref[0] = dot(h, wb_ref[...])


def _inproj(x, shift, scale, mod_map, g, wa, wb, tm, precise):
    nb, t, _ = x.shape
    mod_block = (1,) + shift.shape[1:]
    return pl.pallas_call(
        functools.partial(_inproj_kernel, precise=precise),
        grid=(nb, t // tm),
        in_specs=[pl.BlockSpec((1, tm, D_MODEL), lambda b, i: (b, i, 0)),
                  pl.BlockSpec(mod_block, functools.partial(mod_map, 0)),
                  pl.BlockSpec(mod_block, functools.partial(mod_map, 1)),
                  pl.BlockSpec((1, D_MODEL), lambda b, i: (0, 0)),
                  pl.BlockSpec((D_MODEL, N_COLS_A), lambda b, i: (0, 0)),
                  pl.BlockSpec((D_MODEL, N_COLS_B), lambda b, i: (0, 0))],
        out_specs=[pl.BlockSpec((1, tm, N_COLS_A), lambda b, i: (b, i, 0)),
                   pl.BlockSpec((1, tm, N_COLS_B), lambda b, i: (b, i, 0))],
        out_shape=[jax.ShapeDtypeStruct((nb, t, N_COLS_A), F32),
                   jax.ShapeDtypeStruct((nb, t, N_COLS_B), F32)],
        compiler_params=_cparams(("arbitrary", "arbitrary")),
        name="norm1_inproj",
    )(x, shift, scale, g.reshape(1, D_MODEL), wa, wb)


def _store_head_pairs(ref, x, y):
    rows = x.shape[0]
    for h in range(N_HEADS):
        sl = slice(h * HEAD_DIM, (h + 1) * HEAD_DIM)
        ref[0, pl.ds(h, rows, stride=N_HEADS), :] = jnp.concatenate([x[:, sl], y[:, sl]], axis=1)


def _prep_kernel(pa_ref, prev_ref, mu_ref, w0_ref, wup_ref, a0_ref, gup_ref, kk_ref, ka_ref,
                 wb_out, kr_out, kv_out, g_out, carry_ref, *, seq):
    pa = pa_ref[0]
    if seq:
        @pl.when(pl.program_id(1) == 0)
        def _():
            carry_ref[...] = jnp.zeros_like(carry_ref)
        rolled = pltpu.roll(pa, 1, axis=0)
        row = lax.broadcasted_iota(jnp.int32, pa.shape, 0)
        prev = jnp.where(row == 0, carry_ref[...], rolled)
        carry_ref[...] = pa[pa.shape[0] - 1:, :]
    else:
        prev = prev_ref[0]
    z = pa + (prev - pa) * mu_ref[...]
    r = z[:, 0:D_A]
    k = z[:, D_A:2 * D_A]
    v = z[:, 2 * D_A:3 * D_A]
    lo = 3 * D_A
    za = z[:, lo:lo + R_DECAY + R_AAA]
    lane = lax.broadcasted_iota(jnp.int32, za.shape, 1)
    za = jnp.where(lane < R_DECAY, jnp.tanh(za), za)
    lw = _dot3(za, wup_ref[...])
    w_log = -_softplus(-(w0_ref[...] + lw[:, :D_A])) - 0.5
    decay = jnp.exp(-jnp.exp(w_log))
    a = _sigmoid(a0_ref[...] + lw[:, D_A:])
    gd = z[:, lo + R_DECAY + R_AAA:]
    g = _dot3(_sigmoid(gd), gup_ref[...])
    kk = k * kk_ref[...]
    _store_head_pairs(wb_out, decay, kk * a)
    _store_head_pairs(kr_out, k * (1.0 + (a - 1.0) * ka_ref[...]), r)
    _store_head_pairs(kv_out, kk, v)
    g_out[0] = g


def _prep(pa, prev, p, tm, seq):
    nb, t, _ = pa.shape
    wup = jnp.zeros((R_DECAY + R_AAA, 2 * D_A), F32)
    wup = wup.at[:R_DECAY, :D_A].set(p['rk_w_up']).at[R_DECAY:, D_A:].set(p['rk_a_up'])
    vec = lambda a: a.reshape(1, -1)
    row_spec = lambda n: pl.BlockSpec((1, n), lambda b, i: (0, 0))
    tile = pl.BlockSpec((1, tm, D_A), lambda b, i: (b, i, 0))
    pair_tile = pl.BlockSpec((1, tm * N_HEADS, LANES), lambda b, i: (b, i, 0))
    if prev is None:
        prev = jnp.zeros((1, SUBLANES, N_COLS_A), F32)
        prev_spec = pl.BlockSpec((1, SUBLANES, N_COLS_A), lambda b, i: (0, 0, 0))
    else:
        prev_spec = pl.BlockSpec((1, tm, N_COLS_A), lambda b, i: (b, i, 0))
    return pl.pallas_call(
        functools.partial(_prep_kernel, seq=seq),
        grid=(nb, t // tm),
        in_specs=[pl.BlockSpec((1, tm, N_COLS_A), lambda b, i: (b, i, 0)),
                  prev_spec,
                  row_spec(N_COLS_A), row_spec(D_A),
                  pl.BlockSpec((R_DECAY + R_AAA, 2 * D_A), lambda b, i: (0, 0)),
                  row_spec(D_A),
                  pl.BlockSpec((R_GATE, D_A), lambda b, i: (0, 0)),
                  row_spec(D_A), row_spec(D_A)],
        out_specs=[pair_tile] * 3 + [tile],
        out_shape=[jax.ShapeDtypeStruct((nb, t * N_HEADS, LANES), F32)] * 3
                  + [jax.ShapeDtypeStruct((nb, t, D_A), F32)],
        scratch_shapes=[pltpu.VMEM((1, N_COLS_A), F32)],
        compiler_params=_cparams(("arbitrary", "arbitrary")),
        name="rwkv_prep",
    )(pa, prev, vec(p['rk_mu']), vec(p['rk_w0']), wup, vec(p['rk_a0']), p['rk_g_up'],
      vec(p['rk_k_k']), vec(p['rk_k_a']))


SCAN_SLOTS = 4


def _scan_kernel(wb_ref, kr_ref, kv_ref, kvnext_ref, s0_ref, lnw_ref, lnb_ref, rk_ref,
                 out_ref, sfin_ref, s_ref, sa_ref, inv_ref, wd_ref, kkd_ref, bd_ref, kd_ref, rd_ref, vd_ref, *, tt):
    ti = pl.program_id(1)
    upper_half = lax.broadcasted_iota(jnp.int32, (HALF_ROWS, LANES), 1) >= LANES // 2
    unroll = SCAN_SLOTS if tt % SCAN_SLOTS == 0 else 1

    def expand(pair):
        return jnp.concatenate([pair, pair], axis=0).T

    def prepare(slot, s, kk_pair=None):
        t1 = expand(wb_ref[0, s])
        wd_ref[slot] = t1[:HEAD_DIM]
        bd_ref[slot] = t1[HEAD_DIM:]
        t2 = expand(kr_ref[0, s])
        kd_ref[slot] = t2[:HEAD_DIM]
        rd_ref[slot] = t2[HEAD_DIM:]
        t3 = expand(kv_ref[0, s])
        vd_ref[slot] = jnp.where(upper_half, t3[HEAD_DIM + HALF_ROWS:], t3[HEAD_DIM:HEAD_DIM + HALF_ROWS])
        kkd_ref[slot] = t3[:HEAD_DIM] if kk_pair is None else expand(kk_pair)[:HEAD_DIM]

    prepare(0, 0)
    if tt > 1:
        prepare(1, 1)
    else:
        kkd_ref[1] = expand(kvnext_ref[0, 0])[:HEAD_DIM]

    def inv_norm2(kk_rows):
        s2 = jnp.sum(kk_rows * kk_rows, axis=0, keepdims=True)
        return 1.0 / jnp.maximum(s2, 1e-24)

    @pl.when(ti == 0)
    def _():
        s_ref[...] = s0_ref[0]
        acc = jnp.zeros((HALF_ROWS, LANES), F32)
        for j in range(HEAD_DIM):
            acc = acc + s0_ref[0, j] * kkd_ref[0, j:j + 1, :]
        sa_ref[...] = acc
        inv_ref[...] = inv_norm2(kkd_ref[0])

    def finish(y, cv):
        tot = jnp.sum(y, axis=0, keepdims=True)
        tot = tot + pltpu.roll(tot, LANES // 2, axis=1)
        d = y - tot * (1.0 / HEAD_DIM)
        sq = jnp.sum(d * d, axis=0, keepdims=True)
        sq = sq + pltpu.roll(sq, LANES // 2, axis=1)
        yn = d * lax.rsqrt(sq * (1.0 / HEAD_DIM) + LN_X_EPS)
        return yn * lnw_ref[...] + lnb_ref[...] + cv

    def step(t, u, carry):
        sa, inv2, y_prev, cv_prev = carry
        nxt_slot = (u + 1) % SCAN_SLOTS
        out_ref[0, jnp.maximum(t - 1, 0)] = finish(y_prev, cv_prev)
        sae = sa * (-inv2)
        v = vd_ref[u]
        acc_y = jnp.zeros((HALF_ROWS, LANES), F32)
        acc_s = jnp.zeros((HALF_ROWS, LANES), F32)
        for j in range(HEAD_DIM):
            row = pl.ds(j, 1)
            s_new = s_ref[j] * wd_ref[u, row, :] + sae * bd_ref[u, row, :] + v * kd_ref[u, row, :]
            s_ref[j] = s_new
            acc_y = acc_y + s_new * rd_ref[u, row, :]
            acc_s = acc_s + s_new * kkd_ref[nxt_slot, row, :]
        c = jnp.sum(rd_ref[u] * kd_ref[u] * rk_ref[...], axis=0, keepdims=True)
        if unroll > 1:
            ahead = jnp.minimum(t + 2, tt - 1)
            kk_pair = None
            if (u + 2) % SCAN_SLOTS == 0:
                kk_pair = jnp.where(t + 2 == tt, kvnext_ref[0, 0], kv_ref[0, ahead])
            prepare((u + 2) % SCAN_SLOTS, ahead, kk_pair)
        return acc_s, inv_norm2(kkd_ref[nxt_slot]), acc_y, c * v

    def steps(q, carry):
        for u in range(unroll):
            carry = step(q * unroll + u, u, carry)
        return carry

    zeros = jnp.zeros((HALF_ROWS, LANES), F32)
    sa, inv2, y_last, cv_last = lax.fori_loop(0, tt // unroll, steps, (sa_ref[...], inv_ref[...], zeros, zeros))
    out_ref[0, tt - 1] = finish(y_last, cv_last)
    sa_ref[...] = sa
    inv_ref[...] = inv2

    @pl.when(ti == pl.num_programs(1) - 1)
    def _():
        sfin_ref[0] = s_ref[...]


def _from_scan_rows(y):
    g, t = y.shape[:2]
    x = y.reshape(g, t, HALF_ROWS, 2, GROUP_BATCH, N_HEADS).transpose(0, 4, 1, 5, 3, 2)
    return x.reshape(g * GROUP_BATCH, t, D_A)


def _state_to_scan(s):
    g = s.shape[0] // GROUP_BATCH
    y = s.reshape(g, GROUP_BATCH, N_HEADS, 2, HALF_ROWS, HEAD_DIM).transpose(0, 5, 4, 3, 1, 2)
    return y.reshape(g, HEAD_DIM, HALF_ROWS, LANES)


def _state_from_scan(y):
    g = y.shape[0]
    s = y.reshape(g, HEAD_DIM, HALF_ROWS, 2, GROUP_BATCH, N_HEADS).transpose(0, 4, 5, 3, 2, 1)
    return s.reshape(g * GROUP_BATCH, N_HEADS, HEAD_DIM, HEAD_DIM)


def _head_rows(x):
    y = x.reshape(N_HEADS, 2, HALF_ROWS).transpose(2, 1, 0)
    y = jnp.broadcast_to(y[:, :, None, :], (HALF_ROWS, 2, GROUP_BATCH, N_HEADS))
    return y.reshape(HALF_ROWS, LANES)


def _head_keys(x):
    y = jnp.broadcast_to(x.T[:, None, None, :], (HEAD_DIM, 2, GROUP_BATCH, N_HEADS))
    return y.reshape(HEAD_DIM, LANES)


def _pairs_to_groups(x, t):
    nb = x.shape[0]
    g = nb // GROUP_BATCH
    y = x.reshape(g, GROUP_BATCH, t, N_HEADS, LANES).transpose(0, 2, 1, 3, 4)
    return y.reshape(g, t, GROUP_BATCH * N_HEADS, LANES)


def _wkv_scan(wb, kr, kv, s0, p, tt):
    g, t = wb.shape[:2]
    pair_tile = pl.BlockSpec((1, tt, HEAD_DIM, LANES), lambda gi, i: (gi, i, 0, 0))
    next_step = pl.BlockSpec((1, 1, HEAD_DIM, LANES), lambda gi, i: (gi, jnp.minimum((i + 1) * tt, t - 1), 0, 0))
    row_tile = pl.BlockSpec((1, tt, HALF_ROWS, LANES), lambda gi, i: (gi, i, 0, 0))
    state = pl.BlockSpec((1, HEAD_DIM, HALF_ROWS, LANES), lambda gi, i: (gi, 0, 0, 0))
    const = lambda n: pl.BlockSpec((n, LANES), lambda gi, i: (0, 0))
    keys = pltpu.VMEM((SCAN_SLOTS, HEAD_DIM, LANES), F32)
    out, sfin = pl.pallas_call(
        functools.partial(_scan_kernel, tt=tt),
        grid=(g, t // tt),
        in_specs=[pair_tile, pair_tile, pair_tile, next_step,
                  state, const(HALF_ROWS), const(HALF_ROWS), const(HEAD_DIM)],
        out_specs=[row_tile, state],
        out_shape=[jax.ShapeDtypeStruct((g, t, HALF_ROWS, LANES), F32),
                   jax.ShapeDtypeStruct((g, HEAD_DIM, HALF_ROWS, LANES), F32)],
        scratch_shapes=[pltpu.VMEM((HEAD_DIM, HALF_ROWS, LANES), F32),
                        pltpu.VMEM((HALF_ROWS, LANES), F32),
                        pltpu.VMEM((1, LANES), F32),
                        keys, keys, keys, keys, keys,
                        pltpu.VMEM((SCAN_SLOTS, HALF_ROWS, LANES), F32)],
        compiler_params=_cparams(("arbitrary", "arbitrary")),
        name="wkv_scan",
    )(wb, kr, kv, kv, _state_to_scan(s0),
      _head_rows(p['rk_lnx_w']), _head_rows(p['rk_lnx_b']), _head_keys(p['rk_r_k']))
    return _from_scan_rows(out), _state_from_scan(sfin)


def _gelu(x):
    return 0.5 * x * (1.0 + jnp.tanh(0.7978845608028654 * (x + 0.044715 * (x * x * x))))


def _lru_gates(xc, wri_ref, bri_ref, nsl_ref, precise):
    dot = _dot3 if precise else _dot
    gates = _sigmoid(dot(xc, wri_ref[...]) + bri_ref[...])
    gate_r = gates[:, :D_B]
    gate_i = gates[:, D_B:]
    log_a = gate_r * nsl_ref[...]
    a = jnp.exp(log_a)
    th = jnp.tanh(log_a)
    one_minus_a2 = -2.0 * th / (1.0 - th)
    bt = jnp.sqrt(one_minus_a2) * (gate_i * xc)
    return a, bt


def _lru_seq_kernel(pb_ref, cw_ref, cb_ref, wri_ref, bri_ref, nsl_ref, ng_ref,
                    out_ref, hlast_ref, xprev_ref, h_ref):
    tm = pb_ref.shape[1]

    @pl.when(pl.program_id(1) == 0)
    def _():
        xprev_ref[...] = jnp.zeros_like(xprev_ref)
        h_ref[...] = jnp.zeros_like(h_ref)

    pb = pb_ref[0]
    yb = pb[:, :D_B]
    xb = pb[:, D_B:]
    xprev = xprev_ref[...]
    row8 = lax.broadcasted_iota(jnp.int32, (SUBLANES, D_B), 0)

    def shifted(d):
        rolled = pltpu.roll(xb, d, axis=0)
        top = jnp.where(row8 < d, pltpu.roll(xprev, d, axis=0), rolled[:SUBLANES])
        return jnp.concatenate([top, rolled[SUBLANES:]], axis=0)

    xc = cb_ref[...] + cw_ref[3:4, :] * xb
    for d in range(1, CONV_W):
        xc = xc + cw_ref[3 - d:4 - d, :] * shifted(d)
    xprev_ref[...] = xb[tm - SUBLANES:, :]

    a, x = _lru_gates(xc, wri_ref, bri_ref, nsl_ref, False)
    row = lax.broadcasted_iota(jnp.int32, (tm, D_B), 0)
    d = 1
    while d < tm:
        keep = row >= d
        a_s = jnp.where(keep, pltpu.roll(a, d, axis=0), 1.0)
        x_s = jnp.where(keep, pltpu.roll(x, d, axis=0), 0.0)
        x = a * x_s + x
        a = a * a_s
        d *= 2
    h = a * h_ref[...] + x
    h_ref[...] = h[tm - 1:, :]
    hlast_ref[0] = h[tm - 1:, :]
    out_ref[0] = _rms(h * _gelu(yb), ng_ref[...]).astype(out_ref.dtype)


def _lru_params(p):
    eye = jnp.eye(N_BLOCKS_B, dtype=F32)
    bd = lambda w: (eye[:, None, :, None] * w[:, :, None, :]).reshape(D_B, D_B)
    wri = jnp.concatenate([bd(p['lru_w_r']), bd(p['lru_w_i'])], axis=1)
    bri = jnp.concatenate([p['lru_b_r'], p['lru_b_i']]).reshape(1, 2 * D_B)
    nsl = (-LRU_C * jax.nn.softplus(-p['lru_lambda'])).reshape(1, D_B)
    return wri, bri, nsl


def _lru_seq(pb, p, tm):
    nb, t, _ = pb.shape
    wri, bri, nsl = _lru_params(p)
    row_spec = lambda n: pl.BlockSpec((1, n), lambda b, i: (0, 0))
    return pl.pallas_call(
        _lru_seq_kernel,
        grid=(nb, t // tm),
        in_specs=[pl.BlockSpec((1, tm, N_COLS_B), lambda b, i: (b, i, 0)),
                  pl.BlockSpec((CONV_W, D_B), lambda b, i: (0, 0)),
                  row_spec(D_B),
                  pl.BlockSpec((D_B, 2 * D_B), lambda b, i: (0, 0)),
                  row_spec(2 * D_B), row_spec(D_B), row_spec(D_B)],
        out_specs=[pl.BlockSpec((1, tm, D_B), lambda b, i: (b, i, 0)),
                   pl.BlockSpec((1, 1, D_B), lambda b, i: (b, 0, 0))],
        out_shape=[jax.ShapeDtypeStruct((nb, t, D_B), BF16),
                   jax.ShapeDtypeStruct((nb, 1, D_B), F32)],
        scratch_shapes=[pltpu.VMEM((SUBLANES, D_B), F32), pltpu.VMEM((1, D_B), F32)],
        compiler_params=_cparams(("arbitrary", "arbitrary")),
        name="rglru_seq",
    )(pb, p['lru_conv_w'], p['lru_conv_b'].reshape(1, D_B), wri.astype(BF16), bri, nsl,
      p['lru_norm_g'].reshape(1, D_B))


def _lru_step_kernel(pb_ref, conv_ref, h0_ref, cw_ref, cb_ref, wri_ref, bri_ref, nsl_ref, ng_ref,
                     out_ref, hnew_ref):
    pb = pb_ref[...]
    yb = pb[:, :D_B]
    xb = pb[:, D_B:]
    xc = cb_ref[...] + cw_ref[3:4, :] * xb
    for j in range(CONV_W - 1):
        xc = xc + cw_ref[j:j + 1, :] * conv_ref[j]
    a, x = _lru_gates(xc, wri_ref, bri_ref, nsl_ref, True)
    h = a * h0_ref[...] + x
    hnew_ref[...] = h
    out_ref[...] = _rms(h * _gelu(yb), ng_ref[...]).astype(out_ref.dtype)


def _lru_step(pb, conv0, h0, p):
    n = pb.shape[0]
    wri, bri, nsl = _lru_params(p)
    return pl.pallas_call(
        _lru_step_kernel,
        out_shape=[jax.ShapeDtypeStruct((n, D_B), BF16), jax.ShapeDtypeStruct((n, D_B), F32)],
        compiler_params=pltpu.CompilerParams(vmem_limit_bytes=VMEM_LIMIT),
        name="rglru_step",
    )(pb, conv0, h0, p['lru_conv_w'], p['lru_conv_b'].reshape(1, D_B), wri, bri, nsl,
      p['lru_norm_g'].reshape(1, D_B))


def _to_token_tiles(ref, x):
    rows = x.shape[0]
    for c in range(TILE_CHUNKS):
        ref[pl.ds(c, rows, stride=TILE_CHUNKS), :] = x[:, c * LANES:(c + 1) * LANES]


def _from_token_tiles(ref, row0, rows):
    return jnp.concatenate(
        [ref[pl.ds(row0 + c, rows, stride=TILE_CHUNKS), :] for c in range(TILE_CHUNKS)], axis=1)


def _post_kernel(x_ref, wkv_ref, g_ref, yb_ref, gate1_ref, shift2_ref, scale2_ref, n2_ref,
                 wo_ref, rw_ref, rb_ref, tri_ref, x1_ref, h2_ref, ti_ref, rk_ref, tg_ref, cnt_ref):
    ya = (wkv_ref[0] * g_ref[0]).astype(BF16)
    mixed = (jnp.dot(ya, wo_ref[:D_A, :], preferred_element_type=F32)
             + jnp.dot(yb_ref[0], wo_ref[D_A:, :], preferred_element_type=F32))
    x1 = x_ref[0] + gate1_ref[0] * mixed
    x1_ref[0] = x1
    h2 = _rms(x1, n2_ref[...]) * (1.0 + scale2_ref[0]) + shift2_ref[0]
    _to_token_tiles(h2_ref, h2)
    logits = _dot3_nt(rw_ref[...], h2) + rb_ref[...]
    eidx = lax.broadcasted_iota(jnp.int32, logits.shape, 0)
    vals, idxs = [], []
    cur = logits
    for _ in range(TOP_K):
        m = jnp.max(cur, axis=0, keepdims=True)
        i = jnp.min(jnp.where(cur == m, eidx, N_EXPERTS), axis=0, keepdims=True)
        vals.append(m)
        idxs.append(i)
        cur = jnp.where(eidx == i, -jnp.inf, cur)
    ex = [jnp.exp(v - vals[0]) for v in vals]
    den = ex[0] + ex[1] + ex[2] + ex[3]
    sel = [eidx == i for i in idxs]
    onehot = (sel[0] | sel[1] | sel[2] | sel[3]).astype(F32)
    incl = jnp.dot(onehot.astype(BF16), tri_ref[...], preferred_element_type=F32)
    rank = incl - onehot
    cnt_ref[0] = jnp.broadcast_to(jnp.sum(onehot, axis=1, keepdims=True), cnt_ref.shape[1:])
    for k in range(TOP_K):
        ti_ref[0, k:k + 1, :] = idxs[k]
        rk_ref[0, k:k + 1, :] = jnp.sum(jnp.where(sel[k], rank, 0.0), axis=0, keepdims=True).astype(jnp.int32)
        tg_ref[0, k:k + 1, :] = ex[k] / den


def _post(x, wkv, g, yb, gate1, shift2, scale2, mod_map, p, tm):
    nb, t, _ = x.shape
    mod_block = (1,) + gate1.shape[1:]
    tile = lambda n: pl.BlockSpec((1, tm, n), lambda b, i: (b, i, 0))
    full = lambda a: pl.BlockSpec(a.shape, lambda b, i: (0,) * a.ndim)
    mspec = lambda j: pl.BlockSpec(mod_block, functools.partial(mod_map, j))
    n2 = p['norm2_g'].reshape(1, D_MODEL)
    wo = p['w_out'].astype(BF16)
    rw = p['router_w'].T
    rb = p['router_b'].reshape(N_EXPERTS, 1)
    tri = jnp.triu(jnp.ones((tm, tm), BF16))
    nt = t // tm
    n = nb * t
    topk = pl.BlockSpec((1, TOP_K, tm), lambda b, i: (b * nt + i, 0, 0))
    topk_i = jax.ShapeDtypeStruct((nb * nt, TOP_K, tm), jnp.int32)
    x1, h2, ti, rk, tg, cnt = pl.pallas_call(
        _post_kernel,
        grid=(nb, nt),
        in_specs=[tile(D_MODEL), tile(D_A), tile(D_A), tile(D_B), mspec(2), mspec(3), mspec(4),
                  full(n2), full(wo), full(rw), full(rb), full(tri)],
        out_specs=[tile(D_MODEL),
                   pl.BlockSpec((tm * TILE_CHUNKS, LANES), lambda b, i: (b * nt + i, 0)),
                   topk, topk, topk,
                   pl.BlockSpec((1, N_EXPERTS, LANES), lambda b, i: (b * nt + i, 0, 0))],
        out_shape=[jax.ShapeDtypeStruct((nb, t, D_MODEL), F32),
                   jax.ShapeDtypeStruct((n * TILE_CHUNKS, LANES), F32),
                   topk_i, topk_i,
                   jax.ShapeDtypeStruct((nb * nt, TOP_K, tm), F32),
                   jax.ShapeDtypeStruct((nb * nt, N_EXPERTS, LANES), F32)],
        compiler_params=_cparams(("arbitrary", "arbitrary")),
        name="outproj_router",
    )(x, wkv, g, yb, gate1, shift2, scale2, n2, wo, rw, rb, tri)
    return x1, h2, ti, rk, tg, cnt[:, :, 0].astype(jnp.int32)


def _plan(routes):
    cnt = jnp.concatenate([r[2] for r in routes], axis=0)
    total = jnp.sum(cnt, axis=0)
    run_start = jnp.cumsum(cnt, axis=0) - cnt
    padded = jnp.where(total > 0, (total + MOE_CHUNK - 1 + MOE_ROWS - 1) // MOE_ROWS * MOE_ROWS, 0)
    pad_end = jnp.cumsum(padded)
    pad_start = pad_end - padded
    n_chunks = (cnt + MOE_CHUNK - 1) // MOE_CHUNK
    loc_off = (jnp.cumsum(n_chunks, axis=1) - n_chunks) * MOE_CHUNK
    dst0 = pad_start[None, :] + run_start
    n_tokens = sum(r[0].shape[0] * r[0].shape[2] for r in routes)
    n_blocks = -(-(n_tokens * TOP_K + N_EXPERTS * (MOE_CHUNK - 1)) // MOE_ROWS) + N_EXPERTS
    block_row0 = jnp.arange(n_blocks, dtype=jnp.int32) * MOE_ROWS
    block_e = jnp.minimum(jnp.sum(pad_end[None, :] <= block_row0[:, None], axis=1), N_EXPERTS - 1)
    n_used = (pad_end[-1] // MOE_ROWS).reshape(1)
    as_i32 = lambda a: a.astype(jnp.int32)
    experts = jnp.arange(N_EXPERTS, dtype=jnp.int32)
    groups, t0 = [], 0
    for ti, rk, c in routes:
        nt = ti.shape[0]
        sl = slice(t0, t0 + nt)
        off = jnp.sum(jnp.where(ti[..., None] == experts, loc_off[sl, None, None, :], 0), axis=-1)
        lpos = as_i32(rk + off).reshape(nt, -1)
        table = as_i32(jnp.stack([n_chunks[sl], loc_off[sl], dst0[sl]], axis=0).reshape(3, -1))
        groups.append((lpos, table))
        t0 += nt
    return groups, as_i32(block_e), as_i32(n_used), as_i32(pad_end), as_i32(padded), n_blocks


def _tile_rows(row):
    return pl.ds(pl.multiple_of(row * TILE_CHUNKS, TILE_CHUNKS), TILE_CHUNKS)


def _chunk_rows(row0):
    return pl.ds(pl.multiple_of(row0 * TILE_CHUNKS, TILE_CHUNKS), MOE_CHUNK * TILE_CHUNKS)


def _block_rows(block):
    rows = MOE_ROWS * TILE_CHUNKS
    return pl.ds(pl.multiple_of(block * rows, rows), rows)


def _for_each_chunk(table_ref, tile, n_tiles, fn):
    def per_expert(e, n):
        col = tile * N_EXPERTS + e
        nc = table_ref[col]
        lo = table_ref[n_tiles * N_EXPERTS + col]
        d0 = table_ref[2 * n_tiles * N_EXPERTS + col]

        def per_chunk(c, carry):
            fn(lo + c * MOE_CHUNK, d0 + c * MOE_CHUNK)
            return carry
        lax.fori_loop(0, nc, per_chunk, 0)
        return n + nc
    return lax.fori_loop(0, N_EXPERTS, per_expert, jnp.int32(0))


def _scatter_kernel(*refs, first):
    if first:
        table_ref, pad_end_ref, padded_ref, h2_ref, lpos_hbm, xs_hbm, idx_ref, loc_ref, zero_ref, sem, isem = refs
    else:
        table_ref, pad_end_ref, padded_ref, h2_ref, lpos_hbm, _, xs_hbm, idx_ref, loc_ref, zero_ref, sem, isem = refs
    i = pl.program_id(0)
    n_tiles = pl.num_programs(0)
    tm = h2_ref.shape[0] // TILE_CHUNKS
    idx_copy = pltpu.make_async_copy(lpos_hbm.at[i], idx_ref, isem)
    idx_copy.start()

    @pl.when(i == 0)
    def _():
        loc_ref[...] = jnp.zeros_like(loc_ref)

    if first:
        @pl.when(i == 0)
        def _():
            zero_ref[...] = jnp.zeros_like(zero_ref)
            n_used = pad_end_ref[N_EXPERTS - 1] // MOE_ROWS
            n_blocks = xs_hbm.shape[0] // (MOE_ROWS * TILE_CHUNKS)

            def fill(e, n_started):
                n_fill = jnp.minimum(padded_ref[e] // MOE_ROWS, 2)
                last = pad_end_ref[e] // MOE_ROWS - 1

                def one(j, c):
                    pltpu.make_async_copy(zero_ref, xs_hbm.at[_block_rows(last - j), :], sem).start()
                    return c
                lax.fori_loop(0, n_fill, one, 0)
                return n_started + n_fill
            n_started = lax.fori_loop(0, N_EXPERTS, fill, jnp.int32(0))

            def fill_tail(blk, c):
                pltpu.make_async_copy(zero_ref, xs_hbm.at[_block_rows(blk), :], sem).start()
                return c
            lax.fori_loop(n_used, n_blocks, fill_tail, 0)

            def drain(j, c):
                pltpu.make_async_copy(zero_ref, xs_hbm.at[_block_rows(0), :], sem).wait()
                return c
            lax.fori_loop(0, n_started + (n_blocks - n_used), drain, 0)

    idx_copy.wait()

    def place(r, c):
        row = h2_ref[_tile_rows(r), :]
        for k in range(TOP_K):
            loc_ref[_tile_rows(idx_ref[k * tm + r]), :] = row
        return c
    lax.fori_loop(0, tm, place, 0, unroll=8)

    def send(lo, d0):
        pltpu.make_async_copy(loc_ref.at[_chunk_rows(lo), :], xs_hbm.at[_chunk_rows(d0), :], sem).start()
    n_sent = _for_each_chunk(table_ref, i, n_tiles, send)

    def drain_chunks(j, c):
        pltpu.make_async_copy(loc_ref.at[_chunk_rows(0), :], xs_hbm.at[_chunk_rows(0), :], sem).wait()
        return c
    lax.fori_loop(0, n_sent, drain_chunks, 0)


def _scatter(h2, lpos, table, pad_end, padded, n_rows, tm, xs=None):
    nt = lpos.shape[0]
    first = xs is None
    loc_rows = (TOP_K * tm + N_EXPERTS * MOE_CHUNK) * TILE_CHUNKS
    in_specs = [pl.BlockSpec((tm * TILE_CHUNKS, LANES), lambda i, *_: (i, 0)),
                pl.BlockSpec(memory_space=pl.ANY)]
    args = [table.reshape(-1), pad_end, padded, h2, lpos]
    if not first:
        in_specs.append(pl.BlockSpec(memory_space=pl.ANY))
        args.append(xs)
    grid_spec = pltpu.PrefetchScalarGridSpec(
        num_scalar_prefetch=3,
        grid=(nt,),
        in_specs=in_specs,
        out_specs=pl.BlockSpec(memory_space=pl.ANY),
        scratch_shapes=[pltpu.SMEM((TOP_K * tm,), jnp.int32),
                        pltpu.VMEM((loc_rows, LANES), F32),
                        pltpu.VMEM((MOE_ROWS * TILE_CHUNKS, LANES), F32),
                        pltpu.SemaphoreType.DMA, pltpu.SemaphoreType.DMA],
    )
    return pl.pallas_call(
        functools.partial(_scatter_kernel, first=first),
        grid_spec=grid_spec,
        out_shape=jax.ShapeDtypeStruct((n_rows * TILE_CHUNKS, LANES), F32),
        input_output_aliases={} if first else {5: 0},
        compiler_params=_cparams(("arbitrary",)),
        name="moe_dispatch",
    )(*args)


def _mlp_kernel(be_ref, nu_ref, xs_ref, wg_ref, bg_ref, wu_ref, bu_ref, wd_ref, bd_ref, out_ref,
                wg16_ref, wu16_ref, wd16_ref):
    i = pl.program_id(0)
    used = i < nu_ref[0]

    @pl.when(used & ((i == 0) | (be_ref[i] != be_ref[jnp.maximum(i - 1, 0)])))
    def _():
        wg16_ref[...] = wg_ref[0].astype(BF16)
        wu16_ref[...] = wu_ref[0].astype(BF16)
        wd16_ref[...] = wd_ref[0].astype(BF16)

    @pl.when(used)
    def _():
        x = _from_token_tiles(xs_ref, 0, MOE_ROWS).astype(BF16)
        gt = jnp.dot(x, wg16_ref[...], preferred_element_type=F32) + bg_ref[0]
        up = jnp.dot(x, wu16_ref[...], preferred_element_type=F32) + bu_ref[0]
        gt = jnp.minimum(gt, SWIGLU_LIMIT)
        up = jnp.clip(up, -SWIGLU_LIMIT, SWIGLU_LIMIT)
        glu = gt * _sigmoid(gt * SWIGLU_ALPHA)
        mid = ((up + 1.0) * glu).astype(BF16)
        _to_token_tiles(out_ref, jnp.dot(mid, wd16_ref[...], preferred_element_type=F32) + bd_ref[0])

    @pl.when(jnp.logical_not(used))
    def _():
        out_ref[...] = jnp.zeros_like(out_ref)


def _mlp(xs, block_e, n_used, n_blocks, wts):
    wg, bg, wu, bu, wd, bd = wts
    row_map = lambda i, be, nu: (i, 0)
    w_map = lambda i, be, nu: (be[i], 0, 0)
    wspec = pl.BlockSpec((1, D_MODEL, D_FF), w_map)
    bspec = pl.BlockSpec((1, 1, D_FF), w_map)
    rows = pl.BlockSpec((MOE_ROWS * TILE_CHUNKS, LANES), row_map)
    grid_spec = pltpu.PrefetchScalarGridSpec(
        num_scalar_prefetch=2,
        grid=(n_blocks,),
        in_specs=[rows, wspec, bspec, wspec, bspec, wspec, bspec],
        out_specs=rows,
        scratch_shapes=[pltpu.VMEM((D_MODEL, D_FF), BF16)] * 3,
    )
    return pl.pallas_call(
        _mlp_kernel,
        grid_spec=grid_spec,
        out_shape=jax.ShapeDtypeStruct(xs.shape, F32),
        compiler_params=_cparams(("arbitrary",)),
        name="moe_experts",
    )(block_e, n_used, xs, wg, bg, wu, bu, wd, bd)


def _gather_kernel(table_ref, x1_ref, gate2_ref, fg_ref, lpos_hbm, tg_hbm, rows_hbm, y_ref,
                   idx_ref, gsm_ref, loc_ref, ff_ref, sem, isem):
    i = pl.program_id(1) + pl.program_id(0) * pl.num_programs(1)
    n_tiles = pl.num_programs(0) * pl.num_programs(1)
    tm = x1_ref.shape[1]
    idx_copy = pltpu.make_async_copy(lpos_hbm.at[i], idx_ref, isem)
    gate_copy = pltpu.make_async_copy(tg_hbm.at[i], gsm_ref, isem)
    idx_copy.start()
    gate_copy.start()

    def fetch(lo, d0):
        pltpu.make_async_copy(rows_hbm.at[_chunk_rows(d0), :], loc_ref.at[_chunk_rows(lo), :], sem).start()
    n_fetched = _for_each_chunk(table_ref, i, n_tiles, fetch)

    def drain_chunks(j, c):
        pltpu.make_async_copy(rows_hbm.at[_chunk_rows(0), :], loc_ref.at[_chunk_rows(0), :], sem).wait()
        return c
    lax.fori_loop(0, n_fetched, drain_chunks, 0)
    idx_copy.wait()
    gate_copy.wait()

    def mix(r, c):
        acc = gsm_ref[r] * loc_ref[_tile_rows(idx_ref[r]), :]
        for k in range(1, TOP_K):
            acc = acc + gsm_ref[k * tm + r] * loc_ref[_tile_rows(idx_ref[k * tm + r]), :]
        ff_ref[_tile_rows(r), :] = acc
        return c
    lax.fori_loop(0, tm, mix, 0, unroll=8)

    x2 = x1_ref[0] + gate2_ref[0] * _from_token_tiles(ff_ref, 0, tm)
    y_ref[0] = _rms(x2, fg_ref[...])


def _gather(x1, tg, gate2, mod_map, final_g, lpos, table, rows, tm):
    nb, t, _ = x1.shape
    nt = t // tm
    mod_block = (1,) + gate2.shape[1:]
    loc_rows = (TOP_K * tm + N_EXPERTS * MOE_CHUNK) * TILE_CHUNKS
    grid_spec = pltpu.PrefetchScalarGridSpec(
        num_scalar_prefetch=1,
        grid=(nb, nt),
        in_specs=[pl.BlockSpec((1, tm, D_MODEL), lambda b, i, *_: (b, i, 0)),
                  pl.BlockSpec(mod_block, lambda b, i, *_: mod_map(5, b, i)),
                  pl.BlockSpec((1, D_MODEL), lambda b, i, *_: (0, 0)),
                  pl.BlockSpec(memory_space=pl.ANY),
                  pl.BlockSpec(memory_space=pl.ANY),
                  pl.BlockSpec(memory_space=pl.ANY)],
        out_specs=pl.BlockSpec((1, tm, D_MODEL), lambda b, i, *_: (b, i, 0)),
        scratch_shapes=[pltpu.SMEM((TOP_K * tm,), jnp.int32),
                        pltpu.SMEM((TOP_K * tm,), F32),
                        pltpu.VMEM((loc_rows, LANES), F32),
                        pltpu.VMEM((tm * TILE_CHUNKS, LANES), F32),
                        pltpu.SemaphoreType.DMA, pltpu.SemaphoreType.DMA],
    )
    return pl.pallas_call(
        _gather_kernel,
        grid_spec=grid_spec,
        out_shape=jax.ShapeDtypeStruct((nb, t, D_MODEL), F32),
        compiler_params=_cparams(("arbitrary", "arbitrary")),
        name="moe_combine",
    )(table.reshape(-1), x1, gate2, final_g.reshape(1, D_MODEL), lpos, tg.reshape(tg.shape[0], -1), rows)


def _moe(route_p, route_s, mod_p, mod_s, final_g, wts):
    routes = [route_p, route_s]
    groups, block_e, n_used, pad_end, padded, n_blocks = _plan([(r[2], r[3], r[5]) for r in routes])
    xs = None
    for (x1, h2, ti, rk, tg, cnt), (lpos, table) in zip(routes, groups):
        xs = _scatter(h2, lpos, table, pad_end, padded, n_blocks * MOE_ROWS, ti.shape[2], xs)
    rows = _mlp(xs, block_e, n_used, n_blocks, wts)
    outs = []
    for (x1, h2, ti, rk, tg, cnt), (lpos, table), (mod, mod_map) in zip(routes, groups, (mod_p, mod_s)):
        outs.append(_gather(x1, tg, mod, mod_map, final_g, lpos, table, rows, ti.shape[2]))
    return outs


def _forward(x_prompt, x_sample, c_prompt, c_sample, state_wkv, state_shift, state_conv, state_lru, p, final_g):
    bp, tp, _ = x_prompt.shape
    bs = x_sample.shape[0]
    tm = min(512, tp)
    tt = min(32, tp)

    mod = _ada(jnp.concatenate([c_prompt, c_sample], axis=0), p['w_ada'], p['b_ada'])
    mod_p = mod[:bp].reshape(bp * N_MOD, 1, D_MODEL)
    mod_s = mod[bp:].reshape(bs, N_MOD, D_MODEL).transpose(1, 0, 2)
    map_p = lambda j, b, i: (b * N_MOD + j, 0, 0)
    map_s = lambda j, b, i: (j, 0, 0)

    wa = p['w_in'][:, :N_COLS_A]
    wb = p['w_in'][:, N_COLS_A:]
    wts = (p['w_gate'], p['b_gate'].reshape(N_EXPERTS, 1, D_FF),
           p['w_up'], p['b_up'].reshape(N_EXPERTS, 1, D_FF),
           p['w_down'], p['b_down'].reshape(N_EXPERTS, 1, D_MODEL))

    pa, pb = _inproj(x_prompt, mod_p, mod_p, map_p, p['norm1_g'], wa.astype(BF16), wb.astype(BF16), tm, False)
    wb_p, kr_p, kv_p, g = _prep(pa, None, p, tm, True)
    s0 = jnp.zeros((bp, N_HEADS, HEAD_DIM, HEAD_DIM), F32)
    wkv_out, wkv_p = _wkv_scan(*(_pairs_to_groups(a, tp) for a in (wb_p, kr_p, kv_p)), s0, p, tt)
    yb, lru_p = _lru_seq(pb, p, tm)
    route_p = _post(x_prompt, wkv_out, g, yb, mod_p, mod_p, mod_p, map_p, p, tm)
    shift_p = pa[:, -1, :]
    conv_p = pb[:, tp - (CONV_W - 1):, D_B:]

    xs = x_sample.reshape(1, bs, D_MODEL)
    pa_s, pb_s = _inproj(xs, mod_s, mod_s, map_s, p['norm1_g'], wa, wb, bs, True)
    wb_s, kr_s, kv_s, g = _prep(pa_s, state_shift.reshape(1, bs, N_COLS_A), p, bs, False)
    as_seq = lambda a: _pairs_to_groups(a.reshape(bs, N_HEADS, LANES), 1)
    wkv_out, wkv_s = _wkv_scan(as_seq(wb_s), as_seq(kr_s), as_seq(kv_s), state_wkv, p, 1)
    conv0 = state_conv.transpose(1, 0, 2)
    yb, lru_s = _lru_step(pb_s[0], conv0, state_lru, p)
    route_s = _post(xs, wkv_out.reshape(1, bs, D_A), g, yb.reshape(1, bs, D_B),
                    mod_s, mod_s, mod_s, map_s, p, bs)
    y_prompt, y_sample = _moe(route_p, route_s, (mod_p, map_p), (mod_s, map_s), final_g, wts)
    shift_s = pa_s[0]
    conv_s = jnp.concatenate([state_conv[:, 1:], pb_s[0][:, None, D_B:]], axis=1)

    return (y_prompt, y_sample.reshape(bs, 1, D_MODEL),
            wkv_p[None], shift_p[None], conv_p[None], lru_p.reshape(bp, D_B)[None],
            wkv_s[None], shift_s[None], conv_s[None], lru_s[None])


def kernel(x_prompt, x_sample, c_prompt, c_sample, state_wkv, state_shift, state_conv, state_lru, w_ada, b_ada, norm1_g, norm2_g, w_in, rk_mu, rk_w0, rk_w_up, rk_a0, rk_a_up, rk_g_up, rk_k_k, rk_k_a, rk_r_k, rk_lnx_w, rk_lnx_b, lru_conv_w, lru_conv_b, lru_w_r, lru_b_r, lru_w_i, lru_b_i, lru_lambda, lru_norm_g, w_out, router_w, router_b, w_gate, b_gate, w_up, b_up, w_down, b_down, final_g):
    assert w_ada.shape[0] == 1, "single-layer trunk"
    p = dict(w_ada=w_ada[0], b_ada=b_ada[0], norm1_g=norm1_g[0], norm2_g=norm2_g[0], w_in=w_in[0],
             rk_mu=rk_mu[0], rk_w0=rk_w0[0], rk_w_up=rk_w_up[0], rk_a0=rk_a0[0], rk_a_up=rk_a_up[0],
             rk_g_up=rk_g_up[0], rk_k_k=rk_k_k[0], rk_k_a=rk_k_a[0], rk_r_k=rk_r_k[0],
             rk_lnx_w=rk_lnx_w[0], rk_lnx_b=rk_lnx_b[0], lru_conv_w=lru_conv_w[0],
             lru_conv_b=lru_conv_b[0], lru_w_r=lru_w_r[0], lru_b_r=lru_b_r[0], lru_w_i=lru_w_i[0],
             lru_b_i=lru_b_i[0], lru_lambda=lru_lambda[0], lru_norm_g=lru_norm_g[0], w_out=w_out[0],
             router_w=router_w[0], router_b=router_b[0], w_gate=w_gate[0], b_gate=b_gate[0],
             w_up=w_up[0], b_up=b_up[0], w_down=w_down[0], b_down=b_down[0])
    return _forward(x_prompt, x_sample, c_prompt, c_sample, state_wkv[0], state_shift[0], state_conv[0],
                    state_lru[0], p, final_g)
```

```python
import functools

import jax
import jax.numpy as jnp
from jax import lax
from jax.experimental import pallas as pl
from jax.experimental.pallas import tpu as pltpu

F32 = jnp.float32
BF16 = jnp.bfloat16

D_MODEL = 1024
D_A = 512
HEAD_DIM = 64
N_HEADS = 8
D_B = 512
N_BLOCKS_B = 8
CONV_W = 4
LRU_C = 8.0
R_DECAY = 64
R_AAA = 64
R_GATE = 128
N_COLS_A = 3 * D_A + R_DECAY + R_AAA + R_GATE
N_COLS_B = 2 * D_B
N_EXPERTS = 32
TOP_K = 4
D_FF = 1024
SWIGLU_LIMIT = 7.0
SWIGLU_ALPHA = 1.702
RMS_EPS = 1e-6
LN_X_EPS = 64e-5
N_MOD = 6

LANES = 128
SUBLANES = 8
GROUP_BATCH = 8
HALF_ROWS = HEAD_DIM // 2
MOE_ROWS = 256
MOE_CHUNK = 16
TILE_CHUNKS = D_MODEL // LANES
VMEM_LIMIT = 56 * 1024 * 1024


def _cparams(sem):
    return pltpu.CompilerParams(dimension_semantics=sem, vmem_limit_bytes=VMEM_LIMIT)


def _dot(a, b):
    return jnp.dot(a.astype(BF16), b.astype(BF16), preferred_element_type=F32)


def _split(a):
    hi = a.astype(BF16)
    lo = (a - hi.astype(F32)).astype(BF16)
    return hi, lo


def _dot3(a, b):
    ah, al = _split(a)
    bh, bl = _split(b)
    return (jnp.dot(ah, bh, preferred_element_type=F32)
            + (jnp.dot(al, bh, preferred_element_type=F32) + jnp.dot(ah, bl, preferred_element_type=F32)))


def _dot3_nt(a, b):
    dn = (((1,), (1,)), ((), ()))
    ah, al = _split(a)
    bh, bl = _split(b)
    d = lambda x, y: lax.dot_general(x, y, dn, preferred_element_type=F32)
    return d(ah, bh) + (d(al, bh) + d(ah, bl))


def _softplus(x):
    return jnp.maximum(x, 0.0) + jnp.log1p(jnp.exp(-jnp.abs(x)))


def _sigmoid(x):
    return 1.0 / (1.0 + jnp.exp(-x))


def _rms(x, g):
    ms = jnp.mean(x * x, axis=-1, keepdims=True)
    return x * lax.rsqrt(ms + RMS_EPS) * g


def _ada_kernel(c_ref, w_ref, b_ref, o_ref):
    c = c_ref[...]
    s = c * _sigmoid(c)
    o_ref[...] = _dot3(s, w_ref[...]) + b_ref[...]


def _ada(c, w_ada, b_ada):
    rows = c.shape[0]
    ncol = w_ada.shape[1]
    tn = D_MODEL
    return pl.pallas_call(
        _ada_kernel,
        grid=(ncol // tn,),
        in_specs=[pl.BlockSpec((rows, D_MODEL), lambda j: (0, 0)),
                  pl.BlockSpec((D_MODEL, tn), lambda j: (0, j)),
                  pl.BlockSpec((1, tn), lambda j: (0, j))],
        out_specs=pl.BlockSpec((rows, tn), lambda j: (0, j)),
        out_shape=jax.ShapeDtypeStruct((rows, ncol), F32),
        compiler_params=_cparams(("arbitrary",)),
        name="ada_mod",
    )(c, w_ada, b_ada.reshape(1, ncol))


def _inproj_kernel(x_ref, shift_ref, scale_ref, g_ref, wa_ref, wb_ref, pa_ref, pb_ref, *, precise):
    x = x_ref[0]
    h = _rms(x, g_ref[...]) * (1.0 + scale_ref[0]) + shift_ref[0]
    dot = _dot3 if precise else _dot
    pa_ref[0] = dot(h, wa_ref[...])
    pb_ref[0] = dot(h, wb_ref[...])


def _inproj(x, shift, scale, mod_map, g, wa, wb, tm, precise):
    nb, t, _ = x.shape
    mod_block = (1,) + shift.shape[1:]
    return pl.pallas_call(
        functools.partial(_inproj_kernel, precise=precise),
        grid=(nb, t // tm),
        in_specs=[pl.BlockSpec((1, tm, D_MODEL), lambda b, i: (b, i, 0)),
                  pl.BlockSpec(mod_block, functools.partial(mod_map, 0)),
                  pl.BlockSpec(mod_block, functools.partial(mod_map, 1)),
                  pl.BlockSpec((1, D_MODEL), lambda b, i: (0, 0)),
                  pl.BlockSpec((D_MODEL, N_COLS_A), lambda b, i: (0, 0)),
                  pl.BlockSpec((D_MODEL, N_COLS_B), lambda b, i: (0, 0))],
        out_specs=[pl.BlockSpec((1, tm, N_COLS_A), lambda b, i: (b, i, 0)),
                   pl.BlockSpec((1, tm, N_COLS_B), lambda b, i: (b, i, 0))],
        out_shape=[jax.ShapeDtypeStruct((nb, t, N_COLS_A), F32),
                   jax.ShapeDtypeStruct((nb, t, N_COLS_B), F32)],
        compiler_params=_cparams(("arbitrary", "arbitrary")),
        name="norm1_inproj",
    )(x, shift, scale, g.reshape(1, D_MODEL), wa, wb)


def _store_head_pairs(ref, stage_ref, x, y):
    rows = x.shape[0]
    flat = ref.shape[0] == 1
    for h in range(N_HEADS):
        sl = slice(h * HEAD_DIM, (h + 1) * HEAD_DIM)
        pair = jnp.concatenate([x[:, sl], y[:, sl]], axis=1)
        if flat:
            ref[0, pl.ds(h, rows, stride=N_HEADS), :] = pair
        else:
            stage_ref[pl.ds(h, rows, stride=N_HEADS), :] = pair
    if not flat:
        ref[...] = stage_ref[...].reshape(rows, N_HEADS, LANES)


def _prep_kernel(pa_ref, prev_ref, mu_ref, w0_ref, wup_ref, a0_ref, gup_ref, kk_ref, ka_ref,
                 wb_out, kr_out, kv_out, g_out, carry_ref, stage_ref, *, seq):
    pa = pa_ref[0]
    if seq:
        @pl.when(pl.program_id(1) == 0)
        def _():
            carry_ref[...] = jnp.zeros_like(carry_ref)
        rolled = pltpu.roll(pa, 1, axis=0)
        row = lax.broadcasted_iota(jnp.int32, pa.shape, 0)
        prev = jnp.where(row == 0, carry_ref[...], rolled)
        carry_ref[...] = pa[pa.shape[0] - 1:, :]
    else:
        prev = prev_ref[0]
    z = pa + (prev - pa) * mu_ref[...]
    r = z[:, 0:D_A]
    k = z[:, D_A:2 * D_A]
    v = z[:, 2 * D_A:3 * D_A]
    lo = 3 * D_A
    za = z[:, lo:lo + R_DECAY + R_AAA]
    lane = lax.broadcasted_iota(jnp.int32, za.shape, 1)
    za = jnp.where(lane < R_DECAY, jnp.tanh(za), za)
    lw = _dot3(za, wup_ref[...])
    w_log = -_softplus(-(w0_ref[...] + lw[:, :D_A])) - 0.5
    decay = jnp.exp(-jnp.exp(w_log))
    a = _sigmoid(a0_ref[...] + lw[:, D_A:])
    gd = z[:, lo + R_DECAY + R_AAA:]
    g = _dot3(_sigmoid(gd), gup_ref[...])
    kk = k * kk_ref[...]
    _store_head_pairs(wb_out, stage_ref, decay, kk * a)
    _store_head_pairs(kr_out, stage_ref, k * (1.0 + (a - 1.0) * ka_ref[...]), r)
    _store_head_pairs(kv_out, stage_ref, kk, v)
    g_out[0] = g


def _prep(pa, prev, p, tm, seq):
    nb, t, _ = pa.shape
    wup = jnp.zeros((R_DECAY + R_AAA, 2 * D_A), F32)
    wup = wup.at[:R_DECAY, :D_A].set(p['rk_w_up']).at[R_DECAY:, D_A:].set(p['rk_a_up'])
    vec = lambda a: a.reshape(1, -1)
    row_spec = lambda n: pl.BlockSpec((1, n), lambda b, i: (0, 0))
    tile = pl.BlockSpec((1, tm, D_A), lambda b, i: (b, i, 0))
    if seq:
        assert nb == GROUP_BATCH
        pair_tile = pl.BlockSpec((tm, N_HEADS, LANES), lambda b, i: (i, b, 0))
        pair_shape = jax.ShapeDtypeStruct((t, nb * N_HEADS, LANES), F32)
    else:
        pair_tile = pl.BlockSpec((1, tm * N_HEADS, LANES), lambda b, i: (b, i, 0))
        pair_shape = jax.ShapeDtypeStruct((nb, t * N_HEADS, LANES), F32)
    if prev is None:
        prev = jnp.zeros((1, SUBLANES, N_COLS_A), F32)
        prev_spec = pl.BlockSpec((1, SUBLANES, N_COLS_A), lambda b, i: (0, 0, 0))
    else:
        prev_spec = pl.BlockSpec((1, tm, N_COLS_A), lambda b, i: (b, i, 0))
    return pl.pallas_call(
        functools.partial(_prep_kernel, seq=seq),
        grid=(nb, t // tm),
        in_specs=[pl.BlockSpec((1, tm, N_COLS_A), lambda b, i: (b, i, 0)),
                  prev_spec,
                  row_spec(N_COLS_A), row_spec(D_A),
                  pl.BlockSpec((R_DECAY + R_AAA, 2 * D_A), lambda b, i: (0, 0)),
                  row_spec(D_A),
                  pl.BlockSpec((R_GATE, D_A), lambda b, i: (0, 0)),
                  row_spec(D_A), row_spec(D_A)],
        out_specs=[pair_tile] * 3 + [tile],
        out_shape=[pair_shape] * 3 + [jax.ShapeDtypeStruct((nb, t, D_A), F32)],
        scratch_shapes=[pltpu.VMEM((1, N_COLS_A), F32), pltpu.VMEM((tm * N_HEADS, LANES), F32)],
        compiler_params=_cparams(("arbitrary", "arbitrary")),
        name="rwkv_prep",
    )(pa, prev, vec(p['rk_mu']), vec(p['rk_w0']), wup, vec(p['rk_a0']), p['rk_g_up'],
      vec(p['rk_k_k']), vec(p['rk_k_a']))


SCAN_SLOTS = 4


def _scan_kernel(wb_ref, kr_ref, kv_ref, kvnext_ref, s0_ref, lnw_ref, lnb_ref, rk_ref,
                 out_ref, sfin_ref, s_ref, sa_ref, inv_ref, wd_ref, kkd_ref, bd_ref, kd_ref, rd_ref, vd_ref, *, tt):
    ti = pl.program_id(1)
    upper_half = lax.broadcasted_iota(jnp.int32, (HALF_ROWS, LANES), 1) >= LANES // 2
    unroll = SCAN_SLOTS if tt % SCAN_SLOTS == 0 else 1

    def expand(pair):
        return jnp.concatenate([pair, pair], axis=0).T

    def prepare(slot, s, kk_pair=None):
        t1 = expand(wb_ref[0, s])
        wd_ref[slot] = t1[:HEAD_DIM]
        bd_ref[slot] = t1[HEAD_DIM:]
        t2 = expand(kr_ref[0, s])
        kd_ref[slot] = t2[:HEAD_DIM]
        rd_ref[slot] = t2[HEAD_DIM:]
        t3 = expand(kv_ref[0, s])
        vd_ref[slot] = jnp.where(upper_half, t3[HEAD_DIM + HALF_ROWS:], t3[HEAD_DIM:HEAD_DIM + HALF_ROWS])
        kkd_ref[slot] = t3[:HEAD_DIM] if kk_pair is None else expand(kk_pair)[:HEAD_DIM]

    prepare(0, 0)
    if tt > 1:
        prepare(1, 1)
    else:
        kkd_ref[1] = expand(kvnext_ref[0, 0])[:HEAD_DIM]

    def inv_norm2(kk_rows):
        s2 = jnp.sum(kk_rows * kk_rows, axis=0, keepdims=True)
        return 1.0 / jnp.maximum(s2, 1e-24)

    @pl.when(ti == 0)
    def _():
        s_ref[...] = s0_ref[0]
        acc = jnp.zeros((HALF_ROWS, LANES), F32)
        for j in range(HEAD_DIM):
            acc = acc + s0_ref[0, j] * kkd_ref[0, j:j + 1, :]
        sa_ref[...] = acc
        inv_ref[...] = inv_norm2(kkd_ref[0])

    def finish(y, cv):
        tot = jnp.sum(y, axis=0, keepdims=True)
        tot = tot + pltpu.roll(tot, LANES // 2, axis=1)
        d = y - tot * (1.0 / HEAD_DIM)
        sq = jnp.sum(d * d, axis=0, keepdims=True)
        sq = sq + pltpu.roll(sq, LANES // 2, axis=1)
        yn = d * lax.rsqrt(sq * (1.0 / HEAD_DIM) + LN_X_EPS)
        return yn * lnw_ref[...] + lnb_ref[...] + cv

    def step(t, u, carry):
        sa, inv2, y_prev, cv_prev = carry
        nxt_slot = (u + 1) % SCAN_SLOTS
        out_ref[0, jnp.maximum(t - 1, 0)] = finish(y_prev, cv_prev)
        sae = sa * (-inv2)
        v = vd_ref[u]
        acc_y = jnp.zeros((HALF_ROWS, LANES), F32)
        acc_s = jnp.zeros((HALF_ROWS, LANES), F32)
        for j in range(HEAD_DIM):
            row = pl.ds(j, 1)
            s_new = s_ref[j] * wd_ref[u, row, :] + sae * bd_ref[u, row, :] + v * kd_ref[u, row, :]
            s_ref[j] = s_new
            acc_y = acc_y + s_new * rd_ref[u, row, :]
            acc_s = acc_s + s_new * kkd_ref[nxt_slot, row, :]
        c = jnp.sum(rd_ref[u] * kd_ref[u] * rk_ref[...], axis=0, keepdims=True)
        if unroll > 1:
            ahead = jnp.minimum(t + 2, tt - 1)
            kk_pair = None
            if (u + 2) % SCAN_SLOTS == 0:
                kk_pair = jnp.where(t + 2 == tt, kvnext_ref[0, 0], kv_ref[0, ahead])
            prepare((u + 2) % SCAN_SLOTS, ahead, kk_pair)
        return acc_s, inv_norm2(kkd_ref[nxt_slot]), acc_y, c * v

    def steps(q, carry):
        for u in range(unroll):
            carry = step(q * unroll + u, u, carry)
        return carry

    zeros = jnp.zeros((HALF_ROWS, LANES), F32)
    sa, inv2, y_last, cv_last = lax.fori_loop(0, tt // unroll, steps, (sa_ref[...], inv_ref[...], zeros, zeros))
    out_ref[0, tt - 1] = finish(y_last, cv_last)
    sa_ref[...] = sa
    inv_ref[...] = inv2

    @pl.when(ti == pl.num_programs(1) - 1)
    def _():
        sfin_ref[0] = s_ref[...]


def _from_scan_rows(y):
    g, t = y.shape[:2]
    x = y.reshape(g, t, HALF_ROWS, 2, GROUP_BATCH, N_HEADS).transpose(0, 4, 1, 5, 3, 2)
    return x.reshape(g * GROUP_BATCH, t, D_A)


def _state_to_scan(s):
    g = s.shape[0] // GROUP_BATCH
    y = s.reshape(g, GROUP_BATCH, N_HEADS, 2, HALF_ROWS, HEAD_DIM).transpose(0, 5, 4, 3, 1, 2)
    return y.reshape(g, HEAD_DIM, HALF_ROWS, LANES)


def _state_from_scan(y):
    g = y.shape[0]
    s = y.reshape(g, HEAD_DIM, HALF_ROWS, 2, GROUP_BATCH, N_HEADS).transpose(0, 4, 5, 3, 2, 1)
    return s.reshape(g * GROUP_BATCH, N_HEADS, HEAD_DIM, HEAD_DIM)


def _head_rows(x):
    y = x.reshape(N_HEADS, 2, HALF_ROWS).transpose(2, 1, 0)
    y = jnp.broadcast_to(y[:, :, None, :], (HALF_ROWS, 2, GROUP_BATCH, N_HEADS))
    return y.reshape(HALF_ROWS, LANES)


def _head_keys(x):
    y = jnp.broadcast_to(x.T[:, None, None, :], (HEAD_DIM, 2, GROUP_BATCH, N_HEADS))
    return y.reshape(HEAD_DIM, LANES)


def _pairs_to_groups(x, t):
    nb = x.shape[0]
    g = nb // GROUP_BATCH
    y = x.reshape(g, GROUP_BATCH, t, N_HEADS, LANES).transpose(0, 2, 1, 3, 4)
    return y.reshape(g, t, GROUP_BATCH * N_HEADS, LANES)


def _wkv_scan(wb, kr, kv, s0, p, tt):
    g, t = wb.shape[:2]
    pair_tile = pl.BlockSpec((1, tt, HEAD_DIM, LANES), lambda gi, i: (gi, i, 0, 0))
    next_step = pl.BlockSpec((1, 1, HEAD_DIM, LANES), lambda gi, i: (gi, jnp.minimum((i + 1) * tt, t - 1), 0, 0))
    row_tile = pl.BlockSpec((1, tt, HALF_ROWS, LANES), lambda gi, i: (gi, i, 0, 0))
    state = pl.BlockSpec((1, HEAD_DIM, HALF_ROWS, LANES), lambda gi, i: (gi, 0, 0, 0))
    const = lambda n: pl.BlockSpec((n, LANES), lambda gi, i: (0, 0))
    keys = pltpu.VMEM((SCAN_SLOTS, HEAD_DIM, LANES), F32)
    out, sfin = pl.pallas_call(
        functools.partial(_scan_kernel, tt=tt),
        grid=(g, t // tt),
        in_specs=[pair_tile, pair_tile, pair_tile, next_step,
                  state, const(HALF_ROWS), const(HALF_ROWS), const(HEAD_DIM)],
        out_specs=[row_tile, state],
        out_shape=[jax.ShapeDtypeStruct((g, t, HALF_ROWS, LANES), F32),
                   jax.ShapeDtypeStruct((g, HEAD_DIM, HALF_ROWS, LANES), F32)],
        scratch_shapes=[pltpu.VMEM((HEAD_DIM, HALF_ROWS, LANES), F32),
                        pltpu.VMEM((HALF_ROWS, LANES), F32),
                        pltpu.VMEM((1, LANES), F32),
                        keys, keys, keys, keys, keys,
                        pltpu.VMEM((SCAN_SLOTS, HALF_ROWS, LANES), F32)],
        compiler_params=_cparams(("arbitrary", "arbitrary")),
        name="wkv_scan",
    )(wb, kr, kv, kv, _state_to_scan(s0),
      _head_rows(p['rk_lnx_w']), _head_rows(p['rk_lnx_b']), _head_keys(p['rk_r_k']))
    return _from_scan_rows(out), _state_from_scan(sfin)


def _gelu(x):
    return 0.5 * x * (1.0 + jnp.tanh(0.7978845608028654 * (x + 0.044715 * (x * x * x))))


def _lru_gates(xc, wri_ref, bri_ref, nsl_ref, precise):
    dot = _dot3 if precise else _dot
    gates = _sigmoid(dot(xc, wri_ref[...]) + bri_ref[...])
    gate_r = gates[:, :D_B]
    gate_i = gates[:, D_B:]
    log_a = gate_r * nsl_ref[...]
    a = jnp.exp(log_a)
    th = jnp.tanh(log_a)
    one_minus_a2 = -2.0 * th / (1.0 - th)
    bt = jnp.sqrt(one_minus_a2) * (gate_i * xc)
    return a, bt


def _lru_seq_kernel(pb_ref, cw_ref, cb_ref, wri_ref, bri_ref, nsl_ref, ng_ref,
                    out_ref, hlast_ref, xprev_ref, h_ref):
    tm = pb_ref.shape[1]

    @pl.when(pl.program_id(1) == 0)
    def _():
        xprev_ref[...] = jnp.zeros_like(xprev_ref)
        h_ref[...] = jnp.zeros_like(h_ref)

    pb = pb_ref[0]
    yb = pb[:, :D_B]
    xb = pb[:, D_B:]
    xprev = xprev_ref[...]
    row8 = lax.broadcasted_iota(jnp.int32, (SUBLANES, D_B), 0)

    def shifted(d):
        rolled = pltpu.roll(xb, d, axis=0)
        top = jnp.where(row8 < d, pltpu.roll(xprev, d, axis=0), rolled[:SUBLANES])
        return jnp.concatenate([top, rolled[SUBLANES:]], axis=0)

    xc = cb_ref[...] + cw_ref[3:4, :] * xb
    for d in range(1, CONV_W):
        xc = xc + cw_ref[3 - d:4 - d, :] * shifted(d)
    xprev_ref[...] = xb[tm - SUBLANES:, :]

    a, x = _lru_gates(xc, wri_ref, bri_ref, nsl_ref, False)
    row = lax.broadcasted_iota(jnp.int32, (tm, D_B), 0)
    d = 1
    while d < tm:
        keep = row >= d
        a_s = jnp.where(keep, pltpu.roll(a, d, axis=0), 1.0)
        x_s = jnp.where(keep, pltpu.roll(x, d, axis=0), 0.0)
        x = a * x_s + x
        a = a * a_s
        d *= 2
    h = a * h_ref[...] + x
    h_ref[...] = h[tm - 1:, :]
    hlast_ref[0] = h[tm - 1:, :]
    out_ref[0] = _rms(h * _gelu(yb), ng_ref[...]).astype(out_ref.dtype)


def _lru_params(p):
    eye = jnp.eye(N_BLOCKS_B, dtype=F32)
    bd = lambda w: (eye[:, None, :, None] * w[:, :, None, :]).reshape(D_B, D_B)
    wri = jnp.concatenate([bd(p['lru_w_r']), bd(p['lru_w_i'])], axis=1)
    bri = jnp.concatenate([p['lru_b_r'], p['lru_b_i']]).reshape(1, 2 * D_B)
    nsl = (-LRU_C * jax.nn.softplus(-p['lru_lambda'])).reshape(1, D_B)
    return wri, bri, nsl


def _lru_seq(pb, p, tm):
    nb, t, _ = pb.shape
    wri, bri, nsl = _lru_params(p)
    row_spec = lambda n: pl.BlockSpec((1, n), lambda b, i: (0, 0))
    return pl.pallas_call(
        _lru_seq_kernel,
        grid=(nb, t // tm),
        in_specs=[pl.BlockSpec((1, tm, N_COLS_B), lambda b, i: (b, i, 0)),
                  pl.BlockSpec((CONV_W, D_B), lambda b, i: (0, 0)),
                  row_spec(D_B),
                  pl.BlockSpec((D_B, 2 * D_B), lambda b, i: (0, 0)),
                  row_spec(2 * D_B), row_spec(D_B), row_spec(D_B)],
        out_specs=[pl.BlockSpec((1, tm, D_B), lambda b, i: (b, i, 0)),
                   pl.BlockSpec((1, 1, D_B), lambda b, i: (b, 0, 0))],
        out_shape=[jax.ShapeDtypeStruct((nb, t, D_B), BF16),
                   jax.ShapeDtypeStruct((nb, 1, D_B), F32)],
        scratch_shapes=[pltpu.VMEM((SUBLANES, D_B), F32), pltpu.VMEM((1, D_B), F32)],
        compiler_params=_cparams(("arbitrary", "arbitrary")),
        name="rglru_seq",
    )(pb, p['lru_conv_w'], p['lru_conv_b'].reshape(1, D_B), wri.astype(BF16), bri, nsl,
      p['lru_norm_g'].reshape(1, D_B))


def _lru_step_kernel(pb_ref, conv_ref, h0_ref, cw_ref, cb_ref, wri_ref, bri_ref, nsl_ref, ng_ref,
                     out_ref, hnew_ref):
    pb = pb_ref[...]
    yb = pb[:, :D_B]
    xb = pb[:, D_B:]
    xc = cb_ref[...] + cw_ref[3:4, :] * xb
    for j in range(CONV_W - 1):
        xc = xc + cw_ref[j:j + 1, :] * conv_ref[j]
    a, x = _lru_gates(xc, wri_ref, bri_ref, nsl_ref, True)
    h = a * h0_ref[...] + x
    hnew_ref[...] = h
    out_ref[...] = _rms(h * _gelu(yb), ng_ref[...]).astype(out_ref.dtype)


def _lru_step(pb, conv0, h0, p):
    n = pb.shape[0]
    wri, bri, nsl = _lru_params(p)
    return pl.pallas_call(
        _lru_step_kernel,
        out_shape=[jax.ShapeDtypeStruct((n, D_B), BF16), jax.ShapeDtypeStruct((n, D_B), F32)],
        compiler_params=pltpu.CompilerParams(vmem_limit_bytes=VMEM_LIMIT),
        name="rglru_step",
    )(pb, conv0, h0, p['lru_conv_w'], p['lru_conv_b'].reshape(1, D_B), wri, bri, nsl,
      p['lru_norm_g'].reshape(1, D_B))


def _to_token_tiles(ref, x):
    rows = x.shape[0]
    for c in range(TILE_CHUNKS):
        ref[pl.ds(c, rows, stride=TILE_CHUNKS), :] = x[:, c * LANES:(c + 1) * LANES]


def _from_token_tiles(ref, row0, rows):
    return jnp.concatenate(
        [ref[pl.ds(row0 + c, rows, stride=TILE_CHUNKS), :] for c in range(TILE_CHUNKS)], axis=1)


def _post_kernel(x_ref, wkv_ref, g_ref, yb_ref, gate1_ref, shift2_ref, scale2_ref, n2_ref,
                 wo_ref, rw_ref, rb_ref, tri_ref, x1_ref, h2_ref, ti_ref, rk_ref, tg_ref, cnt_ref):
    ya = (wkv_ref[0] * g_ref[0]).astype(BF16)
    mixed = (jnp.dot(ya, wo_ref[:D_A, :], preferred_element_type=F32)
             + jnp.dot(yb_ref[0], wo_ref[D_A:, :], preferred_element_type=F32))
    x1 = x_ref[0] + gate1_ref[0] * mixed
    x1_ref[0] = x1
    h2 = _rms(x1, n2_ref[...]) * (1.0 + scale2_ref[0]) + shift2_ref[0]
    _to_token_tiles(h2_ref, h2)
    logits = _dot3_nt(rw_ref[...], h2) + rb_ref[...]
    eidx = lax.broadcasted_iota(jnp.int32, logits.shape, 0)
    vals, idxs = [], []
    cur = logits
    for _ in range(TOP_K):
        m = jnp.max(cur, axis=0, keepdims=True)
        i = jnp.min(jnp.where(cur == m, eidx, N_EXPERTS), axis=0, keepdims=True)
        vals.append(m)
        idxs.append(i)
        cur = jnp.where(eidx == i, -jnp.inf, cur)
    ex = [jnp.exp(v - vals[0]) for v in vals]
    den = ex[0] + ex[1] + ex[2] + ex[3]
    sel = [eidx == i for i in idxs]
    onehot = (sel[0] | sel[1] | sel[2] | sel[3]).astype(F32)
    incl = jnp.dot(onehot.astype(BF16), tri_ref[...], preferred_element_type=F32)
    rank = incl - onehot
    cnt_ref[0] = jnp.broadcast_to(jnp.sum(onehot, axis=1, keepdims=True), cnt_ref.shape[1:])
    for k in range(TOP_K):
        ti_ref[0, k:k + 1, :] = idxs[k]
        rk_ref[0, k:k + 1, :] = jnp.sum(jnp.where(sel[k], rank, 0.0), axis=0, keepdims=True).astype(jnp.int32)
        tg_ref[0, k:k + 1, :] = ex[k] / den


def _post(x, wkv, g, yb, gate1, shift2, scale2, mod_map, p, tm):
    nb, t, _ = x.shape
    mod_block = (1,) + gate1.shape[1:]
    tile = lambda n: pl.BlockSpec((1, tm, n), lambda b, i: (b, i, 0))
    full = lambda a: pl.BlockSpec(a.shape, lambda b, i: (0,) * a.ndim)
    mspec = lambda j: pl.BlockSpec(mod_block, functools.partial(mod_map, j))
    n2 = p['norm2_g'].reshape(1, D_MODEL)
    wo = p['w_out'].astype(BF16)
    rw = p['router_w'].T
    rb = p['router_b'].reshape(N_EXPERTS, 1)
    tri = jnp.triu(jnp.ones((tm, tm), BF16))
    nt = t // tm
    n = nb * t
    topk = pl.BlockSpec((1, TOP_K, tm), lambda b, i: (b * nt + i, 0, 0))
    topk_i = jax.ShapeDtypeStruct((nb * nt, TOP_K, tm), jnp.int32)
    x1, h2, ti, rk, tg, cnt = pl.pallas_call(
        _post_kernel,
        grid=(nb, nt),
        in_specs=[tile(D_MODEL), tile(D_A), tile(D_A), tile(D_B), mspec(2), mspec(3), mspec(4),
                  full(n2), full(wo), full(rw), full(rb), full(tri)],
        out_specs=[tile(D_MODEL),
                   pl.BlockSpec((tm * TILE_CHUNKS, LANES), lambda b, i: (b * nt + i, 0)),
                   topk, topk, topk,
                   pl.BlockSpec((1, N_EXPERTS, LANES), lambda b, i: (b * nt + i, 0, 0))],
        out_shape=[jax.ShapeDtypeStruct((nb, t, D_MODEL), F32),
                   jax.ShapeDtypeStruct((n * TILE_CHUNKS, LANES), F32),
                   topk_i, topk_i,
                   jax.ShapeDtypeStruct((nb * nt, TOP_K, tm), F32),
                   jax.ShapeDtypeStruct((nb * nt, N_EXPERTS, LANES), F32)],
        compiler_params=_cparams(("arbitrary", "arbitrary")),
        name="outproj_router",
    )(x, wkv, g, yb, gate1, shift2, scale2, n2, wo, rw, rb, tri)
    return x1, h2, ti, rk, tg, cnt[:, :, 0].astype(jnp.int32)


def _plan(routes):
    cnt = jnp.concatenate([r[2] for r in routes], axis=0)
    total = jnp.sum(cnt, axis=0)
    run_start = jnp.cumsum(cnt, axis=0) - cnt
    padded = (total + MOE_ROWS - 1) // MOE_ROWS * MOE_ROWS
    pad_end = jnp.cumsum(padded)
    pad_start = pad_end - padded
    loc_off = jnp.cumsum(cnt, axis=1) - cnt
    dst0 = pad_start[None, :] + run_start
    n_tokens = sum(r[0].shape[0] * r[0].shape[2] for r in routes)
    n_blocks = -(-n_tokens * TOP_K // MOE_ROWS) + N_EXPERTS
    block_row0 = jnp.arange(n_blocks, dtype=jnp.int32) * MOE_ROWS
    block_e = jnp.minimum(jnp.sum(pad_end[None, :] <= block_row0[:, None], axis=1), N_EXPERTS - 1)
    n_used = (pad_end[-1] // MOE_ROWS).reshape(1)
    as_i32 = lambda a: a.astype(jnp.int32)
    experts = jnp.arange(N_EXPERTS, dtype=jnp.int32)
    groups, t0 = [], 0
    for ti, rk, c in routes:
        nt = ti.shape[0]
        sl = slice(t0, t0 + nt)
        off = jnp.sum(jnp.where(ti[..., None] == experts, loc_off[sl, None, None, :], 0), axis=-1)
        lpos = as_i32(rk + off).reshape(nt, -1)
        table = as_i32(jnp.stack([cnt[sl], loc_off[sl], dst0[sl]], axis=0).reshape(3, -1))
        groups.append((lpos, table))
        t0 += nt
    return groups, as_i32(block_e), as_i32(n_used), as_i32(pad_end), as_i32(padded), n_blocks


def _tile_rows(row):
    return pl.ds(pl.multiple_of(row * TILE_CHUNKS, TILE_CHUNKS), TILE_CHUNKS)


def _piece_rows(row0, n_rows):
    return pl.ds(pl.multiple_of(row0 * TILE_CHUNKS, TILE_CHUNKS), n_rows * TILE_CHUNKS)


def _block_rows(block):
    rows = MOE_ROWS * TILE_CHUNKS
    return pl.ds(pl.multiple_of(block * rows, rows), rows)


def _for_each_piece(table_ref, tile, n_tiles, fn):
    def per_expert(e, carry):
        col = tile * N_EXPERTS + e
        cnt = table_ref[col]
        lo = table_ref[n_tiles * N_EXPERTS + col]
        d0 = table_ref[2 * n_tiles * N_EXPERTS + col]

        def full(c, c2):
            fn(lo + c * MOE_CHUNK, d0 + c * MOE_CHUNK, MOE_CHUNK)
            return c2
        lax.fori_loop(0, cnt // MOE_CHUNK, full, 0)
        bit = MOE_CHUNK // 2
        while bit >= 1:
            done = cnt - cnt % (2 * bit)

            @pl.when((cnt & bit) != 0)
            def _(done=done, bit=bit):
                fn(lo + done, d0 + done, bit)
            bit //= 2
        return carry
    lax.fori_loop(0, N_EXPERTS, per_expert, 0)


def _scatter_kernel(*refs, first):
    idx_ref, loc_ref, zero_ref, sem, isem, zsem = refs[-6:]
    table_ref, pad_end_ref, padded_ref, h2_ref, lpos_hbm = refs[:5]
    xs_hbm = refs[-7]
    i = pl.program_id(0)
    n_tiles = pl.num_programs(0)
    tm = h2_ref.shape[0] // TILE_CHUNKS
    slot = i % 2

    def idx_copy(tile, s):
        dst = idx_ref.at[pl.ds(s * (TOP_K * tm), TOP_K * tm)]
        return pltpu.make_async_copy(lpos_hbm.at[tile], dst, isem.at[s])

    def drain(s):
        for _ in range(TOP_K):
            pltpu.make_async_copy(h2_ref, xs_hbm.at[pl.ds(0, tm * TILE_CHUNKS), :], sem.at[s]).wait()

    @pl.when(i == 0)
    def _():
        idx_copy(0, 0).start()

    if first:
        @pl.when(i == 0)
        def _():
            zero_ref[...] = jnp.zeros_like(zero_ref)
            n_used = pad_end_ref[N_EXPERTS - 1] // MOE_ROWS
            n_blocks = xs_hbm.shape[0] // (MOE_ROWS * TILE_CHUNKS)

            def fill(e, n_started):
                n_fill = jnp.minimum(padded_ref[e] // MOE_ROWS, 2)
                last = pad_end_ref[e] // MOE_ROWS - 1

                def one(j, c):
                    pltpu.make_async_copy(zero_ref, xs_hbm.at[_block_rows(last - j), :], zsem).start()
                    return c
                lax.fori_loop(0, n_fill, one, 0)
                return n_started + n_fill
            n_started = lax.fori_loop(0, N_EXPERTS, fill, jnp.int32(0))

            def fill_tail(blk, c):
                pltpu.make_async_copy(zero_ref, xs_hbm.at[_block_rows(blk), :], zsem).start()
                return c
            lax.fori_loop(n_used, n_blocks, fill_tail, 0)

            def drain_fill(j, c):
                pltpu.make_async_copy(zero_ref, xs_hbm.at[_block_rows(0), :], zsem).wait()
                return c
            lax.fori_loop(0, n_started + (n_blocks - n_used), drain_fill, 0)

    @pl.when(i + 1 < n_tiles)
    def _():
        idx_copy(i + 1, 1 - slot).start()

    @pl.when(i >= 2)
    def _():
        drain(slot)
    idx_copy(i, slot).wait()
    base = slot * (TOP_K * tm)

    def place(r, c):
        row = h2_ref[_tile_rows(r), :]
        for k in range(TOP_K):
            loc_ref[slot, _tile_rows(idx_ref[base + k * tm + r]), :] = row
        return c
    lax.fori_loop(0, tm, place, 0, unroll=8)

    def send(lo, d0, n):
        pltpu.make_async_copy(loc_ref.at[slot, _piece_rows(lo, n), :], xs_hbm.at[_piece_rows(d0, n), :],
                              sem.at[slot]).start()
    _for_each_piece(table_ref, i, n_tiles, send)

    @pl.when(i == n_tiles - 1)
    def _():
        @pl.when(i >= 1)
        def _():
            drain(1 - slot)
        drain(slot)


def _scatter(h2, lpos, table, pad_end, padded, n_rows, tm, xs=None):
    nt = lpos.shape[0]
    first = xs is None
    assert first or nt * tm <= MOE_ROWS, "a later call may add at most MOE_ROWS rows per expert"
    loc_rows = TOP_K * tm * TILE_CHUNKS
    in_specs = [pl.BlockSpec((tm * TILE_CHUNKS, LANES), lambda i, *_: (i, 0)),
                pl.BlockSpec(memory_space=pl.ANY)]
    args = [table.reshape(-1), pad_end, padded, h2, lpos]
    if not first:
        in_specs.append(pl.BlockSpec(memory_space=pl.ANY))
        args.append(xs)
    grid_spec = pltpu.PrefetchScalarGridSpec(
        num_scalar_prefetch=3,
        grid=(nt,),
        in_specs=in_specs,
        out_specs=pl.BlockSpec(memory_space=pl.ANY),
        scratch_shapes=[pltpu.SMEM((2 * TOP_K * tm,), jnp.int32),
                        pltpu.VMEM((2, loc_rows, LANES), F32),
                        pltpu.VMEM((MOE_ROWS * TILE_CHUNKS, LANES), F32),
                        pltpu.SemaphoreType.DMA((2,)), pltpu.SemaphoreType.DMA((2,)), pltpu.SemaphoreType.DMA],
    )
    return pl.pallas_call(
        functools.partial(_scatter_kernel, first=first),
        grid_spec=grid_spec,
        out_shape=jax.ShapeDtypeStruct((n_rows * TILE_CHUNKS, LANES), F32),
        input_output_aliases={} if first else {5: 0},
        compiler_params=_cparams(("arbitrary",)),
        name="moe_dispatch",
    )(*args)


def _mlp_kernel(be_ref, nu_ref, xs_ref, wg_ref, bg_ref, wu_ref, bu_ref, wd_ref, bd_ref, out_ref,
                wg16_ref, wu16_ref, wd16_ref):
    i = pl.program_id(0)
    used = i < nu_ref[0]

    @pl.when(used & ((i == 0) | (be_ref[i] != be_ref[jnp.maximum(i - 1, 0)])))
    def _():
        wg16_ref[...] = wg_ref[0].astype(BF16)
        wu16_ref[...] = wu_ref[0].astype(BF16)
        wd16_ref[...] = wd_ref[0].astype(BF16)

    @pl.when(used)
    def _():
        x = _from_token_tiles(xs_ref, 0, MOE_ROWS).astype(BF16)
        gt = jnp.dot(x, wg16_ref[...], preferred_element_type=F32) + bg_ref[0]
        up = jnp.dot(x, wu16_ref[...], preferred_element_type=F32) + bu_ref[0]
        gt = jnp.minimum(gt, SWIGLU_LIMIT)
        up = jnp.clip(up, -SWIGLU_LIMIT, SWIGLU_LIMIT)
        glu = gt * _sigmoid(gt * SWIGLU_ALPHA)
        mid = ((up + 1.0) * glu).astype(BF16)
        _to_token_tiles(out_ref, jnp.dot(mid, wd16_ref[...], preferred_element_type=F32) + bd_ref[0])

    @pl.when(jnp.logical_not(used))
    def _():
        out_ref[...] = jnp.zeros_like(out_ref)


def _mlp(xs, block_e, n_used, n_blocks, wts):
    wg, bg, wu, bu, wd, bd = wts
    row_map = lambda i, be, nu: (i, 0)
    w_map = lambda i, be, nu: (be[i], 0, 0)
    wspec = pl.BlockSpec((1, D_MODEL, D_FF), w_map)
    bspec = pl.BlockSpec((1, 1, D_FF), w_map)
    rows = pl.BlockSpec((MOE_ROWS * TILE_CHUNKS, LANES), row_map)
    grid_spec = pltpu.PrefetchScalarGridSpec(
        num_scalar_prefetch=2,
        grid=(n_blocks,),
        in_specs=[rows, wspec, bspec, wspec, bspec, wspec, bspec],
        out_specs=rows,
        scratch_shapes=[pltpu.VMEM((D_MODEL, D_FF), BF16)] * 3,
    )
    return pl.pallas_call(
        _mlp_kernel,
        grid_spec=grid_spec,
        out_shape=jax.ShapeDtypeStruct(xs.shape, F32),
        compiler_params=_cparams(("arbitrary",)),
        name="moe_experts",
    )(block_e, n_used, xs, wg, bg, wu, bu, wd, bd)


def _gather_kernel(table_ref, x1_ref, gate2_ref, fg_ref, lpos_hbm, tg_hbm, rows_hbm, y_ref,
                   idx_ref, gsm_ref, loc_ref, ff_ref, sem, isem):
    i = pl.program_id(1) + pl.program_id(0) * pl.num_programs(1)
    n_tiles = pl.num_programs(0) * pl.num_programs(1)
    tm = x1_ref.shape[1]
    slot = i % 2

    def meta_copies(tile, s):
        seg = pl.ds(s * (TOP_K * tm), TOP_K * tm)
        return (pltpu.make_async_copy(lpos_hbm.at[tile], idx_ref.at[seg], isem.at[s]),
                pltpu.make_async_copy(tg_hbm.at[tile], gsm_ref.at[seg], isem.at[s]))

    def fetch_tile(tile, s):
        for c in meta_copies(tile, s):
            c.start()

        def fetch(lo, d0, n):
            pltpu.make_async_copy(rows_hbm.at[_piece_rows(d0, n), :], loc_ref.at[s, _piece_rows(lo, n), :],
                                  sem.at[s]).start()
        _for_each_piece(table_ref, tile, n_tiles, fetch)

    @pl.when(i == 0)
    def _():
        fetch_tile(0, 0)

    @pl.when(i + 1 < n_tiles)
    def _():
        fetch_tile(i + 1, 1 - slot)

    pltpu.make_async_copy(rows_hbm.at[pl.ds(0, TOP_K * tm * TILE_CHUNKS), :], loc_ref.at[slot], sem.at[slot]).wait()
    for c in meta_copies(i, slot):
        c.wait()
    base = slot * (TOP_K * tm)

    def mix(r, c):
        acc = gsm_ref[base + r] * loc_ref[slot, _tile_rows(idx_ref[base + r]), :]
        for k in range(1, TOP_K):
            acc = acc + gsm_ref[base + k * tm + r] * loc_ref[slot, _tile_rows(idx_ref[base + k * tm + r]), :]
        ff_ref[_tile_rows(r), :] = acc
        return c
    lax.fori_loop(0, tm, mix, 0, unroll=8)

    x2 = x1_ref[0] + gate2_ref[0] * _from_token_tiles(ff_ref, 0, tm)
    y_ref[0] = _rms(x2, fg_ref[...])


def _gather(x1, tg, gate2, mod_map, final_g, lpos, table, rows, tm):
    nb, t, _ = x1.shape
    nt = t // tm
    mod_block = (1,) + gate2.shape[1:]
    loc_rows = TOP_K * tm * TILE_CHUNKS
    grid_spec = pltpu.PrefetchScalarGridSpec(
        num_scalar_prefetch=1,
        grid=(nb, nt),
        in_specs=[pl.BlockSpec((1, tm, D_MODEL), lambda b, i, *_: (b, i, 0)),
                  pl.BlockSpec(mod_block, lambda b, i, *_: mod_map(5, b, i)),
                  pl.BlockSpec((1, D_MODEL), lambda b, i, *_: (0, 0)),
                  pl.BlockSpec(memory_space=pl.ANY),
                  pl.BlockSpec(memory_space=pl.ANY),
                  pl.BlockSpec(memory_space=pl.ANY)],
        out_specs=pl.BlockSpec((1, tm, D_MODEL), lambda b, i, *_: (b, i, 0)),
        scratch_shapes=[pltpu.SMEM((2 * TOP_K * tm,), jnp.int32),
                        pltpu.SMEM((2 * TOP_K * tm,), F32),
                        pltpu.VMEM((2, loc_rows, LANES), F32),
                        pltpu.VMEM((tm * TILE_CHUNKS, LANES), F32),
                        pltpu.SemaphoreType.DMA((2,)), pltpu.SemaphoreType.DMA((2,))],
    )
    return pl.pallas_call(
        _gather_kernel,
        grid_spec=grid_spec,
        out_shape=jax.ShapeDtypeStruct((nb, t, D_MODEL), F32),
        compiler_params=_cparams(("arbitrary", "arbitrary")),
        name="moe_combine",
    )(table.reshape(-1), x1, gate2, final_g.reshape(1, D_MODEL), lpos, tg.reshape(tg.shape[0], -1), rows)


def _moe(route_p, route_s, mod_p, mod_s, final_g, wts):
    routes = [route_p, route_s]
    groups, block_e, n_used, pad_end, padded, n_blocks = _plan([(r[2], r[3], r[5]) for r in routes])
    xs = None
    for (x1, h2, ti, rk, tg, cnt), (lpos, table) in zip(routes, groups):
        xs = _scatter(h2, lpos, table, pad_end, padded, n_blocks * MOE_ROWS, ti.shape[2], xs)
    rows = _mlp(xs, block_e, n_used, n_blocks, wts)
    outs = []
    for (x1, h2, ti, rk, tg, cnt), (lpos, table), (mod, mod_map) in zip(routes, groups, (mod_p, mod_s)):
        outs.append(_gather(x1, tg, mod, mod_map, final_g, lpos, table, rows, ti.shape[2]))
    return outs


def _forward(x_prompt, x_sample, c_prompt, c_sample, state_wkv, state_shift, state_conv, state_lru, p, final_g):
    bp, tp, _ = x_prompt.shape
    bs = x_sample.shape[0]
    tm = min(512, tp)
    tt = min(32, tp)

    mod = _ada(jnp.concatenate([c_prompt, c_sample], axis=0), p['w_ada'], p['b_ada'])
    mod_p = mod[:bp].reshape(bp * N_MOD, 1, D_MODEL)
    mod_s = mod[bp:].reshape(bs, N_MOD, D_MODEL).transpose(1, 0, 2)
    map_p = lambda j, b, i: (b * N_MOD + j, 0, 0)
    map_s = lambda j, b, i: (j, 0, 0)

    wa = p['w_in'][:, :N_COLS_A]
    wb = p['w_in'][:, N_COLS_A:]
    wts = (p['w_gate'], p['b_gate'].reshape(N_EXPERTS, 1, D_FF),
           p['w_up'], p['b_up'].reshape(N_EXPERTS, 1, D_FF),
           p['w_down'], p['b_down'].reshape(N_EXPERTS, 1, D_MODEL))

    pa, pb = _inproj(x_prompt, mod_p, mod_p, map_p, p['norm1_g'], wa.astype(BF16), wb.astype(BF16), tm, False)
    wb_p, kr_p, kv_p, g = _prep(pa, None, p, tm, True)
    s0 = jnp.zeros((bp, N_HEADS, HEAD_DIM, HEAD_DIM), F32)
    wkv_out, wkv_p = _wkv_scan(wb_p[None], kr_p[None], kv_p[None], s0, p, tt)
    yb, lru_p = _lru_seq(pb, p, tm)
    route_p = _post(x_prompt, wkv_out, g, yb, mod_p, mod_p, mod_p, map_p, p, tm)
    shift_p = pa[:, -1, :]
    conv_p = pb[:, tp - (CONV_W - 1):, D_B:]

    xs = x_sample.reshape(1, bs, D_MODEL)
    pa_s, pb_s = _inproj(xs, mod_s, mod_s, map_s, p['norm1_g'], wa, wb, bs, True)
    wb_s, kr_s, kv_s, g = _prep(pa_s, state_shift.reshape(1, bs, N_COLS_A), p, bs, False)
    as_seq = lambda a: _pairs_to_groups(a.reshape(bs, N_HEADS, LANES), 1)
    wkv_out, wkv_s = _wkv_scan(as_seq(wb_s), as_seq(kr_s), as_seq(kv_s), state_wkv, p, 1)
    conv0 = state_conv.transpose(1, 0, 2)
    yb, lru_s = _lru_step(pb_s[0], conv0, state_lru, p)
    route_s = _post(xs, wkv_out.reshape(1, bs, D_A), g, yb.reshape(1, bs, D_B),
                    mod_s, mod_s, mod_s, map_s, p, bs)
    y_prompt, y_sample = _moe(route_p, route_s, (mod_p, map_p), (mod_s, map_s), final_g, wts)
    shift_s = pa_s[0]
    conv_s = jnp.concatenate([state_conv[:, 1:], pb_s[0][:, None, D_B:]], axis=1)

    return (y_prompt, y_sample.reshape(bs, 1, D_MODEL),
            wkv_p[None], shift_p[None], conv_p[None], lru_p.reshape(bp, D_B)[None],
            wkv_s[None], shift_s[None], conv_s[None], lru_s[None])


def kernel(x_prompt, x_sample, c_prompt, c_sample, state_wkv, state_shift, state_conv, state_lru, w_ada, b_ada, norm1_g, norm2_g, w_in, rk_mu, rk_w0, rk_w_up, rk_a0, rk_a_up, rk_g_up, rk_k_k, rk_k_a, rk_r_k, rk_lnx_w, rk_lnx_b, lru_conv_w, lru_conv_b, lru_w_r, lru_b_r, lru_w_i, lru_b_i, lru_lambda, lru_norm_g, w_out, router_w, router_b, w_gate, b_gate, w_up, b_up, w_down, b_down, final_g):
    assert w_ada.shape[0] == 1, "single-layer trunk"
    p = dict(w_ada=w_ada[0], b_ada=b_ada[0], norm1_g=norm1_g[0], norm2_g=norm2_g[0], w_in=w_in[0],
             rk_mu=rk_mu[0], rk_w0=rk_w0[0], rk_w_up=rk_w_up[0], rk_a0=rk_a0[0], rk_a_up=rk_a_up[0],
             rk_g_up=rk_g_up[0], rk_k_k=rk_k_k[0], rk_k_a=rk_k_a[0], rk_r_k=rk_r_k[0],
             rk_lnx_w=rk_lnx_w[0], rk_lnx_b=rk_lnx_b[0], lru_conv_w=lru_conv_w[0],
             lru_conv_b=lru_conv_b[0], lru_w_r=lru_w_r[0], lru_b_r=lru_b_r[0], lru_w_i=lru_w_i[0],
             lru_b_i=lru_b_i[0], lru_lambda=lru_lambda[0], lru_norm_g=lru_norm_g[0], w_out=w_out[0],
             router_w=router_w[0], router_b=router_b[0], w_gate=w_gate[0], b_gate=b_gate[0],
             w_up=w_up[0], b_up=b_up[0], w_down=w_down[0], b_down=b_down[0])
    return _forward(x_prompt, x_sample, c_prompt, c_sample, state_wkv[0], state_shift[0], state_conv[0],
                    state_lru[0], p, final_g)
```

```python
import functools

import jax
import jax.numpy as jnp
from jax import lax
from jax.experimental import pallas as pl
from jax.experimental.pallas import tpu as pltpu

F32 = jnp.float32
BF16 = jnp.bfloat16

D_MODEL = 1024
D_A = 512
HEAD_DIM = 64
N_HEADS = 8
D_B = 512
N_BLOCKS_B = 8
CONV_W = 4
LRU_C = 8.0
R_DECAY = 64
R_AAA = 64
R_GATE = 128
N_COLS_A = 3 * D_A + R_DECAY + R_AAA + R_GATE
N_COLS_B = 2 * D_B
N_EXPERTS = 32
TOP_K = 4
D_FF = 1024
SWIGLU_LIMIT = 7.0
SWIGLU_ALPHA = 1.702
RMS_EPS = 1e-6
LN_X_EPS = 64e-5
N_MOD = 6

LANES = 128
SUBLANES = 8
GROUP_BATCH = 8
HALF_ROWS = HEAD_DIM // 2
MOE_ROWS = 256
MOE_CHUNK = 16
TILE_CHUNKS = D_MODEL // LANES
VMEM_LIMIT = 56 * 1024 * 1024


def _cparams(sem):
    return pltpu.CompilerParams(dimension_semantics=sem, vmem_limit_bytes=VMEM_LIMIT)


def _dot(a, b):
    return jnp.dot(a.astype(BF16), b.astype(BF16), preferred_element_type=F32)


def _split(a):
    hi = a.astype(BF16)
    lo = (a - hi.astype(F32)).astype(BF16)
    return hi, lo


def _dot3(a, b):
    ah, al = _split(a)
    bh, bl = _split(b)
    return (jnp.dot(ah, bh, preferred_element_type=F32)
            + (jnp.dot(al, bh, preferred_element_type=F32) + jnp.dot(ah, bl, preferred_element_type=F32)))


def _dot3_nt(a, b):
    dn = (((1,), (1,)), ((), ()))
    ah, al = _split(a)
    bh, bl = _split(b)
    d = lambda x, y: lax.dot_general(x, y, dn, preferred_element_type=F32)
    return d(ah, bh) + (d(al, bh) + d(ah, bl))


def _softplus(x):
    return jnp.maximum(x, 0.0) + jnp.log1p(jnp.exp(-jnp.abs(x)))


def _sigmoid(x):
    return 1.0 / (1.0 + jnp.exp(-x))


def _rms(x, g):
    ms = jnp.mean(x * x, axis=-1, keepdims=True)
    return x * lax.rsqrt(ms + RMS_EPS) * g


def _ada_kernel(c_ref, w_ref, b_ref, o_ref):
    c = c_ref[...]
    s = c * _sigmoid(c)
    o_ref[...] = _dot3(s, w_ref[...]) + b_ref[...]


def _ada(c, w_ada, b_ada):
    rows = c.shape[0]
    ncol = w_ada.shape[1]
    tn = D_MODEL
    return pl.pallas_call(
        _ada_kernel,
        grid=(ncol // tn,),
        in_specs=[pl.BlockSpec((rows, D_MODEL), lambda j: (0, 0)),
                  pl.BlockSpec((D_MODEL, tn), lambda j: (0, j)),
                  pl.BlockSpec((1, tn), lambda j: (0, j))],
        out_specs=pl.BlockSpec((rows, tn), lambda j: (0, j)),
        out_shape=jax.ShapeDtypeStruct((rows, ncol), F32),
        compiler_params=_cparams(("arbitrary",)),
        name="ada_mod",
    )(c, w_ada, b_ada.reshape(1, ncol))


def _inproj_kernel(x_ref, shift_ref, scale_ref, g_ref, wa_ref, wb_ref, pa_ref, pb_ref, *, precise):
    x = x_ref[0]
    h = _rms(x, g_ref[...]) * (1.0 + scale_ref[0]) + shift_ref[0]
    dot = _dot3 if precise else _dot
    pa_ref[0] = dot(h, wa_ref[...])
    pb_ref[0] = dot(h, wb_ref[...])


def _inproj(x, shift, scale, mod_map, g, wa, wb, tm, precise):
    nb, t, _ = x.shape
    mod_block = (1,) + shift.shape[1:]
    return pl.pallas_call(
        functools.partial(_inproj_kernel, precise=precise),
        grid=(nb, t // tm),
        in_specs=[pl.BlockSpec((1, tm, D_MODEL), lambda b, i: (b, i, 0)),
                  pl.BlockSpec(mod_block, functools.partial(mod_map, 0)),
                  pl.BlockSpec(mod_block, functools.partial(mod_map, 1)),
                  pl.BlockSpec((1, D_MODEL), lambda b, i: (0, 0)),
                  pl.BlockSpec((D_MODEL, N_COLS_A), lambda b, i: (0, 0)),
                  pl.BlockSpec((D_MODEL, N_COLS_B), lambda b, i: (0, 0))],
        out_specs=[pl.BlockSpec((1, tm, N_COLS_A), lambda b, i: (b, i, 0)),
                   pl.BlockSpec((1, tm, N_COLS_B), lambda b, i: (b, i, 0))],
        out_shape=[jax.ShapeDtypeStruct((nb, t, N_COLS_A), F32),
                   jax.ShapeDtypeStruct((nb, t, N_COLS_B), F32)],
        compiler_params=_cparams(("arbitrary", "arbitrary")),
        name="norm1_inproj",
    )(x, shift, scale, g.reshape(1, D_MODEL), wa, wb)


def _store_head_pairs(ref, stage_ref, x, y):
    rows = x.shape[0]
    flat = ref.shape[0] == 1
    for h in range(N_HEADS):
        sl = slice(h * HEAD_DIM, (h + 1) * HEAD_DIM)
        pair = jnp.concatenate([x[:, sl], y[:, sl]], axis=1)
        if flat:
            ref[0, pl.ds(h, rows, stride=N_HEADS), :] = pair
        else:
            stage_ref[pl.ds(h, rows, stride=N_HEADS), :] = pair
    if not flat:
        ref[...] = stage_ref[...].reshape(rows, N_HEADS, LANES)


def _prep_kernel(pa_ref, prev_ref, mu_ref, w0_ref, wup_ref, a0_ref, gup_ref, kk_ref, ka_ref,
                 wb_out, kr_out, kv_out, g_out, carry_ref, stage_ref, *, seq):
    pa = pa_ref[0]
    if seq:
        @pl.when(pl.program_id(1) == 0)
        def _():
            carry_ref[...] = jnp.zeros_like(carry_ref)
        rolled = pltpu.roll(pa, 1, axis=0)
        row = lax.broadcasted_iota(jnp.int32, pa.shape, 0)
        prev = jnp.where(row == 0, carry_ref[...], rolled)
        carry_ref[...] = pa[pa.shape[0] - 1:, :]
    else:
        prev = prev_ref[0]
    z = pa + (prev - pa) * mu_ref[...]
    r = z[:, 0:D_A]
    k = z[:, D_A:2 * D_A]
    v = z[:, 2 * D_A:3 * D_A]
    lo = 3 * D_A
    za = z[:, lo:lo + R_DECAY + R_AAA]
    lane = lax.broadcasted_iota(jnp.int32, za.shape, 1)
    za = jnp.where(lane < R_DECAY, jnp.tanh(za), za)
    lw = _dot3(za, wup_ref[...])
    w_log = -_softplus(-(w0_ref[...] + lw[:, :D_A])) - 0.5
    decay = jnp.exp(-jnp.exp(w_log))
    a = _sigmoid(a0_ref[...] + lw[:, D_A:])
    gd = z[:, lo + R_DECAY + R_AAA:]
    g = _dot3(_sigmoid(gd), gup_ref[...])
    kk = k * kk_ref[...]
    _store_head_pairs(wb_out, stage_ref, decay, kk * a)
    _store_head_pairs(kr_out, stage_ref, k * (1.0 + (a - 1.0) * ka_ref[...]), r)
    _store_head_pairs(kv_out, stage_ref, kk, v)
    g_out[0] = g


def _prep(pa, prev, p, tm, seq):
    nb, t, _ = pa.shape
    wup = jnp.zeros((R_DECAY + R_AAA, 2 * D_A), F32)
    wup = wup.at[:R_DECAY, :D_A].set(p['rk_w_up']).at[R_DECAY:, D_A:].set(p['rk_a_up'])
    vec = lambda a: a.reshape(1, -1)
    row_spec = lambda n: pl.BlockSpec((1, n), lambda b, i: (0, 0))
    tile = pl.BlockSpec((1, tm, D_A), lambda b, i: (b, i, 0))
    if seq:
        assert nb == GROUP_BATCH
        pair_tile = pl.BlockSpec((tm, N_HEADS, LANES), lambda b, i: (i, b, 0))
        pair_shape = jax.ShapeDtypeStruct((t, nb * N_HEADS, LANES), F32)
    else:
        pair_tile = pl.BlockSpec((1, tm * N_HEADS, LANES), lambda b, i: (b, i, 0))
        pair_shape = jax.ShapeDtypeStruct((nb, t * N_HEADS, LANES), F32)
    if prev is None:
        prev = jnp.zeros((1, SUBLANES, N_COLS_A), F32)
        prev_spec = pl.BlockSpec((1, SUBLANES, N_COLS_A), lambda b, i: (0, 0, 0))
    else:
        prev_spec = pl.BlockSpec((1, tm, N_COLS_A), lambda b, i: (b, i, 0))
    return pl.pallas_call(
        functools.partial(_prep_kernel, seq=seq),
        grid=(nb, t // tm),
        in_specs=[pl.BlockSpec((1, tm, N_COLS_A), lambda b, i: (b, i, 0)),
                  prev_spec,
                  row_spec(N_COLS_A), row_spec(D_A),
                  pl.BlockSpec((R_DECAY + R_AAA, 2 * D_A), lambda b, i: (0, 0)),
                  row_spec(D_A),
                  pl.BlockSpec((R_GATE, D_A), lambda b, i: (0, 0)),
                  row_spec(D_A), row_spec(D_A)],
        out_specs=[pair_tile] * 3 + [tile],
        out_shape=[pair_shape] * 3 + [jax.ShapeDtypeStruct((nb, t, D_A), F32)],
        scratch_shapes=[pltpu.VMEM((1, N_COLS_A), F32), pltpu.VMEM((tm * N_HEADS, LANES), F32)],
        compiler_params=_cparams(("arbitrary", "arbitrary")),
        name="rwkv_prep",
    )(pa, prev, vec(p['rk_mu']), vec(p['rk_w0']), wup, vec(p['rk_a0']), p['rk_g_up'],
      vec(p['rk_k_k']), vec(p['rk_k_a']))


SCAN_SLOTS = 4


def _scan_kernel(wb_ref, kr_ref, kv_ref, kvnext_ref, s0_ref, lnw_ref, lnb_ref, rk_ref,
                 out_ref, sfin_ref, s_ref, sa_ref, inv_ref, wd_ref, kkd_ref, bd_ref, kd_ref, rd_ref, vd_ref, *, tt):
    ti = pl.program_id(1)
    upper_half = lax.broadcasted_iota(jnp.int32, (HALF_ROWS, LANES), 1) >= LANES // 2
    unroll = SCAN_SLOTS if tt % SCAN_SLOTS == 0 else 1

    def expand(pair):
        return jnp.concatenate([pair, pair], axis=0).T

    def prepare(slot, s, kk_pair=None):
        t1 = expand(wb_ref[0, s])
        wd_ref[slot] = t1[:HEAD_DIM]
        bd_ref[slot] = t1[HEAD_DIM:]
        t2 = expand(kr_ref[0, s])
        kd_ref[slot] = t2[:HEAD_DIM]
        rd_ref[slot] = t2[HEAD_DIM:]
        t3 = expand(kv_ref[0, s])
        vd_ref[slot] = jnp.where(upper_half, t3[HEAD_DIM + HALF_ROWS:], t3[HEAD_DIM:HEAD_DIM + HALF_ROWS])
        kkd_ref[slot] = t3[:HEAD_DIM] if kk_pair is None else expand(kk_pair)[:HEAD_DIM]

    prepare(0, 0)
    if tt > 1:
        prepare(1, 1)
    else:
        kkd_ref[1] = expand(kvnext_ref[0, 0])[:HEAD_DIM]

    def inv_norm2(kk_rows):
        s2 = jnp.sum(kk_rows * kk_rows, axis=0, keepdims=True)
        return 1.0 / jnp.maximum(s2, 1e-24)

    @pl.when(ti == 0)
    def _():
        s_ref[...] = s0_ref[0]
        acc = jnp.zeros((HALF_ROWS, LANES), F32)
        for j in range(HEAD_DIM):
            acc = acc + s0_ref[0, j] * kkd_ref[0, j:j + 1, :]
        sa_ref[...] = acc
        inv_ref[...] = inv_norm2(kkd_ref[0])

    def finish(y, cv):
        tot = jnp.sum(y, axis=0, keepdims=True)
        tot = tot + pltpu.roll(tot, LANES // 2, axis=1)
        d = y - tot * (1.0 / HEAD_DIM)
        sq = jnp.sum(d * d, axis=0, keepdims=True)
        sq = sq + pltpu.roll(sq, LANES // 2, axis=1)
        yn = d * lax.rsqrt(sq * (1.0 / HEAD_DIM) + LN_X_EPS)
        return yn * lnw_ref[...] + lnb_ref[...] + cv

    def step(t, u, carry):
        sa, inv2, y_prev, cv_prev = carry
        nxt_slot = (u + 1) % SCAN_SLOTS
        out_ref[0, jnp.maximum(t - 1, 0)] = finish(y_prev, cv_prev)
        sae = sa * (-inv2)
        v = vd_ref[u]
        acc_y = jnp.zeros((HALF_ROWS, LANES), F32)
        acc_s = jnp.zeros((HALF_ROWS, LANES), F32)
        for j in range(HEAD_DIM):
            row = pl.ds(j, 1)
            s_new = s_ref[j] * wd_ref[u, row, :] + sae * bd_ref[u, row, :] + v * kd_ref[u, row, :]
            s_ref[j] = s_new
            acc_y = acc_y + s_new * rd_ref[u, row, :]
            acc_s = acc_s + s_new * kkd_ref[nxt_slot, row, :]
        c = jnp.sum(rd_ref[u] * kd_ref[u] * rk_ref[...], axis=0, keepdims=True)
        if unroll > 1:
            ahead = jnp.minimum(t + 2, tt - 1)
            kk_pair = None
            if (u + 2) % SCAN_SLOTS == 0:
                kk_pair = jnp.where(t + 2 == tt, kvnext_ref[0, 0], kv_ref[0, ahead])
            prepare((u + 2) % SCAN_SLOTS, ahead, kk_pair)
        return acc_s, inv_norm2(kkd_ref[nxt_slot]), acc_y, c * v

    def steps(q, carry):
        for u in range(unroll):
            carry = step(q * unroll + u, u, carry)
        return carry

    zeros = jnp.zeros((HALF_ROWS, LANES), F32)
    sa, inv2, y_last, cv_last = lax.fori_loop(0, tt // unroll, steps, (sa_ref[...], inv_ref[...], zeros, zeros))
    out_ref[0, tt - 1] = finish(y_last, cv_last)
    sa_ref[...] = sa
    inv_ref[...] = inv2

    @pl.when(ti == pl.num_programs(1) - 1)
    def _():
        sfin_ref[0] = s_ref[...]


def _from_scan_rows(y):
    g, t = y.shape[:2]
    x = y.reshape(g, t, HALF_ROWS, 2, GROUP_BATCH, N_HEADS).transpose(0, 4, 1, 5, 3, 2)
    return x.reshape(g * GROUP_BATCH, t, D_A)


def _state_to_scan(s):
    g = s.shape[0] // GROUP_BATCH
    y = s.reshape(g, GROUP_BATCH, N_HEADS, 2, HALF_ROWS, HEAD_DIM).transpose(0, 5, 4, 3, 1, 2)
    return y.reshape(g, HEAD_DIM, HALF_ROWS, LANES)


def _state_from_scan(y):
    g = y.shape[0]
    s = y.reshape(g, HEAD_DIM, HALF_ROWS, 2, GROUP_BATCH, N_HEADS).transpose(0, 4, 5, 3, 2, 1)
    return s.reshape(g * GROUP_BATCH, N_HEADS, HEAD_DIM, HEAD_DIM)


def _head_rows(x):
    y = x.reshape(N_HEADS, 2, HALF_ROWS).transpose(2, 1, 0)
    y = jnp.broadcast_to(y[:, :, None, :], (HALF_ROWS, 2, GROUP_BATCH, N_HEADS))
    return y.reshape(HALF_ROWS, LANES)


def _head_keys(x):
    y = jnp.broadcast_to(x.T[:, None, None, :], (HEAD_DIM, 2, GROUP_BATCH, N_HEADS))
    return y.reshape(HEAD_DIM, LANES)


def _pairs_to_groups(x, t):
    nb = x.shape[0]
    g = nb // GROUP_BATCH
    y = x.reshape(g, GROUP_BATCH, t, N_HEADS, LANES).transpose(0, 2, 1, 3, 4)
    return y.reshape(g, t, GROUP_BATCH * N_HEADS, LANES)


def _wkv_scan(wb, kr, kv, s0, p, tt):
    g, t = wb.shape[:2]
    pair_tile = pl.BlockSpec((1, tt, HEAD_DIM, LANES), lambda gi, i: (gi, i, 0, 0))
    next_step = pl.BlockSpec((1, 1, HEAD_DIM, LANES), lambda gi, i: (gi, jnp.minimum((i + 1) * tt, t - 1), 0, 0))
    row_tile = pl.BlockSpec((1, tt, HALF_ROWS, LANES), lambda gi, i: (gi, i, 0, 0))
    state = pl.BlockSpec((1, HEAD_DIM, HALF_ROWS, LANES), lambda gi, i: (gi, 0, 0, 0))
    const = lambda n: pl.BlockSpec((n, LANES), lambda gi, i: (0, 0))
    keys = pltpu.VMEM((SCAN_SLOTS, HEAD_DIM, LANES), F32)
    out, sfin = pl.pallas_call(
        functools.partial(_scan_kernel, tt=tt),
        grid=(g, t // tt),
        in_specs=[pair_tile, pair_tile, pair_tile, next_step,
                  state, const(HALF_ROWS), const(HALF_ROWS), const(HEAD_DIM)],
        out_specs=[row_tile, state],
        out_shape=[jax.ShapeDtypeStruct((g, t, HALF_ROWS, LANES), F32),
                   jax.ShapeDtypeStruct((g, HEAD_DIM, HALF_ROWS, LANES), F32)],
        scratch_shapes=[pltpu.VMEM((HEAD_DIM, HALF_ROWS, LANES), F32),
                        pltpu.VMEM((HALF_ROWS, LANES), F32),
                        pltpu.VMEM((1, LANES), F32),
                        keys, keys, keys, keys, keys,
                        pltpu.VMEM((SCAN_SLOTS, HALF_ROWS, LANES), F32)],
        compiler_params=_cparams(("arbitrary", "arbitrary")),
        name="wkv_scan",
    )(wb, kr, kv, kv, _state_to_scan(s0),
      _head_rows(p['rk_lnx_w']), _head_rows(p['rk_lnx_b']), _head_keys(p['rk_r_k']))
    return _from_scan_rows(out), _state_from_scan(sfin)


def _gelu(x):
    return 0.5 * x * (1.0 + jnp.tanh(0.7978845608028654 * (x + 0.044715 * (x * x * x))))


def _lru_gates(xc, wri_ref, bri_ref, nsl_ref, precise):
    dot = _dot3 if precise else _dot
    gates = _sigmoid(dot(xc, wri_ref[...]) + bri_ref[...])
    gate_r = gates[:, :D_B]
    gate_i = gates[:, D_B:]
    log_a = gate_r * nsl_ref[...]
    a = jnp.exp(log_a)
    th = jnp.tanh(log_a)
    one_minus_a2 = -2.0 * th / (1.0 - th)
    bt = jnp.sqrt(one_minus_a2) * (gate_i * xc)
    return a, bt


def _lru_seq_kernel(pb_ref, cw_ref, cb_ref, wri_ref, bri_ref, nsl_ref, ng_ref,
                    out_ref, hlast_ref, xprev_ref, h_ref):
    tm = pb_ref.shape[1]

    @pl.when(pl.program_id(1) == 0)
    def _():
        xprev_ref[...] = jnp.zeros_like(xprev_ref)
        h_ref[...] = jnp.zeros_like(h_ref)

    pb = pb_ref[0]
    yb = pb[:, :D_B]
    xb = pb[:, D_B:]
    xprev = xprev_ref[...]
    row8 = lax.broadcasted_iota(jnp.int32, (SUBLANES, D_B), 0)

    def shifted(d):
        rolled = pltpu.roll(xb, d, axis=0)
        top = jnp.where(row8 < d, pltpu.roll(xprev, d, axis=0), rolled[:SUBLANES])
        return jnp.concatenate([top, rolled[SUBLANES:]], axis=0)

    xc = cb_ref[...] + cw_ref[3:4, :] * xb
    for d in range(1, CONV_W):
        xc = xc + cw_ref[3 - d:4 - d, :] * shifted(d)
    xprev_ref[...] = xb[tm - SUBLANES:, :]

    a, x = _lru_gates(xc, wri_ref, bri_ref, nsl_ref, False)
    row = lax.broadcasted_iota(jnp.int32, (tm, D_B), 0)
    d = 1
    while d < tm:
        keep = row >= d
        a_s = jnp.where(keep, pltpu.roll(a, d, axis=0), 1.0)
        x_s = jnp.where(keep, pltpu.roll(x, d, axis=0), 0.0)
        x = a * x_s + x
        a = a * a_s
        d *= 2
    h = a * h_ref[...] + x
    h_ref[...] = h[tm - 1:, :]
    hlast_ref[0] = h[tm - 1:, :]
    out_ref[0] = _rms(h * _gelu(yb), ng_ref[...]).astype(out_ref.dtype)


def _lru_params(p):
    eye = jnp.eye(N_BLOCKS_B, dtype=F32)
    bd = lambda w: (eye[:, None, :, None] * w[:, :, None, :]).reshape(D_B, D_B)
    wri = jnp.concatenate([bd(p['lru_w_r']), bd(p['lru_w_i'])], axis=1)
    bri = jnp.concatenate([p['lru_b_r'], p['lru_b_i']]).reshape(1, 2 * D_B)
    nsl = (-LRU_C * jax.nn.softplus(-p['lru_lambda'])).reshape(1, D_B)
    return wri, bri, nsl


def _lru_seq(pb, p, tm):
    nb, t, _ = pb.shape
    wri, bri, nsl = _lru_params(p)
    row_spec = lambda n: pl.BlockSpec((1, n), lambda b, i: (0, 0))
    return pl.pallas_call(
        _lru_seq_kernel,
        grid=(nb, t // tm),
        in_specs=[pl.BlockSpec((1, tm, N_COLS_B), lambda b, i: (b, i, 0)),
                  pl.BlockSpec((CONV_W, D_B), lambda b, i: (0, 0)),
                  row_spec(D_B),
                  pl.BlockSpec((D_B, 2 * D_B), lambda b, i: (0, 0)),
                  row_spec(2 * D_B), row_spec(D_B), row_spec(D_B)],
        out_specs=[pl.BlockSpec((1, tm, D_B), lambda b, i: (b, i, 0)),
                   pl.BlockSpec((1, 1, D_B), lambda b, i: (b, 0, 0))],
        out_shape=[jax.ShapeDtypeStruct((nb, t, D_B), BF16),
                   jax.ShapeDtypeStruct((nb, 1, D_B), F32)],
        scratch_shapes=[pltpu.VMEM((SUBLANES, D_B), F32), pltpu.VMEM((1, D_B), F32)],
        compiler_params=_cparams(("arbitrary", "arbitrary")),
        name="rglru_seq",
    )(pb, p['lru_conv_w'], p['lru_conv_b'].reshape(1, D_B), wri.astype(BF16), bri, nsl,
      p['lru_norm_g'].reshape(1, D_B))


def _lru_step_kernel(pb_ref, conv_ref, h0_ref, cw_ref, cb_ref, wri_ref, bri_ref, nsl_ref, ng_ref,
                     out_ref, hnew_ref):
    pb = pb_ref[...]
    yb = pb[:, :D_B]
    xb = pb[:, D_B:]
    xc = cb_ref[...] + cw_ref[3:4, :] * xb
    for j in range(CONV_W - 1):
        xc = xc + cw_ref[j:j + 1, :] * conv_ref[j]
    a, x = _lru_gates(xc, wri_ref, bri_ref, nsl_ref, True)
    h = a * h0_ref[...] + x
    hnew_ref[...] = h
    out_ref[...] = _rms(h * _gelu(yb), ng_ref[...]).astype(out_ref.dtype)


def _lru_step(pb, conv0, h0, p):
    n = pb.shape[0]
    wri, bri, nsl = _lru_params(p)
    return pl.pallas_call(
        _lru_step_kernel,
        out_shape=[jax.ShapeDtypeStruct((n, D_B), BF16), jax.ShapeDtypeStruct((n, D_B), F32)],
        compiler_params=pltpu.CompilerParams(vmem_limit_bytes=VMEM_LIMIT),
        name="rglru_step",
    )(pb, conv0, h0, p['lru_conv_w'], p['lru_conv_b'].reshape(1, D_B), wri, bri, nsl,
      p['lru_norm_g'].reshape(1, D_B))


def _to_token_tiles(ref, x):
    rows = x.shape[0]
    for c in range(TILE_CHUNKS):
        ref[pl.ds(c, rows, stride=TILE_CHUNKS), :] = x[:, c * LANES:(c + 1) * LANES]


def _from_token_tiles(ref, row0, rows):
    return jnp.concatenate(
        [ref[pl.ds(row0 + c, rows, stride=TILE_CHUNKS), :] for c in range(TILE_CHUNKS)], axis=1)


def _post_kernel(x_ref, wkv_ref, g_ref, yb_ref, gate1_ref, shift2_ref, scale2_ref, n2_ref,
                 wo_ref, rw_ref, rb_ref, tri_ref, x1_ref, h2_ref, ti_ref, rk_ref, tg_ref, cnt_ref):
    ya = (wkv_ref[0] * g_ref[0]).astype(BF16)
    mixed = (jnp.dot(ya, wo_ref[:D_A, :], preferred_element_type=F32)
             + jnp.dot(yb_ref[0], wo_ref[D_A:, :], preferred_element_type=F32))
    x1 = x_ref[0] + gate1_ref[0] * mixed
    x1_ref[0] = x1
    h2 = _rms(x1, n2_ref[...]) * (1.0 + scale2_ref[0]) + shift2_ref[0]
    _to_token_tiles(h2_ref, h2)
    logits = _dot3_nt(rw_ref[...], h2) + rb_ref[...]
    eidx = lax.broadcasted_iota(jnp.int32, logits.shape, 0)
    vals, idxs = [], []
    cur = logits
    for _ in range(TOP_K):
        m = jnp.max(cur, axis=0, keepdims=True)
        i = jnp.min(jnp.where(cur == m, eidx, N_EXPERTS), axis=0, keepdims=True)
        vals.append(m)
        idxs.append(i)
        cur = jnp.where(eidx == i, -jnp.inf, cur)
    ex = [jnp.exp(v - vals[0]) for v in vals]
    den = ex[0] + ex[1] + ex[2] + ex[3]
    sel = [eidx == i for i in idxs]
    onehot = (sel[0] | sel[1] | sel[2] | sel[3]).astype(F32)
    incl = jnp.dot(onehot.astype(BF16), tri_ref[...], preferred_element_type=F32)
    rank = incl - onehot
    cnt_ref[0] = jnp.broadcast_to(jnp.sum(onehot, axis=1, keepdims=True), cnt_ref.shape[1:])
    for k in range(TOP_K):
        ti_ref[0, k:k + 1, :] = idxs[k]
        rk_ref[0, k:k + 1, :] = jnp.sum(jnp.where(sel[k], rank, 0.0), axis=0, keepdims=True).astype(jnp.int32)
        tg_ref[0, k:k + 1, :] = ex[k] / den


def _post(x, wkv, g, yb, gate1, shift2, scale2, mod_map, p, tm):
    nb, t, _ = x.shape
    mod_block = (1,) + gate1.shape[1:]
    tile = lambda n: pl.BlockSpec((1, tm, n), lambda b, i: (b, i, 0))
    full = lambda a: pl.BlockSpec(a.shape, lambda b, i: (0,) * a.ndim)
    mspec = lambda j: pl.BlockSpec(mod_block, functools.partial(mod_map, j))
    n2 = p['norm2_g'].reshape(1, D_MODEL)
    wo = p['w_out'].astype(BF16)
    rw = p['router_w'].T
    rb = p['router_b'].reshape(N_EXPERTS, 1)
    tri = jnp.triu(jnp.ones((tm, tm), BF16))
    nt = t // tm
    n = nb * t
    topk = pl.BlockSpec((1, TOP_K, tm), lambda b, i: (b * nt + i, 0, 0))
    topk_i = jax.ShapeDtypeStruct((nb * nt, TOP_K, tm), jnp.int32)
    x1, h2, ti, rk, tg, cnt = pl.pallas_call(
        _post_kernel,
        grid=(nb, nt),
        in_specs=[tile(D_MODEL), tile(D_A), tile(D_A), tile(D_B), mspec(2), mspec(3), mspec(4),
                  full(n2), full(wo), full(rw), full(rb), full(tri)],
        out_specs=[tile(D_MODEL),
                   pl.BlockSpec((tm * TILE_CHUNKS, LANES), lambda b, i: (b * nt + i, 0)),
                   topk, topk, topk,
                   pl.BlockSpec((1, N_EXPERTS, LANES), lambda b, i: (b * nt + i, 0, 0))],
        out_shape=[jax.ShapeDtypeStruct((nb, t, D_MODEL), F32),
                   jax.ShapeDtypeStruct((n * TILE_CHUNKS, LANES), F32),
                   topk_i, topk_i,
                   jax.ShapeDtypeStruct((nb * nt, TOP_K, tm), F32),
                   jax.ShapeDtypeStruct((nb * nt, N_EXPERTS, LANES), F32)],
        compiler_params=_cparams(("arbitrary", "arbitrary")),
        name="outproj_router",
    )(x, wkv, g, yb, gate1, shift2, scale2, n2, wo, rw, rb, tri)
    return x1, h2, ti, rk, tg, cnt[:, :, 0].astype(jnp.int32)


def _plan(routes):
    cnt = jnp.concatenate([r[2] for r in routes], axis=0)
    total = jnp.sum(cnt, axis=0)
    run_start = jnp.cumsum(cnt, axis=0) - cnt
    padded = (total + MOE_ROWS - 1) // MOE_ROWS * MOE_ROWS
    pad_end = jnp.cumsum(padded)
    pad_start = pad_end - padded
    loc_off = jnp.cumsum(cnt, axis=1) - cnt
    dst0 = pad_start[None, :] + run_start
    n_tokens = sum(r[0].shape[0] * r[0].shape[2] for r in routes)
    n_blocks = -(-n_tokens * TOP_K // MOE_ROWS) + N_EXPERTS
    block_row0 = jnp.arange(n_blocks, dtype=jnp.int32) * MOE_ROWS
    block_e = jnp.minimum(jnp.sum(pad_end[None, :] <= block_row0[:, None], axis=1), N_EXPERTS - 1)
    n_used = (pad_end[-1] // MOE_ROWS).reshape(1)
    as_i32 = lambda a: a.astype(jnp.int32)
    experts = jnp.arange(N_EXPERTS, dtype=jnp.int32)
    groups, t0 = [], 0
    for ti, rk, c in routes:
        nt = ti.shape[0]
        sl = slice(t0, t0 + nt)
        off = jnp.sum(jnp.where(ti[..., None] == experts, loc_off[sl, None, None, :], 0), axis=-1)
        lpos = as_i32((rk + off) * TILE_CHUNKS).reshape(nt, -1)
        table = as_i32(jnp.stack([cnt[sl], loc_off[sl], dst0[sl]], axis=0).reshape(3, -1))
        groups.append((lpos, table))
        t0 += nt
    return groups, as_i32(block_e), as_i32(n_used), as_i32(pad_end), as_i32(padded), n_blocks


def _tile_rows(row):
    return pl.ds(pl.multiple_of(row * TILE_CHUNKS, TILE_CHUNKS), TILE_CHUNKS)


def _sublane_rows(row8):
    return pl.ds(pl.multiple_of(row8, TILE_CHUNKS), TILE_CHUNKS)


def _piece_rows(row0, n_rows):
    return pl.ds(pl.multiple_of(row0 * TILE_CHUNKS, TILE_CHUNKS), n_rows * TILE_CHUNKS)


def _block_rows(block):
    rows = MOE_ROWS * TILE_CHUNKS
    return pl.ds(pl.multiple_of(block * rows, rows), rows)


def _for_each_piece(table_ref, tile, n_tiles, fn):
    def per_expert(e, carry):
        col = tile * N_EXPERTS + e
        cnt = table_ref[col]
        lo = table_ref[n_tiles * N_EXPERTS + col]
        d0 = table_ref[2 * n_tiles * N_EXPERTS + col]

        def full(c, c2):
            fn(lo + c * MOE_CHUNK, d0 + c * MOE_CHUNK, MOE_CHUNK)
            return c2
        lax.fori_loop(0, cnt // MOE_CHUNK, full, 0)
        bit = MOE_CHUNK // 2
        while bit >= 1:
            done = cnt - cnt % (2 * bit)

            @pl.when((cnt & bit) != 0)
            def _(done=done, bit=bit):
                fn(lo + done, d0 + done, bit)
            bit //= 2
        return carry
    lax.fori_loop(0, N_EXPERTS, per_expert, 0)


def _scatter_kernel(*refs, first):
    idx_ref, loc_ref, zero_ref, sem, isem, zsem = refs[-6:]
    table_ref, pad_end_ref, padded_ref, h2_ref, lpos_hbm = refs[:5]
    xs_hbm = refs[-7]
    i = pl.program_id(0)
    n_tiles = pl.num_programs(0)
    tm = h2_ref.shape[0] // TILE_CHUNKS
    slot = i % 2

    def idx_copy(tile, s):
        dst = idx_ref.at[pl.ds(s * (TOP_K * tm), TOP_K * tm)]
        return pltpu.make_async_copy(lpos_hbm.at[tile], dst, isem.at[s])

    def drain(s):
        for _ in range(TOP_K):
            pltpu.make_async_copy(h2_ref, xs_hbm.at[pl.ds(0, tm * TILE_CHUNKS), :], sem.at[s]).wait()

    @pl.when(i == 0)
    def _():
        idx_copy(0, 0).start()

    if first:
        @pl.when(i == 0)
        def _():
            zero_ref[...] = jnp.zeros_like(zero_ref)
            n_used = pad_end_ref[N_EXPERTS - 1] // MOE_ROWS
            n_blocks = xs_hbm.shape[0] // (MOE_ROWS * TILE_CHUNKS)

            def fill(e, n_started):
                n_fill = jnp.minimum(padded_ref[e] // MOE_ROWS, 2)
                last = pad_end_ref[e] // MOE_ROWS - 1

                def one(j, c):
                    pltpu.make_async_copy(zero_ref, xs_hbm.at[_block_rows(last - j), :], zsem).start()
                    return c
                lax.fori_loop(0, n_fill, one, 0)
                return n_started + n_fill
            n_started = lax.fori_loop(0, N_EXPERTS, fill, jnp.int32(0))

            def fill_tail(blk, c):
                pltpu.make_async_copy(zero_ref, xs_hbm.at[_block_rows(blk), :], zsem).start()
                return c
            lax.fori_loop(n_used, n_blocks, fill_tail, 0)

            def drain_fill(j, c):
                pltpu.make_async_copy(zero_ref, xs_hbm.at[_block_rows(0), :], zsem).wait()
                return c
            lax.fori_loop(0, n_started + (n_blocks - n_used), drain_fill, 0)

    def step(s):
        @pl.when(i + 1 < n_tiles)
        def _():
            idx_copy(i + 1, 1 - s).start()

        @pl.when(i >= 2)
        def _():
            drain(s)
        idx_copy(i, s).wait()
        base = s * (TOP_K * tm)

        def place(r, c):
            row = h2_ref[_tile_rows(r), :]
            for k in range(TOP_K):
                loc_ref[s, _sublane_rows(idx_ref[base + k * tm + r]), :] = row
            return c
        lax.fori_loop(0, tm, place, 0, unroll=8)

        def send(lo, d0, n):
            pltpu.make_async_copy(loc_ref.at[s, _piece_rows(lo, n), :], xs_hbm.at[_piece_rows(d0, n), :],
                                  sem.at[s]).start()
        _for_each_piece(table_ref, i, n_tiles, send)

        @pl.when(i == n_tiles - 1)
        def _():
            @pl.when(i >= 1)
            def _():
                drain(1 - s)
            drain(s)

    for s in range(2):
        pl.when(slot == s)(functools.partial(step, s))


def _scatter(h2, lpos, table, pad_end, padded, n_rows, tm, xs=None):
    nt = lpos.shape[0]
    first = xs is None
    assert first or nt * tm <= MOE_ROWS, "a later call may add at most MOE_ROWS rows per expert"
    loc_rows = TOP_K * tm * TILE_CHUNKS
    in_specs = [pl.BlockSpec((tm * TILE_CHUNKS, LANES), lambda i, *_: (i, 0)),
                pl.BlockSpec(memory_space=pl.ANY)]
    args = [table.reshape(-1), pad_end, padded, h2, lpos]
    if not first:
        in_specs.append(pl.BlockSpec(memory_space=pl.ANY))
        args.append(xs)
    grid_spec = pltpu.PrefetchScalarGridSpec(
        num_scalar_prefetch=3,
        grid=(nt,),
        in_specs=in_specs,
        out_specs=pl.BlockSpec(memory_space=pl.ANY),
        scratch_shapes=[pltpu.SMEM((2 * TOP_K * tm,), jnp.int32),
                        pltpu.VMEM((2, loc_rows, LANES), F32),
                        pltpu.VMEM((MOE_ROWS * TILE_CHUNKS, LANES), F32),
                        pltpu.SemaphoreType.DMA((2,)), pltpu.SemaphoreType.DMA((2,)), pltpu.SemaphoreType.DMA],
    )
    return pl.pallas_call(
        functools.partial(_scatter_kernel, first=first),
        grid_spec=grid_spec,
        out_shape=jax.ShapeDtypeStruct((n_rows * TILE_CHUNKS, LANES), F32),
        input_output_aliases={} if first else {5: 0},
        compiler_params=_cparams(("arbitrary",)),
        name="moe_dispatch",
    )(*args)


def _mlp_kernel(be_ref, nu_ref, xs_ref, wg_ref, bg_ref, wu_ref, bu_ref, wd_ref, bd_ref, out_ref,
                wg16_ref, wu16_ref, wd16_ref):
    i = pl.program_id(0)
    used = i < nu_ref[0]

    @pl.when(used & ((i == 0) | (be_ref[i] != be_ref[jnp.maximum(i - 1, 0)])))
    def _():
        wg16_ref[...] = wg_ref[0].astype(BF16)
        wu16_ref[...] = wu_ref[0].astype(BF16)
        wd16_ref[...] = wd_ref[0].astype(BF16)

    @pl.when(used)
    def _():
        x = _from_token_tiles(xs_ref, 0, MOE_ROWS).astype(BF16)
        gt = jnp.dot(x, wg16_ref[...], preferred_element_type=F32) + bg_ref[0]
        up = jnp.dot(x, wu16_ref[...], preferred_element_type=F32) + bu_ref[0]
        gt = jnp.minimum(gt, SWIGLU_LIMIT)
        up = jnp.clip(up, -SWIGLU_LIMIT, SWIGLU_LIMIT)
        glu = gt * _sigmoid(gt * SWIGLU_ALPHA)
        mid = ((up + 1.0) * glu).astype(BF16)
        _to_token_tiles(out_ref, jnp.dot(mid, wd16_ref[...], preferred_element_type=F32) + bd_ref[0])

    @pl.when(jnp.logical_not(used))
    def _():
        out_ref[...] = jnp.zeros_like(out_ref)


def _mlp(xs, block_e, n_used, n_blocks, wts):
    wg, bg, wu, bu, wd, bd = wts
    row_map = lambda i, be, nu: (i, 0)
    w_map = lambda i, be, nu: (be[i], 0, 0)
    wspec = pl.BlockSpec((1, D_MODEL, D_FF), w_map)
    bspec = pl.BlockSpec((1, 1, D_FF), w_map)
    rows = pl.BlockSpec((MOE_ROWS * TILE_CHUNKS, LANES), row_map)
    grid_spec = pltpu.PrefetchScalarGridSpec(
        num_scalar_prefetch=2,
        grid=(n_blocks,),
        in_specs=[rows, wspec, bspec, wspec, bspec, wspec, bspec],
        out_specs=rows,
        scratch_shapes=[pltpu.VMEM((D_MODEL, D_FF), BF16)] * 3,
    )
    return pl.pallas_call(
        _mlp_kernel,
        grid_spec=grid_spec,
        out_shape=jax.ShapeDtypeStruct(xs.shape, F32),
        compiler_params=_cparams(("arbitrary",)),
        name="moe_experts",
    )(block_e, n_used, xs, wg, bg, wu, bu, wd, bd)


def _gather_kernel(table_ref, x1_ref, gate2_ref, fg_ref, lpos_hbm, tg_hbm, rows_hbm, y_ref,
                   idx_ref, gsm_ref, loc_ref, ff_ref, sem, isem):
    i = pl.program_id(1) + pl.program_id(0) * pl.num_programs(1)
    n_tiles = pl.num_programs(0) * pl.num_programs(1)
    tm = x1_ref.shape[1]
    slot = i % 2

    def meta_copies(tile, s):
        seg = pl.ds(s * (TOP_K * tm), TOP_K * tm)
        return (pltpu.make_async_copy(lpos_hbm.at[tile], idx_ref.at[seg], isem.at[s]),
                pltpu.make_async_copy(tg_hbm.at[tile], gsm_ref.at[seg], isem.at[s]))

    def fetch_tile(tile, s):
        for c in meta_copies(tile, s):
            c.start()

        def fetch(lo, d0, n):
            pltpu.make_async_copy(rows_hbm.at[_piece_rows(d0, n), :], loc_ref.at[s, _piece_rows(lo, n), :],
                                  sem.at[s]).start()
        _for_each_piece(table_ref, tile, n_tiles, fetch)

    @pl.when(i == 0)
    def _():
        fetch_tile(0, 0)

    def step(s):
        @pl.when(i + 1 < n_tiles)
        def _():
            fetch_tile(i + 1, 1 - s)

        pltpu.make_async_copy(rows_hbm.at[pl.ds(0, TOP_K * tm * TILE_CHUNKS), :], loc_ref.at[s], sem.at[s]).wait()
        for c in meta_copies(i, s):
            c.wait()
        base = s * (TOP_K * tm)

        def mix(r, c):
            acc = gsm_ref[base + r] * loc_ref[s, _sublane_rows(idx_ref[base + r]), :]
            for k in range(1, TOP_K):
                acc = acc + gsm_ref[base + k * tm + r] * loc_ref[s, _sublane_rows(idx_ref[base + k * tm + r]), :]
            ff_ref[_tile_rows(r), :] = acc
            return c
        lax.fori_loop(0, tm, mix, 0, unroll=8)

    for s in range(2):
        pl.when(slot == s)(functools.partial(step, s))

    x2 = x1_ref[0] + gate2_ref[0] * _from_token_tiles(ff_ref, 0, tm)
    y_ref[0] = _rms(x2, fg_ref[...])


def _gather(x1, tg, gate2, mod_map, final_g, lpos, table, rows, tm):
    nb, t, _ = x1.shape
    nt = t // tm
    mod_block = (1,) + gate2.shape[1:]
    loc_rows = TOP_K * tm * TILE_CHUNKS
    grid_spec = pltpu.PrefetchScalarGridSpec(
        num_scalar_prefetch=1,
        grid=(nb, nt),
        in_specs=[pl.BlockSpec((1, tm, D_MODEL), lambda b, i, *_: (b, i, 0)),
                  pl.BlockSpec(mod_block, lambda b, i, *_: mod_map(5, b, i)),
                  pl.BlockSpec((1, D_MODEL), lambda b, i, *_: (0, 0)),
                  pl.BlockSpec(memory_space=pl.ANY),
                  pl.BlockSpec(memory_space=pl.ANY),
                  pl.BlockSpec(memory_space=pl.ANY)],
        out_specs=pl.BlockSpec((1, tm, D_MODEL), lambda b, i, *_: (b, i, 0)),
        scratch_shapes=[pltpu.SMEM((2 * TOP_K * tm,), jnp.int32),
                        pltpu.SMEM((2 * TOP_K * tm,), F32),
                        pltpu.VMEM((2, loc_rows, LANES), F32),
                        pltpu.VMEM((tm * TILE_CHUNKS, LANES), F32),
                        pltpu.SemaphoreType.DMA((2,)), pltpu.SemaphoreType.DMA((2,))],
    )
    return pl.pallas_call(
        _gather_kernel,
        grid_spec=grid_spec,
        out_shape=jax.ShapeDtypeStruct((nb, t, D_MODEL), F32),
        compiler_params=_cparams(("arbitrary", "arbitrary")),
        name="moe_combine",
    )(table.reshape(-1), x1, gate2, final_g.reshape(1, D_MODEL), lpos, tg.reshape(tg.shape[0], -1), rows)


def _moe(route_p, route_s, mod_p, mod_s, final_g, wts):
    routes = [route_p, route_s]
    groups, block_e, n_used, pad_end, padded, n_blocks = _plan([(r[2], r[3], r[5]) for r in routes])
    xs = None
    for (x1, h2, ti, rk, tg, cnt), (lpos, table) in zip(routes, groups):
        xs = _scatter(h2, lpos, table, pad_end, padded, n_blocks * MOE_ROWS, ti.shape[2], xs)
    rows = _mlp(xs, block_e, n_used, n_blocks, wts)
    outs = []
    for (x1, h2, ti, rk, tg, cnt), (lpos, table), (mod, mod_map) in zip(routes, groups, (mod_p, mod_s)):
        outs.append(_gather(x1, tg, mod, mod_map, final_g, lpos, table, rows, ti.shape[2]))
    return outs


def _forward(x_prompt, x_sample, c_prompt, c_sample, state_wkv, state_shift, state_conv, state_lru, p, final_g):
    bp, tp, _ = x_prompt.shape
    bs = x_sample.shape[0]
    tm = min(512, tp)
    tt = min(64, tp)

    mod = _ada(jnp.concatenate([c_prompt, c_sample], axis=0), p['w_ada'], p['b_ada'])
    mod_p = mod[:bp].reshape(bp * N_MOD, 1, D_MODEL)
    mod_s = mod[bp:].reshape(bs, N_MOD, D_MODEL).transpose(1, 0, 2)
    map_p = lambda j, b, i: (b * N_MOD + j, 0, 0)
    map_s = lambda j, b, i: (j, 0, 0)

    wa = p['w_in'][:, :N_COLS_A]
    wb = p['w_in'][:, N_COLS_A:]
    wts = (p['w_gate'], p['b_gate'].reshape(N_EXPERTS, 1, D_FF),
           p['w_up'], p['b_up'].reshape(N_EXPERTS, 1, D_FF),
           p['w_down'], p['b_down'].reshape(N_EXPERTS, 1, D_MODEL))

    pa, pb = _inproj(x_prompt, mod_p, mod_p, map_p, p['norm1_g'], wa.astype(BF16), wb.astype(BF16), tm, False)
    wb_p, kr_p, kv_p, g = _prep(pa, None, p, tm, True)
    s0 = jnp.zeros((bp, N_HEADS, HEAD_DIM, HEAD_DIM), F32)
    wkv_out, wkv_p = _wkv_scan(wb_p[None], kr_p[None], kv_p[None], s0, p, tt)
    yb, lru_p = _lru_seq(pb, p, tm)
    route_p = _post(x_prompt, wkv_out, g, yb, mod_p, mod_p, mod_p, map_p, p, tm)
    shift_p = pa[:, -1, :]
    conv_p = pb[:, tp - (CONV_W - 1):, D_B:]

    xs = x_sample.reshape(1, bs, D_MODEL)
    pa_s, pb_s = _inproj(xs, mod_s, mod_s, map_s, p['norm1_g'], wa, wb, bs, True)
    wb_s, kr_s, kv_s, g = _prep(pa_s, state_shift.reshape(1, bs, N_COLS_A), p, bs, False)
    as_seq = lambda a: _pairs_to_groups(a.reshape(bs, N_HEADS, LANES), 1)
    wkv_out, wkv_s = _wkv_scan(as_seq(wb_s), as_seq(kr_s), as_seq(kv_s), state_wkv, p, 1)
    conv0 = state_conv.transpose(1, 0, 2)
    yb, lru_s = _lru_step(pb_s[0], conv0, state_lru, p)
    route_s = _post(xs, wkv_out.reshape(1, bs, D_A), g, yb.reshape(1, bs, D_B),
                    mod_s, mod_s, mod_s, map_s, p, bs)
    y_prompt, y_sample = _moe(route_p, route_s, (mod_p, map_p), (mod_s, map_s), final_g, wts)
    shift_s = pa_s[0]
    conv_s = jnp.concatenate([state_conv[:, 1:], pb_s[0][:, None, D_B:]], axis=1)

    return (y_prompt, y_sample.reshape(bs, 1, D_MODEL),
            wkv_p[None], shift_p[None], conv_p[None], lru_p.reshape(bp, D_B)[None],
            wkv_s[None], shift_s[None], conv_s[None], lru_s[None])


def kernel(x_prompt, x_sample, c_prompt, c_sample, state_wkv, state_shift, state_conv, state_lru, w_ada, b_ada, norm1_g, norm2_g, w_in, rk_mu, rk_w0, rk_w_up, rk_a0, rk_a_up, rk_g_up, rk_k_k, rk_k_a, rk_r_k, rk_lnx_w, rk_lnx_b, lru_conv_w, lru_conv_b, lru_w_r, lru_b_r, lru_w_i, lru_b_i, lru_lambda, lru_norm_g, w_out, router_w, router_b, w_gate, b_gate, w_up, b_up, w_down, b_down, final_g):
    assert w_ada.shape[0] == 1, "single-layer trunk"
    p = dict(w_ada=w_ada[0], b_ada=b_ada[0], norm1_g=norm1_g[0], norm2_g=norm2_g[0], w_in=w_in[0],
             rk_mu=rk_mu[0], rk_w0=rk_w0[0], rk_w_up=rk_w_up[0], rk_a0=rk_a0[0], rk_a_up=rk_a_up[0],
             rk_g_up=rk_g_up[0], rk_k_k=rk_k_k[0], rk_k_a=rk_k_a[0], rk_r_k=rk_r_k[0],
             rk_lnx_w=rk_lnx_w[0], rk_lnx_b=rk_lnx_b[0], lru_conv_w=lru_conv_w[0],
             lru_conv_b=lru_conv_b[0], lru_w_r=lru_w_r[0], lru_b_r=lru_b_r[0], lru_w_i=lru_w_i[0],
             lru_b_i=lru_b_i[0], lru_lambda=lru_lambda[0], lru_norm_g=lru_norm_g[0], w_out=w_out[0],
             router_w=router_w[0], router_b=router_b[0], w_gate=w_gate[0], b_gate=b_gate[0],
             w_up=w_up[0], b_up=b_up[0], w_down=w_down[0], b_down=b_down[0])
    return _forward(x_prompt, x_sample, c_prompt, c_sample, state_wkv[0], state_shift[0], state_conv[0],
                    state_lru[0], p, final_g)
```

```python
import functools

import jax
import jax.numpy as jnp
from jax import lax
from jax.experimental import pallas as pl
from jax.experimental.pallas import tpu as pltpu

F32 = jnp.float32
BF16 = jnp.bfloat16

D_MODEL = 1024
D_A = 512
HEAD_DIM = 64
N_HEADS = 8
D_B = 512
N_BLOCKS_B = 8
CONV_W = 4
LRU_C = 8.0
R_DECAY = 64
R_AAA = 64
R_GATE = 128
N_COLS_A = 3 * D_A + R_DECAY + R_AAA + R_GATE
N_COLS_B = 2 * D_B
N_EXPERTS = 32
TOP_K = 4
D_FF = 1024
SWIGLU_LIMIT = 7.0
SWIGLU_ALPHA = 1.702
RMS_EPS = 1e-6
LN_X_EPS = 64e-5
N_MOD = 6

LANES = 128
SUBLANES = 8
GROUP_BATCH = 8
HALF_ROWS = HEAD_DIM // 2
MOE_ROWS = 256
MOE_CHUNK = 16
TILE_CHUNKS = D_MODEL // LANES
VMEM_LIMIT = 56 * 1024 * 1024


def _cparams(sem):
    return pltpu.CompilerParams(dimension_semantics=sem, vmem_limit_bytes=VMEM_LIMIT)


def _dot(a, b):
    return jnp.dot(a.astype(BF16), b.astype(BF16), preferred_element_type=F32)


def _split(a):
    hi = a.astype(BF16)
    lo = (a - hi.astype(F32)).astype(BF16)
    return hi, lo


def _dot3(a, b):
    ah, al = _split(a)
    bh, bl = _split(b)
    return (jnp.dot(ah, bh, preferred_element_type=F32)
            + (jnp.dot(al, bh, preferred_element_type=F32) + jnp.dot(ah, bl, preferred_element_type=F32)))


def _dot3_nt(a, b):
    dn = (((1,), (1,)), ((), ()))
    ah, al = _split(a)
    bh, bl = _split(b)
    d = lambda x, y: lax.dot_general(x, y, dn, preferred_element_type=F32)
    return d(ah, bh) + (d(al, bh) + d(ah, bl))


def _softplus(x):
    return jnp.maximum(x, 0.0) + jnp.log1p(jnp.exp(-jnp.abs(x)))


def _sigmoid(x):
    return 1.0 / (1.0 + jnp.exp(-x))


def _rms(x, g):
    ms = jnp.mean(x * x, axis=-1, keepdims=True)
    return x * lax.rsqrt(ms + RMS_EPS) * g


def _ada_kernel(c_ref, w_ref, b_ref, o_ref):
    c = c_ref[...]
    s = c * _sigmoid(c)
    o_ref[...] = _dot3(s, w_ref[...]) + b_ref[...]


def _ada(c, w_ada, b_ada):
    rows = c.shape[0]
    ncol = w_ada.shape[1]
    tn = D_MODEL
    return pl.pallas_call(
        _ada_kernel,
        grid=(ncol // tn,),
        in_specs=[pl.BlockSpec((rows, D_MODEL), lambda j: (0, 0)),
                  pl.BlockSpec((D_MODEL, tn), lambda j: (0, j)),
                  pl.BlockSpec((1, tn), lambda j: (0, j))],
        out_specs=pl.BlockSpec((rows, tn), lambda j: (0, j)),
        out_shape=jax.ShapeDtypeStruct((rows, ncol), F32),
        compiler_params=_cparams(("arbitrary",)),
        name="ada_mod",
    )(c, w_ada, b_ada.reshape(1, ncol))


def _inproj_kernel(x_ref, shift_ref, scale_ref, g_ref, wa_ref, wb_ref, pa_ref, pb_ref, *, precise):
    x = x_ref[0]
    h = _rms(x, g_ref[...]) * (1.0 + scale_ref[0]) + shift_ref[0]
    dot = _dot3 if precise else _dot
    pa_ref[0] = dot(h, wa_ref[...])
    pb_ref[0] = dot(h, wb_ref[...])


def _inproj(x, shift, scale, mod_map, g, wa, wb, tm, precise):
    nb, t, _ = x.shape
    mod_block = (1,) + shift.shape[1:]
    return pl.pallas_call(
        functools.partial(_inproj_kernel, precise=precise),
        grid=(nb, t // tm),
        in_specs=[pl.BlockSpec((1, tm, D_MODEL), lambda b, i: (b, i, 0)),
                  pl.BlockSpec(mod_block, functools.partial(mod_map, 0)),
                  pl.BlockSpec(mod_block, functools.partial(mod_map, 1)),
                  pl.BlockSpec((1, D_MODEL), lambda b, i: (0, 0)),
                  pl.BlockSpec((D_MODEL, N_COLS_A), lambda b, i: (0, 0)),
                  pl.BlockSpec((D_MODEL, N_COLS_B), lambda b, i: (0, 0))],
        out_specs=[pl.BlockSpec((1, tm, N_COLS_A), lambda b, i: (b, i, 0)),
                   pl.BlockSpec((1, tm, N_COLS_B), lambda b, i: (b, i, 0))],
        out_shape=[jax.ShapeDtypeStruct((nb, t, N_COLS_A), F32),
                   jax.ShapeDtypeStruct((nb, t, N_COLS_B), F32)],
        compiler_params=_cparams(("arbitrary", "arbitrary")),
        name="norm1_inproj",
    )(x, shift, scale, g.reshape(1, D_MODEL), wa, wb)


def _store_head_pairs(ref, stage_ref, x, y):
    rows = x.shape[0]
    flat = ref.shape[0] == 1
    for h in range(N_HEADS):
        sl = slice(h * HEAD_DIM, (h + 1) * HEAD_DIM)
        pair = jnp.concatenate([x[:, sl], y[:, sl]], axis=1)
        if flat:
            ref[0, pl.ds(h, rows, stride=N_HEADS), :] = pair
        else:
            stage_ref[pl.ds(h, rows, stride=N_HEADS), :] = pair
    if not flat:
        ref[...] = stage_ref[...].reshape(rows, N_HEADS, LANES)


def _prep_kernel(pa_ref, prev_ref, mu_ref, w0_ref, wup_ref, a0_ref, gup_ref, kk_ref, ka_ref,
                 wb_out, kr_out, kv_out, g_out, carry_ref, stage_ref, *, seq):
    pa = pa_ref[0]
    if seq:
        @pl.when(pl.program_id(1) == 0)
        def _():
            carry_ref[...] = jnp.zeros_like(carry_ref)
        rolled = pltpu.roll(pa, 1, axis=0)
        row = lax.broadcasted_iota(jnp.int32, pa.shape, 0)
        prev = jnp.where(row == 0, carry_ref[...], rolled)
        carry_ref[...] = pa[pa.shape[0] - 1:, :]
    else:
        prev = prev_ref[0]
    z = pa + (prev - pa) * mu_ref[...]
    r = z[:, 0:D_A]
    k = z[:, D_A:2 * D_A]
    v = z[:, 2 * D_A:3 * D_A]
    lo = 3 * D_A
    za = z[:, lo:lo + R_DECAY + R_AAA]
    lane = lax.broadcasted_iota(jnp.int32, za.shape, 1)
    za = jnp.where(lane < R_DECAY, jnp.tanh(za), za)
    lw = _dot3(za, wup_ref[...])
    w_log = -_softplus(-(w0_ref[...] + lw[:, :D_A])) - 0.5
    decay = jnp.exp(-jnp.exp(w_log))
    a = _sigmoid(a0_ref[...] + lw[:, D_A:])
    gd = z[:, lo + R_DECAY + R_AAA:]
    g = _dot3(_sigmoid(gd), gup_ref[...])
    kk = k * kk_ref[...]
    _store_head_pairs(wb_out, stage_ref, decay, kk * a)
    _store_head_pairs(kr_out, stage_ref, k * (1.0 + (a - 1.0) * ka_ref[...]), r)
    _store_head_pairs(kv_out, stage_ref, kk, v)
    g_out[0] = g


def _prep(pa, prev, p, tm, seq):
    nb, t, _ = pa.shape
    wup = jnp.zeros((R_DECAY + R_AAA, 2 * D_A), F32)
    wup = wup.at[:R_DECAY, :D_A].set(p['rk_w_up']).at[R_DECAY:, D_A:].set(p['rk_a_up'])
    vec = lambda a: a.reshape(1, -1)
    row_spec = lambda n: pl.BlockSpec((1, n), lambda b, i: (0, 0))
    tile = pl.BlockSpec((1, tm, D_A), lambda b, i: (b, i, 0))
    if seq:
        assert nb == GROUP_BATCH
        pair_tile = pl.BlockSpec((tm, N_HEADS, LANES), lambda b, i: (i, b, 0))
        pair_shape = jax.ShapeDtypeStruct((t, nb * N_HEADS, LANES), F32)
    else:
        pair_tile = pl.BlockSpec((1, tm * N_HEADS, LANES), lambda b, i: (b, i, 0))
        pair_shape = jax.ShapeDtypeStruct((nb, t * N_HEADS, LANES), F32)
    if prev is None:
        prev = jnp.zeros((1, SUBLANES, N_COLS_A), F32)
        prev_spec = pl.BlockSpec((1, SUBLANES, N_COLS_A), lambda b, i: (0, 0, 0))
    else:
        prev_spec = pl.BlockSpec((1, tm, N_COLS_A), lambda b, i: (b, i, 0))
    return pl.pallas_call(
        functools.partial(_prep_kernel, seq=seq),
        grid=(nb, t // tm),
        in_specs=[pl.BlockSpec((1, tm, N_COLS_A), lambda b, i: (b, i, 0)),
                  prev_spec,
                  row_spec(N_COLS_A), row_spec(D_A),
                  pl.BlockSpec((R_DECAY + R_AAA, 2 * D_A), lambda b, i: (0, 0)),
                  row_spec(D_A),
                  pl.BlockSpec((R_GATE, D_A), lambda b, i: (0, 0)),
                  row_spec(D_A), row_spec(D_A)],
        out_specs=[pair_tile] * 3 + [tile],
        out_shape=[pair_shape] * 3 + [jax.ShapeDtypeStruct((nb, t, D_A), F32)],
        scratch_shapes=[pltpu.VMEM((1, N_COLS_A), F32), pltpu.VMEM((tm * N_HEADS, LANES), F32)],
        compiler_params=_cparams(("arbitrary", "arbitrary")),
        name="rwkv_prep",
    )(pa, prev, vec(p['rk_mu']), vec(p['rk_w0']), wup, vec(p['rk_a0']), p['rk_g_up'],
      vec(p['rk_k_k']), vec(p['rk_k_a']))


SCAN_SLOTS = 4


def _scan_kernel(wb_ref, kr_ref, kv_ref, kvnext_ref, s0_ref, lnw_ref, lnb_ref, rk_ref,
                 out_ref, sfin_ref, s_ref, sa_ref, inv_ref, wd_ref, kkd_ref, bd_ref, kd_ref, rd_ref, vd_ref, *, tt):
    ti = pl.program_id(1)
    upper_half = lax.broadcasted_iota(jnp.int32, (HALF_ROWS, LANES), 1) >= LANES // 2
    unroll = SCAN_SLOTS if tt % SCAN_SLOTS == 0 else 1

    def expand(pair):
        return jnp.concatenate([pair, pair], axis=0).T

    def prepare(slot, s, kk_pair=None):
        t1 = expand(wb_ref[0, s])
        wd_ref[slot] = t1[:HEAD_DIM]
        bd_ref[slot] = t1[HEAD_DIM:]
        t2 = expand(kr_ref[0, s])
        kd_ref[slot] = t2[:HEAD_DIM]
        rd_ref[slot] = t2[HEAD_DIM:]
        t3 = expand(kv_ref[0, s])
        vd_ref[slot] = jnp.where(upper_half, t3[HEAD_DIM + HALF_ROWS:], t3[HEAD_DIM:HEAD_DIM + HALF_ROWS])
        kkd_ref[slot] = t3[:HEAD_DIM] if kk_pair is None else expand(kk_pair)[:HEAD_DIM]

    prepare(0, 0)
    if tt > 1:
        prepare(1, 1)
    else:
        kkd_ref[1] = expand(kvnext_ref[0, 0])[:HEAD_DIM]

    def inv_norm2(kk_rows):
        s2 = jnp.sum(kk_rows * kk_rows, axis=0, keepdims=True)
        return 1.0 / jnp.maximum(s2, 1e-24)

    @pl.when(ti == 0)
    def _():
        s_ref[...] = s0_ref[0]
        acc = jnp.zeros((HALF_ROWS, LANES), F32)
        for j in range(HEAD_DIM):
            acc = acc + s0_ref[0, j] * kkd_ref[0, j:j + 1, :]
        sa_ref[...] = acc
        inv_ref[...] = inv_norm2(kkd_ref[0])

    def finish(y, cv):
        tot = jnp.sum(y, axis=0, keepdims=True)
        tot = tot + pltpu.roll(tot, LANES // 2, axis=1)
        d = y - tot * (1.0 / HEAD_DIM)
        sq = jnp.sum(d * d, axis=0, keepdims=True)
        sq = sq + pltpu.roll(sq, LANES // 2, axis=1)
        yn = d * lax.rsqrt(sq * (1.0 / HEAD_DIM) + LN_X_EPS)
        return yn * lnw_ref[...] + lnb_ref[...] + cv

    def step(t, u, carry):
        sa, inv2, y_prev, cv_prev = carry
        nxt_slot = (u + 1) % SCAN_SLOTS
        out_ref[0, jnp.maximum(t - 1, 0)] = finish(y_prev, cv_prev)
        sae = sa * (-inv2)
        v = vd_ref[u]
        acc_y = jnp.zeros((HALF_ROWS, LANES), F32)
        acc_s = jnp.zeros((HALF_ROWS, LANES), F32)
        for j in range(HEAD_DIM):
            row = pl.ds(j, 1)
            s_new = s_ref[j] * wd_ref[u, row, :] + sae * bd_ref[u, row, :] + v * kd_ref[u, row, :]
            s_ref[j] = s_new
            acc_y = acc_y + s_new * rd_ref[u, row, :]
            acc_s = acc_s + s_new * kkd_ref[nxt_slot, row, :]
        c = jnp.sum(rd_ref[u] * kd_ref[u] * rk_ref[...], axis=0, keepdims=True)
        if unroll > 1:
            ahead = jnp.minimum(t + 2, tt - 1)
            kk_pair = None
            if (u + 2) % SCAN_SLOTS == 0:
                kk_pair = jnp.where(t + 2 == tt, kvnext_ref[0, 0], kv_ref[0, ahead])
            prepare((u + 2) % SCAN_SLOTS, ahead, kk_pair)
        return acc_s, inv_norm2(kkd_ref[nxt_slot]), acc_y, c * v

    def steps(q, carry):
        for u in range(unroll):
            carry = step(q * unroll + u, u, carry)
        return carry

    zeros = jnp.zeros((HALF_ROWS, LANES), F32)
    sa, inv2, y_last, cv_last = lax.fori_loop(0, tt // unroll, steps, (sa_ref[...], inv_ref[...], zeros, zeros))
    out_ref[0, tt - 1] = finish(y_last, cv_last)
    sa_ref[...] = sa
    inv_ref[...] = inv2

    @pl.when(ti == pl.num_programs(1) - 1)
    def _():
        sfin_ref[0] = s_ref[...]


def _from_scan_rows(y):
    g, t = y.shape[:2]
    x = y.reshape(g, t, HALF_ROWS, 2, GROUP_BATCH, N_HEADS).transpose(0, 4, 1, 5, 3, 2)
    return x.reshape(g * GROUP_BATCH, t, D_A)


def _state_to_scan(s):
    g = s.shape[0] // GROUP_BATCH
    y = s.reshape(g, GROUP_BATCH, N_HEADS, 2, HALF_ROWS, HEAD_DIM).transpose(0, 5, 4, 3, 1, 2)
    return y.reshape(g, HEAD_DIM, HALF_ROWS, LANES)


def _state_from_scan(y):
    g = y.shape[0]
    s = y.reshape(g, HEAD_DIM, HALF_ROWS, 2, GROUP_BATCH, N_HEADS).transpose(0, 4, 5, 3, 2, 1)
    return s.reshape(g * GROUP_BATCH, N_HEADS, HEAD_DIM, HEAD_DIM)


def _head_rows(x):
    y = x.reshape(N_HEADS, 2, HALF_ROWS).transpose(2, 1, 0)
    y = jnp.broadcast_to(y[:, :, None, :], (HALF_ROWS, 2, GROUP_BATCH, N_HEADS))
    return y.reshape(HALF_ROWS, LANES)


def _head_keys(x):
    y = jnp.broadcast_to(x.T[:, None, None, :], (HEAD_DIM, 2, GROUP_BATCH, N_HEADS))
    return y.reshape(HEAD_DIM, LANES)


def _pairs_to_groups(x, t):
    nb = x.shape[0]
    g = nb // GROUP_BATCH
    y = x.reshape(g, GROUP_BATCH, t, N_HEADS, LANES).transpose(0, 2, 1, 3, 4)
    return y.reshape(g, t, GROUP_BATCH * N_HEADS, LANES)


def _wkv_scan(wb, kr, kv, s0, p, tt):
    g, t = wb.shape[:2]
    pair_tile = pl.BlockSpec((1, tt, HEAD_DIM, LANES), lambda gi, i: (gi, i, 0, 0))
    next_step = pl.BlockSpec((1, 1, HEAD_DIM, LANES), lambda gi, i: (gi, jnp.minimum((i + 1) * tt, t - 1), 0, 0))
    row_tile = pl.BlockSpec((1, tt, HALF_ROWS, LANES), lambda gi, i: (gi, i, 0, 0))
    state = pl.BlockSpec((1, HEAD_DIM, HALF_ROWS, LANES), lambda gi, i: (gi, 0, 0, 0))
    const = lambda n: pl.BlockSpec((n, LANES), lambda gi, i: (0, 0))
    keys = pltpu.VMEM((SCAN_SLOTS, HEAD_DIM, LANES), F32)
    out, sfin = pl.pallas_call(
        functools.partial(_scan_kernel, tt=tt),
        grid=(g, t // tt),
        in_specs=[pair_tile, pair_tile, pair_tile, next_step,
                  state, const(HALF_ROWS), const(HALF_ROWS), const(HEAD_DIM)],
        out_specs=[row_tile, state],
        out_shape=[jax.ShapeDtypeStruct((g, t, HALF_ROWS, LANES), F32),
                   jax.ShapeDtypeStruct((g, HEAD_DIM, HALF_ROWS, LANES), F32)],
        scratch_shapes=[pltpu.VMEM((HEAD_DIM, HALF_ROWS, LANES), F32),
                        pltpu.VMEM((HALF_ROWS, LANES), F32),
                        pltpu.VMEM((1, LANES), F32),
                        keys, keys, keys, keys, keys,
                        pltpu.VMEM((SCAN_SLOTS, HALF_ROWS, LANES), F32)],
        compiler_params=_cparams(("arbitrary", "arbitrary")),
        name="wkv_scan",
    )(wb, kr, kv, kv, _state_to_scan(s0),
      _head_rows(p['rk_lnx_w']), _head_rows(p['rk_lnx_b']), _head_keys(p['rk_r_k']))
    return _from_scan_rows(out), _state_from_scan(sfin)


def _gelu(x):
    return 0.5 * x * (1.0 + jnp.tanh(0.7978845608028654 * (x + 0.044715 * (x * x * x))))


def _lru_gates(xc, wri_ref, bri_ref, nsl_ref, precise):
    dot = _dot3 if precise else _dot
    gates = _sigmoid(dot(xc, wri_ref[...]) + bri_ref[...])
    gate_r = gates[:, :D_B]
    gate_i = gates[:, D_B:]
    log_a = gate_r * nsl_ref[...]
    a = jnp.exp(log_a)
    th = jnp.tanh(log_a)
    one_minus_a2 = -2.0 * th / (1.0 - th)
    bt = jnp.sqrt(one_minus_a2) * (gate_i * xc)
    return a, bt


def _lru_seq_kernel(pb_ref, cw_ref, cb_ref, wri_ref, bri_ref, nsl_ref, ng_ref,
                    out_ref, hlast_ref, xprev_ref, h_ref):
    tm = pb_ref.shape[1]

    @pl.when(pl.program_id(1) == 0)
    def _():
        xprev_ref[...] = jnp.zeros_like(xprev_ref)
        h_ref[...] = jnp.zeros_like(h_ref)

    pb = pb_ref[0]
    yb = pb[:, :D_B]
    xb = pb[:, D_B:]
    xprev = xprev_ref[...]
    row8 = lax.broadcasted_iota(jnp.int32, (SUBLANES, D_B), 0)

    def shifted(d):
        rolled = pltpu.roll(xb, d, axis=0)
        top = jnp.where(row8 < d, pltpu.roll(xprev, d, axis=0), rolled[:SUBLANES])
        return jnp.concatenate([top, rolled[SUBLANES:]], axis=0)

    xc = cb_ref[...] + cw_ref[3:4, :] * xb
    for d in range(1, CONV_W):
        xc = xc + cw_ref[3 - d:4 - d, :] * shifted(d)
    xprev_ref[...] = xb[tm - SUBLANES:, :]

    a, x = _lru_gates(xc, wri_ref, bri_ref, nsl_ref, False)
    row = lax.broadcasted_iota(jnp.int32, (tm, D_B), 0)
    d = 1
    while d < tm:
        if d < SUBLANES:
            keep = row >= d
            a_s = jnp.where(keep, pltpu.roll(a, d, axis=0), 1.0)
            x_s = jnp.where(keep, pltpu.roll(x, d, axis=0), 0.0)
            x = a * x_s + x
            a = a * a_s
        else:
            x = jnp.concatenate([x[:d], a[d:] * x[:tm - d] + x[d:]], axis=0)
            a = jnp.concatenate([a[:d], a[d:] * a[:tm - d]], axis=0)
        d *= 2
    h = a * h_ref[...] + x
    h_ref[...] = h[tm - 1:, :]
    hlast_ref[0] = h[tm - 1:, :]
    out_ref[0] = _rms(h * _gelu(yb), ng_ref[...]).astype(out_ref.dtype)


def _lru_params(p):
    eye = jnp.eye(N_BLOCKS_B, dtype=F32)
    bd = lambda w: (eye[:, None, :, None] * w[:, :, None, :]).reshape(D_B, D_B)
    wri = jnp.concatenate([bd(p['lru_w_r']), bd(p['lru_w_i'])], axis=1)
    bri = jnp.concatenate([p['lru_b_r'], p['lru_b_i']]).reshape(1, 2 * D_B)
    nsl = (-LRU_C * jax.nn.softplus(-p['lru_lambda'])).reshape(1, D_B)
    return wri, bri, nsl


def _lru_seq(pb, p, tm):
    nb, t, _ = pb.shape
    wri, bri, nsl = _lru_params(p)
    row_spec = lambda n: pl.BlockSpec((1, n), lambda b, i: (0, 0))
    return pl.pallas_call(
        _lru_seq_kernel,
        grid=(nb, t // tm),
        in_specs=[pl.BlockSpec((1, tm, N_COLS_B), lambda b, i: (b, i, 0)),
                  pl.BlockSpec((CONV_W, D_B), lambda b, i: (0, 0)),
                  row_spec(D_B),
                  pl.BlockSpec((D_B, 2 * D_B), lambda b, i: (0, 0)),
                  row_spec(2 * D_B), row_spec(D_B), row_spec(D_B)],
        out_specs=[pl.BlockSpec((1, tm, D_B), lambda b, i: (b, i, 0)),
                   pl.BlockSpec((1, 1, D_B), lambda b, i: (b, 0, 0))],
        out_shape=[jax.ShapeDtypeStruct((nb, t, D_B), BF16),
                   jax.ShapeDtypeStruct((nb, 1, D_B), F32)],
        scratch_shapes=[pltpu.VMEM((SUBLANES, D_B), F32), pltpu.VMEM((1, D_B), F32)],
        compiler_params=_cparams(("arbitrary", "arbitrary")),
        name="rglru_seq",
    )(pb, p['lru_conv_w'], p['lru_conv_b'].reshape(1, D_B), wri.astype(BF16), bri, nsl,
      p['lru_norm_g'].reshape(1, D_B))


def _lru_step_kernel(pb_ref, conv_ref, h0_ref, cw_ref, cb_ref, wri_ref, bri_ref, nsl_ref, ng_ref,
                     out_ref, hnew_ref):
    pb = pb_ref[...]
    yb = pb[:, :D_B]
    xb = pb[:, D_B:]
    xc = cb_ref[...] + cw_ref[3:4, :] * xb
    for j in range(CONV_W - 1):
        xc = xc + cw_ref[j:j + 1, :] * conv_ref[j]
    a, x = _lru_gates(xc, wri_ref, bri_ref, nsl_ref, True)
    h = a * h0_ref[...] + x
    hnew_ref[...] = h
    out_ref[...] = _rms(h * _gelu(yb), ng_ref[...]).astype(out_ref.dtype)


def _lru_step(pb, conv0, h0, p):
    n = pb.shape[0]
    wri, bri, nsl = _lru_params(p)
    return pl.pallas_call(
        _lru_step_kernel,
        out_shape=[jax.ShapeDtypeStruct((n, D_B), BF16), jax.ShapeDtypeStruct((n, D_B), F32)],
        compiler_params=pltpu.CompilerParams(vmem_limit_bytes=VMEM_LIMIT),
        name="rglru_step",
    )(pb, conv0, h0, p['lru_conv_w'], p['lru_conv_b'].reshape(1, D_B), wri, bri, nsl,
      p['lru_norm_g'].reshape(1, D_B))


def _to_token_tiles(ref, x):
    rows = x.shape[0]
    for c in range(TILE_CHUNKS):
        ref[pl.ds(c, rows, stride=TILE_CHUNKS), :] = x[:, c * LANES:(c + 1) * LANES]


def _from_token_tiles(ref, row0, rows):
    return jnp.concatenate(
        [ref[pl.ds(row0 + c, rows, stride=TILE_CHUNKS), :] for c in range(TILE_CHUNKS)], axis=1)


def _post_kernel(x_ref, wkv_ref, g_ref, yb_ref, gate1_ref, shift2_ref, scale2_ref, n2_ref,
                 wo_ref, rw_ref, rb_ref, tri_ref, x1_ref, h2_ref, ti_ref, rk_ref, tg_ref, cnt_ref):
    ya = (wkv_ref[0] * g_ref[0]).astype(BF16)
    mixed = (jnp.dot(ya, wo_ref[:D_A, :], preferred_element_type=F32)
             + jnp.dot(yb_ref[0], wo_ref[D_A:, :], preferred_element_type=F32))
    x1 = x_ref[0] + gate1_ref[0] * mixed
    x1_ref[0] = x1
    h2 = _rms(x1, n2_ref[...]) * (1.0 + scale2_ref[0]) + shift2_ref[0]
    _to_token_tiles(h2_ref, h2)
    logits = _dot3_nt(rw_ref[...], h2) + rb_ref[...]
    eidx = lax.broadcasted_iota(jnp.int32, logits.shape, 0)
    vals, idxs = [], []
    cur = logits
    for _ in range(TOP_K):
        m = jnp.max(cur, axis=0, keepdims=True)
        i = jnp.min(jnp.where(cur == m, eidx, N_EXPERTS), axis=0, keepdims=True)
        vals.append(m)
        idxs.append(i)
        cur = jnp.where(eidx == i, -jnp.inf, cur)
    ex = [jnp.exp(v - vals[0]) for v in vals]
    den = ex[0] + ex[1] + ex[2] + ex[3]
    sel = [eidx == i for i in idxs]
    onehot = (sel[0] | sel[1] | sel[2] | sel[3]).astype(F32)
    incl = jnp.dot(onehot.astype(BF16), tri_ref[...], preferred_element_type=F32)
    rank = incl - onehot
    cnt_ref[0] = jnp.broadcast_to(jnp.sum(onehot, axis=1, keepdims=True), cnt_ref.shape[1:])
    for k in range(TOP_K):
        ti_ref[0, k:k + 1, :] = idxs[k]
        rk_ref[0, k:k + 1, :] = jnp.sum(jnp.where(sel[k], rank, 0.0), axis=0, keepdims=True).astype(jnp.int32)
        tg_ref[0, k:k + 1, :] = ex[k] / den


def _post(x, wkv, g, yb, gate1, shift2, scale2, mod_map, p, tm):
    nb, t, _ = x.shape
    mod_block = (1,) + gate1.shape[1:]
    tile = lambda n: pl.BlockSpec((1, tm, n), lambda b, i: (b, i, 0))
    full = lambda a: pl.BlockSpec(a.shape, lambda b, i: (0,) * a.ndim)
    mspec = lambda j: pl.BlockSpec(mod_block, functools.partial(mod_map, j))
    n2 = p['norm2_g'].reshape(1, D_MODEL)
    wo = p['w_out'].astype(BF16)
    rw = p['router_w'].T
    rb = p['router_b'].reshape(N_EXPERTS, 1)
    tri = jnp.triu(jnp.ones((tm, tm), BF16))
    nt = t // tm
    n = nb * t
    topk = pl.BlockSpec((1, TOP_K, tm), lambda b, i: (b * nt + i, 0, 0))
    topk_i = jax.ShapeDtypeStruct((nb * nt, TOP_K, tm), jnp.int32)
    x1, h2, ti, rk, tg, cnt = pl.pallas_call(
        _post_kernel,
        grid=(nb, nt),
        in_specs=[tile(D_MODEL), tile(D_A), tile(D_A), tile(D_B), mspec(2), mspec(3), mspec(4),
                  full(n2), full(wo), full(rw), full(rb), full(tri)],
        out_specs=[tile(D_MODEL),
                   pl.BlockSpec((tm * TILE_CHUNKS, LANES), lambda b, i: (b * nt + i, 0)),
                   topk, topk, topk,
                   pl.BlockSpec((1, N_EXPERTS, LANES), lambda b, i: (b * nt + i, 0, 0))],
        out_shape=[jax.ShapeDtypeStruct((nb, t, D_MODEL), F32),
                   jax.ShapeDtypeStruct((n * TILE_CHUNKS, LANES), F32),
                   topk_i, topk_i,
                   jax.ShapeDtypeStruct((nb * nt, TOP_K, tm), F32),
                   jax.ShapeDtypeStruct((nb * nt, N_EXPERTS, LANES), F32)],
        compiler_params=_cparams(("arbitrary", "arbitrary")),
        name="outproj_router",
    )(x, wkv, g, yb, gate1, shift2, scale2, n2, wo, rw, rb, tri)
    return x1, h2, ti, rk, tg, cnt[:, :, 0].astype(jnp.int32)


def _plan(routes):
    cnt = jnp.concatenate([r[2] for r in routes], axis=0)
    total = jnp.sum(cnt, axis=0)
    run_start = jnp.cumsum(cnt, axis=0) - cnt
    padded = (total + MOE_ROWS - 1) // MOE_ROWS * MOE_ROWS
    pad_end = jnp.cumsum(padded)
    pad_start = pad_end - padded
    loc_off = jnp.cumsum(cnt, axis=1) - cnt
    dst0 = pad_start[None, :] + run_start
    n_tokens = sum(r[0].shape[0] * r[0].shape[2] for r in routes)
    n_blocks = -(-n_tokens * TOP_K // MOE_ROWS) + N_EXPERTS
    block_row0 = jnp.arange(n_blocks, dtype=jnp.int32) * MOE_ROWS
    block_e = jnp.minimum(jnp.sum(pad_end[None, :] <= block_row0[:, None], axis=1), N_EXPERTS - 1)
    n_used = (pad_end[-1] // MOE_ROWS).reshape(1)
    as_i32 = lambda a: a.astype(jnp.int32)
    experts = jnp.arange(N_EXPERTS, dtype=jnp.int32)
    groups, t0 = [], 0
    for ti, rk, c in routes:
        nt = ti.shape[0]
        sl = slice(t0, t0 + nt)
        off = jnp.sum(jnp.where(ti[..., None] == experts, loc_off[sl, None, None, :], 0), axis=-1)
        lpos = as_i32((rk + off) * TILE_CHUNKS).reshape(nt, -1)
        table = as_i32(jnp.stack([cnt[sl], loc_off[sl], dst0[sl]], axis=0).reshape(3, -1))
        groups.append((lpos, table))
        t0 += nt
    return groups, as_i32(block_e), as_i32(n_used), as_i32(pad_end), as_i32(padded), n_blocks


def _tile_rows(row):
    return pl.ds(pl.multiple_of(row * TILE_CHUNKS, TILE_CHUNKS), TILE_CHUNKS)


def _sublane_rows(row8):
    return pl.ds(pl.multiple_of(row8, TILE_CHUNKS), TILE_CHUNKS)


def _piece_rows(row0, n_rows):
    return pl.ds(pl.multiple_of(row0 * TILE_CHUNKS, TILE_CHUNKS), n_rows * TILE_CHUNKS)


def _block_rows(block):
    rows = MOE_ROWS * TILE_CHUNKS
    return pl.ds(pl.multiple_of(block * rows, rows), rows)


def _for_each_piece(table_ref, tile, n_tiles, fn):
    def per_expert(e, carry):
        col = tile * N_EXPERTS + e
        cnt = table_ref[col]
        lo = table_ref[n_tiles * N_EXPERTS + col]
        d0 = table_ref[2 * n_tiles * N_EXPERTS + col]

        def full(c, c2):
            fn(lo + c * MOE_CHUNK, d0 + c * MOE_CHUNK, MOE_CHUNK)
            return c2
        lax.fori_loop(0, cnt // MOE_CHUNK, full, 0)
        bit = MOE_CHUNK // 2
        while bit >= 1:
            done = cnt - cnt % (2 * bit)

            @pl.when((cnt & bit) != 0)
            def _(done=done, bit=bit):
                fn(lo + done, d0 + done, bit)
            bit //= 2
        return carry
    lax.fori_loop(0, N_EXPERTS, per_expert, 0)


def _scatter_kernel(*refs, first):
    idx_ref, loc_ref, zero_ref, sem, isem, zsem = refs[-6:]
    table_ref, pad_end_ref, padded_ref, h2_ref, lpos_hbm = refs[:5]
    xs_hbm = refs[-7]
    i = pl.program_id(0)
    n_tiles = pl.num_programs(0)
    tm = h2_ref.shape[0] // TILE_CHUNKS
    slot = i % 2

    def idx_copy(tile, s):
        dst = idx_ref.at[pl.ds(s * (TOP_K * tm), TOP_K * tm)]
        return pltpu.make_async_copy(lpos_hbm.at[tile], dst, isem.at[s])

    def drain(s):
        for _ in range(TOP_K):
            pltpu.make_async_copy(h2_ref, xs_hbm.at[pl.ds(0, tm * TILE_CHUNKS), :], sem.at[s]).wait()

    @pl.when(i == 0)
    def _():
        idx_copy(0, 0).start()

    if first:
        @pl.when(i == 0)
        def _():
            zero_ref[...] = jnp.zeros_like(zero_ref)
            n_used = pad_end_ref[N_EXPERTS - 1] // MOE_ROWS
            n_blocks = xs_hbm.shape[0] // (MOE_ROWS * TILE_CHUNKS)

            def fill(e, n_started):
                n_fill = jnp.minimum(padded_ref[e] // MOE_ROWS, 2)
                last = pad_end_ref[e] // MOE_ROWS - 1

                def one(j, c):
                    pltpu.make_async_copy(zero_ref, xs_hbm.at[_block_rows(last - j), :], zsem).start()
                    return c
                lax.fori_loop(0, n_fill, one, 0)
                return n_started + n_fill
            n_started = lax.fori_loop(0, N_EXPERTS, fill, jnp.int32(0))

            def fill_tail(blk, c):
                pltpu.make_async_copy(zero_ref, xs_hbm.at[_block_rows(blk), :], zsem).start()
                return c
            lax.fori_loop(n_used, n_blocks, fill_tail, 0)

            def drain_fill(j, c):
                pltpu.make_async_copy(zero_ref, xs_hbm.at[_block_rows(0), :], zsem).wait()
                return c
            lax.fori_loop(0, n_started + (n_blocks - n_used), drain_fill, 0)

    def step(s):
        @pl.when(i + 1 < n_tiles)
        def _():
            idx_copy(i + 1, 1 - s).start()

        @pl.when(i >= 2)
        def _():
            drain(s)
        idx_copy(i, s).wait()
        base = s * (TOP_K * tm)

        def place(r, c):
            row = h2_ref[_tile_rows(r), :]
            for k in range(TOP_K):
                loc_ref[s, _sublane_rows(idx_ref[base + k * tm + r]), :] = row
            return c
        lax.fori_loop(0, tm, place, 0, unroll=8)

        def send(lo, d0, n):
            pltpu.make_async_copy(loc_ref.at[s, _piece_rows(lo, n), :], xs_hbm.at[_piece_rows(d0, n), :],
                                  sem.at[s]).start()
        _for_each_piece(table_ref, i, n_tiles, send)

        @pl.when(i == n_tiles - 1)
        def _():
            @pl.when(i >= 1)
            def _():
                drain(1 - s)
            drain(s)

    for s in range(2):
        pl.when(slot == s)(functools.partial(step, s))


def _scatter(h2, lpos, table, pad_end, padded, n_rows, tm, xs=None):
    nt = lpos.shape[0]
    first = xs is None
    assert first or nt * tm <= MOE_ROWS, "a later call may add at most MOE_ROWS rows per expert"
    loc_rows = TOP_K * tm * TILE_CHUNKS
    in_specs = [pl.BlockSpec((tm * TILE_CHUNKS, LANES), lambda i, *_: (i, 0)),
                pl.BlockSpec(memory_space=pl.ANY)]
    args = [table.reshape(-1), pad_end, padded, h2, lpos]
    if not first:
        in_specs.append(pl.BlockSpec(memory_space=pl.ANY))
        args.append(xs)
    grid_spec = pltpu.PrefetchScalarGridSpec(
        num_scalar_prefetch=3,
        grid=(nt,),
        in_specs=in_specs,
        out_specs=pl.BlockSpec(memory_space=pl.ANY),
        scratch_shapes=[pltpu.SMEM((2 * TOP_K * tm,), jnp.int32),
                        pltpu.VMEM((2, loc_rows, LANES), F32),
                        pltpu.VMEM((MOE_ROWS * TILE_CHUNKS, LANES), F32),
                        pltpu.SemaphoreType.DMA((2,)), pltpu.SemaphoreType.DMA((2,)), pltpu.SemaphoreType.DMA],
    )
    return pl.pallas_call(
        functools.partial(_scatter_kernel, first=first),
        grid_spec=grid_spec,
        out_shape=jax.ShapeDtypeStruct((n_rows * TILE_CHUNKS, LANES), F32),
        input_output_aliases={} if first else {5: 0},
        compiler_params=_cparams(("arbitrary",)),
        name="moe_dispatch",
    )(*args)


def _mlp_kernel(be_ref, nu_ref, next_ref, par_ref, xs_ref, bias_ref, wg_hbm, wu_hbm, wd_hbm, out_ref,
                w32_ref, w16_ref, sem):
    i = pl.program_id(0)
    used = i < nu_ref[0]
    e = be_ref[i]

    def weight_copies(expert, s):
        return [pltpu.make_async_copy(w_hbm.at[expert], w32_ref.at[s, j], sem.at[s])
                for j, w_hbm in enumerate((wg_hbm, wu_hbm, wd_hbm))]

    def load_expert(s):
        @pl.when(i == 0)
        def _():
            for c in weight_copies(e, s):
                c.start()
        for c in weight_copies(e, s):
            c.wait()
        nxt = next_ref[e]

        @pl.when(nxt >= 0)
        def _():
            for c in weight_copies(nxt, 1 - s):
                c.start()
        for j in range(3):
            w16_ref[j] = w32_ref[s, j].astype(BF16)

    first_of_expert = used & ((i == 0) | (e != be_ref[jnp.maximum(i - 1, 0)]))
    for s in range(2):
        pl.when(first_of_expert & (par_ref[e] == s))(functools.partial(load_expert, s))

    @pl.when(used)
    def _():
        x = _from_token_tiles(xs_ref, 0, MOE_ROWS).astype(BF16)
        gt = jnp.dot(x, w16_ref[0], preferred_element_type=F32) + bias_ref[0, 0:1, :]
        up = jnp.dot(x, w16_ref[1], preferred_element_type=F32) + bias_ref[0, 1:2, :]
        gt = jnp.minimum(gt, SWIGLU_LIMIT)
        up = jnp.clip(up, -SWIGLU_LIMIT, SWIGLU_LIMIT)
        glu = gt * _sigmoid(gt * SWIGLU_ALPHA)
        mid = ((up + 1.0) * glu).astype(BF16)
        _to_token_tiles(out_ref, jnp.dot(mid, w16_ref[2], preferred_element_type=F32) + bias_ref[0, 2:3, :])

    @pl.when(jnp.logical_not(used))
    def _():
        out_ref[...] = jnp.zeros_like(out_ref)


def _mlp(xs, block_e, n_used, padded, n_blocks, wts):
    wg, bg, wu, bu, wd, bd = wts
    assert D_FF == D_MODEL
    bias = jnp.stack([bg, bu, bd], axis=1)
    nonempty = padded > 0
    experts = jnp.arange(N_EXPERTS, dtype=jnp.int32)
    parity = ((jnp.cumsum(nonempty) - nonempty) % 2).astype(jnp.int32)
    later = nonempty[None, :] & (experts[None, :] > experts[:, None])
    nxt = jnp.min(jnp.where(later, experts[None, :], N_EXPERTS), axis=1)
    nxt = jnp.where(nxt == N_EXPERTS, -1, nxt).astype(jnp.int32)
    rows = pl.BlockSpec((MOE_ROWS * TILE_CHUNKS, LANES), lambda i, *_: (i, 0))
    hbm = pl.BlockSpec(memory_space=pl.ANY)
    grid_spec = pltpu.PrefetchScalarGridSpec(
        num_scalar_prefetch=4,
        grid=(n_blocks,),
        in_specs=[rows, pl.BlockSpec((1, 3, D_FF), lambda i, be, *_: (be[i], 0, 0)), hbm, hbm, hbm],
        out_specs=rows,
        scratch_shapes=[pltpu.VMEM((2, 3, D_MODEL, D_FF), F32),
                        pltpu.VMEM((3, D_MODEL, D_FF), BF16),
                        pltpu.SemaphoreType.DMA((2,))],
    )
    return pl.pallas_call(
        _mlp_kernel,
        grid_spec=grid_spec,
        out_shape=jax.ShapeDtypeStruct(xs.shape, F32),
        compiler_params=_cparams(("arbitrary",)),
        name="moe_experts",
    )(block_e, n_used, nxt, parity, xs, bias, wg, wu, wd)


def _gather_kernel(table_ref, x1_ref, gate2_ref, fg_ref, lpos_hbm, tg_hbm, rows_hbm, y_ref,
                   idx_ref, gsm_ref, loc_ref, ff_ref, sem, isem):
    i = pl.program_id(1) + pl.program_id(0) * pl.num_programs(1)
    n_tiles = pl.num_programs(0) * pl.num_programs(1)
    tm = x1_ref.shape[1]
    slot = i % 2

    def meta_copies(tile, s):
        seg = pl.ds(s * (TOP_K * tm), TOP_K * tm)
        return (pltpu.make_async_copy(lpos_hbm.at[tile], idx_ref.at[seg], isem.at[s]),
                pltpu.make_async_copy(tg_hbm.at[tile], gsm_ref.at[seg], isem.at[s]))

    def fetch_tile(tile, s):
        for c in meta_copies(tile, s):
            c.start()

        def fetch(lo, d0, n):
            pltpu.make_async_copy(rows_hbm.at[_piece_rows(d0, n), :], loc_ref.at[s, _piece_rows(lo, n), :],
                                  sem.at[s]).start()
        _for_each_piece(table_ref, tile, n_tiles, fetch)

    @pl.when(i == 0)
    def _():
        fetch_tile(0, 0)

    def step(s):
        @pl.when(i + 1 < n_tiles)
        def _():
            fetch_tile(i + 1, 1 - s)

        pltpu.make_async_copy(rows_hbm.at[pl.ds(0, TOP_K * tm * TILE_CHUNKS), :], loc_ref.at[s], sem.at[s]).wait()
        for c in meta_copies(i, s):
            c.wait()
        base = s * (TOP_K * tm)

        def mix(r, c):
            acc = gsm_ref[base + r] * loc_ref[s, _sublane_rows(idx_ref[base + r]), :]
            for k in range(1, TOP_K):
                acc = acc + gsm_ref[base + k * tm + r] * loc_ref[s, _sublane_rows(idx_ref[base + k * tm + r]), :]
            ff_ref[_tile_rows(r), :] = acc
            return c
        lax.fori_loop(0, tm, mix, 0, unroll=8)

    for s in range(2):
        pl.when(slot == s)(functools.partial(step, s))

    x2 = x1_ref[0] + gate2_ref[0] * _from_token_tiles(ff_ref, 0, tm)
    y_ref[0] = _rms(x2, fg_ref[...])


def _gather(x1, tg, gate2, mod_map, final_g, lpos, table, rows, tm):
    nb, t, _ = x1.shape
    nt = t // tm
    mod_block = (1,) + gate2.shape[1:]
    loc_rows = TOP_K * tm * TILE_CHUNKS
    grid_spec = pltpu.PrefetchScalarGridSpec(
        num_scalar_prefetch=1,
        grid=(nb, nt),
        in_specs=[pl.BlockSpec((1, tm, D_MODEL), lambda b, i, *_: (b, i, 0)),
                  pl.BlockSpec(mod_block, lambda b, i, *_: mod_map(5, b, i)),
                  pl.BlockSpec((1, D_MODEL), lambda b, i, *_: (0, 0)),
                  pl.BlockSpec(memory_space=pl.ANY),
                  pl.BlockSpec(memory_space=pl.ANY),
                  pl.BlockSpec(memory_space=pl.ANY)],
        out_specs=pl.BlockSpec((1, tm, D_MODEL), lambda b, i, *_: (b, i, 0)),
        scratch_shapes=[pltpu.SMEM((2 * TOP_K * tm,), jnp.int32),
                        pltpu.SMEM((2 * TOP_K * tm,), F32),
                        pltpu.VMEM((2, loc_rows, LANES), F32),
                        pltpu.VMEM((tm * TILE_CHUNKS, LANES), F32),
                        pltpu.SemaphoreType.DMA((2,)), pltpu.SemaphoreType.DMA((2,))],
    )
    return pl.pallas_call(
        _gather_kernel,
        grid_spec=grid_spec,
        out_shape=jax.ShapeDtypeStruct((nb, t, D_MODEL), F32),
        compiler_params=_cparams(("arbitrary", "arbitrary")),
        name="moe_combine",
    )(table.reshape(-1), x1, gate2, final_g.reshape(1, D_MODEL), lpos, tg.reshape(tg.shape[0], -1), rows)


def _moe(route_p, route_s, mod_p, mod_s, final_g, wts):
    routes = [route_p, route_s]
    groups, block_e, n_used, pad_end, padded, n_blocks = _plan([(r[2], r[3], r[5]) for r in routes])
    xs = None
    for (x1, h2, ti, rk, tg, cnt), (lpos, table) in zip(routes, groups):
        xs = _scatter(h2, lpos, table, pad_end, padded, n_blocks * MOE_ROWS, ti.shape[2], xs)
    rows = _mlp(xs, block_e, n_used, padded, n_blocks, wts)
    outs = []
    for (x1, h2, ti, rk, tg, cnt), (lpos, table), (mod, mod_map) in zip(routes, groups, (mod_p, mod_s)):
        outs.append(_gather(x1, tg, mod, mod_map, final_g, lpos, table, rows, ti.shape[2]))
    return outs


def _forward(x_prompt, x_sample, c_prompt, c_sample, state_wkv, state_shift, state_conv, state_lru, p, final_g):
    bp, tp, _ = x_prompt.shape
    bs = x_sample.shape[0]
    tm = min(512, tp)
    tt = min(64, tp)

    mod = _ada(jnp.concatenate([c_prompt, c_sample], axis=0), p['w_ada'], p['b_ada'])
    mod_p = mod[:bp].reshape(bp * N_MOD, 1, D_MODEL)
    mod_s = mod[bp:].reshape(bs, N_MOD, D_MODEL).transpose(1, 0, 2)
    map_p = lambda j, b, i: (b * N_MOD + j, 0, 0)
    map_s = lambda j, b, i: (j, 0, 0)

    wa = p['w_in'][:, :N_COLS_A]
    wb = p['w_in'][:, N_COLS_A:]
    wts = (p['w_gate'], p['b_gate'], p['w_up'], p['b_up'], p['w_down'], p['b_down'])

    pa, pb = _inproj(x_prompt, mod_p, mod_p, map_p, p['norm1_g'], wa.astype(BF16), wb.astype(BF16), tm, False)
    wb_p, kr_p, kv_p, g = _prep(pa, None, p, tm, True)
    s0 = jnp.zeros((bp, N_HEADS, HEAD_DIM, HEAD_DIM), F32)
    wkv_out, wkv_p = _wkv_scan(wb_p[None], kr_p[None], kv_p[None], s0, p, tt)
    yb, lru_p = _lru_seq(pb, p, tm)
    route_p = _post(x_prompt, wkv_out, g, yb, mod_p, mod_p, mod_p, map_p, p, tm)
    shift_p = pa[:, -1, :]
    conv_p = pb[:, tp - (CONV_W - 1):, D_B:]

    xs = x_sample.reshape(1, bs, D_MODEL)
    pa_s, pb_s = _inproj(xs, mod_s, mod_s, map_s, p['norm1_g'], wa, wb, bs, True)
    wb_s, kr_s, kv_s, g = _prep(pa_s, state_shift.reshape(1, bs, N_COLS_A), p, bs, False)
    as_seq = lambda a: _pairs_to_groups(a.reshape(bs, N_HEADS, LANES), 1)
    wkv_out, wkv_s = _wkv_scan(as_seq(wb_s), as_seq(kr_s), as_seq(kv_s), state_wkv, p, 1)
    conv0 = state_conv.transpose(1, 0, 2)
    yb, lru_s = _lru_step(pb_s[0], conv0, state_lru, p)
    route_s = _post(xs, wkv_out.reshape(1, bs, D_A), g, yb.reshape(1, bs, D_B),
                    mod_s, mod_s, mod_s, map_s, p, bs)
    y_prompt, y_sample = _moe(route_p, route_s, (mod_p, map_p), (mod_s, map_s), final_g, wts)
    shift_s = pa_s[0]
    conv_s = jnp.concatenate([state_conv[:, 1:], pb_s[0][:, None, D_B:]], axis=1)

    return (y_prompt, y_sample.reshape(bs, 1, D_MODEL),
            wkv_p[None], shift_p[None], conv_p[None], lru_p.reshape(bp, D_B)[None],
            wkv_s[None], shift_s[None], conv_s[None], lru_s[None])


def kernel(x_prompt, x_sample, c_prompt, c_sample, state_wkv, state_shift, state_conv, state_lru, w_ada, b_ada, norm1_g, norm2_g, w_in, rk_mu, rk_w0, rk_w_up, rk_a0, rk_a_up, rk_g_up, rk_k_k, rk_k_a, rk_r_k, rk_lnx_w, rk_lnx_b, lru_conv_w, lru_conv_b, lru_w_r, lru_b_r, lru_w_i, lru_b_i, lru_lambda, lru_norm_g, w_out, router_w, router_b, w_gate, b_gate, w_up, b_up, w_down, b_down, final_g):
    assert w_ada.shape[0] == 1, "single-layer trunk"
    p = dict(w_ada=w_ada[0], b_ada=b_ada[0], norm1_g=norm1_g[0], norm2_g=norm2_g[0], w_in=w_in[0],
             rk_mu=rk_mu[0], rk_w0=rk_w0[0], rk_w_up=rk_w_up[0], rk_a0=rk_a0[0], rk_a_up=rk_a_up[0],
             rk_g_up=rk_g_up[0], rk_k_k=rk_k_k[0], rk_k_a=rk_k_a[0], rk_r_k=rk_r_k[0],
             rk_lnx_w=rk_lnx_w[0], rk_lnx_b=rk_lnx_b[0], lru_conv_w=lru_conv_w[0],
             lru_conv_b=lru_conv_b[0], lru_w_r=lru_w_r[0], lru_b_r=lru_b_r[0], lru_w_i=lru_w_i[0],
             lru_b_i=lru_b_i[0], lru_lambda=lru_lambda[0], lru_norm_g=lru_norm_g[0], w_out=w_out[0],
             router_w=router_w[0], router_b=router_b[0], w_gate=w_gate[0], b_gate=b_gate[0],
             w_up=w_up[0], b_up=b_up[0], w_down=w_down[0], b_down=b_down[0])
    return _forward(x_prompt, x_sample, c_prompt, c_sample, state_wkv[0], state_shift[0], state_conv[0],
                    state_lru[0], p, final_g)
```

```python
import functools

import jax
import jax.numpy as jnp
from jax import lax
from jax.experimental import pallas as pl
from jax.experimental.pallas import tpu as pltpu

F32 = jnp.float32
BF16 = jnp.bfloat16

D_MODEL = 1024
D_A = 512
HEAD_DIM = 64
N_HEADS = 8
D_B = 512
N_BLOCKS_B = 8
CONV_W = 4
LRU_C = 8.0
R_DECAY = 64
R_AAA = 64
R_GATE = 128
N_COLS_A = 3 * D_A + R_DECAY + R_AAA + R_GATE
N_COLS_B = 2 * D_B
N_EXPERTS = 32
TOP_K = 4
D_FF = 1024
SWIGLU_LIMIT = 7.0
SWIGLU_ALPHA = 1.702
RMS_EPS = 1e-6
LN_X_EPS = 64e-5
N_MOD = 6

LANES = 128
SUBLANES = 8
GROUP_BATCH = 8
HALF_ROWS = HEAD_DIM // 2
MOE_ROWS = 256
MOE_CHUNK = 16
TILE_CHUNKS = D_MODEL // LANES
VMEM_LIMIT = 56 * 1024 * 1024


def _cparams(sem):
    return pltpu.CompilerParams(dimension_semantics=sem, vmem_limit_bytes=VMEM_LIMIT)


def _dot(a, b):
    return jnp.dot(a.astype(BF16), b.astype(BF16), preferred_element_type=F32)


def _split(a):
    hi = a.astype(BF16)
    lo = (a - hi.astype(F32)).astype(BF16)
    return hi, lo


def _dot3(a, b):
    ah, al = _split(a)
    bh, bl = _split(b)
    return (jnp.dot(ah, bh, preferred_element_type=F32)
            + (jnp.dot(al, bh, preferred_element_type=F32) + jnp.dot(ah, bl, preferred_element_type=F32)))


def _dot3_nt(a, b):
    dn = (((1,), (1,)), ((), ()))
    ah, al = _split(a)
    bh, bl = _split(b)
    d = lambda x, y: lax.dot_general(x, y, dn, preferred_element_type=F32)
    return d(ah, bh) + (d(al, bh) + d(ah, bl))


def _softplus(x):
    return jnp.maximum(x, 0.0) + jnp.log1p(jnp.exp(-jnp.abs(x)))


def _sigmoid(x):
    return 1.0 / (1.0 + jnp.exp(-x))


def _rms(x, g):
    ms = jnp.mean(x * x, axis=-1, keepdims=True)
    return x * lax.rsqrt(ms + RMS_EPS) * g


def _ada_kernel(c_ref, w_ref, b_ref, o_ref):
    c = c_ref[...]
    s = c * _sigmoid(c)
    o_ref[...] = _dot3(s, w_ref[...]) + b_ref[...]


def _ada(c, w_ada, b_ada):
    rows = c.shape[0]
    ncol = w_ada.shape[1]
    tn = D_MODEL
    return pl.pallas_call(
        _ada_kernel,
        grid=(ncol // tn,),
        in_specs=[pl.BlockSpec((rows, D_MODEL), lambda j: (0, 0)),
                  pl.BlockSpec((D_MODEL, tn), lambda j: (0, j)),
                  pl.BlockSpec((1, tn), lambda j: (0, j))],
        out_specs=pl.BlockSpec((rows, tn), lambda j: (0, j)),
        out_shape=jax.ShapeDtypeStruct((rows, ncol), F32),
        compiler_params=_cparams(("arbitrary",)),
        name="ada_mod",
    )(c, w_ada, b_ada.reshape(1, ncol))


def _inproj_kernel(x_ref, shift_ref, scale_ref, g_ref, wa_ref, wb_ref, pa_ref, pb_ref, *, precise):
    x = x_ref[0]
    h = _rms(x, g_ref[...]) * (1.0 + scale_ref[0]) + shift_ref[0]
    dot = _dot3 if precise else _dot
    pa_ref[0] = dot(h, wa_ref[...])
    pb_ref[0] = dot(h, wb_ref[...])


def _inproj(x, shift, scale, mod_map, g, wa, wb, tm, precise):
    nb, t, _ = x.shape
    mod_block = (1,) + shift.shape[1:]
    return pl.pallas_call(
        functools.partial(_inproj_kernel, precise=precise),
        grid=(nb, t // tm),
        in_specs=[pl.BlockSpec((1, tm, D_MODEL), lambda b, i: (b, i, 0)),
                  pl.BlockSpec(mod_block, functools.partial(mod_map, 0)),
                  pl.BlockSpec(mod_block, functools.partial(mod_map, 1)),
                  pl.BlockSpec((1, D_MODEL), lambda b, i: (0, 0)),
                  pl.BlockSpec((D_MODEL, N_COLS_A), lambda b, i: (0, 0)),
                  pl.BlockSpec((D_MODEL, N_COLS_B), lambda b, i: (0, 0))],
        out_specs=[pl.BlockSpec((1, tm, N_COLS_A), lambda b, i: (b, i, 0)),
                   pl.BlockSpec((1, tm, N_COLS_B), lambda b, i: (b, i, 0))],
        out_shape=[jax.ShapeDtypeStruct((nb, t, N_COLS_A), F32),
                   jax.ShapeDtypeStruct((nb, t, N_COLS_B), F32)],
        compiler_params=_cparams(("arbitrary", "arbitrary")),
        name="norm1_inproj",
    )(x, shift, scale, g.reshape(1, D_MODEL), wa, wb)


def _store_head_pairs(ref, stage_ref, x, y):
    rows = x.shape[0]
    flat = ref.shape[0] == 1
    for h in range(N_HEADS):
        sl = slice(h * HEAD_DIM, (h + 1) * HEAD_DIM)
        pair = jnp.concatenate([x[:, sl], y[:, sl]], axis=1)
        if flat:
            ref[0, pl.ds(h, rows, stride=N_HEADS), :] = pair
        else:
            stage_ref[pl.ds(h, rows, stride=N_HEADS), :] = pair
    if not flat:
        ref[...] = stage_ref[...].reshape(rows, N_HEADS, LANES)


def _prep_math(pa, prev, mu_ref, w0_ref, wup_ref, a0_ref, gup_ref, kk_ref, ka_ref,
               wb_out, kr_out, kv_out, g_out, stage_ref):
    z = pa + (prev - pa) * mu_ref[...]
    r = z[:, 0:D_A]
    k = z[:, D_A:2 * D_A]
    v = z[:, 2 * D_A:3 * D_A]
    lo = 3 * D_A
    za = z[:, lo:lo + R_DECAY + R_AAA]
    lane = lax.broadcasted_iota(jnp.int32, za.shape, 1)
    za = jnp.where(lane < R_DECAY, jnp.tanh(za), za)
    lw = _dot3(za, wup_ref[...])
    w_log = -_softplus(-(w0_ref[...] + lw[:, :D_A])) - 0.5
    decay = jnp.exp(-jnp.exp(w_log))
    a = _sigmoid(a0_ref[...] + lw[:, D_A:])
    gd = z[:, lo + R_DECAY + R_AAA:]
    g = _dot3(_sigmoid(gd), gup_ref[...])
    kk = k * kk_ref[...]
    _store_head_pairs(wb_out, stage_ref, decay, kk * a)
    _store_head_pairs(kr_out, stage_ref, k * (1.0 + (a - 1.0) * ka_ref[...]), r)
    _store_head_pairs(kv_out, stage_ref, kk, v)
    g_out[0] = g


def _prep_kernel(pa_ref, prev_ref, *refs):
    _prep_math(pa_ref[0], prev_ref[0], *refs, None)


def _prep_params(p):
    wup = jnp.zeros((R_DECAY + R_AAA, 2 * D_A), F32)
    wup = wup.at[:R_DECAY, :D_A].set(p['rk_w_up']).at[R_DECAY:, D_A:].set(p['rk_a_up'])
    vec = lambda a: a.reshape(1, -1)
    return (vec(p['rk_mu']), vec(p['rk_w0']), wup, vec(p['rk_a0']), p['rk_g_up'],
            vec(p['rk_k_k']), vec(p['rk_k_a']))


def _const_spec(a):
    return pl.BlockSpec(a.shape, lambda b, i: (0,) * a.ndim)


def _prep(pa, prev, p, tm):
    nb, t, _ = pa.shape
    params = _prep_params(p)
    rows = pl.BlockSpec((1, tm, N_COLS_A), lambda b, i: (b, i, 0))
    return pl.pallas_call(
        _prep_kernel,
        grid=(nb, t // tm),
        in_specs=[rows, rows] + [_const_spec(a) for a in params],
        out_specs=[pl.BlockSpec((1, tm * N_HEADS, LANES), lambda b, i: (b, i, 0))] * 3
                  + [pl.BlockSpec((1, tm, D_A), lambda b, i: (b, i, 0))],
        out_shape=[jax.ShapeDtypeStruct((nb, t * N_HEADS, LANES), F32)] * 3
                  + [jax.ShapeDtypeStruct((nb, t, D_A), F32)],
        compiler_params=_cparams(("arbitrary", "arbitrary")),
        name="rwkv_prep",
    )(pa, prev, *params)


SCAN_SLOTS = 4


def _scan_kernel(wb_ref, kr_ref, kv_ref, kvnext_ref, s0_ref, lnw_ref, lnb_ref, rk_ref,
                 out_ref, sfin_ref, s_ref, sa_ref, inv_ref, wd_ref, kkd_ref, bd_ref, kd_ref, rd_ref, vd_ref, *, tt):
    ti = pl.program_id(1)
    upper_half = lax.broadcasted_iota(jnp.int32, (HALF_ROWS, LANES), 1) >= LANES // 2
    unroll = SCAN_SLOTS if tt % SCAN_SLOTS == 0 else 1

    def expand(pair):
        return jnp.concatenate([pair, pair], axis=0).T

    def prepare(slot, s, kk_pair=None):
        t1 = expand(wb_ref[0, s])
        wd_ref[slot] = t1[:HEAD_DIM]
        bd_ref[slot] = t1[HEAD_DIM:]
        t2 = expand(kr_ref[0, s])
        kd_ref[slot] = t2[:HEAD_DIM]
        rd_ref[slot] = t2[HEAD_DIM:]
        t3 = expand(kv_ref[0, s])
        vd_ref[slot] = jnp.where(upper_half, t3[HEAD_DIM + HALF_ROWS:], t3[HEAD_DIM:HEAD_DIM + HALF_ROWS])
        kkd_ref[slot] = t3[:HEAD_DIM] if kk_pair is None else expand(kk_pair)[:HEAD_DIM]

    prepare(0, 0)
    if tt > 1:
        prepare(1, 1)
    else:
        kkd_ref[1] = expand(kvnext_ref[0, 0])[:HEAD_DIM]

    def inv_norm2(kk_rows):
        s2 = jnp.sum(kk_rows * kk_rows, axis=0, keepdims=True)
        return 1.0 / jnp.maximum(s2, 1e-24)

    @pl.when(ti == 0)
    def _():
        s_ref[...] = s0_ref[0]
        acc = jnp.zeros((HALF_ROWS, LANES), F32)
        for j in range(HEAD_DIM):
            acc = acc + s0_ref[0, j] * kkd_ref[0, j:j + 1, :]
        sa_ref[...] = acc
        inv_ref[...] = inv_norm2(kkd_ref[0])

    def finish(y, cv):
        tot = jnp.sum(y, axis=0, keepdims=True)
        tot = tot + pltpu.roll(tot, LANES // 2, axis=1)
        d = y - tot * (1.0 / HEAD_DIM)
        sq = jnp.sum(d * d, axis=0, keepdims=True)
        sq = sq + pltpu.roll(sq, LANES // 2, axis=1)
        yn = d * lax.rsqrt(sq * (1.0 / HEAD_DIM) + LN_X_EPS)
        return yn * lnw_ref[...] + lnb_ref[...] + cv

    def step(t, u, carry):
        sa, inv2, y_prev, cv_prev = carry
        nxt_slot = (u + 1) % SCAN_SLOTS
        out_ref[0, jnp.maximum(t - 1, 0)] = finish(y_prev, cv_prev)
        sae = sa * (-inv2)
        v = vd_ref[u]
        acc_y = jnp.zeros((HALF_ROWS, LANES), F32)
        acc_s = jnp.zeros((HALF_ROWS, LANES), F32)
        for j in range(HEAD_DIM):
            row = pl.ds(j, 1)
            s_new = s_ref[j] * wd_ref[u, row, :] + sae * bd_ref[u, row, :] + v * kd_ref[u, row, :]
            s_ref[j] = s_new
            acc_y = acc_y + s_new * rd_ref[u, row, :]
            acc_s = acc_s + s_new * kkd_ref[nxt_slot, row, :]
        c = jnp.sum(rd_ref[u] * kd_ref[u] * rk_ref[...], axis=0, keepdims=True)
        if unroll > 1:
            ahead = jnp.minimum(t + 2, tt - 1)
            kk_pair = None
            if (u + 2) % SCAN_SLOTS == 0:
                kk_pair = jnp.where(t + 2 == tt, kvnext_ref[0, 0], kv_ref[0, ahead])
            prepare((u + 2) % SCAN_SLOTS, ahead, kk_pair)
        return acc_s, inv_norm2(kkd_ref[nxt_slot]), acc_y, c * v

    def steps(q, carry):
        for u in range(unroll):
            carry = step(q * unroll + u, u, carry)
        return carry

    zeros = jnp.zeros((HALF_ROWS, LANES), F32)
    sa, inv2, y_last, cv_last = lax.fori_loop(0, tt // unroll, steps, (sa_ref[...], inv_ref[...], zeros, zeros))
    out_ref[0, tt - 1] = finish(y_last, cv_last)
    sa_ref[...] = sa
    inv_ref[...] = inv2

    @pl.when(ti == pl.num_programs(1) - 1)
    def _():
        sfin_ref[0] = s_ref[...]


def _from_scan_rows(y):
    g, t = y.shape[:2]
    x = y.reshape(g, t, HALF_ROWS, 2, GROUP_BATCH, N_HEADS).transpose(0, 4, 1, 5, 3, 2)
    return x.reshape(g * GROUP_BATCH, t, D_A)


def _state_to_scan(s):
    g = s.shape[0] // GROUP_BATCH
    y = s.reshape(g, GROUP_BATCH, N_HEADS, 2, HALF_ROWS, HEAD_DIM).transpose(0, 5, 4, 3, 1, 2)
    return y.reshape(g, HEAD_DIM, HALF_ROWS, LANES)


def _state_from_scan(y):
    g = y.shape[0]
    s = y.reshape(g, HEAD_DIM, HALF_ROWS, 2, GROUP_BATCH, N_HEADS).transpose(0, 4, 5, 3, 2, 1)
    return s.reshape(g * GROUP_BATCH, N_HEADS, HEAD_DIM, HEAD_DIM)


def _head_rows(x):
    y = x.reshape(N_HEADS, 2, HALF_ROWS).transpose(2, 1, 0)
    y = jnp.broadcast_to(y[:, :, None, :], (HALF_ROWS, 2, GROUP_BATCH, N_HEADS))
    return y.reshape(HALF_ROWS, LANES)


def _head_keys(x):
    y = jnp.broadcast_to(x.T[:, None, None, :], (HEAD_DIM, 2, GROUP_BATCH, N_HEADS))
    return y.reshape(HEAD_DIM, LANES)


def _pairs_to_groups(x, t):
    nb = x.shape[0]
    g = nb // GROUP_BATCH
    y = x.reshape(g, GROUP_BATCH, t, N_HEADS, LANES).transpose(0, 2, 1, 3, 4)
    return y.reshape(g, t, GROUP_BATCH * N_HEADS, LANES)


def _wkv_scan(wb, kr, kv, s0, p, tt):
    g, t = wb.shape[:2]
    pair_tile = pl.BlockSpec((1, tt, HEAD_DIM, LANES), lambda gi, i: (gi, i, 0, 0))
    next_step = pl.BlockSpec((1, 1, HEAD_DIM, LANES), lambda gi, i: (gi, jnp.minimum((i + 1) * tt, t - 1), 0, 0))
    row_tile = pl.BlockSpec((1, tt, HALF_ROWS, LANES), lambda gi, i: (gi, i, 0, 0))
    state = pl.BlockSpec((1, HEAD_DIM, HALF_ROWS, LANES), lambda gi, i: (gi, 0, 0, 0))
    const = lambda n: pl.BlockSpec((n, LANES), lambda gi, i: (0, 0))
    keys = pltpu.VMEM((SCAN_SLOTS, HEAD_DIM, LANES), F32)
    out, sfin = pl.pallas_call(
        functools.partial(_scan_kernel, tt=tt),
        grid=(g, t // tt),
        in_specs=[pair_tile, pair_tile, pair_tile, next_step,
                  state, const(HALF_ROWS), const(HALF_ROWS), const(HEAD_DIM)],
        out_specs=[row_tile, state],
        out_shape=[jax.ShapeDtypeStruct((g, t, HALF_ROWS, LANES), F32),
                   jax.ShapeDtypeStruct((g, HEAD_DIM, HALF_ROWS, LANES), F32)],
        scratch_shapes=[pltpu.VMEM((HEAD_DIM, HALF_ROWS, LANES), F32),
                        pltpu.VMEM((HALF_ROWS, LANES), F32),
                        pltpu.VMEM((1, LANES), F32),
                        keys, keys, keys, keys, keys,
                        pltpu.VMEM((SCAN_SLOTS, HALF_ROWS, LANES), F32)],
        compiler_params=_cparams(("arbitrary", "arbitrary")),
        name="wkv_scan",
    )(wb, kr, kv, kv, _state_to_scan(s0),
      _head_rows(p['rk_lnx_w']), _head_rows(p['rk_lnx_b']), _head_keys(p['rk_r_k']))
    return _from_scan_rows(out), _state_from_scan(sfin)


def _gelu(x):
    return 0.5 * x * (1.0 + jnp.tanh(0.7978845608028654 * (x + 0.044715 * (x * x * x))))


def _lru_gates(xc, wri_ref, bri_ref, nsl_ref, precise):
    dot = _dot3 if precise else _dot
    gates = _sigmoid(dot(xc, wri_ref[...]) + bri_ref[...])
    gate_r = gates[:, :D_B]
    gate_i = gates[:, D_B:]
    log_a = gate_r * nsl_ref[...]
    a = jnp.exp(log_a)
    th = jnp.tanh(log_a)
    one_minus_a2 = -2.0 * th / (1.0 - th)
    bt = jnp.sqrt(one_minus_a2) * (gate_i * xc)
    return a, bt


def _lru_seq_math(pb, cw_ref, cb_ref, wri_ref, bri_ref, nsl_ref, ng_ref, out_ref, hlast_ref, xprev_ref, h_ref):
    tm = pb.shape[0]
    yb = pb[:, :D_B]
    xb = pb[:, D_B:]
    xprev = xprev_ref[...]
    row8 = lax.broadcasted_iota(jnp.int32, (SUBLANES, D_B), 0)

    def shifted(d):
        rolled = pltpu.roll(xb, d, axis=0)
        top = jnp.where(row8 < d, pltpu.roll(xprev, d, axis=0), rolled[:SUBLANES])
        return jnp.concatenate([top, rolled[SUBLANES:]], axis=0)

    xc = cb_ref[...] + cw_ref[3:4, :] * xb
    for d in range(1, CONV_W):
        xc = xc + cw_ref[3 - d:4 - d, :] * shifted(d)
    xprev_ref[...] = xb[tm - SUBLANES:, :]

    a, x = _lru_gates(xc, wri_ref, bri_ref, nsl_ref, False)
    row = lax.broadcasted_iota(jnp.int32, (tm, D_B), 0)
    d = 1
    while d < tm:
        if d < SUBLANES:
            keep = row >= d
            a_s = jnp.where(keep, pltpu.roll(a, d, axis=0), 1.0)
            x_s = jnp.where(keep, pltpu.roll(x, d, axis=0), 0.0)
            x = a * x_s + x
            a = a * a_s
        else:
            x = jnp.concatenate([x[:d], a[d:] * x[:tm - d] + x[d:]], axis=0)
            a = jnp.concatenate([a[:d], a[d:] * a[:tm - d]], axis=0)
        d *= 2
    h = a * h_ref[...] + x
    h_ref[...] = h[tm - 1:, :]
    hlast_ref[0] = h[tm - 1:, :]
    out_ref[0] = _rms(h * _gelu(yb), ng_ref[...]).astype(out_ref.dtype)


def _lru_params(p):
    eye = jnp.eye(N_BLOCKS_B, dtype=F32)
    bd = lambda w: (eye[:, None, :, None] * w[:, :, None, :]).reshape(D_B, D_B)
    wri = jnp.concatenate([bd(p['lru_w_r']), bd(p['lru_w_i'])], axis=1)
    bri = jnp.concatenate([p['lru_b_r'], p['lru_b_i']]).reshape(1, 2 * D_B)
    nsl = (-LRU_C * jax.nn.softplus(-p['lru_lambda'])).reshape(1, D_B)
    return wri, bri, nsl


def _front_kernel(x_ref, shift_ref, scale_ref, g1_ref, wa_ref, wb_ref,
                  mu_ref, w0_ref, wup_ref, a0_ref, gup_ref, kk_ref, ka_ref,
                  cw_ref, cb_ref, wri_ref, bri_ref, nsl_ref, ng_ref,
                  wb_out, kr_out, kv_out, g_out, yb_out, hlast_out, patail_out, xtail_out,
                  carry_ref, stage_ref, xprev_ref, h_ref):
    @pl.when(pl.program_id(1) == 0)
    def _():
        carry_ref[...] = jnp.zeros_like(carry_ref)
        xprev_ref[...] = jnp.zeros_like(xprev_ref)
        h_ref[...] = jnp.zeros_like(h_ref)

    h = (_rms(x_ref[0], g1_ref[...]) * (1.0 + scale_ref[0]) + shift_ref[0]).astype(BF16)
    pa = jnp.dot(h, wa_ref[...], preferred_element_type=F32)
    pb = jnp.dot(h, wb_ref[...], preferred_element_type=F32)
    tm = pa.shape[0]
    patail_out[0] = pa[tm - SUBLANES:, :]
    xtail_out[0] = pb[tm - SUBLANES:, D_B:]

    rolled = pltpu.roll(pa, 1, axis=0)
    row = lax.broadcasted_iota(jnp.int32, pa.shape, 0)
    prev = jnp.where(row == 0, carry_ref[...], rolled)
    carry_ref[...] = pa[tm - 1:, :]
    _prep_math(pa, prev, mu_ref, w0_ref, wup_ref, a0_ref, gup_ref, kk_ref, ka_ref,
               wb_out, kr_out, kv_out, g_out, stage_ref)
    _lru_seq_math(pb, cw_ref, cb_ref, wri_ref, bri_ref, nsl_ref, ng_ref, yb_out, hlast_out, xprev_ref, h_ref)


def _front(x, shift, scale, mod_map, p, wa, wb, tm):
    nb, t, _ = x.shape
    assert nb == GROUP_BATCH
    wri, bri, nsl = _lru_params(p)
    consts = ((p['norm1_g'].reshape(1, D_MODEL), wa, wb) + _prep_params(p)
              + (p['lru_conv_w'], p['lru_conv_b'].reshape(1, D_B), wri.astype(BF16), bri, nsl,
                 p['lru_norm_g'].reshape(1, D_B)))
    mod_block = (1,) + shift.shape[1:]
    tile = lambda n: pl.BlockSpec((1, tm, n), lambda b, i: (b, i, 0))
    tail = lambda n: pl.BlockSpec((1, SUBLANES, n), lambda b, i: (b, 0, 0))
    pair_tile = pl.BlockSpec((tm, N_HEADS, LANES), lambda b, i: (i, b, 0))
    pair_shape = jax.ShapeDtypeStruct((t, nb * N_HEADS, LANES), F32)
    return pl.pallas_call(
        _front_kernel,
        grid=(nb, t // tm),
        in_specs=[tile(D_MODEL),
                  pl.BlockSpec(mod_block, functools.partial(mod_map, 0)),
                  pl.BlockSpec(mod_block, functools.partial(mod_map, 1))] + [_const_spec(a) for a in consts],
        out_specs=[pair_tile] * 3 + [tile(D_A), tile(D_B), pl.BlockSpec((1, 1, D_B), lambda b, i: (b, 0, 0)),
                                     tail(N_COLS_A), tail(D_B)],
        out_shape=[pair_shape] * 3 + [jax.ShapeDtypeStruct((nb, t, D_A), F32),
                                      jax.ShapeDtypeStruct((nb, t, D_B), BF16),
                                      jax.ShapeDtypeStruct((nb, 1, D_B), F32),
                                      jax.ShapeDtypeStruct((nb, SUBLANES, N_COLS_A), F32),
                                      jax.ShapeDtypeStruct((nb, SUBLANES, D_B), F32)],
        scratch_shapes=[pltpu.VMEM((1, N_COLS_A), F32), pltpu.VMEM((tm * N_HEADS, LANES), F32),
                        pltpu.VMEM((SUBLANES, D_B), F32), pltpu.VMEM((1, D_B), F32)],
        compiler_params=_cparams(("arbitrary", "arbitrary")),
        name="prompt_front",
    )(x, shift, scale, *consts)


def _lru_step_kernel(pb_ref, conv_ref, h0_ref, cw_ref, cb_ref, wri_ref, bri_ref, nsl_ref, ng_ref,
                     out_ref, hnew_ref):
    pb = pb_ref[...]
    yb = pb[:, :D_B]
    xb = pb[:, D_B:]
    xc = cb_ref[...] + cw_ref[3:4, :] * xb
    for j in range(CONV_W - 1):
        xc = xc + cw_ref[j:j + 1, :] * conv_ref[j]
    a, x = _lru_gates(xc, wri_ref, bri_ref, nsl_ref, True)
    h = a * h0_ref[...] + x
    hnew_ref[...] = h
    out_ref[...] = _rms(h * _gelu(yb), ng_ref[...]).astype(out_ref.dtype)


def _lru_step(pb, conv0, h0, p):
    n = pb.shape[0]
    wri, bri, nsl = _lru_params(p)
    return pl.pallas_call(
        _lru_step_kernel,
        out_shape=[jax.ShapeDtypeStruct((n, D_B), BF16), jax.ShapeDtypeStruct((n, D_B), F32)],
        compiler_params=pltpu.CompilerParams(vmem_limit_bytes=VMEM_LIMIT),
        name="rglru_step",
    )(pb, conv0, h0, p['lru_conv_w'], p['lru_conv_b'].reshape(1, D_B), wri, bri, nsl,
      p['lru_norm_g'].reshape(1, D_B))


def _to_token_tiles(ref, x):
    rows = x.shape[0]
    for c in range(TILE_CHUNKS):
        ref[pl.ds(c, rows, stride=TILE_CHUNKS), :] = x[:, c * LANES:(c + 1) * LANES]


def _from_token_tiles(ref, row0, rows):
    return jnp.concatenate(
        [ref[pl.ds(row0 + c, rows, stride=TILE_CHUNKS), :] for c in range(TILE_CHUNKS)], axis=1)


def _post_kernel(x_ref, wkv_ref, g_ref, yb_ref, gate1_ref, shift2_ref, scale2_ref, n2_ref,
                 wo_ref, rw_ref, rb_ref, tri_ref, x1_ref, h2_ref, ti_ref, rk_ref, tg_ref, cnt_ref):
    ya = (wkv_ref[0] * g_ref[0]).astype(BF16)
    mixed = (jnp.dot(ya, wo_ref[:D_A, :], preferred_element_type=F32)
             + jnp.dot(yb_ref[0], wo_ref[D_A:, :], preferred_element_type=F32))
    x1 = x_ref[0] + gate1_ref[0] * mixed
    x1_ref[0] = x1
    h2 = _rms(x1, n2_ref[...]) * (1.0 + scale2_ref[0]) + shift2_ref[0]
    _to_token_tiles(h2_ref, h2)
    logits = _dot3_nt(rw_ref[...], h2) + rb_ref[...]
    eidx = lax.broadcasted_iota(jnp.int32, logits.shape, 0)
    vals, idxs = [], []
    cur = logits
    for _ in range(TOP_K):
        m = jnp.max(cur, axis=0, keepdims=True)
        i = jnp.min(jnp.where(cur == m, eidx, N_EXPERTS), axis=0, keepdims=True)
        vals.append(m)
        idxs.append(i)
        cur = jnp.where(eidx == i, -jnp.inf, cur)
    ex = [jnp.exp(v - vals[0]) for v in vals]
    den = ex[0] + ex[1] + ex[2] + ex[3]
    sel = [eidx == i for i in idxs]
    onehot = (sel[0] | sel[1] | sel[2] | sel[3]).astype(F32)
    incl = jnp.dot(onehot.astype(BF16), tri_ref[...], preferred_element_type=F32)
    rank = incl - onehot
    cnt_ref[0] = jnp.broadcast_to(jnp.sum(onehot, axis=1, keepdims=True), cnt_ref.shape[1:])
    for k in range(TOP_K):
        ti_ref[0, k:k + 1, :] = idxs[k]
        rk_ref[0, k:k + 1, :] = jnp.sum(jnp.where(sel[k], rank, 0.0), axis=0, keepdims=True).astype(jnp.int32)
        tg_ref[0, k:k + 1, :] = ex[k] / den


def _post(x, wkv, g, yb, gate1, shift2, scale2, mod_map, p, tm):
    nb, t, _ = x.shape
    mod_block = (1,) + gate1.shape[1:]
    tile = lambda n: pl.BlockSpec((1, tm, n), lambda b, i: (b, i, 0))
    full = lambda a: pl.BlockSpec(a.shape, lambda b, i: (0,) * a.ndim)
    mspec = lambda j: pl.BlockSpec(mod_block, functools.partial(mod_map, j))
    n2 = p['norm2_g'].reshape(1, D_MODEL)
    wo = p['w_out'].astype(BF16)
    rw = p['router_w'].T
    rb = p['router_b'].reshape(N_EXPERTS, 1)
    tri = jnp.triu(jnp.ones((tm, tm), BF16))
    nt = t // tm
    n = nb * t
    topk = pl.BlockSpec((1, TOP_K, tm), lambda b, i: (b * nt + i, 0, 0))
    topk_i = jax.ShapeDtypeStruct((nb * nt, TOP_K, tm), jnp.int32)
    x1, h2, ti, rk, tg, cnt = pl.pallas_call(
        _post_kernel,
        grid=(nb, nt),
        in_specs=[tile(D_MODEL), tile(D_A), tile(D_A), tile(D_B), mspec(2), mspec(3), mspec(4),
                  full(n2), full(wo), full(rw), full(rb), full(tri)],
        out_specs=[tile(D_MODEL),
                   pl.BlockSpec((tm * TILE_CHUNKS, LANES), lambda b, i: (b * nt + i, 0)),
                   topk, topk, topk,
                   pl.BlockSpec((1, N_EXPERTS, LANES), lambda b, i: (b * nt + i, 0, 0))],
        out_shape=[jax.ShapeDtypeStruct((nb, t, D_MODEL), F32),
                   jax.ShapeDtypeStruct((n * TILE_CHUNKS, LANES), F32),
                   topk_i, topk_i,
                   jax.ShapeDtypeStruct((nb * nt, TOP_K, tm), F32),
                   jax.ShapeDtypeStruct((nb * nt, N_EXPERTS, LANES), F32)],
        compiler_params=_cparams(("arbitrary", "arbitrary")),
        name="outproj_router",
    )(x, wkv, g, yb, gate1, shift2, scale2, n2, wo, rw, rb, tri)
    return x1, h2, ti, rk, tg, cnt[:, :, 0].astype(jnp.int32)


def _plan(routes):
    cnt = jnp.concatenate([r[2] for r in routes], axis=0)
    total = jnp.sum(cnt, axis=0)
    run_start = jnp.cumsum(cnt, axis=0) - cnt
    padded = (total + MOE_ROWS - 1) // MOE_ROWS * MOE_ROWS
    pad_end = jnp.cumsum(padded)
    pad_start = pad_end - padded
    loc_off = jnp.cumsum(cnt, axis=1) - cnt
    dst0 = pad_start[None, :] + run_start
    n_tokens = sum(r[0].shape[0] * r[0].shape[2] for r in routes)
    n_blocks = -(-n_tokens * TOP_K // MOE_ROWS) + N_EXPERTS
    block_row0 = jnp.arange(n_blocks, dtype=jnp.int32) * MOE_ROWS
    block_e = jnp.minimum(jnp.sum(pad_end[None, :] <= block_row0[:, None], axis=1), N_EXPERTS - 1)
    n_used = (pad_end[-1] // MOE_ROWS).reshape(1)
    as_i32 = lambda a: a.astype(jnp.int32)
    experts = jnp.arange(N_EXPERTS, dtype=jnp.int32)
    groups, t0 = [], 0
    for ti, rk, c in routes:
        nt = ti.shape[0]
        sl = slice(t0, t0 + nt)
        off = jnp.sum(jnp.where(ti[..., None] == experts, loc_off[sl, None, None, :], 0), axis=-1)
        lpos = as_i32((rk + off) * TILE_CHUNKS).reshape(nt, -1)
        table = as_i32(jnp.stack([cnt[sl], loc_off[sl], dst0[sl]], axis=0).reshape(3, -1))
        groups.append((lpos, table))
        t0 += nt
    return groups, as_i32(block_e), as_i32(n_used), as_i32(pad_end), as_i32(padded), n_blocks


def _tile_rows(row):
    return pl.ds(pl.multiple_of(row * TILE_CHUNKS, TILE_CHUNKS), TILE_CHUNKS)


def _sublane_rows(row8):
    return pl.ds(pl.multiple_of(row8, TILE_CHUNKS), TILE_CHUNKS)


def _piece_rows(row0, n_rows):
    return pl.ds(pl.multiple_of(row0 * TILE_CHUNKS, TILE_CHUNKS), n_rows * TILE_CHUNKS)


def _block_rows(block):
    rows = MOE_ROWS * TILE_CHUNKS
    return pl.ds(pl.multiple_of(block * rows, rows), rows)


def _for_each_piece(table_ref, tile, n_tiles, fn):
    def per_expert(e, carry):
        col = tile * N_EXPERTS + e
        cnt = table_ref[col]
        lo = table_ref[n_tiles * N_EXPERTS + col]
        d0 = table_ref[2 * n_tiles * N_EXPERTS + col]

        def full(c, c2):
            fn(lo + c * MOE_CHUNK, d0 + c * MOE_CHUNK, MOE_CHUNK)
            return c2
        lax.fori_loop(0, cnt // MOE_CHUNK, full, 0)
        bit = MOE_CHUNK // 2
        while bit >= 1:
            done = cnt - cnt % (2 * bit)

            @pl.when((cnt & bit) != 0)
            def _(done=done, bit=bit):
                fn(lo + done, d0 + done, bit)
            bit //= 2
        return carry
    lax.fori_loop(0, N_EXPERTS, per_expert, 0)


def _scatter_kernel(*refs, first):
    idx_ref, loc_ref, zero_ref, sem, isem, zsem = refs[-6:]
    table_ref, pad_end_ref, padded_ref, h2_ref, lpos_hbm = refs[:5]
    xs_hbm = refs[-7]
    i = pl.program_id(0)
    n_tiles = pl.num_programs(0)
    tm = h2_ref.shape[0] // TILE_CHUNKS
    slot = i % 2

    def idx_copy(tile, s):
        dst = idx_ref.at[pl.ds(s * (TOP_K * tm), TOP_K * tm)]
        return pltpu.make_async_copy(lpos_hbm.at[tile], dst, isem.at[s])

    def drain(s):
        for _ in range(TOP_K):
            pltpu.make_async_copy(h2_ref, xs_hbm.at[pl.ds(0, tm * TILE_CHUNKS), :], sem.at[s]).wait()

    @pl.when(i == 0)
    def _():
        idx_copy(0, 0).start()

    if first:
        @pl.when(i == 0)
        def _():
            zero_ref[...] = jnp.zeros_like(zero_ref)
            n_used = pad_end_ref[N_EXPERTS - 1] // MOE_ROWS
            n_blocks = xs_hbm.shape[0] // (MOE_ROWS * TILE_CHUNKS)

            def fill(e, n_started):
                n_fill = jnp.minimum(padded_ref[e] // MOE_ROWS, 2)
                last = pad_end_ref[e] // MOE_ROWS - 1

                def one(j, c):
                    pltpu.make_async_copy(zero_ref, xs_hbm.at[_block_rows(last - j), :], zsem).start()
                    return c
                lax.fori_loop(0, n_fill, one, 0)
                return n_started + n_fill
            n_started = lax.fori_loop(0, N_EXPERTS, fill, jnp.int32(0))

            def fill_tail(blk, c):
                pltpu.make_async_copy(zero_ref, xs_hbm.at[_block_rows(blk), :], zsem).start()
                return c
            lax.fori_loop(n_used, n_blocks, fill_tail, 0)

            def drain_fill(j, c):
                pltpu.make_async_copy(zero_ref, xs_hbm.at[_block_rows(0), :], zsem).wait()
                return c
            lax.fori_loop(0, n_started + (n_blocks - n_used), drain_fill, 0)

    def step(s):
        @pl.when(i + 1 < n_tiles)
        def _():
            idx_copy(i + 1, 1 - s).start()

        @pl.when(i >= 2)
        def _():
            drain(s)
        idx_copy(i, s).wait()
        base = s * (TOP_K * tm)

        def place(r, c):
            row = h2_ref[_tile_rows(r), :]
            for k in range(TOP_K):
                loc_ref[s, _sublane_rows(idx_ref[base + k * tm + r]), :] = row
            return c
        lax.fori_loop(0, tm, place, 0, unroll=8)

        def send(lo, d0, n):
            pltpu.make_async_copy(loc_ref.at[s, _piece_rows(lo, n), :], xs_hbm.at[_piece_rows(d0, n), :],
                                  sem.at[s]).start()
        _for_each_piece(table_ref, i, n_tiles, send)

        @pl.when(i == n_tiles - 1)
        def _():
            @pl.when(i >= 1)
            def _():
                drain(1 - s)
            drain(s)

    for s in range(2):
        pl.when(slot == s)(functools.partial(step, s))


def _scatter(h2, lpos, table, pad_end, padded, n_rows, tm, xs=None):
    nt = lpos.shape[0]
    first = xs is None
    assert first or nt * tm <= MOE_ROWS, "a later call may add at most MOE_ROWS rows per expert"
    loc_rows = TOP_K * tm * TILE_CHUNKS
    in_specs = [pl.BlockSpec((tm * TILE_CHUNKS, LANES), lambda i, *_: (i, 0)),
                pl.BlockSpec(memory_space=pl.ANY)]
    args = [table.reshape(-1), pad_end, padded, h2, lpos]
    if not first:
        in_specs.append(pl.BlockSpec(memory_space=pl.ANY))
        args.append(xs)
    grid_spec = pltpu.PrefetchScalarGridSpec(
        num_scalar_prefetch=3,
        grid=(nt,),
        in_specs=in_specs,
        out_specs=pl.BlockSpec(memory_space=pl.ANY),
        scratch_shapes=[pltpu.SMEM((2 * TOP_K * tm,), jnp.int32),
                        pltpu.VMEM((2, loc_rows, LANES), F32),
                        pltpu.VMEM((MOE_ROWS * TILE_CHUNKS, LANES), F32),
                        pltpu.SemaphoreType.DMA((2,)), pltpu.SemaphoreType.DMA((2,)), pltpu.SemaphoreType.DMA],
    )
    return pl.pallas_call(
        functools.partial(_scatter_kernel, first=first),
        grid_spec=grid_spec,
        out_shape=jax.ShapeDtypeStruct((n_rows * TILE_CHUNKS, LANES), F32),
        input_output_aliases={} if first else {5: 0},
        compiler_params=_cparams(("arbitrary",)),
        name="moe_dispatch",
    )(*args)


def _mlp_kernel(be_ref, nu_ref, next_ref, par_ref, xs_ref, bias_ref, wg_hbm, wu_hbm, wd_hbm, out_ref,
                w32_ref, w16_ref, sem):
    i = pl.program_id(0)
    used = i < nu_ref[0]
    e = be_ref[i]

    def weight_copies(expert, s):
        return [pltpu.make_async_copy(w_hbm.at[expert], w32_ref.at[s, j], sem.at[s])
                for j, w_hbm in enumerate((wg_hbm, wu_hbm, wd_hbm))]

    def load_expert(s):
        @pl.when(i == 0)
        def _():
            for c in weight_copies(e, s):
                c.start()
        for c in weight_copies(e, s):
            c.wait()
        nxt = next_ref[e]

        @pl.when(nxt >= 0)
        def _():
            for c in weight_copies(nxt, 1 - s):
                c.start()
        for j in range(3):
            w16_ref[j] = w32_ref[s, j].astype(BF16)

    first_of_expert = used & ((i == 0) | (e != be_ref[jnp.maximum(i - 1, 0)]))
    for s in range(2):
        pl.when(first_of_expert & (par_ref[e] == s))(functools.partial(load_expert, s))

    @pl.when(used)
    def _():
        x = _from_token_tiles(xs_ref, 0, MOE_ROWS).astype(BF16)
        gt = jnp.dot(x, w16_ref[0], preferred_element_type=F32) + bias_ref[0, 0:1, :]
        up = jnp.dot(x, w16_ref[1], preferred_element_type=F32) + bias_ref[0, 1:2, :]
        gt = jnp.minimum(gt, SWIGLU_LIMIT)
        up = jnp.clip(up, -SWIGLU_LIMIT, SWIGLU_LIMIT)
        glu = gt * _sigmoid(gt * SWIGLU_ALPHA)
        mid = ((up + 1.0) * glu).astype(BF16)
        _to_token_tiles(out_ref, jnp.dot(mid, w16_ref[2], preferred_element_type=F32) + bias_ref[0, 2:3, :])

    @pl.when(jnp.logical_not(used))
    def _():
        out_ref[...] = jnp.zeros_like(out_ref)


def _mlp(xs, block_e, n_used, padded, n_blocks, wts):
    wg, bg, wu, bu, wd, bd = wts
    assert D_FF == D_MODEL
    bias = jnp.stack([bg, bu, bd], axis=1)
    nonempty = padded > 0
    experts = jnp.arange(N_EXPERTS, dtype=jnp.int32)
    parity = ((jnp.cumsum(nonempty) - nonempty) % 2).astype(jnp.int32)
    later = nonempty[None, :] & (experts[None, :] > experts[:, None])
    nxt = jnp.min(jnp.where(later, experts[None, :], N_EXPERTS), axis=1)
    nxt = jnp.where(nxt == N_EXPERTS, -1, nxt).astype(jnp.int32)
    rows = pl.BlockSpec((MOE_ROWS * TILE_CHUNKS, LANES), lambda i, *_: (i, 0))
    hbm = pl.BlockSpec(memory_space=pl.ANY)
    grid_spec = pltpu.PrefetchScalarGridSpec(
        num_scalar_prefetch=4,
        grid=(n_blocks,),
        in_specs=[rows, pl.BlockSpec((1, 3, D_FF), lambda i, be, *_: (be[i], 0, 0)), hbm, hbm, hbm],
        out_specs=rows,
        scratch_shapes=[pltpu.VMEM((2, 3, D_MODEL, D_FF), F32),
                        pltpu.VMEM((3, D_MODEL, D_FF), BF16),
                        pltpu.SemaphoreType.DMA((2,))],
    )
    return pl.pallas_call(
        _mlp_kernel,
        grid_spec=grid_spec,
        out_shape=jax.ShapeDtypeStruct(xs.shape, F32),
        compiler_params=_cparams(("arbitrary",)),
        name="moe_experts",
    )(block_e, n_used, nxt, parity, xs, bias, wg, wu, wd)


def _gather_kernel(table_ref, x1_ref, gate2_ref, fg_ref, lpos_hbm, tg_hbm, rows_hbm, y_ref,
                   idx_ref, gsm_ref, loc_ref, ff_ref, sem, isem):
    i = pl.program_id(1) + pl.program_id(0) * pl.num_programs(1)
    n_tiles = pl.num_programs(0) * pl.num_programs(1)
    tm = x1_ref.shape[1]
    slot = i % 2

    def meta_copies(tile, s):
        seg = pl.ds(s * (TOP_K * tm), TOP_K * tm)
        return (pltpu.make_async_copy(lpos_hbm.at[tile], idx_ref.at[seg], isem.at[s]),
                pltpu.make_async_copy(tg_hbm.at[tile], gsm_ref.at[seg], isem.at[s]))

    def fetch_tile(tile, s):
        for c in meta_copies(tile, s):
            c.start()

        def fetch(lo, d0, n):
            pltpu.make_async_copy(rows_hbm.at[_piece_rows(d0, n), :], loc_ref.at[s, _piece_rows(lo, n), :],
                                  sem.at[s]).start()
        _for_each_piece(table_ref, tile, n_tiles, fetch)

    @pl.when(i == 0)
    def _():
        fetch_tile(0, 0)

    def step(s):
        @pl.when(i + 1 < n_tiles)
        def _():
            fetch_tile(i + 1, 1 - s)

        pltpu.make_async_copy(rows_hbm.at[pl.ds(0, TOP_K * tm * TILE_CHUNKS), :], loc_ref.at[s], sem.at[s]).wait()
        for c in meta_copies(i, s):
            c.wait()
        base = s * (TOP_K * tm)

        def mix(r, c):
            acc = gsm_ref[base + r] * loc_ref[s, _sublane_rows(idx_ref[base + r]), :]
            for k in range(1, TOP_K):
                acc = acc + gsm_ref[base + k * tm + r] * loc_ref[s, _sublane_rows(idx_ref[base + k * tm + r]), :]
            ff_ref[_tile_rows(r), :] = acc
            return c
        lax.fori_loop(0, tm, mix, 0, unroll=8)

    for s in range(2):
        pl.when(slot == s)(functools.partial(step, s))

    x2 = x1_ref[0] + gate2_ref[0] * _from_token_tiles(ff_ref, 0, tm)
    y_ref[0] = _rms(x2, fg_ref[...])


def _gather(x1, tg, gate2, mod_map, final_g, lpos, table, rows, tm):
    nb, t, _ = x1.shape
    nt = t // tm
    mod_block = (1,) + gate2.shape[1:]
    loc_rows = TOP_K * tm * TILE_CHUNKS
    grid_spec = pltpu.PrefetchScalarGridSpec(
        num_scalar_prefetch=1,
        grid=(nb, nt),
        in_specs=[pl.BlockSpec((1, tm, D_MODEL), lambda b, i, *_: (b, i, 0)),
                  pl.BlockSpec(mod_block, lambda b, i, *_: mod_map(5, b, i)),
                  pl.BlockSpec((1, D_MODEL), lambda b, i, *_: (0, 0)),
                  pl.BlockSpec(memory_space=pl.ANY),
                  pl.BlockSpec(memory_space=pl.ANY),
                  pl.BlockSpec(memory_space=pl.ANY)],
        out_specs=pl.BlockSpec((1, tm, D_MODEL), lambda b, i, *_: (b, i, 0)),
        scratch_shapes=[pltpu.SMEM((2 * TOP_K * tm,), jnp.int32),
                        pltpu.SMEM((2 * TOP_K * tm,), F32),
                        pltpu.VMEM((2, loc_rows, LANES), F32),
                        pltpu.VMEM((tm * TILE_CHUNKS, LANES), F32),
                        pltpu.SemaphoreType.DMA((2,)), pltpu.SemaphoreType.DMA((2,))],
    )
    return pl.pallas_call(
        _gather_kernel,
        grid_spec=grid_spec,
        out_shape=jax.ShapeDtypeStruct((nb, t, D_MODEL), F32),
        compiler_params=_cparams(("arbitrary", "arbitrary")),
        name="moe_combine",
    )(table.reshape(-1), x1, gate2, final_g.reshape(1, D_MODEL), lpos, tg.reshape(tg.shape[0], -1), rows)


def _moe(route_p, route_s, mod_p, mod_s, final_g, wts):
    routes = [route_p, route_s]
    groups, block_e, n_used, pad_end, padded, n_blocks = _plan([(r[2], r[3], r[5]) for r in routes])
    xs = None
    for (x1, h2, ti, rk, tg, cnt), (lpos, table) in zip(routes, groups):
        xs = _scatter(h2, lpos, table, pad_end, padded, n_blocks * MOE_ROWS, ti.shape[2], xs)
    rows = _mlp(xs, block_e, n_used, padded, n_blocks, wts)
    outs = []
    for (x1, h2, ti, rk, tg, cnt), (lpos, table), (mod, mod_map) in zip(routes, groups, (mod_p, mod_s)):
        outs.append(_gather(x1, tg, mod, mod_map, final_g, lpos, table, rows, ti.shape[2]))
    return outs


def _forward(x_prompt, x_sample, c_prompt, c_sample, state_wkv, state_shift, state_conv, state_lru, p, final_g):
    bp, tp, _ = x_prompt.shape
    bs = x_sample.shape[0]
    tm = min(512, tp)
    tt = min(64, tp)

    mod = _ada(jnp.concatenate([c_prompt, c_sample], axis=0), p['w_ada'], p['b_ada'])
    mod_p = mod[:bp].reshape(bp * N_MOD, 1, D_MODEL)
    mod_s = mod[bp:].reshape(bs, N_MOD, D_MODEL).transpose(1, 0, 2)
    map_p = lambda j, b, i: (b * N_MOD + j, 0, 0)
    map_s = lambda j, b, i: (j, 0, 0)

    wa = p['w_in'][:, :N_COLS_A]
    wb = p['w_in'][:, N_COLS_A:]
    wts = (p['w_gate'], p['b_gate'], p['w_up'], p['b_up'], p['w_down'], p['b_down'])

    wb_p, kr_p, kv_p, g, yb, lru_p, pa_tail, x_tail = _front(x_prompt, mod_p, mod_p, map_p, p,
                                                             wa.astype(BF16), wb.astype(BF16), tm)
    s0 = jnp.zeros((bp, N_HEADS, HEAD_DIM, HEAD_DIM), F32)
    wkv_out, wkv_p = _wkv_scan(wb_p[None], kr_p[None], kv_p[None], s0, p, tt)
    route_p = _post(x_prompt, wkv_out, g, yb, mod_p, mod_p, mod_p, map_p, p, tm)
    shift_p = pa_tail[:, -1, :]
    conv_p = x_tail[:, SUBLANES - (CONV_W - 1):, :]

    xs = x_sample.reshape(1, bs, D_MODEL)
    pa_s, pb_s = _inproj(xs, mod_s, mod_s, map_s, p['norm1_g'], wa, wb, bs, True)
    wb_s, kr_s, kv_s, g = _prep(pa_s, state_shift.reshape(1, bs, N_COLS_A), p, bs)
    as_seq = lambda a: _pairs_to_groups(a.reshape(bs, N_HEADS, LANES), 1)
    wkv_out, wkv_s = _wkv_scan(as_seq(wb_s), as_seq(kr_s), as_seq(kv_s), state_wkv, p, 1)
    conv0 = state_conv.transpose(1, 0, 2)
    yb, lru_s = _lru_step(pb_s[0], conv0, state_lru, p)
    route_s = _post(xs, wkv_out.reshape(1, bs, D_A), g, yb.reshape(1, bs, D_B),
                    mod_s, mod_s, mod_s, map_s, p, bs)
    y_prompt, y_sample = _moe(route_p, route_s, (mod_p, map_p), (mod_s, map_s), final_g, wts)
    shift_s = pa_s[0]
    conv_s = jnp.concatenate([state_conv[:, 1:], pb_s[0][:, None, D_B:]], axis=1)

    return (y_prompt, y_sample.reshape(bs, 1, D_MODEL),
            wkv_p[None], shift_p[None], conv_p[None], lru_p.reshape(bp, D_B)[None],
            wkv_s[None], shift_s[None], conv_s[None], lru_s[None])


def kernel(x_prompt, x_sample, c_prompt, c_sample, state_wkv, state_shift, state_conv, state_lru, w_ada, b_ada, norm1_g, norm2_g, w_in, rk_mu, rk_w0, rk_w_up, rk_a0, rk_a_up, rk_g_up, rk_k_k, rk_k_a, rk_r_k, rk_lnx_w, rk_lnx_b, lru_conv_w, lru_conv_b, lru_w_r, lru_b_r, lru_w_i, lru_b_i, lru_lambda, lru_norm_g, w_out, router_w, router_b, w_gate, b_gate, w_up, b_up, w_down, b_down, final_g):
    assert w_ada.shape[0] == 1, "single-layer trunk"
    p = dict(w_ada=w_ada[0], b_ada=b_ada[0], norm1_g=norm1_g[0], norm2_g=norm2_g[0], w_in=w_in[0],
             rk_mu=rk_mu[0], rk_w0=rk_w0[0], rk_w_up=rk_w_up[0], rk_a0=rk_a0[0], rk_a_up=rk_a_up[0],
             rk_g_up=rk_g_up[0], rk_k_k=rk_k_k[0], rk_k_a=rk_k_a[0], rk_r_k=rk_r_k[0],
             rk_lnx_w=rk_lnx_w[0], rk_lnx_b=rk_lnx_b[0], lru_conv_w=lru_conv_w[0],
             lru_conv_b=lru_conv_b[0], lru_w_r=lru_w_r[0], lru_b_r=lru_b_r[0], lru_w_i=lru_w_i[0],
             lru_b_i=lru_b_i[0], lru_lambda=lru_lambda[0], lru_norm_g=lru_norm_g[0], w_out=w_out[0],
             router_w=router_w[0], router_b=router_b[0], w_gate=w_gate[0], b_gate=b_gate[0],
             w_up=w_up[0], b_up=b_up[0], w_down=w_down[0], b_down=b_down[0])
    return _forward(x_prompt, x_sample, c_prompt, c_sample, state_wkv[0], state_shift[0], state_conv[0],
                    state_lru[0], p, final_g)
```

```python
import functools

import jax
import jax.numpy as jnp
from jax import lax
from jax.experimental import pallas as pl
from jax.experimental.pallas import tpu as pltpu

F32 = jnp.float32
BF16 = jnp.bfloat16

D_MODEL = 1024
D_A = 512
HEAD_DIM = 64
N_HEADS = 8
D_B = 512
N_BLOCKS_B = 8
CONV_W = 4
LRU_C = 8.0
R_DECAY = 64
R_AAA = 64
R_GATE = 128
N_COLS_A = 3 * D_A + R_DECAY + R_AAA + R_GATE
N_COLS_B = 2 * D_B
N_EXPERTS = 32
TOP_K = 4
D_FF = 1024
SWIGLU_LIMIT = 7.0
SWIGLU_ALPHA = 1.702
RMS_EPS = 1e-6
LN_X_EPS = 64e-5
N_MOD = 6

LANES = 128
SUBLANES = 8
GROUP_BATCH = 8
HALF_ROWS = HEAD_DIM // 2
MOE_ROWS = 256
MOE_CHUNK = 16
TILE_CHUNKS = D_MODEL // LANES
VMEM_LIMIT = 56 * 1024 * 1024


def _cparams(sem):
    return pltpu.CompilerParams(dimension_semantics=sem, vmem_limit_bytes=VMEM_LIMIT)


def _dot(a, b):
    return jnp.dot(a.astype(BF16), b.astype(BF16), preferred_element_type=F32)


def _split(a):
    hi = a.astype(BF16)
    lo = (a - hi.astype(F32)).astype(BF16)
    return hi, lo


def _dot3(a, b):
    ah, al = _split(a)
    bh, bl = _split(b)
    return (jnp.dot(ah, bh, preferred_element_type=F32)
            + (jnp.dot(al, bh, preferred_element_type=F32) + jnp.dot(ah, bl, preferred_element_type=F32)))


def _dot3_nt(a, b):
    dn = (((1,), (1,)), ((), ()))
    ah, al = _split(a)
    bh, bl = _split(b)
    d = lambda x, y: lax.dot_general(x, y, dn, preferred_element_type=F32)
    return d(ah, bh) + (d(al, bh) + d(ah, bl))


def _softplus(x):
    return jnp.maximum(x, 0.0) + jnp.log1p(jnp.exp(-jnp.abs(x)))


def _sigmoid(x):
    return 1.0 / (1.0 + jnp.exp(-x))


def _rms(x, g):
    ms = jnp.mean(x * x, axis=-1, keepdims=True)
    return x * lax.rsqrt(ms + RMS_EPS) * g


def _ada_kernel(c_ref, w_ref, b_ref, o_ref):
    c = c_ref[...]
    s = c * _sigmoid(c)
    o_ref[...] = _dot3(s, w_ref[...]) + b_ref[...]


def _ada(c, w_ada, b_ada):
    rows = c.shape[0]
    ncol = w_ada.shape[1]
    tn = D_MODEL
    return pl.pallas_call(
        _ada_kernel,
        grid=(ncol // tn,),
        in_specs=[pl.BlockSpec((rows, D_MODEL), lambda j: (0, 0)),
                  pl.BlockSpec((D_MODEL, tn), lambda j: (0, j)),
                  pl.BlockSpec((1, tn), lambda j: (0, j))],
        out_specs=pl.BlockSpec((rows, tn), lambda j: (0, j)),
        out_shape=jax.ShapeDtypeStruct((rows, ncol), F32),
        compiler_params=_cparams(("arbitrary",)),
        name="ada_mod",
    )(c, w_ada, b_ada.reshape(1, ncol))


def _inproj_kernel(x_ref, shift_ref, scale_ref, g_ref, w_ref, pa_ref, pb_ref):
    x = x_ref[0]
    h = _rms(x, g_ref[...]) * (1.0 + scale_ref[0]) + shift_ref[0]
    proj = _dot3(h, w_ref[...])
    pa_ref[0] = proj[:, :N_COLS_A]
    pb_ref[0] = proj[:, N_COLS_A:]


def _inproj(x, shift, scale, mod_map, g, w, tm):
    nb, t, _ = x.shape
    mod_block = (1,) + shift.shape[1:]
    return pl.pallas_call(
        _inproj_kernel,
        grid=(nb, t // tm),
        in_specs=[pl.BlockSpec((1, tm, D_MODEL), lambda b, i: (b, i, 0)),
                  pl.BlockSpec(mod_block, functools.partial(mod_map, 0)),
                  pl.BlockSpec(mod_block, functools.partial(mod_map, 1)),
                  pl.BlockSpec((1, D_MODEL), lambda b, i: (0, 0)),
                  pl.BlockSpec((D_MODEL, N_COLS_A + N_COLS_B), lambda b, i: (0, 0))],
        out_specs=[pl.BlockSpec((1, tm, N_COLS_A), lambda b, i: (b, i, 0)),
                   pl.BlockSpec((1, tm, N_COLS_B), lambda b, i: (b, i, 0))],
        out_shape=[jax.ShapeDtypeStruct((nb, t, N_COLS_A), F32),
                   jax.ShapeDtypeStruct((nb, t, N_COLS_B), F32)],
        compiler_params=_cparams(("arbitrary", "arbitrary")),
        name="norm1_inproj",
    )(x, shift, scale, g.reshape(1, D_MODEL), w)


def _store_head_pairs(ref, stage_ref, x, y):
    rows = x.shape[0]
    flat = ref.shape[0] == 1
    for h in range(N_HEADS):
        sl = slice(h * HEAD_DIM, (h + 1) * HEAD_DIM)
        pair = jnp.concatenate([x[:, sl], y[:, sl]], axis=1)
        if flat:
            ref[0, pl.ds(h, rows, stride=N_HEADS), :] = pair
        else:
            stage_ref[pl.ds(h, rows, stride=N_HEADS), :] = pair
    if not flat:
        ref[...] = stage_ref[...].reshape(rows, N_HEADS, LANES)


def _prep_math(pa, prev, mu_ref, w0_ref, wup_ref, a0_ref, gup_ref, kk_ref, ka_ref,
               wb_out, kr_out, kv_out, g_out, stage_ref):
    z = pa + (prev - pa) * mu_ref[...]
    r = z[:, 0:D_A]
    k = z[:, D_A:2 * D_A]
    v = z[:, 2 * D_A:3 * D_A]
    lo = 3 * D_A
    za = z[:, lo:lo + R_DECAY + R_AAA]
    lane = lax.broadcasted_iota(jnp.int32, za.shape, 1)
    za = jnp.where(lane < R_DECAY, jnp.tanh(za), za)
    lw = _dot3(za, wup_ref[...])
    w_log = -_softplus(-(w0_ref[...] + lw[:, :D_A])) - 0.5
    decay = jnp.exp(-jnp.exp(w_log))
    a = _sigmoid(a0_ref[...] + lw[:, D_A:])
    gd = z[:, lo + R_DECAY + R_AAA:]
    g = _dot3(_sigmoid(gd), gup_ref[...])
    kk = k * kk_ref[...]
    _store_head_pairs(wb_out, stage_ref, decay, kk * a)
    _store_head_pairs(kr_out, stage_ref, k * (1.0 + (a - 1.0) * ka_ref[...]), r)
    _store_head_pairs(kv_out, stage_ref, kk, v)
    g_out[0] = g


def _prep_kernel(pa_ref, prev_ref, *refs):
    _prep_math(pa_ref[0], prev_ref[0], *refs, None)


def _prep_params(p):
    wup = jnp.zeros((R_DECAY + R_AAA, 2 * D_A), F32)
    wup = wup.at[:R_DECAY, :D_A].set(p['rk_w_up']).at[R_DECAY:, D_A:].set(p['rk_a_up'])
    vec = lambda a: a.reshape(1, -1)
    return (vec(p['rk_mu']), vec(p['rk_w0']), wup, vec(p['rk_a0']), p['rk_g_up'],
            vec(p['rk_k_k']), vec(p['rk_k_a']))


def _const_spec(a):
    return pl.BlockSpec(a.shape, lambda b, i: (0,) * a.ndim)


def _prep(pa, prev, p, tm):
    nb, t, _ = pa.shape
    params = _prep_params(p)
    rows = pl.BlockSpec((1, tm, N_COLS_A), lambda b, i: (b, i, 0))
    return pl.pallas_call(
        _prep_kernel,
        grid=(nb, t // tm),
        in_specs=[rows, rows] + [_const_spec(a) for a in params],
        out_specs=[pl.BlockSpec((1, tm * N_HEADS, LANES), lambda b, i: (b, i, 0))] * 3
                  + [pl.BlockSpec((1, tm, D_A), lambda b, i: (b, i, 0))],
        out_shape=[jax.ShapeDtypeStruct((nb, t * N_HEADS, LANES), F32)] * 3
                  + [jax.ShapeDtypeStruct((nb, t, D_A), F32)],
        compiler_params=_cparams(("arbitrary", "arbitrary")),
        name="rwkv_prep",
    )(pa, prev, *params)


SCAN_SLOTS = 4


def _scan_kernel(wb_ref, kr_ref, kv_ref, kvnext_ref, s0_ref, lnw_ref, lnb_ref, rk_ref,
                 out_ref, sfin_ref, s_ref, sa_ref, inv_ref, wd_ref, kkd_ref, bd_ref, kd_ref, rd_ref, vd_ref, *, tt):
    ti = pl.program_id(1)
    upper_half = lax.broadcasted_iota(jnp.int32, (HALF_ROWS, LANES), 1) >= LANES // 2
    unroll = SCAN_SLOTS if tt % SCAN_SLOTS == 0 else 1

    def expand(pair):
        return jnp.concatenate([pair, pair], axis=0).T

    def prepare(slot, s, kk_pair=None):
        t1 = expand(wb_ref[0, s])
        wd_ref[slot] = t1[:HEAD_DIM]
        bd_ref[slot] = t1[HEAD_DIM:]
        t2 = expand(kr_ref[0, s])
        kd_ref[slot] = t2[:HEAD_DIM]
        rd_ref[slot] = t2[HEAD_DIM:]
        t3 = expand(kv_ref[0, s])
        vd_ref[slot] = jnp.where(upper_half, t3[HEAD_DIM + HALF_ROWS:], t3[HEAD_DIM:HEAD_DIM + HALF_ROWS])
        kkd_ref[slot] = t3[:HEAD_DIM] if kk_pair is None else expand(kk_pair)[:HEAD_DIM]

    prepare(0, 0)
    if tt > 1:
        prepare(1, 1)
    else:
        kkd_ref[1] = expand(kvnext_ref[0, 0])[:HEAD_DIM]

    def inv_norm2(kk_rows):
        s2 = jnp.sum(kk_rows * kk_rows, axis=0, keepdims=True)
        return 1.0 / jnp.maximum(s2, 1e-24)

    @pl.when(ti == 0)
    def _():
        s_ref[...] = s0_ref[0]
        acc = jnp.zeros((HALF_ROWS, LANES), F32)
        for j in range(HEAD_DIM):
            acc = acc + s0_ref[0, j] * kkd_ref[0, j:j + 1, :]
        sa_ref[...] = acc
        inv_ref[...] = inv_norm2(kkd_ref[0])

    def finish(y, cv):
        tot = jnp.sum(y, axis=0, keepdims=True)
        tot = tot + pltpu.roll(tot, LANES // 2, axis=1)
        d = y - tot * (1.0 / HEAD_DIM)
        sq = jnp.sum(d * d, axis=0, keepdims=True)
        sq = sq + pltpu.roll(sq, LANES // 2, axis=1)
        yn = d * lax.rsqrt(sq * (1.0 / HEAD_DIM) + LN_X_EPS)
        return yn * lnw_ref[...] + lnb_ref[...] + cv

    def step(t, u, carry):
        sa, inv2, y_prev, cv_prev = carry
        nxt_slot = (u + 1) % SCAN_SLOTS
        out_ref[0, jnp.maximum(t - 1, 0)] = finish(y_prev, cv_prev)
        sae = sa * (-inv2)
        v = vd_ref[u]
        acc_y = jnp.zeros((HALF_ROWS, LANES), F32)
        acc_s = jnp.zeros((HALF_ROWS, LANES), F32)
        for j in range(HEAD_DIM):
            row = pl.ds(j, 1)
            s_new = s_ref[j] * wd_ref[u, row, :] + sae * bd_ref[u, row, :] + v * kd_ref[u, row, :]
            s_ref[j] = s_new
            acc_y = acc_y + s_new * rd_ref[u, row, :]
            acc_s = acc_s + s_new * kkd_ref[nxt_slot, row, :]
        c = jnp.sum(rd_ref[u] * kd_ref[u] * rk_ref[...], axis=0, keepdims=True)
        if unroll > 1:
            ahead = jnp.minimum(t + 2, tt - 1)
            kk_pair = None
            if (u + 2) % SCAN_SLOTS == 0:
                kk_pair = jnp.where(t + 2 == tt, kvnext_ref[0, 0], kv_ref[0, ahead])
            prepare((u + 2) % SCAN_SLOTS, ahead, kk_pair)
        return acc_s, inv_norm2(kkd_ref[nxt_slot]), acc_y, c * v

    def steps(q, carry):
        for u in range(unroll):
            carry = step(q * unroll + u, u, carry)
        return carry

    zeros = jnp.zeros((HALF_ROWS, LANES), F32)
    sa, inv2, y_last, cv_last = lax.fori_loop(0, tt // unroll, steps, (sa_ref[...], inv_ref[...], zeros, zeros))
    out_ref[0, tt - 1] = finish(y_last, cv_last)
    sa_ref[...] = sa
    inv_ref[...] = inv2

    @pl.when(ti == pl.num_programs(1) - 1)
    def _():
        sfin_ref[0] = s_ref[...]


def _from_scan_rows(y):
    g, t = y.shape[:2]
    x = y.reshape(g, t, HALF_ROWS, 2, GROUP_BATCH, N_HEADS).transpose(0, 4, 1, 5, 3, 2)
    return x.reshape(g * GROUP_BATCH, t, D_A)


def _state_to_scan(s):
    g = s.shape[0] // GROUP_BATCH
    y = s.reshape(g, GROUP_BATCH, N_HEADS, 2, HALF_ROWS, HEAD_DIM).transpose(0, 5, 4, 3, 1, 2)
    return y.reshape(g, HEAD_DIM, HALF_ROWS, LANES)


def _state_from_scan(y):
    g = y.shape[0]
    s = y.reshape(g, HEAD_DIM, HALF_ROWS, 2, GROUP_BATCH, N_HEADS).transpose(0, 4, 5, 3, 2, 1)
    return s.reshape(g * GROUP_BATCH, N_HEADS, HEAD_DIM, HEAD_DIM)


def _head_rows(x):
    y = x.reshape(N_HEADS, 2, HALF_ROWS).transpose(2, 1, 0)
    y = jnp.broadcast_to(y[:, :, None, :], (HALF_ROWS, 2, GROUP_BATCH, N_HEADS))
    return y.reshape(HALF_ROWS, LANES)


def _head_keys(x):
    y = jnp.broadcast_to(x.T[:, None, None, :], (HEAD_DIM, 2, GROUP_BATCH, N_HEADS))
    return y.reshape(HEAD_DIM, LANES)


def _pairs_to_groups(x, t):
    nb = x.shape[0]
    g = nb // GROUP_BATCH
    y = x.reshape(g, GROUP_BATCH, t, N_HEADS, LANES).transpose(0, 2, 1, 3, 4)
    return y.reshape(g, t, GROUP_BATCH * N_HEADS, LANES)


def _wkv_scan(wb, kr, kv, s0, p, tt):
    g, t = wb.shape[:2]
    pair_tile = pl.BlockSpec((1, tt, HEAD_DIM, LANES), lambda gi, i: (gi, i, 0, 0))
    next_step = pl.BlockSpec((1, 1, HEAD_DIM, LANES), lambda gi, i: (gi, jnp.minimum((i + 1) * tt, t - 1), 0, 0))
    row_tile = pl.BlockSpec((1, tt, HALF_ROWS, LANES), lambda gi, i: (gi, i, 0, 0))
    state = pl.BlockSpec((1, HEAD_DIM, HALF_ROWS, LANES), lambda gi, i: (gi, 0, 0, 0))
    const = lambda n: pl.BlockSpec((n, LANES), lambda gi, i: (0, 0))
    keys = pltpu.VMEM((SCAN_SLOTS, HEAD_DIM, LANES), F32)
    out, sfin = pl.pallas_call(
        functools.partial(_scan_kernel, tt=tt),
        grid=(g, t // tt),
        in_specs=[pair_tile, pair_tile, pair_tile, next_step,
                  state, const(HALF_ROWS), const(HALF_ROWS), const(HEAD_DIM)],
        out_specs=[row_tile, state],
        out_shape=[jax.ShapeDtypeStruct((g, t, HALF_ROWS, LANES), F32),
                   jax.ShapeDtypeStruct((g, HEAD_DIM, HALF_ROWS, LANES), F32)],
        scratch_shapes=[pltpu.VMEM((HEAD_DIM, HALF_ROWS, LANES), F32),
                        pltpu.VMEM((HALF_ROWS, LANES), F32),
                        pltpu.VMEM((1, LANES), F32),
                        keys, keys, keys, keys, keys,
                        pltpu.VMEM((SCAN_SLOTS, HALF_ROWS, LANES), F32)],
        compiler_params=_cparams(("arbitrary", "arbitrary")),
        name="wkv_scan",
    )(wb, kr, kv, kv, _state_to_scan(s0),
      _head_rows(p['rk_lnx_w']), _head_rows(p['rk_lnx_b']), _head_keys(p['rk_r_k']))
    return _from_scan_rows(out), _state_from_scan(sfin)


def _gelu(x):
    return 0.5 * x * (1.0 + jnp.tanh(0.7978845608028654 * (x + 0.044715 * (x * x * x))))


def _lru_gates(xc, wri_ref, bri_ref, nsl_ref, precise):
    dot = _dot3 if precise else _dot
    gates = _sigmoid(dot(xc, wri_ref[...]) + bri_ref[...])
    gate_r = gates[:, :D_B]
    gate_i = gates[:, D_B:]
    log_a = gate_r * nsl_ref[...]
    a = jnp.exp(log_a)
    th = jnp.tanh(log_a)
    one_minus_a2 = -2.0 * th / (1.0 - th)
    bt = jnp.sqrt(one_minus_a2) * (gate_i * xc)
    return a, bt


def _lru_seq_math(pb, cw_ref, cb_ref, wri_ref, bri_ref, nsl_ref, ng_ref, out_ref, hlast_ref, xprev_ref, h_ref):
    tm = pb.shape[0]
    yb = pb[:, :D_B]
    xb = pb[:, D_B:]
    xprev = xprev_ref[...]
    row8 = lax.broadcasted_iota(jnp.int32, (SUBLANES, D_B), 0)

    def shifted(d):
        rolled = pltpu.roll(xb, d, axis=0)
        top = jnp.where(row8 < d, pltpu.roll(xprev, d, axis=0), rolled[:SUBLANES])
        return jnp.concatenate([top, rolled[SUBLANES:]], axis=0)

    xc = cb_ref[...] + cw_ref[3:4, :] * xb
    for d in range(1, CONV_W):
        xc = xc + cw_ref[3 - d:4 - d, :] * shifted(d)
    xprev_ref[...] = xb[tm - SUBLANES:, :]

    a, x = _lru_gates(xc, wri_ref, bri_ref, nsl_ref, False)
    row = lax.broadcasted_iota(jnp.int32, (tm, D_B), 0)
    d = 1
    while d < tm:
        if d < SUBLANES:
            keep = row >= d
            a_s = jnp.where(keep, pltpu.roll(a, d, axis=0), 1.0)
            x_s = jnp.where(keep, pltpu.roll(x, d, axis=0), 0.0)
            x = a * x_s + x
            a = a * a_s
        else:
            x = jnp.concatenate([x[:d], a[d:] * x[:tm - d] + x[d:]], axis=0)
            a = jnp.concatenate([a[:d], a[d:] * a[:tm - d]], axis=0)
        d *= 2
    h = a * h_ref[...] + x
    h_ref[...] = h[tm - 1:, :]
    hlast_ref[0] = h[tm - 1:, :]
    out_ref[0] = _rms(h * _gelu(yb), ng_ref[...]).astype(out_ref.dtype)


def _lru_params(p):
    eye = jnp.eye(N_BLOCKS_B, dtype=F32)
    bd = lambda w: (eye[:, None, :, None] * w[:, :, None, :]).reshape(D_B, D_B)
    wri = jnp.concatenate([bd(p['lru_w_r']), bd(p['lru_w_i'])], axis=1)
    bri = jnp.concatenate([p['lru_b_r'], p['lru_b_i']]).reshape(1, 2 * D_B)
    nsl = (-LRU_C * jax.nn.softplus(-p['lru_lambda'])).reshape(1, D_B)
    return wri, bri, nsl


def _front_kernel(x_ref, shift_ref, scale_ref, g1_ref, w_ref,
                  mu_ref, w0_ref, wup_ref, a0_ref, gup_ref, kk_ref, ka_ref,
                  cw_ref, cb_ref, wri_ref, bri_ref, nsl_ref, ng_ref,
                  wb_out, kr_out, kv_out, g_out, yb_out, hlast_out, patail_out, xtail_out,
                  carry_ref, stage_ref, xprev_ref, h_ref):
    @pl.when(pl.program_id(1) == 0)
    def _():
        carry_ref[...] = jnp.zeros_like(carry_ref)
        xprev_ref[...] = jnp.zeros_like(xprev_ref)
        h_ref[...] = jnp.zeros_like(h_ref)

    h = (_rms(x_ref[0], g1_ref[...]) * (1.0 + scale_ref[0]) + shift_ref[0]).astype(BF16)
    proj = jnp.dot(h, w_ref[...], preferred_element_type=F32)
    pa = proj[:, :N_COLS_A]
    pb = proj[:, N_COLS_A:]
    tm = pa.shape[0]
    patail_out[0] = pa[tm - SUBLANES:, :]
    xtail_out[0] = pb[tm - SUBLANES:, D_B:]

    rolled = pltpu.roll(pa, 1, axis=0)
    row = lax.broadcasted_iota(jnp.int32, pa.shape, 0)
    prev = jnp.where(row == 0, carry_ref[...], rolled)
    carry_ref[...] = pa[tm - 1:, :]
    _prep_math(pa, prev, mu_ref, w0_ref, wup_ref, a0_ref, gup_ref, kk_ref, ka_ref,
               wb_out, kr_out, kv_out, g_out, stage_ref)
    _lru_seq_math(pb, cw_ref, cb_ref, wri_ref, bri_ref, nsl_ref, ng_ref, yb_out, hlast_out, xprev_ref, h_ref)


def _front(x, shift, scale, mod_map, p, w, tm):
    nb, t, _ = x.shape
    assert nb == GROUP_BATCH
    wri, bri, nsl = _lru_params(p)
    consts = ((p['norm1_g'].reshape(1, D_MODEL), w) + _prep_params(p)
              + (p['lru_conv_w'], p['lru_conv_b'].reshape(1, D_B), wri.astype(BF16), bri, nsl,
                 p['lru_norm_g'].reshape(1, D_B)))
    mod_block = (1,) + shift.shape[1:]
    tile = lambda n: pl.BlockSpec((1, tm, n), lambda b, i: (b, i, 0))
    tail = lambda n: pl.BlockSpec((1, SUBLANES, n), lambda b, i: (b, 0, 0))
    pair_tile = pl.BlockSpec((tm, N_HEADS, LANES), lambda b, i: (i, b, 0))
    pair_shape = jax.ShapeDtypeStruct((t, nb * N_HEADS, LANES), F32)
    return pl.pallas_call(
        _front_kernel,
        grid=(nb, t // tm),
        in_specs=[tile(D_MODEL),
                  pl.BlockSpec(mod_block, functools.partial(mod_map, 0)),
                  pl.BlockSpec(mod_block, functools.partial(mod_map, 1))] + [_const_spec(a) for a in consts],
        out_specs=[pair_tile] * 3 + [tile(D_A), tile(D_B), pl.BlockSpec((1, 1, D_B), lambda b, i: (b, 0, 0)),
                                     tail(N_COLS_A), tail(D_B)],
        out_shape=[pair_shape] * 3 + [jax.ShapeDtypeStruct((nb, t, D_A), F32),
                                      jax.ShapeDtypeStruct((nb, t, D_B), BF16),
                                      jax.ShapeDtypeStruct((nb, 1, D_B), F32),
                                      jax.ShapeDtypeStruct((nb, SUBLANES, N_COLS_A), F32),
                                      jax.ShapeDtypeStruct((nb, SUBLANES, D_B), F32)],
        scratch_shapes=[pltpu.VMEM((1, N_COLS_A), F32), pltpu.VMEM((tm * N_HEADS, LANES), F32),
                        pltpu.VMEM((SUBLANES, D_B), F32), pltpu.VMEM((1, D_B), F32)],
        compiler_params=_cparams(("arbitrary", "arbitrary")),
        name="prompt_front",
    )(x, shift, scale, *consts)


def _lru_step_kernel(pb_ref, conv_ref, h0_ref, cw_ref, cb_ref, wri_ref, bri_ref, nsl_ref, ng_ref,
                     out_ref, hnew_ref):
    pb = pb_ref[...]
    yb = pb[:, :D_B]
    xb = pb[:, D_B:]
    xc = cb_ref[...] + cw_ref[3:4, :] * xb
    for j in range(CONV_W - 1):
        xc = xc + cw_ref[j:j + 1, :] * conv_ref[j]
    a, x = _lru_gates(xc, wri_ref, bri_ref, nsl_ref, True)
    h = a * h0_ref[...] + x
    hnew_ref[...] = h
    out_ref[...] = _rms(h * _gelu(yb), ng_ref[...]).astype(out_ref.dtype)


def _lru_step(pb, conv0, h0, p):
    n = pb.shape[0]
    wri, bri, nsl = _lru_params(p)
    return pl.pallas_call(
        _lru_step_kernel,
        out_shape=[jax.ShapeDtypeStruct((n, D_B), BF16), jax.ShapeDtypeStruct((n, D_B), F32)],
        compiler_params=pltpu.CompilerParams(vmem_limit_bytes=VMEM_LIMIT),
        name="rglru_step",
    )(pb, conv0, h0, p['lru_conv_w'], p['lru_conv_b'].reshape(1, D_B), wri, bri, nsl,
      p['lru_norm_g'].reshape(1, D_B))


def _to_token_tiles(ref, x):
    rows = x.shape[0]
    for c in range(TILE_CHUNKS):
        ref[pl.ds(c, rows, stride=TILE_CHUNKS), :] = x[:, c * LANES:(c + 1) * LANES]


def _from_token_tiles(ref, row0, rows):
    return jnp.concatenate(
        [ref[pl.ds(row0 + c, rows, stride=TILE_CHUNKS), :] for c in range(TILE_CHUNKS)], axis=1)


def _post_kernel(x_ref, wkv_ref, g_ref, yb_ref, gate1_ref, shift2_ref, scale2_ref, n2_ref,
                 wo_ref, rw_ref, rb_ref, tri_ref, x1_ref, h2_ref, ti_ref, rk_ref, tg_ref, cnt_ref):
    ya = (wkv_ref[0] * g_ref[0]).astype(BF16)
    mixed = (jnp.dot(ya, wo_ref[:D_A, :], preferred_element_type=F32)
             + jnp.dot(yb_ref[0], wo_ref[D_A:, :], preferred_element_type=F32))
    x1 = x_ref[0] + gate1_ref[0] * mixed
    x1_ref[0] = x1
    h2 = _rms(x1, n2_ref[...]) * (1.0 + scale2_ref[0]) + shift2_ref[0]
    _to_token_tiles(h2_ref, h2)
    logits = _dot3_nt(rw_ref[...], h2) + rb_ref[...]
    eidx = lax.broadcasted_iota(jnp.int32, logits.shape, 0)
    vals, idxs = [], []
    cur = logits
    for _ in range(TOP_K):
        m = jnp.max(cur, axis=0, keepdims=True)
        i = jnp.min(jnp.where(cur == m, eidx, N_EXPERTS), axis=0, keepdims=True)
        vals.append(m)
        idxs.append(i)
        cur = jnp.where(eidx == i, -jnp.inf, cur)
    ex = [jnp.exp(v - vals[0]) for v in vals]
    den = ex[0] + ex[1] + ex[2] + ex[3]
    sel = [eidx == i for i in idxs]
    onehot = (sel[0] | sel[1] | sel[2] | sel[3]).astype(F32)
    incl = jnp.dot(onehot.astype(BF16), tri_ref[...], preferred_element_type=F32)
    rank = incl - onehot
    cnt_ref[0] = jnp.broadcast_to(jnp.sum(onehot, axis=1, keepdims=True), cnt_ref.shape[1:])
    for k in range(TOP_K):
        ti_ref[0, k:k + 1, :] = idxs[k]
        rk_ref[0, k:k + 1, :] = jnp.sum(jnp.where(sel[k], rank, 0.0), axis=0, keepdims=True).astype(jnp.int32)
        tg_ref[0, k:k + 1, :] = ex[k] / den


def _post(x, wkv, g, yb, gate1, shift2, scale2, mod_map, p, tm):
    nb, t, _ = x.shape
    mod_block = (1,) + gate1.shape[1:]
    tile = lambda n: pl.BlockSpec((1, tm, n), lambda b, i: (b, i, 0))
    full = lambda a: pl.BlockSpec(a.shape, lambda b, i: (0,) * a.ndim)
    mspec = lambda j: pl.BlockSpec(mod_block, functools.partial(mod_map, j))
    n2 = p['norm2_g'].reshape(1, D_MODEL)
    wo = p['w_out'].astype(BF16)
    rw = p['router_w'].T
    rb = p['router_b'].reshape(N_EXPERTS, 1)
    tri = jnp.triu(jnp.ones((tm, tm), BF16))
    nt = t // tm
    n = nb * t
    topk = pl.BlockSpec((1, TOP_K, tm), lambda b, i: (b * nt + i, 0, 0))
    topk_i = jax.ShapeDtypeStruct((nb * nt, TOP_K, tm), jnp.int32)
    x1, h2, ti, rk, tg, cnt = pl.pallas_call(
        _post_kernel,
        grid=(nb, nt),
        in_specs=[tile(D_MODEL), tile(D_A), tile(D_A), tile(D_B), mspec(2), mspec(3), mspec(4),
                  full(n2), full(wo), full(rw), full(rb), full(tri)],
        out_specs=[tile(D_MODEL),
                   pl.BlockSpec((tm * TILE_CHUNKS, LANES), lambda b, i: (b * nt + i, 0)),
                   topk, topk, topk,
                   pl.BlockSpec((1, N_EXPERTS, LANES), lambda b, i: (b * nt + i, 0, 0))],
        out_shape=[jax.ShapeDtypeStruct((nb, t, D_MODEL), F32),
                   jax.ShapeDtypeStruct((n * TILE_CHUNKS, LANES), F32),
                   topk_i, topk_i,
                   jax.ShapeDtypeStruct((nb * nt, TOP_K, tm), F32),
                   jax.ShapeDtypeStruct((nb * nt, N_EXPERTS, LANES), F32)],
        compiler_params=_cparams(("arbitrary", "arbitrary")),
        name="outproj_router",
    )(x, wkv, g, yb, gate1, shift2, scale2, n2, wo, rw, rb, tri)
    return x1, h2, ti, rk, tg, cnt[:, :, 0].astype(jnp.int32)


def _plan(routes):
    cnt = jnp.concatenate([r[2] for r in routes], axis=0)
    total = jnp.sum(cnt, axis=0)
    run_start = jnp.cumsum(cnt, axis=0) - cnt
    padded = (total + MOE_ROWS - 1) // MOE_ROWS * MOE_ROWS
    pad_end = jnp.cumsum(padded)
    pad_start = pad_end - padded
    loc_off = jnp.cumsum(cnt, axis=1) - cnt
    dst0 = pad_start[None, :] + run_start
    n_tokens = sum(r[0].shape[0] * r[0].shape[2] for r in routes)
    n_blocks = -(-n_tokens * TOP_K // MOE_ROWS) + N_EXPERTS
    block_row0 = jnp.arange(n_blocks, dtype=jnp.int32) * MOE_ROWS
    block_e = jnp.minimum(jnp.sum(pad_end[None, :] <= block_row0[:, None], axis=1), N_EXPERTS - 1)
    n_used = (pad_end[-1] // MOE_ROWS).reshape(1)
    as_i32 = lambda a: a.astype(jnp.int32)
    experts = jnp.arange(N_EXPERTS, dtype=jnp.int32)
    groups, t0 = [], 0
    for ti, rk, c in routes:
        nt = ti.shape[0]
        sl = slice(t0, t0 + nt)
        off = jnp.sum(jnp.where(ti[..., None] == experts, loc_off[sl, None, None, :], 0), axis=-1)
        lpos = as_i32((rk + off) * TILE_CHUNKS).reshape(nt, -1)
        table = as_i32(jnp.stack([cnt[sl], loc_off[sl], dst0[sl]], axis=0).reshape(3, -1))
        groups.append((lpos, table))
        t0 += nt
    return groups, as_i32(block_e), as_i32(n_used), as_i32(pad_end), as_i32(padded), n_blocks


def _tile_rows(row):
    return pl.ds(pl.multiple_of(row * TILE_CHUNKS, TILE_CHUNKS), TILE_CHUNKS)


def _sublane_rows(row8):
    return pl.ds(pl.multiple_of(row8, TILE_CHUNKS), TILE_CHUNKS)


def _piece_rows(row0, n_rows):
    return pl.ds(pl.multiple_of(row0 * TILE_CHUNKS, TILE_CHUNKS), n_rows * TILE_CHUNKS)


def _block_rows(block):
    rows = MOE_ROWS * TILE_CHUNKS
    return pl.ds(pl.multiple_of(block * rows, rows), rows)


def _for_each_piece(table_ref, tile, n_tiles, fn):
    def per_expert(e, carry):
        col = tile * N_EXPERTS + e
        cnt = table_ref[col]
        lo = table_ref[n_tiles * N_EXPERTS + col]
        d0 = table_ref[2 * n_tiles * N_EXPERTS + col]

        def full(c, c2):
            fn(lo + c * MOE_CHUNK, d0 + c * MOE_CHUNK, MOE_CHUNK)
            return c2
        lax.fori_loop(0, cnt // MOE_CHUNK, full, 0)
        bit = MOE_CHUNK // 2
        while bit >= 1:
            done = cnt - cnt % (2 * bit)

            @pl.when((cnt & bit) != 0)
            def _(done=done, bit=bit):
                fn(lo + done, d0 + done, bit)
            bit //= 2
        return carry
    lax.fori_loop(0, N_EXPERTS, per_expert, 0)


def _scatter_kernel(*refs, first):
    idx_ref, loc_ref, zero_ref, sem, isem, zsem = refs[-6:]
    table_ref, pad_end_ref, padded_ref, h2_ref, lpos_hbm = refs[:5]
    xs_hbm = refs[-7]
    i = pl.program_id(0)
    n_tiles = pl.num_programs(0)
    tm = h2_ref.shape[0] // TILE_CHUNKS
    slot = i % 2

    def idx_copy(tile, s):
        dst = idx_ref.at[pl.ds(s * (TOP_K * tm), TOP_K * tm)]
        return pltpu.make_async_copy(lpos_hbm.at[tile], dst, isem.at[s])

    def drain(s):
        for _ in range(TOP_K):
            pltpu.make_async_copy(h2_ref, xs_hbm.at[pl.ds(0, tm * TILE_CHUNKS), :], sem.at[s]).wait()

    @pl.when(i == 0)
    def _():
        idx_copy(0, 0).start()

    if first:
        @pl.when(i == 0)
        def _():
            zero_ref[...] = jnp.zeros_like(zero_ref)
            n_used = pad_end_ref[N_EXPERTS - 1] // MOE_ROWS
            n_blocks = xs_hbm.shape[0] // (MOE_ROWS * TILE_CHUNKS)

            def fill(e, n_started):
                n_fill = jnp.minimum(padded_ref[e] // MOE_ROWS, 2)
                last = pad_end_ref[e] // MOE_ROWS - 1

                def one(j, c):
                    pltpu.make_async_copy(zero_ref, xs_hbm.at[_block_rows(last - j), :], zsem).start()
                    return c
                lax.fori_loop(0, n_fill, one, 0)
                return n_started + n_fill
            n_started = lax.fori_loop(0, N_EXPERTS, fill, jnp.int32(0))

            def fill_tail(blk, c):
                pltpu.make_async_copy(zero_ref, xs_hbm.at[_block_rows(blk), :], zsem).start()
                return c
            lax.fori_loop(n_used, n_blocks, fill_tail, 0)

            def drain_fill(j, c):
                pltpu.make_async_copy(zero_ref, xs_hbm.at[_block_rows(0), :], zsem).wait()
                return c
            lax.fori_loop(0, n_started + (n_blocks - n_used), drain_fill, 0)

    def step(s):
        @pl.when(i + 1 < n_tiles)
        def _():
            idx_copy(i + 1, 1 - s).start()

        @pl.when(i >= 2)
        def _():
            drain(s)
        idx_copy(i, s).wait()
        base = s * (TOP_K * tm)

        def place(r, c):
            row = h2_ref[_tile_rows(r), :]
            for k in range(TOP_K):
                loc_ref[s, _sublane_rows(idx_ref[base + k * tm + r]), :] = row
            return c
        lax.fori_loop(0, tm, place, 0, unroll=8)

        def send(lo, d0, n):
            pltpu.make_async_copy(loc_ref.at[s, _piece_rows(lo, n), :], xs_hbm.at[_piece_rows(d0, n), :],
                                  sem.at[s]).start()
        _for_each_piece(table_ref, i, n_tiles, send)

        @pl.when(i == n_tiles - 1)
        def _():
            @pl.when(i >= 1)
            def _():
                drain(1 - s)
            drain(s)

    for s in range(2):
        pl.when(slot == s)(functools.partial(step, s))


def _scatter(h2, lpos, table, pad_end, padded, n_rows, tm, xs=None):
    nt = lpos.shape[0]
    first = xs is None
    assert first or nt * tm <= MOE_ROWS, "a later call may add at most MOE_ROWS rows per expert"
    loc_rows = TOP_K * tm * TILE_CHUNKS
    in_specs = [pl.BlockSpec((tm * TILE_CHUNKS, LANES), lambda i, *_: (i, 0)),
                pl.BlockSpec(memory_space=pl.ANY)]
    args = [table.reshape(-1), pad_end, padded, h2, lpos]
    if not first:
        in_specs.append(pl.BlockSpec(memory_space=pl.ANY))
        args.append(xs)
    grid_spec = pltpu.PrefetchScalarGridSpec(
        num_scalar_prefetch=3,
        grid=(nt,),
        in_specs=in_specs,
        out_specs=pl.BlockSpec(memory_space=pl.ANY),
        scratch_shapes=[pltpu.SMEM((2 * TOP_K * tm,), jnp.int32),
                        pltpu.VMEM((2, loc_rows, LANES), F32),
                        pltpu.VMEM((MOE_ROWS * TILE_CHUNKS, LANES), F32),
                        pltpu.SemaphoreType.DMA((2,)), pltpu.SemaphoreType.DMA((2,)), pltpu.SemaphoreType.DMA],
    )
    return pl.pallas_call(
        functools.partial(_scatter_kernel, first=first),
        grid_spec=grid_spec,
        out_shape=jax.ShapeDtypeStruct((n_rows * TILE_CHUNKS, LANES), F32),
        input_output_aliases={} if first else {5: 0},
        compiler_params=_cparams(("arbitrary",)),
        name="moe_dispatch",
    )(*args)


def _mlp_kernel(be_ref, nu_ref, next_ref, par_ref, xs_ref, bias_ref, wg_hbm, wu_hbm, wd_hbm, out_ref,
                w32_ref, w16_ref, sem):
    i = pl.program_id(0)
    used = i < nu_ref[0]
    e = be_ref[i]

    def weight_copies(expert, s):
        return [pltpu.make_async_copy(w_hbm.at[expert], w32_ref.at[s, j], sem.at[s])
                for j, w_hbm in enumerate((wg_hbm, wu_hbm, wd_hbm))]

    def load_expert(s):
        @pl.when(i == 0)
        def _():
            for c in weight_copies(e, s):
                c.start()
        for c in weight_copies(e, s):
            c.wait()
        nxt = next_ref[e]

        @pl.when(nxt >= 0)
        def _():
            for c in weight_copies(nxt, 1 - s):
                c.start()
        for j in range(3):
            w16_ref[j] = w32_ref[s, j].astype(BF16)

    first_of_expert = used & ((i == 0) | (e != be_ref[jnp.maximum(i - 1, 0)]))
    for s in range(2):
        pl.when(first_of_expert & (par_ref[e] == s))(functools.partial(load_expert, s))

    @pl.when(used)
    def _():
        x = _from_token_tiles(xs_ref, 0, MOE_ROWS).astype(BF16)
        gt = jnp.dot(x, w16_ref[0], preferred_element_type=F32) + bias_ref[0, 0:1, :]
        up = jnp.dot(x, w16_ref[1], preferred_element_type=F32) + bias_ref[0, 1:2, :]
        gt = jnp.minimum(gt, SWIGLU_LIMIT)
        up = jnp.clip(up, -SWIGLU_LIMIT, SWIGLU_LIMIT)
        glu = gt * _sigmoid(gt * SWIGLU_ALPHA)
        mid = ((up + 1.0) * glu).astype(BF16)
        _to_token_tiles(out_ref, jnp.dot(mid, w16_ref[2], preferred_element_type=F32) + bias_ref[0, 2:3, :])

    @pl.when(jnp.logical_not(used))
    def _():
        out_ref[...] = jnp.zeros_like(out_ref)


def _mlp(xs, block_e, n_used, padded, n_blocks, wts):
    wg, bg, wu, bu, wd, bd = wts
    assert D_FF == D_MODEL
    bias = jnp.stack([bg, bu, bd], axis=1)
    nonempty = padded > 0
    experts = jnp.arange(N_EXPERTS, dtype=jnp.int32)
    parity = ((jnp.cumsum(nonempty) - nonempty) % 2).astype(jnp.int32)
    later = nonempty[None, :] & (experts[None, :] > experts[:, None])
    nxt = jnp.min(jnp.where(later, experts[None, :], N_EXPERTS), axis=1)
    nxt = jnp.where(nxt == N_EXPERTS, -1, nxt).astype(jnp.int32)
    rows = pl.BlockSpec((MOE_ROWS * TILE_CHUNKS, LANES), lambda i, *_: (i, 0))
    hbm = pl.BlockSpec(memory_space=pl.ANY)
    grid_spec = pltpu.PrefetchScalarGridSpec(
        num_scalar_prefetch=4,
        grid=(n_blocks,),
        in_specs=[rows, pl.BlockSpec((1, 3, D_FF), lambda i, be, *_: (be[i], 0, 0)), hbm, hbm, hbm],
        out_specs=rows,
        scratch_shapes=[pltpu.VMEM((2, 3, D_MODEL, D_FF), F32),
                        pltpu.VMEM((3, D_MODEL, D_FF), BF16),
                        pltpu.SemaphoreType.DMA((2,))],
    )
    return pl.pallas_call(
        _mlp_kernel,
        grid_spec=grid_spec,
        out_shape=jax.ShapeDtypeStruct(xs.shape, F32),
        compiler_params=_cparams(("arbitrary",)),
        name="moe_experts",
    )(block_e, n_used, nxt, parity, xs, bias, wg, wu, wd)


def _gather_kernel(table_ref, x1_ref, gate2_ref, fg_ref, lpos_hbm, tg_hbm, rows_hbm, y_ref,
                   idx_ref, gsm_ref, loc_ref, ff_ref, sem, isem):
    i = pl.program_id(1) + pl.program_id(0) * pl.num_programs(1)
    n_tiles = pl.num_programs(0) * pl.num_programs(1)
    tm = x1_ref.shape[1]
    slot = i % 2

    def meta_copies(tile, s):
        seg = pl.ds(s * (TOP_K * tm), TOP_K * tm)
        return (pltpu.make_async_copy(lpos_hbm.at[tile], idx_ref.at[seg], isem.at[s]),
                pltpu.make_async_copy(tg_hbm.at[tile], gsm_ref.at[seg], isem.at[s]))

    def fetch_tile(tile, s):
        for c in meta_copies(tile, s):
            c.start()

        def fetch(lo, d0, n):
            pltpu.make_async_copy(rows_hbm.at[_piece_rows(d0, n), :], loc_ref.at[s, _piece_rows(lo, n), :],
                                  sem.at[s]).start()
        _for_each_piece(table_ref, tile, n_tiles, fetch)

    @pl.when(i == 0)
    def _():
        fetch_tile(0, 0)

    def step(s):
        @pl.when(i + 1 < n_tiles)
        def _():
            fetch_tile(i + 1, 1 - s)

        pltpu.make_async_copy(rows_hbm.at[pl.ds(0, TOP_K * tm * TILE_CHUNKS), :], loc_ref.at[s], sem.at[s]).wait()
        for c in meta_copies(i, s):
            c.wait()
        base = s * (TOP_K * tm)

        def mix(r, c):
            acc = gsm_ref[base + r] * loc_ref[s, _sublane_rows(idx_ref[base + r]), :]
            for k in range(1, TOP_K):
                acc = acc + gsm_ref[base + k * tm + r] * loc_ref[s, _sublane_rows(idx_ref[base + k * tm + r]), :]
            ff_ref[_tile_rows(r), :] = acc
            return c
        lax.fori_loop(0, tm, mix, 0, unroll=8)

    for s in range(2):
        pl.when(slot == s)(functools.partial(step, s))

    x2 = x1_ref[0] + gate2_ref[0] * _from_token_tiles(ff_ref, 0, tm)
    y_ref[0] = _rms(x2, fg_ref[...])


def _gather(x1, tg, gate2, mod_map, final_g, lpos, table, rows, tm):
    nb, t, _ = x1.shape
    nt = t // tm
    mod_block = (1,) + gate2.shape[1:]
    loc_rows = TOP_K * tm * TILE_CHUNKS
    grid_spec = pltpu.PrefetchScalarGridSpec(
        num_scalar_prefetch=1,
        grid=(nb, nt),
        in_specs=[pl.BlockSpec((1, tm, D_MODEL), lambda b, i, *_: (b, i, 0)),
                  pl.BlockSpec(mod_block, lambda b, i, *_: mod_map(5, b, i)),
                  pl.BlockSpec((1, D_MODEL), lambda b, i, *_: (0, 0)),
                  pl.BlockSpec(memory_space=pl.ANY),
                  pl.BlockSpec(memory_space=pl.ANY),
                  pl.BlockSpec(memory_space=pl.ANY)],
        out_specs=pl.BlockSpec((1, tm, D_MODEL), lambda b, i, *_: (b, i, 0)),
        scratch_shapes=[pltpu.SMEM((2 * TOP_K * tm,), jnp.int32),
                        pltpu.SMEM((2 * TOP_K * tm,), F32),
                        pltpu.VMEM((2, loc_rows, LANES), F32),
                        pltpu.VMEM((tm * TILE_CHUNKS, LANES), F32),
                        pltpu.SemaphoreType.DMA((2,)), pltpu.SemaphoreType.DMA((2,))],
    )
    return pl.pallas_call(
        _gather_kernel,
        grid_spec=grid_spec,
        out_shape=jax.ShapeDtypeStruct((nb, t, D_MODEL), F32),
        compiler_params=_cparams(("arbitrary", "arbitrary")),
        name="moe_combine",
    )(table.reshape(-1), x1, gate2, final_g.reshape(1, D_MODEL), lpos, tg.reshape(tg.shape[0], -1), rows)


def _moe(route_p, route_s, mod_p, mod_s, final_g, wts):
    routes = [route_p, route_s]
    groups, block_e, n_used, pad_end, padded, n_blocks = _plan([(r[2], r[3], r[5]) for r in routes])
    xs = None
    for (x1, h2, ti, rk, tg, cnt), (lpos, table) in zip(routes, groups):
        xs = _scatter(h2, lpos, table, pad_end, padded, n_blocks * MOE_ROWS, ti.shape[2], xs)
    rows = _mlp(xs, block_e, n_used, padded, n_blocks, wts)
    outs = []
    for (x1, h2, ti, rk, tg, cnt), (lpos, table), (mod, mod_map) in zip(routes, groups, (mod_p, mod_s)):
        outs.append(_gather(x1, tg, mod, mod_map, final_g, lpos, table, rows, ti.shape[2]))
    return outs


def _forward(x_prompt, x_sample, c_prompt, c_sample, state_wkv, state_shift, state_conv, state_lru, p, final_g):
    bp, tp, _ = x_prompt.shape
    bs = x_sample.shape[0]
    tm = min(512, tp)
    tt = min(64, tp)

    mod = _ada(jnp.concatenate([c_prompt, c_sample], axis=0), p['w_ada'], p['b_ada'])
    mod_p = mod[:bp].reshape(bp * N_MOD, 1, D_MODEL)
    mod_s = mod[bp:].reshape(bs, N_MOD, D_MODEL).transpose(1, 0, 2)
    map_p = lambda j, b, i: (b * N_MOD + j, 0, 0)
    map_s = lambda j, b, i: (j, 0, 0)

    wts = (p['w_gate'], p['b_gate'], p['w_up'], p['b_up'], p['w_down'], p['b_down'])

    wb_p, kr_p, kv_p, g, yb, lru_p, pa_tail, x_tail = _front(x_prompt, mod_p, mod_p, map_p, p,
                                                             p['w_in'].astype(BF16), tm)
    s0 = jnp.zeros((bp, N_HEADS, HEAD_DIM, HEAD_DIM), F32)
    wkv_out, wkv_p = _wkv_scan(wb_p[None], kr_p[None], kv_p[None], s0, p, tt)
    route_p = _post(x_prompt, wkv_out, g, yb, mod_p, mod_p, mod_p, map_p, p, tm)
    shift_p = pa_tail[:, -1, :]
    conv_p = x_tail[:, SUBLANES - (CONV_W - 1):, :]

    xs = x_sample.reshape(1, bs, D_MODEL)
    pa_s, pb_s = _inproj(xs, mod_s, mod_s, map_s, p['norm1_g'], p['w_in'], bs)
    wb_s, kr_s, kv_s, g = _prep(pa_s, state_shift.reshape(1, bs, N_COLS_A), p, bs)
    as_seq = lambda a: _pairs_to_groups(a.reshape(bs, N_HEADS, LANES), 1)
    wkv_out, wkv_s = _wkv_scan(as_seq(wb_s), as_seq(kr_s), as_seq(kv_s), state_wkv, p, 1)
    conv0 = state_conv.transpose(1, 0, 2)
    yb, lru_s = _lru_step(pb_s[0], conv0, state_lru, p)
    route_s = _post(xs, wkv_out.reshape(1, bs, D_A), g, yb.reshape(1, bs, D_B),
                    mod_s, mod_s, mod_s, map_s, p, bs)
    y_prompt, y_sample = _moe(route_p, route_s, (mod_p, map_p), (mod_s, map_s), final_g, wts)
    shift_s = pa_s[0]
    conv_s = jnp.concatenate([state_conv[:, 1:], pb_s[0][:, None, D_B:]], axis=1)

    return (y_prompt, y_sample.reshape(bs, 1, D_MODEL),
            wkv_p[None], shift_p[None], conv_p[None], lru_p.reshape(bp, D_B)[None],
            wkv_s[None], shift_s[None], conv_s[None], lru_s[None])


def kernel(x_prompt, x_sample, c_prompt, c_sample, state_wkv, state_shift, state_conv, state_lru, w_ada, b_ada, norm1_g, norm2_g, w_in, rk_mu, rk_w0, rk_w_up, rk_a0, rk_a_up, rk_g_up, rk_k_k, rk_k_a, rk_r_k, rk_lnx_w, rk_lnx_b, lru_conv_w, lru_conv_b, lru_w_r, lru_b_r, lru_w_i, lru_b_i, lru_lambda, lru_norm_g, w_out, router_w, router_b, w_gate, b_gate, w_up, b_up, w_down, b_down, final_g):
    assert w_ada.shape[0] == 1, "single-layer trunk"
    p = dict(w_ada=w_ada[0], b_ada=b_ada[0], norm1_g=norm1_g[0], norm2_g=norm2_g[0], w_in=w_in[0],
             rk_mu=rk_mu[0], rk_w0=rk_w0[0], rk_w_up=rk_w_up[0], rk_a0=rk_a0[0], rk_a_up=rk_a_up[0],
             rk_g_up=rk_g_up[0], rk_k_k=rk_k_k[0], rk_k_a=rk_k_a[0], rk_r_k=rk_r_k[0],
             rk_lnx_w=rk_lnx_w[0], rk_lnx_b=rk_lnx_b[0], lru_conv_w=lru_conv_w[0],
             lru_conv_b=lru_conv_b[0], lru_w_r=lru_w_r[0], lru_b_r=lru_b_r[0], lru_w_i=lru_w_i[0],
             lru_b_i=lru_b_i[0], lru_lambda=lru_lambda[0], lru_norm_g=lru_norm_g[0], w_out=w_out[0],
             router_w=router_w[0], router_b=router_b[0], w_gate=w_gate[0], b_gate=b_gate[0],
             w_up=w_up[0], b_up=b_up[0], w_down=w_down[0], b_down=b_down[0])
    return _forward(x_prompt, x_sample, c_prompt, c_sample, state_wkv[0], state_shift[0], state_conv[0],
                    state_lru[0], p, final_g)
```

```python
import functools

import jax
import jax.numpy as jnp
from jax import lax
from jax.experimental import pallas as pl
from jax.experimental.pallas import tpu as pltpu

F32 = jnp.float32
BF16 = jnp.bfloat16

D_MODEL = 1024
D_A = 512
HEAD_DIM = 64
N_HEADS = 8
D_B = 512
N_BLOCKS_B = 8
CONV_W = 4
LRU_C = 8.0
R_DECAY = 64
R_AAA = 64
R_GATE = 128
N_COLS_A = 3 * D_A + R_DECAY + R_AAA + R_GATE
N_COLS_B = 2 * D_B
N_EXPERTS = 32
TOP_K = 4
D_FF = 1024
SWIGLU_LIMIT = 7.0
SWIGLU_ALPHA = 1.702
RMS_EPS = 1e-6
LN_X_EPS = 64e-5
N_MOD = 6

LANES = 128
SUBLANES = 8
GROUP_BATCH = 8
HALF_ROWS = HEAD_DIM // 2
MOE_ROWS = 256
MOE_CHUNK = 16
TILE_CHUNKS = D_MODEL // LANES
VMEM_LIMIT = 56 * 1024 * 1024


def _cparams(sem):
    return pltpu.CompilerParams(dimension_semantics=sem, vmem_limit_bytes=VMEM_LIMIT)


def _dot(a, b):
    return jnp.dot(a.astype(BF16), b.astype(BF16), preferred_element_type=F32)


def _split(a):
    hi = a.astype(BF16)
    lo = (a - hi.astype(F32)).astype(BF16)
    return hi, lo


def _dot3(a, b):
    ah, al = _split(a)
    bh, bl = _split(b)
    return (jnp.dot(ah, bh, preferred_element_type=F32)
            + (jnp.dot(al, bh, preferred_element_type=F32) + jnp.dot(ah, bl, preferred_element_type=F32)))


def _dot3_nt(a, b):
    dn = (((1,), (1,)), ((), ()))
    ah, al = _split(a)
    bh, bl = _split(b)
    d = lambda x, y: lax.dot_general(x, y, dn, preferred_element_type=F32)
    return d(ah, bh) + (d(al, bh) + d(ah, bl))


def _softplus(x):
    return jnp.maximum(x, 0.0) + jnp.log1p(jnp.exp(-jnp.abs(x)))


def _sigmoid(x):
    return 1.0 / (1.0 + jnp.exp(-x))


def _rms(x, g):
    ms = jnp.mean(x * x, axis=-1, keepdims=True)
    return x * lax.rsqrt(ms + RMS_EPS) * g


def _ada_kernel(c_ref, w_ref, b_ref, o_ref):
    c = c_ref[...]
    s = c * _sigmoid(c)
    o_ref[...] = _dot3(s, w_ref[...]) + b_ref[...]


def _ada(c, w_ada, b_ada):
    rows = c.shape[0]
    ncol = w_ada.shape[1]
    tn = D_MODEL
    return pl.pallas_call(
        _ada_kernel,
        grid=(ncol // tn,),
        in_specs=[pl.BlockSpec((rows, D_MODEL), lambda j: (0, 0)),
                  pl.BlockSpec((D_MODEL, tn), lambda j: (0, j)),
                  pl.BlockSpec((1, tn), lambda j: (0, j))],
        out_specs=pl.BlockSpec((rows, tn), lambda j: (0, j)),
        out_shape=jax.ShapeDtypeStruct((rows, ncol), F32),
        compiler_params=_cparams(("arbitrary",)),
        name="ada_mod",
    )(c, w_ada, b_ada.reshape(1, ncol))


def _inproj_kernel(x_ref, shift_ref, scale_ref, g_ref, w_ref, pa_ref, pb_ref):
    x = x_ref[0]
    h = _rms(x, g_ref[...]) * (1.0 + scale_ref[0]) + shift_ref[0]
    proj = _dot3(h, w_ref[...])
    pa_ref[0] = proj[:, :N_COLS_A]
    pb_ref[0] = proj[:, N_COLS_A:]


def _inproj(x, shift, scale, mod_map, g, w, tm):
    nb, t, _ = x.shape
    mod_block = (1,) + shift.shape[1:]
    return pl.pallas_call(
        _inproj_kernel,
        grid=(nb, t // tm),
        in_specs=[pl.BlockSpec((1, tm, D_MODEL), lambda b, i: (b, i, 0)),
                  pl.BlockSpec(mod_block, functools.partial(mod_map, 0)),
                  pl.BlockSpec(mod_block, functools.partial(mod_map, 1)),
                  pl.BlockSpec((1, D_MODEL), lambda b, i: (0, 0)),
                  pl.BlockSpec((D_MODEL, N_COLS_A + N_COLS_B), lambda b, i: (0, 0))],
        out_specs=[pl.BlockSpec((1, tm, N_COLS_A), lambda b, i: (b, i, 0)),
                   pl.BlockSpec((1, tm, N_COLS_B), lambda b, i: (b, i, 0))],
        out_shape=[jax.ShapeDtypeStruct((nb, t, N_COLS_A), F32),
                   jax.ShapeDtypeStruct((nb, t, N_COLS_B), F32)],
        compiler_params=_cparams(("arbitrary", "arbitrary")),
        name="norm1_inproj",
    )(x, shift, scale, g.reshape(1, D_MODEL), w)


def _store_head_pairs(ref, stage_ref, x, y):
    rows = x.shape[0]
    flat = ref.shape[0] == 1
    for h in range(N_HEADS):
        sl = slice(h * HEAD_DIM, (h + 1) * HEAD_DIM)
        pair = jnp.concatenate([x[:, sl], y[:, sl]], axis=1)
        if flat:
            ref[0, pl.ds(h, rows, stride=N_HEADS), :] = pair
        else:
            stage_ref[pl.ds(h, rows, stride=N_HEADS), :] = pair
    if not flat:
        ref[...] = stage_ref[...].reshape(rows, N_HEADS, LANES)


def _prep_math(pa, prev, mu_ref, w0_ref, wup_ref, a0_ref, gup_ref, kk_ref, ka_ref,
               wb_out, kr_out, kv_out, g_out, stage_ref):
    z = pa + (prev - pa) * mu_ref[...]
    r = z[:, 0:D_A]
    k = z[:, D_A:2 * D_A]
    v = z[:, 2 * D_A:3 * D_A]
    lo = 3 * D_A
    za = z[:, lo:lo + R_DECAY + R_AAA]
    lane = lax.broadcasted_iota(jnp.int32, za.shape, 1)
    za = jnp.where(lane < R_DECAY, jnp.tanh(za), za)
    lw = _dot3(za, wup_ref[...])
    w_log = -_softplus(-(w0_ref[...] + lw[:, :D_A])) - 0.5
    decay = jnp.exp(-jnp.exp(w_log))
    a = _sigmoid(a0_ref[...] + lw[:, D_A:])
    gd = z[:, lo + R_DECAY + R_AAA:]
    g = _dot3(_sigmoid(gd), gup_ref[...])
    kk = k * kk_ref[...]
    _store_head_pairs(wb_out, stage_ref, decay, kk * a)
    _store_head_pairs(kr_out, stage_ref, k * (1.0 + (a - 1.0) * ka_ref[...]), r)
    _store_head_pairs(kv_out, stage_ref, kk, v)
    g_out[0] = g


def _prep_kernel(pa_ref, prev_ref, *refs):
    _prep_math(pa_ref[0], prev_ref[0], *refs, None)


def _prep_params(p):
    wup = jnp.zeros((R_DECAY + R_AAA, 2 * D_A), F32)
    wup = wup.at[:R_DECAY, :D_A].set(p['rk_w_up']).at[R_DECAY:, D_A:].set(p['rk_a_up'])
    vec = lambda a: a.reshape(1, -1)
    return (vec(p['rk_mu']), vec(p['rk_w0']), wup, vec(p['rk_a0']), p['rk_g_up'],
            vec(p['rk_k_k']), vec(p['rk_k_a']))


def _const_spec(a):
    return pl.BlockSpec(a.shape, lambda b, i: (0,) * a.ndim)


def _prep(pa, prev, p, tm):
    nb, t, _ = pa.shape
    params = _prep_params(p)
    rows = pl.BlockSpec((1, tm, N_COLS_A), lambda b, i: (b, i, 0))
    return pl.pallas_call(
        _prep_kernel,
        grid=(nb, t // tm),
        in_specs=[rows, rows] + [_const_spec(a) for a in params],
        out_specs=[pl.BlockSpec((1, tm * N_HEADS, LANES), lambda b, i: (b, i, 0))] * 3
                  + [pl.BlockSpec((1, tm, D_A), lambda b, i: (b, i, 0))],
        out_shape=[jax.ShapeDtypeStruct((nb, t * N_HEADS, LANES), F32)] * 3
                  + [jax.ShapeDtypeStruct((nb, t, D_A), F32)],
        compiler_params=_cparams(("arbitrary", "arbitrary")),
        name="rwkv_prep",
    )(pa, prev, *params)


SCAN_SLOTS = 4


def _scan_kernel(wb_ref, kr_ref, kv_ref, kvnext_ref, s0_ref, lnw_ref, lnb_ref, rk_ref,
                 out_ref, sfin_ref, s_ref, sa_ref, inv_ref, wd_ref, kkd_ref, bd_ref, kd_ref, rd_ref, vd_ref, *, tt):
    ti = pl.program_id(1)
    upper_half = lax.broadcasted_iota(jnp.int32, (HALF_ROWS, LANES), 1) >= LANES // 2
    unroll = SCAN_SLOTS if tt % SCAN_SLOTS == 0 else 1

    def expand(pair):
        return jnp.concatenate([pair, pair], axis=0).T

    def prepare(slot, s, kk_pair=None):
        t1 = expand(wb_ref[0, s])
        wd_ref[slot] = t1[:HEAD_DIM]
        bd_ref[slot] = t1[HEAD_DIM:]
        t2 = expand(kr_ref[0, s])
        kd_ref[slot] = t2[:HEAD_DIM]
        rd_ref[slot] = t2[HEAD_DIM:]
        t3 = expand(kv_ref[0, s])
        vd_ref[slot] = jnp.where(upper_half, t3[HEAD_DIM + HALF_ROWS:], t3[HEAD_DIM:HEAD_DIM + HALF_ROWS])
        kkd_ref[slot] = t3[:HEAD_DIM] if kk_pair is None else expand(kk_pair)[:HEAD_DIM]

    prepare(0, 0)
    if tt > 1:
        prepare(1, 1)
    else:
        kkd_ref[1] = expand(kvnext_ref[0, 0])[:HEAD_DIM]

    def inv_norm2(kk_rows):
        s2 = jnp.sum(kk_rows * kk_rows, axis=0, keepdims=True)
        return 1.0 / jnp.maximum(s2, 1e-24)

    @pl.when(ti == 0)
    def _():
        s_ref[...] = s0_ref[0]
        acc = jnp.zeros((HALF_ROWS, LANES), F32)
        for j in range(HEAD_DIM):
            acc = acc + s0_ref[0, j] * kkd_ref[0, j:j + 1, :]
        sa_ref[...] = acc
        inv_ref[...] = inv_norm2(kkd_ref[0])

    def finish(y, cv):
        tot = jnp.sum(y, axis=0, keepdims=True)
        tot = tot + pltpu.roll(tot, LANES // 2, axis=1)
        d = y - tot * (1.0 / HEAD_DIM)
        sq = jnp.sum(d * d, axis=0, keepdims=True)
        sq = sq + pltpu.roll(sq, LANES // 2, axis=1)
        yn = d * lax.rsqrt(sq * (1.0 / HEAD_DIM) + LN_X_EPS)
        return yn * lnw_ref[...] + lnb_ref[...] + cv

    def step(t, u, carry):
        sa, inv2, y_prev, cv_prev = carry
        nxt_slot = (u + 1) % SCAN_SLOTS
        out_ref[0, jnp.maximum(t - 1, 0)] = finish(y_prev, cv_prev)
        sae = sa * (-inv2)
        v = vd_ref[u]
        acc_y = jnp.zeros((HALF_ROWS, LANES), F32)
        acc_s = jnp.zeros((HALF_ROWS, LANES), F32)
        for j in range(HEAD_DIM):
            row = pl.ds(j, 1)
            s_new = s_ref[j] * wd_ref[u, row, :] + sae * bd_ref[u, row, :] + v * kd_ref[u, row, :]
            s_ref[j] = s_new
            acc_y = acc_y + s_new * rd_ref[u, row, :]
            acc_s = acc_s + s_new * kkd_ref[nxt_slot, row, :]
        c = jnp.sum(rd_ref[u] * kd_ref[u] * rk_ref[...], axis=0, keepdims=True)
        if unroll > 1:
            ahead = jnp.minimum(t + 2, tt - 1)
            kk_pair = None
            if (u + 2) % SCAN_SLOTS == 0:
                kk_pair = jnp.where(t + 2 == tt, kvnext_ref[0, 0], kv_ref[0, ahead])
            prepare((u + 2) % SCAN_SLOTS, ahead, kk_pair)
        return acc_s, inv_norm2(kkd_ref[nxt_slot]), acc_y, c * v

    def steps(q, carry):
        for u in range(unroll):
            carry = step(q * unroll + u, u, carry)
        return carry

    zeros = jnp.zeros((HALF_ROWS, LANES), F32)
    sa, inv2, y_last, cv_last = lax.fori_loop(0, tt // unroll, steps, (sa_ref[...], inv_ref[...], zeros, zeros))
    out_ref[0, tt - 1] = finish(y_last, cv_last)
    sa_ref[...] = sa
    inv_ref[...] = inv2

    @pl.when(ti == pl.num_programs(1) - 1)
    def _():
        sfin_ref[0] = s_ref[...]


def _from_scan_rows(y):
    g, t = y.shape[:2]
    x = y.reshape(g, t, HALF_ROWS, 2, GROUP_BATCH, N_HEADS).transpose(0, 4, 1, 5, 3, 2)
    return x.reshape(g * GROUP_BATCH, t, D_A)


def _state_to_scan(s):
    g = s.shape[0] // GROUP_BATCH
    y = s.reshape(g, GROUP_BATCH, N_HEADS, 2, HALF_ROWS, HEAD_DIM).transpose(0, 5, 4, 3, 1, 2)
    return y.reshape(g, HEAD_DIM, HALF_ROWS, LANES)


def _state_from_scan(y):
    g = y.shape[0]
    s = y.reshape(g, HEAD_DIM, HALF_ROWS, 2, GROUP_BATCH, N_HEADS).transpose(0, 4, 5, 3, 2, 1)
    return s.reshape(g * GROUP_BATCH, N_HEADS, HEAD_DIM, HEAD_DIM)


def _head_rows(x):
    y = x.reshape(N_HEADS, 2, HALF_ROWS).transpose(2, 1, 0)
    y = jnp.broadcast_to(y[:, :, None, :], (HALF_ROWS, 2, GROUP_BATCH, N_HEADS))
    return y.reshape(HALF_ROWS, LANES)


def _head_keys(x):
    y = jnp.broadcast_to(x.T[:, None, None, :], (HEAD_DIM, 2, GROUP_BATCH, N_HEADS))
    return y.reshape(HEAD_DIM, LANES)


def _pairs_to_groups(x, t):
    nb = x.shape[0]
    g = nb // GROUP_BATCH
    y = x.reshape(g, GROUP_BATCH, t, N_HEADS, LANES).transpose(0, 2, 1, 3, 4)
    return y.reshape(g, t, GROUP_BATCH * N_HEADS, LANES)


def _wkv_scan(wb, kr, kv, s0, p, tt):
    g, t = wb.shape[:2]
    pair_tile = pl.BlockSpec((1, tt, HEAD_DIM, LANES), lambda gi, i: (gi, i, 0, 0))
    next_step = pl.BlockSpec((1, 1, HEAD_DIM, LANES), lambda gi, i: (gi, jnp.minimum((i + 1) * tt, t - 1), 0, 0))
    row_tile = pl.BlockSpec((1, tt, HALF_ROWS, LANES), lambda gi, i: (gi, i, 0, 0))
    state = pl.BlockSpec((1, HEAD_DIM, HALF_ROWS, LANES), lambda gi, i: (gi, 0, 0, 0))
    const = lambda n: pl.BlockSpec((n, LANES), lambda gi, i: (0, 0))
    keys = pltpu.VMEM((SCAN_SLOTS, HEAD_DIM, LANES), F32)
    out, sfin = pl.pallas_call(
        functools.partial(_scan_kernel, tt=tt),
        grid=(g, t // tt),
        in_specs=[pair_tile, pair_tile, pair_tile, next_step,
                  state, const(HALF_ROWS), const(HALF_ROWS), const(HEAD_DIM)],
        out_specs=[row_tile, state],
        out_shape=[jax.ShapeDtypeStruct((g, t, HALF_ROWS, LANES), F32),
                   jax.ShapeDtypeStruct((g, HEAD_DIM, HALF_ROWS, LANES), F32)],
        scratch_shapes=[pltpu.VMEM((HEAD_DIM, HALF_ROWS, LANES), F32),
                        pltpu.VMEM((HALF_ROWS, LANES), F32),
                        pltpu.VMEM((1, LANES), F32),
                        keys, keys, keys, keys, keys,
                        pltpu.VMEM((SCAN_SLOTS, HALF_ROWS, LANES), F32)],
        compiler_params=_cparams(("arbitrary", "arbitrary")),
        name="wkv_scan",
    )(wb, kr, kv, kv, _state_to_scan(s0),
      _head_rows(p['rk_lnx_w']), _head_rows(p['rk_lnx_b']), _head_keys(p['rk_r_k']))
    return _from_scan_rows(out), _state_from_scan(sfin)


def _gelu(x):
    return 0.5 * x * (1.0 + jnp.tanh(0.7978845608028654 * (x + 0.044715 * (x * x * x))))


def _lru_gates(xc, wri_ref, bri_ref, nsl_ref, precise):
    dot = _dot3 if precise else _dot
    gates = _sigmoid(dot(xc, wri_ref[...]) + bri_ref[...])
    gate_r = gates[:, :D_B]
    gate_i = gates[:, D_B:]
    log_a = gate_r * nsl_ref[...]
    a = jnp.exp(log_a)
    th = jnp.tanh(log_a)
    one_minus_a2 = -2.0 * th / (1.0 - th)
    bt = jnp.sqrt(one_minus_a2) * (gate_i * xc)
    return a, bt


def _lru_seq_math(pb, cw_ref, cb_ref, wri_ref, bri_ref, nsl_ref, ng_ref, out_ref, hlast_ref, xprev_ref, h_ref):
    tm = pb.shape[0]
    yb = pb[:, :D_B]
    xb = pb[:, D_B:]
    xprev = xprev_ref[...]
    row8 = lax.broadcasted_iota(jnp.int32, (SUBLANES, D_B), 0)

    def shifted(d):
        rolled = pltpu.roll(xb, d, axis=0)
        top = jnp.where(row8 < d, pltpu.roll(xprev, d, axis=0), rolled[:SUBLANES])
        return jnp.concatenate([top, rolled[SUBLANES:]], axis=0)

    xc = cb_ref[...] + cw_ref[3:4, :] * xb
    for d in range(1, CONV_W):
        xc = xc + cw_ref[3 - d:4 - d, :] * shifted(d)
    xprev_ref[...] = xb[tm - SUBLANES:, :]

    a, x = _lru_gates(xc, wri_ref, bri_ref, nsl_ref, False)
    row = lax.broadcasted_iota(jnp.int32, (tm, D_B), 0)
    d = 1
    while d < tm:
        if d < SUBLANES:
            keep = row >= d
            a_s = jnp.where(keep, pltpu.roll(a, d, axis=0), 1.0)
            x_s = jnp.where(keep, pltpu.roll(x, d, axis=0), 0.0)
            x = a * x_s + x
            a = a * a_s
        else:
            x = jnp.concatenate([x[:d], a[d:] * x[:tm - d] + x[d:]], axis=0)
            a = jnp.concatenate([a[:d], a[d:] * a[:tm - d]], axis=0)
        d *= 2
    h = a * h_ref[...] + x
    h_ref[...] = h[tm - 1:, :]
    hlast_ref[0] = h[tm - 1:, :]
    out_ref[0] = _rms(h * _gelu(yb), ng_ref[...]).astype(out_ref.dtype)


def _lru_params(p):
    eye = jnp.eye(N_BLOCKS_B, dtype=F32)
    bd = lambda w: (eye[:, None, :, None] * w[:, :, None, :]).reshape(D_B, D_B)
    wri = jnp.concatenate([bd(p['lru_w_r']), bd(p['lru_w_i'])], axis=1)
    bri = jnp.concatenate([p['lru_b_r'], p['lru_b_i']]).reshape(1, 2 * D_B)
    nsl = (-LRU_C * jax.nn.softplus(-p['lru_lambda'])).reshape(1, D_B)
    return wri, bri, nsl


def _front_kernel(x_ref, shift_ref, scale_ref, g1_ref, w_ref,
                  mu_ref, w0_ref, wup_ref, a0_ref, gup_ref, kk_ref, ka_ref,
                  cw_ref, cb_ref, wri_ref, bri_ref, nsl_ref, ng_ref,
                  wb_out, kr_out, kv_out, g_out, yb_out, hlast_out, patail_out, xtail_out,
                  carry_ref, stage_ref, xprev_ref, h_ref):
    @pl.when(pl.program_id(1) == 0)
    def _():
        carry_ref[...] = jnp.zeros_like(carry_ref)
        xprev_ref[...] = jnp.zeros_like(xprev_ref)
        h_ref[...] = jnp.zeros_like(h_ref)

    h = (_rms(x_ref[0], g1_ref[...]) * (1.0 + scale_ref[0]) + shift_ref[0]).astype(BF16)
    proj = jnp.dot(h, w_ref[...], preferred_element_type=F32)
    pa = proj[:, :N_COLS_A]
    pb = proj[:, N_COLS_A:]
    tm = pa.shape[0]
    patail_out[0] = pa[tm - SUBLANES:, :]
    xtail_out[0] = pb[tm - SUBLANES:, D_B:]

    rolled = pltpu.roll(pa, 1, axis=0)
    row = lax.broadcasted_iota(jnp.int32, pa.shape, 0)
    prev = jnp.where(row == 0, carry_ref[...], rolled)
    carry_ref[...] = pa[tm - 1:, :]
    _prep_math(pa, prev, mu_ref, w0_ref, wup_ref, a0_ref, gup_ref, kk_ref, ka_ref,
               wb_out, kr_out, kv_out, g_out, stage_ref)
    _lru_seq_math(pb, cw_ref, cb_ref, wri_ref, bri_ref, nsl_ref, ng_ref, yb_out, hlast_out, xprev_ref, h_ref)


def _front(x, shift, scale, mod_map, p, w, tm):
    nb, t, _ = x.shape
    assert nb == GROUP_BATCH
    wri, bri, nsl = _lru_params(p)
    consts = ((p['norm1_g'].reshape(1, D_MODEL), w) + _prep_params(p)
              + (p['lru_conv_w'], p['lru_conv_b'].reshape(1, D_B), wri.astype(BF16), bri, nsl,
                 p['lru_norm_g'].reshape(1, D_B)))
    mod_block = (1,) + shift.shape[1:]
    tile = lambda n: pl.BlockSpec((1, tm, n), lambda b, i: (b, i, 0))
    tail = lambda n: pl.BlockSpec((1, SUBLANES, n), lambda b, i: (b, 0, 0))
    pair_tile = pl.BlockSpec((tm, N_HEADS, LANES), lambda b, i: (i, b, 0))
    pair_shape = jax.ShapeDtypeStruct((t, nb * N_HEADS, LANES), F32)
    return pl.pallas_call(
        _front_kernel,
        grid=(nb, t // tm),
        in_specs=[tile(D_MODEL),
                  pl.BlockSpec(mod_block, functools.partial(mod_map, 0)),
                  pl.BlockSpec(mod_block, functools.partial(mod_map, 1))] + [_const_spec(a) for a in consts],
        out_specs=[pair_tile] * 3 + [tile(D_A), tile(D_B), pl.BlockSpec((1, 1, D_B), lambda b, i: (b, 0, 0)),
                                     tail(N_COLS_A), tail(D_B)],
        out_shape=[pair_shape] * 3 + [jax.ShapeDtypeStruct((nb, t, D_A), F32),
                                      jax.ShapeDtypeStruct((nb, t, D_B), BF16),
                                      jax.ShapeDtypeStruct((nb, 1, D_B), F32),
                                      jax.ShapeDtypeStruct((nb, SUBLANES, N_COLS_A), F32),
                                      jax.ShapeDtypeStruct((nb, SUBLANES, D_B), F32)],
        scratch_shapes=[pltpu.VMEM((1, N_COLS_A), F32), pltpu.VMEM((tm * N_HEADS, LANES), F32),
                        pltpu.VMEM((SUBLANES, D_B), F32), pltpu.VMEM((1, D_B), F32)],
        compiler_params=_cparams(("arbitrary", "arbitrary")),
        name="prompt_front",
    )(x, shift, scale, *consts)


def _lru_step_kernel(pb_ref, conv_ref, h0_ref, cw_ref, cb_ref, wri_ref, bri_ref, nsl_ref, ng_ref,
                     out_ref, hnew_ref):
    pb = pb_ref[...]
    yb = pb[:, :D_B]
    xb = pb[:, D_B:]
    xc = cb_ref[...] + cw_ref[3:4, :] * xb
    for j in range(CONV_W - 1):
        xc = xc + cw_ref[j:j + 1, :] * conv_ref[j]
    a, x = _lru_gates(xc, wri_ref, bri_ref, nsl_ref, True)
    h = a * h0_ref[...] + x
    hnew_ref[...] = h
    out_ref[...] = _rms(h * _gelu(yb), ng_ref[...]).astype(out_ref.dtype)


def _lru_step(pb, conv0, h0, p):
    n = pb.shape[0]
    wri, bri, nsl = _lru_params(p)
    return pl.pallas_call(
        _lru_step_kernel,
        out_shape=[jax.ShapeDtypeStruct((n, D_B), BF16), jax.ShapeDtypeStruct((n, D_B), F32)],
        compiler_params=pltpu.CompilerParams(vmem_limit_bytes=VMEM_LIMIT),
        name="rglru_step",
    )(pb, conv0, h0, p['lru_conv_w'], p['lru_conv_b'].reshape(1, D_B), wri, bri, nsl,
      p['lru_norm_g'].reshape(1, D_B))


def _to_token_tiles(ref, x):
    rows = x.shape[0]
    for c in range(TILE_CHUNKS):
        ref[pl.ds(c, rows, stride=TILE_CHUNKS), :] = x[:, c * LANES:(c + 1) * LANES]


def _from_token_tiles(ref, row0, rows):
    return jnp.concatenate(
        [ref[pl.ds(row0 + c, rows, stride=TILE_CHUNKS), :] for c in range(TILE_CHUNKS)], axis=1)


def _post_kernel(x_ref, wkv_ref, g_ref, yb_ref, gate1_ref, shift2_ref, scale2_ref, n2_ref,
                 wo_ref, rw_ref, rb_ref, tri_ref, x1_ref, h2_ref, ti_ref, rk_ref, tg_ref, cnt_ref):
    ya = (wkv_ref[0] * g_ref[0]).astype(BF16)
    mixed = (jnp.dot(ya, wo_ref[:D_A, :], preferred_element_type=F32)
             + jnp.dot(yb_ref[0], wo_ref[D_A:, :], preferred_element_type=F32))
    x1 = x_ref[0] + gate1_ref[0] * mixed
    x1_ref[0] = x1
    h2 = _rms(x1, n2_ref[...]) * (1.0 + scale2_ref[0]) + shift2_ref[0]
    _to_token_tiles(h2_ref, h2)
    logits = _dot3_nt(rw_ref[...], h2) + rb_ref[...]
    eidx = lax.broadcasted_iota(jnp.int32, logits.shape, 0)
    vals, idxs = [], []
    cur = logits
    for _ in range(TOP_K):
        m = jnp.max(cur, axis=0, keepdims=True)
        i = jnp.min(jnp.where(cur == m, eidx, N_EXPERTS), axis=0, keepdims=True)
        vals.append(m)
        idxs.append(i)
        cur = jnp.where(eidx == i, -jnp.inf, cur)
    ex = [jnp.exp(v - vals[0]) for v in vals]
    den = ex[0] + ex[1] + ex[2] + ex[3]
    sel = [eidx == i for i in idxs]
    onehot = (sel[0] | sel[1] | sel[2] | sel[3]).astype(F32)
    incl = jnp.dot(onehot.astype(BF16), tri_ref[...], preferred_element_type=F32)
    rank = incl - onehot
    cnt_ref[0] = jnp.broadcast_to(jnp.sum(onehot, axis=1, keepdims=True), cnt_ref.shape[1:])
    for k in range(TOP_K):
        ti_ref[0, k:k + 1, :] = idxs[k]
        rk_ref[0, k:k + 1, :] = jnp.sum(jnp.where(sel[k], rank, 0.0), axis=0, keepdims=True).astype(jnp.int32)
        tg_ref[0, k:k + 1, :] = ex[k] / den


def _post(x, wkv, g, yb, gate1, shift2, scale2, mod_map, p, tm):
    nb, t, _ = x.shape
    mod_block = (1,) + gate1.shape[1:]
    tile = lambda n: pl.BlockSpec((1, tm, n), lambda b, i: (b, i, 0))
    full = lambda a: pl.BlockSpec(a.shape, lambda b, i: (0,) * a.ndim)
    mspec = lambda j: pl.BlockSpec(mod_block, functools.partial(mod_map, j))
    n2 = p['norm2_g'].reshape(1, D_MODEL)
    wo = p['w_out'].astype(BF16)
    rw = p['router_w'].T
    rb = p['router_b'].reshape(N_EXPERTS, 1)
    tri = jnp.triu(jnp.ones((tm, tm), BF16))
    nt = t // tm
    n = nb * t
    topk = pl.BlockSpec((1, TOP_K, tm), lambda b, i: (b * nt + i, 0, 0))
    topk_i = jax.ShapeDtypeStruct((nb * nt, TOP_K, tm), jnp.int32)
    x1, h2, ti, rk, tg, cnt = pl.pallas_call(
        _post_kernel,
        grid=(nb, nt),
        in_specs=[tile(D_MODEL), tile(D_A), tile(D_A), tile(D_B), mspec(2), mspec(3), mspec(4),
                  full(n2), full(wo), full(rw), full(rb), full(tri)],
        out_specs=[tile(D_MODEL),
                   pl.BlockSpec((tm * TILE_CHUNKS, LANES), lambda b, i: (b * nt + i, 0)),
                   topk, topk, topk,
                   pl.BlockSpec((1, N_EXPERTS, LANES), lambda b, i: (b * nt + i, 0, 0))],
        out_shape=[jax.ShapeDtypeStruct((nb, t, D_MODEL), F32),
                   jax.ShapeDtypeStruct((n * TILE_CHUNKS, LANES), F32),
                   topk_i, topk_i,
                   jax.ShapeDtypeStruct((nb * nt, TOP_K, tm), F32),
                   jax.ShapeDtypeStruct((nb * nt, N_EXPERTS, LANES), F32)],
        compiler_params=_cparams(("arbitrary", "arbitrary")),
        name="outproj_router",
    )(x, wkv, g, yb, gate1, shift2, scale2, n2, wo, rw, rb, tri)
    return x1, h2, ti, rk, tg, cnt[:, :, 0].astype(jnp.int32)


def _plan(routes):
    cnt = jnp.concatenate([r[2] for r in routes], axis=0)
    total = jnp.sum(cnt, axis=0)
    run_start = jnp.cumsum(cnt, axis=0) - cnt
    padded = (total + MOE_ROWS - 1) // MOE_ROWS * MOE_ROWS
    pad_end = jnp.cumsum(padded)
    pad_start = pad_end - padded
    loc_off = jnp.cumsum(cnt, axis=1) - cnt
    dst0 = pad_start[None, :] + run_start
    n_tokens = sum(r[0].shape[0] * r[0].shape[2] for r in routes)
    n_blocks = -(-n_tokens * TOP_K // MOE_ROWS) + N_EXPERTS
    block_row0 = jnp.arange(n_blocks, dtype=jnp.int32) * MOE_ROWS
    block_e = jnp.minimum(jnp.sum(pad_end[None, :] <= block_row0[:, None], axis=1), N_EXPERTS - 1)
    n_used = (pad_end[-1] // MOE_ROWS).reshape(1)
    as_i32 = lambda a: a.astype(jnp.int32)
    experts = jnp.arange(N_EXPERTS, dtype=jnp.int32)
    groups, t0 = [], 0
    for ti, rk, c in routes:
        nt = ti.shape[0]
        sl = slice(t0, t0 + nt)
        off = jnp.sum(jnp.where(ti[..., None] == experts, loc_off[sl, None, None, :], 0), axis=-1)
        lpos = as_i32((rk + off) * TILE_CHUNKS).reshape(nt, -1)
        table = as_i32(jnp.stack([cnt[sl], loc_off[sl], dst0[sl]], axis=0).reshape(3, -1))
        groups.append((lpos, table))
        t0 += nt
    return groups, as_i32(block_e), as_i32(n_used), as_i32(pad_end), as_i32(padded), n_blocks


def _tile_rows(row):
    return pl.ds(pl.multiple_of(row * TILE_CHUNKS, TILE_CHUNKS), TILE_CHUNKS)


def _sublane_rows(row8):
    return pl.ds(pl.multiple_of(row8, TILE_CHUNKS), TILE_CHUNKS)


def _piece_rows(row0, n_rows):
    return pl.ds(pl.multiple_of(row0 * TILE_CHUNKS, TILE_CHUNKS), n_rows * TILE_CHUNKS)


def _block_rows(block):
    rows = MOE_ROWS * TILE_CHUNKS
    return pl.ds(pl.multiple_of(block * rows, rows), rows)


def _for_each_piece(table_ref, tile, n_tiles, fn):
    def per_expert(e, carry):
        col = tile * N_EXPERTS + e
        cnt = table_ref[col]
        lo = table_ref[n_tiles * N_EXPERTS + col]
        d0 = table_ref[2 * n_tiles * N_EXPERTS + col]

        def full(c, c2):
            fn(lo + c * MOE_CHUNK, d0 + c * MOE_CHUNK, MOE_CHUNK)
            return c2
        lax.fori_loop(0, cnt // MOE_CHUNK, full, 0)
        bit = MOE_CHUNK // 2
        while bit >= 1:
            done = cnt - cnt % (2 * bit)

            @pl.when((cnt & bit) != 0)
            def _(done=done, bit=bit):
                fn(lo + done, d0 + done, bit)
            bit //= 2
        return carry
    lax.fori_loop(0, N_EXPERTS, per_expert, 0)


def _scatter_kernel(*refs, first):
    idx_ref, loc_ref, zero_ref, sem, isem, zsem = refs[-6:]
    table_ref, pad_end_ref, padded_ref, h2_ref, lpos_hbm = refs[:5]
    xs_hbm = refs[-7]
    i = pl.program_id(0)
    n_tiles = pl.num_programs(0)
    tm = h2_ref.shape[0] // TILE_CHUNKS
    slot = i % 2

    def idx_copy(tile, s):
        dst = idx_ref.at[pl.ds(s * (TOP_K * tm), TOP_K * tm)]
        return pltpu.make_async_copy(lpos_hbm.at[tile], dst, isem.at[s])

    def drain(s):
        for _ in range(TOP_K):
            pltpu.make_async_copy(h2_ref, xs_hbm.at[pl.ds(0, tm * TILE_CHUNKS), :], sem.at[s]).wait()

    @pl.when(i == 0)
    def _():
        idx_copy(0, 0).start()

    if first:
        @pl.when(i == 0)
        def _():
            zero_ref[...] = jnp.zeros_like(zero_ref)
            n_used = pad_end_ref[N_EXPERTS - 1] // MOE_ROWS
            n_blocks = xs_hbm.shape[0] // (MOE_ROWS * TILE_CHUNKS)

            def fill(e, n_started):
                n_fill = jnp.minimum(padded_ref[e] // MOE_ROWS, 2)
                last = pad_end_ref[e] // MOE_ROWS - 1

                def one(j, c):
                    pltpu.make_async_copy(zero_ref, xs_hbm.at[_block_rows(last - j), :], zsem).start()
                    return c
                lax.fori_loop(0, n_fill, one, 0)
                return n_started + n_fill
            n_started = lax.fori_loop(0, N_EXPERTS, fill, jnp.int32(0))

            def fill_tail(blk, c):
                pltpu.make_async_copy(zero_ref, xs_hbm.at[_block_rows(blk), :], zsem).start()
                return c
            lax.fori_loop(n_used, n_blocks, fill_tail, 0)

            def drain_fill(j, c):
                pltpu.make_async_copy(zero_ref, xs_hbm.at[_block_rows(0), :], zsem).wait()
                return c
            lax.fori_loop(0, n_started + (n_blocks - n_used), drain_fill, 0)

    def step(s):
        @pl.when(i + 1 < n_tiles)
        def _():
            idx_copy(i + 1, 1 - s).start()

        @pl.when(i >= 2)
        def _():
            drain(s)
        idx_copy(i, s).wait()
        base = s * (TOP_K * tm)

        def place(r, c):
            row = h2_ref[_tile_rows(r), :]
            for k in range(TOP_K):
                loc_ref[s, _sublane_rows(idx_ref[base + k * tm + r]), :] = row
            return c
        lax.fori_loop(0, tm, place, 0, unroll=8)

        def send(lo, d0, n):
            pltpu.make_async_copy(loc_ref.at[s, _piece_rows(lo, n), :], xs_hbm.at[_piece_rows(d0, n), :],
                                  sem.at[s]).start()
        _for_each_piece(table_ref, i, n_tiles, send)

        @pl.when(i == n_tiles - 1)
        def _():
            @pl.when(i >= 1)
            def _():
                drain(1 - s)
            drain(s)

    for s in range(2):
        pl.when(slot == s)(functools.partial(step, s))


def _scatter(h2, lpos, table, pad_end, padded, n_rows, tm, xs=None):
    nt = lpos.shape[0]
    first = xs is None
    assert first or nt * tm <= MOE_ROWS, "a later call may add at most MOE_ROWS rows per expert"
    loc_rows = TOP_K * tm * TILE_CHUNKS
    in_specs = [pl.BlockSpec((tm * TILE_CHUNKS, LANES), lambda i, *_: (i, 0)),
                pl.BlockSpec(memory_space=pl.ANY)]
    args = [table.reshape(-1), pad_end, padded, h2, lpos]
    if not first:
        in_specs.append(pl.BlockSpec(memory_space=pl.ANY))
        args.append(xs)
    grid_spec = pltpu.PrefetchScalarGridSpec(
        num_scalar_prefetch=3,
        grid=(nt,),
        in_specs=in_specs,
        out_specs=pl.BlockSpec(memory_space=pl.ANY),
        scratch_shapes=[pltpu.SMEM((2 * TOP_K * tm,), jnp.int32),
                        pltpu.VMEM((2, loc_rows, LANES), F32),
                        pltpu.VMEM((MOE_ROWS * TILE_CHUNKS, LANES), F32),
                        pltpu.SemaphoreType.DMA((2,)), pltpu.SemaphoreType.DMA((2,)), pltpu.SemaphoreType.DMA],
    )
    return pl.pallas_call(
        functools.partial(_scatter_kernel, first=first),
        grid_spec=grid_spec,
        out_shape=jax.ShapeDtypeStruct((n_rows * TILE_CHUNKS, LANES), F32),
        input_output_aliases={} if first else {5: 0},
        compiler_params=_cparams(("arbitrary",)),
        name="moe_dispatch",
    )(*args)


def _mlp_kernel(be_ref, nu_ref, next_ref, par_ref, xs_ref, bias_ref, wg_hbm, wu_hbm, wd_hbm, out_ref,
                w32_ref, w16_ref, sem):
    i = pl.program_id(0)
    used = i < nu_ref[0]
    e = be_ref[i]

    def weight_copies(expert, s):
        return [pltpu.make_async_copy(w_hbm.at[expert], w32_ref.at[s, j], sem.at[s])
                for j, w_hbm in enumerate((wg_hbm, wu_hbm, wd_hbm))]

    def load_expert(s):
        @pl.when(i == 0)
        def _():
            for c in weight_copies(e, s):
                c.start()
        for c in weight_copies(e, s):
            c.wait()
        nxt = next_ref[e]

        @pl.when(nxt >= 0)
        def _():
            for c in weight_copies(nxt, 1 - s):
                c.start()
        for j in range(3):
            w16_ref[j] = w32_ref[s, j].astype(BF16)

    first_of_expert = used & ((i == 0) | (e != be_ref[jnp.maximum(i - 1, 0)]))
    for s in range(2):
        pl.when(first_of_expert & (par_ref[e] == s))(functools.partial(load_expert, s))

    @pl.when(used)
    def _():
        x = _from_token_tiles(xs_ref, 0, MOE_ROWS).astype(BF16)
        gt = jnp.dot(x, w16_ref[0], preferred_element_type=F32) + bias_ref[0, 0:1, :]
        up = jnp.dot(x, w16_ref[1], preferred_element_type=F32) + bias_ref[0, 1:2, :]
        gt = jnp.minimum(gt, SWIGLU_LIMIT)
        up = jnp.clip(up, -SWIGLU_LIMIT, SWIGLU_LIMIT)
        glu = gt * _sigmoid(gt * SWIGLU_ALPHA)
        mid = ((up + 1.0) * glu).astype(BF16)
        _to_token_tiles(out_ref, jnp.dot(mid, w16_ref[2], preferred_element_type=F32) + bias_ref[0, 2:3, :])

    @pl.when(jnp.logical_not(used))
    def _():
        out_ref[...] = jnp.zeros_like(out_ref)


def _mlp(xs, block_e, n_used, padded, n_blocks, wts):
    wg, bg, wu, bu, wd, bd = wts
    assert D_FF == D_MODEL
    bias = jnp.stack([bg, bu, bd], axis=1)
    nonempty = padded > 0
    experts = jnp.arange(N_EXPERTS, dtype=jnp.int32)
    parity = ((jnp.cumsum(nonempty) - nonempty) % 2).astype(jnp.int32)
    later = nonempty[None, :] & (experts[None, :] > experts[:, None])
    nxt = jnp.min(jnp.where(later, experts[None, :], N_EXPERTS), axis=1)
    nxt = jnp.where(nxt == N_EXPERTS, -1, nxt).astype(jnp.int32)
    rows = pl.BlockSpec((MOE_ROWS * TILE_CHUNKS, LANES), lambda i, *_: (i, 0))
    hbm = pl.BlockSpec(memory_space=pl.ANY)
    grid_spec = pltpu.PrefetchScalarGridSpec(
        num_scalar_prefetch=4,
        grid=(n_blocks,),
        in_specs=[rows, pl.BlockSpec((1, 3, D_FF), lambda i, be, *_: (be[i], 0, 0)), hbm, hbm, hbm],
        out_specs=rows,
        scratch_shapes=[pltpu.VMEM((2, 3, D_MODEL, D_FF), F32),
                        pltpu.VMEM((3, D_MODEL, D_FF), BF16),
                        pltpu.SemaphoreType.DMA((2,))],
    )
    return pl.pallas_call(
        _mlp_kernel,
        grid_spec=grid_spec,
        out_shape=jax.ShapeDtypeStruct(xs.shape, F32),
        compiler_params=_cparams(("arbitrary",)),
        name="moe_experts",
    )(block_e, n_used, nxt, parity, xs, bias, wg, wu, wd)


def _gather_kernel(table_ref, x1_ref, gate2_ref, fg_ref, lpos_hbm, tg_hbm, rows_hbm, y_ref,
                   idx_ref, gsm_ref, loc_ref, ff_ref, sem, isem):
    i = pl.program_id(1) + pl.program_id(0) * pl.num_programs(1)
    n_tiles = pl.num_programs(0) * pl.num_programs(1)
    tm = x1_ref.shape[1]
    slot = i % 2

    def meta_copies(tile, s):
        seg = pl.ds(s * (TOP_K * tm), TOP_K * tm)
        return (pltpu.make_async_copy(lpos_hbm.at[tile], idx_ref.at[seg], isem.at[s]),
                pltpu.make_async_copy(tg_hbm.at[tile], gsm_ref.at[seg], isem.at[s]))

    def fetch_tile(tile, s):
        for c in meta_copies(tile, s):
            c.start()

        def fetch(lo, d0, n):
            pltpu.make_async_copy(rows_hbm.at[_piece_rows(d0, n), :], loc_ref.at[s, _piece_rows(lo, n), :],
                                  sem.at[s]).start()
        _for_each_piece(table_ref, tile, n_tiles, fetch)

    @pl.when(i == 0)
    def _():
        fetch_tile(0, 0)

    def step(s):
        @pl.when(i + 1 < n_tiles)
        def _():
            fetch_tile(i + 1, 1 - s)

        pltpu.make_async_copy(rows_hbm.at[pl.ds(0, TOP_K * tm * TILE_CHUNKS), :], loc_ref.at[s], sem.at[s]).wait()
        for c in meta_copies(i, s):
            c.wait()
        base = s * (TOP_K * tm)

        def mix(r, c):
            acc = gsm_ref[base + r] * loc_ref[s, _sublane_rows(idx_ref[base + r]), :]
            for k in range(1, TOP_K):
                acc = acc + gsm_ref[base + k * tm + r] * loc_ref[s, _sublane_rows(idx_ref[base + k * tm + r]), :]
            ff_ref[_tile_rows(r), :] = acc
            return c
        lax.fori_loop(0, tm, mix, 0, unroll=8)

    for s in range(2):
        pl.when(slot == s)(functools.partial(step, s))

    x2 = x1_ref[0] + gate2_ref[0] * _from_token_tiles(ff_ref, 0, tm)
    y_ref[0] = _rms(x2, fg_ref[...])


def _gather(x1, tg, gate2, mod_map, final_g, lpos, table, rows, tm):
    nb, t, _ = x1.shape
    nt = t // tm
    mod_block = (1,) + gate2.shape[1:]
    loc_rows = TOP_K * tm * TILE_CHUNKS
    grid_spec = pltpu.PrefetchScalarGridSpec(
        num_scalar_prefetch=1,
        grid=(nb, nt),
        in_specs=[pl.BlockSpec((1, tm, D_MODEL), lambda b, i, *_: (b, i, 0)),
                  pl.BlockSpec(mod_block, lambda b, i, *_: mod_map(5, b, i)),
                  pl.BlockSpec((1, D_MODEL), lambda b, i, *_: (0, 0)),
                  pl.BlockSpec(memory_space=pl.ANY),
                  pl.BlockSpec(memory_space=pl.ANY),
                  pl.BlockSpec(memory_space=pl.ANY)],
        out_specs=pl.BlockSpec((1, tm, D_MODEL), lambda b, i, *_: (b, i, 0)),
        scratch_shapes=[pltpu.SMEM((2 * TOP_K * tm,), jnp.int32),
                        pltpu.SMEM((2 * TOP_K * tm,), F32),
                        pltpu.VMEM((2, loc_rows, LANES), F32),
                        pltpu.VMEM((tm * TILE_CHUNKS, LANES), F32),
                        pltpu.SemaphoreType.DMA((2,)), pltpu.SemaphoreType.DMA((2,))],
    )
    return pl.pallas_call(
        _gather_kernel,
        grid_spec=grid_spec,
        out_shape=jax.ShapeDtypeStruct((nb, t, D_MODEL), F32),
        compiler_params=_cparams(("arbitrary", "arbitrary")),
        name="moe_combine",
    )(table.reshape(-1), x1, gate2, final_g.reshape(1, D_MODEL), lpos, tg.reshape(tg.shape[0], -1), rows)


def _moe(route_p, route_s, mod_p, mod_s, final_g, wts):
    routes = [route_p, route_s]
    groups, block_e, n_used, pad_end, padded, n_blocks = _plan([(r[2], r[3], r[5]) for r in routes])
    xs = None
    for (x1, h2, ti, rk, tg, cnt), (lpos, table) in zip(routes, groups):
        xs = _scatter(h2, lpos, table, pad_end, padded, n_blocks * MOE_ROWS, ti.shape[2], xs)
    rows = _mlp(xs, block_e, n_used, padded, n_blocks, wts)
    outs = []
    for (x1, h2, ti, rk, tg, cnt), (lpos, table), (mod, mod_map) in zip(routes, groups, (mod_p, mod_s)):
        outs.append(_gather(x1, tg, mod, mod_map, final_g, lpos, table, rows, ti.shape[2]))
    return outs


def _forward(x_prompt, x_sample, c_prompt, c_sample, state_wkv, state_shift, state_conv, state_lru, p, final_g):
    bp, tp, _ = x_prompt.shape
    bs = x_sample.shape[0]
    tm = min(512, tp)
    tt = min(128, tp)

    mod = _ada(jnp.concatenate([c_prompt, c_sample], axis=0), p['w_ada'], p['b_ada'])
    mod_p = mod[:bp].reshape(bp * N_MOD, 1, D_MODEL)
    mod_s = mod[bp:].reshape(bs, N_MOD, D_MODEL).transpose(1, 0, 2)
    map_p = lambda j, b, i: (b * N_MOD + j, 0, 0)
    map_s = lambda j, b, i: (j, 0, 0)

    wts = (p['w_gate'], p['b_gate'], p['w_up'], p['b_up'], p['w_down'], p['b_down'])

    wb_p, kr_p, kv_p, g, yb, lru_p, pa_tail, x_tail = _front(x_prompt, mod_p, mod_p, map_p, p,
                                                             p['w_in'].astype(BF16), tm)
    s0 = jnp.zeros((bp, N_HEADS, HEAD_DIM, HEAD_DIM), F32)
    wkv_out, wkv_p = _wkv_scan(wb_p[None], kr_p[None], kv_p[None], s0, p, tt)
    route_p = _post(x_prompt, wkv_out, g, yb, mod_p, mod_p, mod_p, map_p, p, tm)
    shift_p = pa_tail[:, -1, :]
    conv_p = x_tail[:, SUBLANES - (CONV_W - 1):, :]

    xs = x_sample.reshape(1, bs, D_MODEL)
    pa_s, pb_s = _inproj(xs, mod_s, mod_s, map_s, p['norm1_g'], p['w_in'], bs)
    wb_s, kr_s, kv_s, g = _prep(pa_s, state_shift.reshape(1, bs, N_COLS_A), p, bs)
    as_seq = lambda a: _pairs_to_groups(a.reshape(bs, N_HEADS, LANES), 1)
    wkv_out, wkv_s = _wkv_scan(as_seq(wb_s), as_seq(kr_s), as_seq(kv_s), state_wkv, p, 1)
    conv0 = state_conv.transpose(1, 0, 2)
    yb, lru_s = _lru_step(pb_s[0], conv0, state_lru, p)
    route_s = _post(xs, wkv_out.reshape(1, bs, D_A), g, yb.reshape(1, bs, D_B),
                    mod_s, mod_s, mod_s, map_s, p, bs)
    y_prompt, y_sample = _moe(route_p, route_s, (mod_p, map_p), (mod_s, map_s), final_g, wts)
    shift_s = pa_s[0]
    conv_s = jnp.concatenate([state_conv[:, 1:], pb_s[0][:, None, D_B:]], axis=1)

    return (y_prompt, y_sample.reshape(bs, 1, D_MODEL),
            wkv_p[None], shift_p[None], conv_p[None], lru_p.reshape(bp, D_B)[None],
            wkv_s[None], shift_s[None], conv_s[None], lru_s[None])


def kernel(x_prompt, x_sample, c_prompt, c_sample, state_wkv, state_shift, state_conv, state_lru, w_ada, b_ada, norm1_g, norm2_g, w_in, rk_mu, rk_w0, rk_w_up, rk_a0, rk_a_up, rk_g_up, rk_k_k, rk_k_a, rk_r_k, rk_lnx_w, rk_lnx_b, lru_conv_w, lru_conv_b, lru_w_r, lru_b_r, lru_w_i, lru_b_i, lru_lambda, lru_norm_g, w_out, router_w, router_b, w_gate, b_gate, w_up, b_up, w_down, b_down, final_g):
    assert w_ada.shape[0] == 1, "single-layer trunk"
    p = dict(w_ada=w_ada[0], b_ada=b_ada[0], norm1_g=norm1_g[0], norm2_g=norm2_g[0], w_in=w_in[0],
             rk_mu=rk_mu[0], rk_w0=rk_w0[0], rk_w_up=rk_w_up[0], rk_a0=rk_a0[0], rk_a_up=rk_a_up[0],
             rk_g_up=rk_g_up[0], rk_k_k=rk_k_k[0], rk_k_a=rk_k_a[0], rk_r_k=rk_r_k[0],
             rk_lnx_w=rk_lnx_w[0], rk_lnx_b=rk_lnx_b[0], lru_conv_w=lru_conv_w[0],
             lru_conv_b=lru_conv_b[0], lru_w_r=lru_w_r[0], lru_b_r=lru_b_r[0], lru_w_i=lru_w_i[0],
             lru_b_i=lru_b_i[0], lru_lambda=lru_lambda[0], lru_norm_g=lru_norm_g[0], w_out=w_out[0],
             router_w=router_w[0], router_b=router_b[0], w_gate=w_gate[0], b_gate=b_gate[0],
             w_up=w_up[0], b_up=b_up[0], w_down=w_down[0], b_down=b_down[0])
    return _forward(x_prompt, x_sample, c_prompt, c_sample, state_wkv[0], state_shift[0], state_conv[0],
                    state_lru[0], p, final_g)
```

```python
import functools

import jax
import jax.numpy as jnp
from jax import lax
from jax.experimental import pallas as pl
from jax.experimental.pallas import tpu as pltpu

F32 = jnp.float32
BF16 = jnp.bfloat16

D_MODEL = 1024
D_A = 512
HEAD_DIM = 64
N_HEADS = 8
D_B = 512
N_BLOCKS_B = 8
CONV_W = 4
LRU_C = 8.0
R_DECAY = 64
R_AAA = 64
R_GATE = 128
N_COLS_A = 3 * D_A + R_DECAY + R_AAA + R_GATE
N_COLS_B = 2 * D_B
N_EXPERTS = 32
TOP_K = 4
D_FF = 1024
SWIGLU_LIMIT = 7.0
SWIGLU_ALPHA = 1.702
RMS_EPS = 1e-6
LN_X_EPS = 64e-5
N_MOD = 6

LANES = 128
SUBLANES = 8
GROUP_BATCH = 8
HALF_ROWS = HEAD_DIM // 2
MOE_ROWS = 256
MOE_CHUNK = 16
TILE_CHUNKS = D_MODEL // LANES
VMEM_LIMIT = 56 * 1024 * 1024


def _cparams(sem):
    return pltpu.CompilerParams(dimension_semantics=sem, vmem_limit_bytes=VMEM_LIMIT)


def _dot(a, b):
    return jnp.dot(a.astype(BF16), b.astype(BF16), preferred_element_type=F32)


def _split(a):
    hi = a.astype(BF16)
    lo = (a - hi.astype(F32)).astype(BF16)
    return hi, lo


def _dot3(a, b):
    ah, al = _split(a)
    bh, bl = _split(b)
    return (jnp.dot(ah, bh, preferred_element_type=F32)
            + (jnp.dot(al, bh, preferred_element_type=F32) + jnp.dot(ah, bl, preferred_element_type=F32)))


def _dot3_nt(a, b):
    dn = (((1,), (1,)), ((), ()))
    ah, al = _split(a)
    bh, bl = _split(b)
    d = lambda x, y: lax.dot_general(x, y, dn, preferred_element_type=F32)
    return d(ah, bh) + (d(al, bh) + d(ah, bl))


def _softplus(x):
    return jnp.maximum(x, 0.0) + jnp.log1p(jnp.exp(-jnp.abs(x)))


def _sigmoid(x):
    return 1.0 / (1.0 + jnp.exp(-x))


def _rms(x, g):
    ms = jnp.mean(x * x, axis=-1, keepdims=True)
    return x * lax.rsqrt(ms + RMS_EPS) * g


def _ada_kernel(c_ref, w_ref, b_ref, o_ref):
    c = c_ref[...]
    s = c * _sigmoid(c)
    o_ref[...] = _dot3(s, w_ref[...]) + b_ref[...]


def _ada(c, w_ada, b_ada):
    rows = c.shape[0]
    ncol = w_ada.shape[1]
    tn = D_MODEL
    return pl.pallas_call(
        _ada_kernel,
        grid=(ncol // tn,),
        in_specs=[pl.BlockSpec((rows, D_MODEL), lambda j: (0, 0)),
                  pl.BlockSpec((D_MODEL, tn), lambda j: (0, j)),
                  pl.BlockSpec((1, tn), lambda j: (0, j))],
        out_specs=pl.BlockSpec((rows, tn), lambda j: (0, j)),
        out_shape=jax.ShapeDtypeStruct((rows, ncol), F32),
        compiler_params=_cparams(("arbitrary",)),
        name="ada_mod",
    )(c, w_ada, b_ada.reshape(1, ncol))


def _inproj_kernel(x_ref, shift_ref, scale_ref, g_ref, w_ref, pa_ref, pb_ref):
    x = x_ref[0]
    h = _rms(x, g_ref[...]) * (1.0 + scale_ref[0]) + shift_ref[0]
    proj = _dot3(h, w_ref[...])
    pa_ref[0] = proj[:, :N_COLS_A]
    pb_ref[0] = proj[:, N_COLS_A:]


def _inproj(x, shift, scale, mod_map, g, w, tm):
    nb, t, _ = x.shape
    mod_block = (1,) + shift.shape[1:]
    return pl.pallas_call(
        _inproj_kernel,
        grid=(nb, t // tm),
        in_specs=[pl.BlockSpec((1, tm, D_MODEL), lambda b, i: (b, i, 0)),
                  pl.BlockSpec(mod_block, functools.partial(mod_map, 0)),
                  pl.BlockSpec(mod_block, functools.partial(mod_map, 1)),
                  pl.BlockSpec((1, D_MODEL), lambda b, i: (0, 0)),
                  pl.BlockSpec((D_MODEL, N_COLS_A + N_COLS_B), lambda b, i: (0, 0))],
        out_specs=[pl.BlockSpec((1, tm, N_COLS_A), lambda b, i: (b, i, 0)),
                   pl.BlockSpec((1, tm, N_COLS_B), lambda b, i: (b, i, 0))],
        out_shape=[jax.ShapeDtypeStruct((nb, t, N_COLS_A), F32),
                   jax.ShapeDtypeStruct((nb, t, N_COLS_B), F32)],
        compiler_params=_cparams(("arbitrary", "arbitrary")),
        name="norm1_inproj",
    )(x, shift, scale, g.reshape(1, D_MODEL), w)


def _store_head_pairs(ref, stage_ref, x, y):
    rows = x.shape[0]
    flat = ref.shape[0] == 1
    for h in range(N_HEADS):
        sl = slice(h * HEAD_DIM, (h + 1) * HEAD_DIM)
        pair = jnp.concatenate([x[:, sl], y[:, sl]], axis=1)
        if flat:
            ref[0, pl.ds(h, rows, stride=N_HEADS), :] = pair
        else:
            stage_ref[pl.ds(h, rows, stride=N_HEADS), :] = pair
    if not flat:
        ref[...] = stage_ref[...].reshape(rows, N_HEADS, LANES)


def _prep_math(pa, prev, mu_ref, w0_ref, wup_ref, a0_ref, gup_ref, kk_ref, ka_ref,
               wb_out, kr_out, kv_out, g_out, stage_ref):
    z = pa + (prev - pa) * mu_ref[...]
    r = z[:, 0:D_A]
    k = z[:, D_A:2 * D_A]
    v = z[:, 2 * D_A:3 * D_A]
    lo = 3 * D_A
    za = z[:, lo:lo + R_DECAY + R_AAA]
    lane = lax.broadcasted_iota(jnp.int32, za.shape, 1)
    za = jnp.where(lane < R_DECAY, jnp.tanh(za), za)
    lw = _dot3(za, wup_ref[...])
    w_log = -_softplus(-(w0_ref[...] + lw[:, :D_A])) - 0.5
    decay = jnp.exp(-jnp.exp(w_log))
    a = _sigmoid(a0_ref[...] + lw[:, D_A:])
    gd = z[:, lo + R_DECAY + R_AAA:]
    g = _dot3(_sigmoid(gd), gup_ref[...])
    kk = k * kk_ref[...]
    _store_head_pairs(wb_out, stage_ref, decay, kk * a)
    _store_head_pairs(kr_out, stage_ref, k * (1.0 + (a - 1.0) * ka_ref[...]), r)
    _store_head_pairs(kv_out, stage_ref, kk, v)
    g_out[0] = g


def _prep_kernel(pa_ref, prev_ref, *refs):
    _prep_math(pa_ref[0], prev_ref[0], *refs, None)


def _prep_params(p):
    wup = jnp.zeros((R_DECAY + R_AAA, 2 * D_A), F32)
    wup = wup.at[:R_DECAY, :D_A].set(p['rk_w_up']).at[R_DECAY:, D_A:].set(p['rk_a_up'])
    vec = lambda a: a.reshape(1, -1)
    return (vec(p['rk_mu']), vec(p['rk_w0']), wup, vec(p['rk_a0']), p['rk_g_up'],
            vec(p['rk_k_k']), vec(p['rk_k_a']))


def _const_spec(a):
    return pl.BlockSpec(a.shape, lambda b, i: (0,) * a.ndim)


def _prep(pa, prev, p, tm):
    nb, t, _ = pa.shape
    params = _prep_params(p)
    rows = pl.BlockSpec((1, tm, N_COLS_A), lambda b, i: (b, i, 0))
    return pl.pallas_call(
        _prep_kernel,
        grid=(nb, t // tm),
        in_specs=[rows, rows] + [_const_spec(a) for a in params],
        out_specs=[pl.BlockSpec((1, tm * N_HEADS, LANES), lambda b, i: (b, i, 0))] * 3
                  + [pl.BlockSpec((1, tm, D_A), lambda b, i: (b, i, 0))],
        out_shape=[jax.ShapeDtypeStruct((nb, t * N_HEADS, LANES), F32)] * 3
                  + [jax.ShapeDtypeStruct((nb, t, D_A), F32)],
        compiler_params=_cparams(("arbitrary", "arbitrary")),
        name="rwkv_prep",
    )(pa, prev, *params)


SCAN_SLOTS = 4


def _scan_kernel(wb_ref, kr_ref, kv_ref, kvnext_ref, s0_ref, lnw_ref, lnb_ref, rk_ref,
                 out_ref, sfin_ref, s_ref, sa_ref, inv_ref, wd_ref, kkd_ref, bd_ref, kd_ref, rd_ref, vd_ref, *, tt):
    ti = pl.program_id(1)
    upper_half = lax.broadcasted_iota(jnp.int32, (HALF_ROWS, LANES), 1) >= LANES // 2
    unroll = SCAN_SLOTS if tt % SCAN_SLOTS == 0 else 1

    def expand(pair):
        return jnp.concatenate([pair, pair], axis=0).T

    def prepare(slot, s, kk_pair=None):
        t1 = expand(wb_ref[0, s])
        wd_ref[slot] = t1[:HEAD_DIM]
        bd_ref[slot] = t1[HEAD_DIM:]
        t2 = expand(kr_ref[0, s])
        kd_ref[slot] = t2[:HEAD_DIM]
        rd_ref[slot] = t2[HEAD_DIM:]
        t3 = expand(kv_ref[0, s])
        vd_ref[slot] = jnp.where(upper_half, t3[HEAD_DIM + HALF_ROWS:], t3[HEAD_DIM:HEAD_DIM + HALF_ROWS])
        kkd_ref[slot] = t3[:HEAD_DIM] if kk_pair is None else expand(kk_pair)[:HEAD_DIM]

    prepare(0, 0)
    if tt > 1:
        prepare(1, 1)
    else:
        kkd_ref[1] = expand(kvnext_ref[0, 0])[:HEAD_DIM]

    def inv_norm2(kk_rows):
        s2 = jnp.sum(kk_rows * kk_rows, axis=0, keepdims=True)
        return 1.0 / jnp.maximum(s2, 1e-24)

    @pl.when(ti == 0)
    def _():
        s_ref[...] = s0_ref[0]
        acc = jnp.zeros((HALF_ROWS, LANES), F32)
        for j in range(HEAD_DIM):
            acc = acc + s0_ref[0, j] * kkd_ref[0, j:j + 1, :]
        sa_ref[...] = acc
        inv_ref[...] = inv_norm2(kkd_ref[0])

    def finish(y, cv):
        tot = jnp.sum(y, axis=0, keepdims=True)
        tot = tot + pltpu.roll(tot, LANES // 2, axis=1)
        d = y - tot * (1.0 / HEAD_DIM)
        sq = jnp.sum(d * d, axis=0, keepdims=True)
        sq = sq + pltpu.roll(sq, LANES // 2, axis=1)
        yn = d * lax.rsqrt(sq * (1.0 / HEAD_DIM) + LN_X_EPS)
        return yn * lnw_ref[...] + lnb_ref[...] + cv

    def step(t, u, carry):
        sa, inv2, y_prev, cv_prev = carry
        nxt_slot = (u + 1) % SCAN_SLOTS
        out_ref[0, jnp.maximum(t - 1, 0)] = finish(y_prev, cv_prev)
        sae = sa * (-inv2)
        anchor = lax.shift_right_logical(lax.shift_right_logical(lax.bitcast_convert_type(sae, jnp.uint32),
                                                                 jnp.uint32(16)), jnp.uint32(16))
        v = vd_ref[u] + lax.bitcast_convert_type(anchor, F32)
        acc_y = jnp.zeros((HALF_ROWS, LANES), F32)
        acc_s = jnp.zeros((HALF_ROWS, LANES), F32)
        for j in range(HEAD_DIM):
            row = pl.ds(j, 1)
            s_new = s_ref[j] * wd_ref[u, row, :] + sae * bd_ref[u, row, :] + v * kd_ref[u, row, :]
            s_ref[j] = s_new
            acc_y = acc_y + s_new * rd_ref[u, row, :]
            acc_s = acc_s + s_new * kkd_ref[nxt_slot, row, :]
        c = jnp.sum(rd_ref[u] * kd_ref[u] * rk_ref[...], axis=0, keepdims=True)
        if unroll > 1:
            ahead = jnp.minimum(t + 2, tt - 1)
            kk_pair = None
            if (u + 2) % SCAN_SLOTS == 0:
                kk_pair = jnp.where(t + 2 == tt, kvnext_ref[0, 0], kv_ref[0, ahead])
            prepare((u + 2) % SCAN_SLOTS, ahead, kk_pair)
        return acc_s, inv_norm2(kkd_ref[nxt_slot]), acc_y, c * v

    def steps(q, carry):
        for u in range(unroll):
            carry = step(q * unroll + u, u, carry)
        return carry

    zeros = jnp.zeros((HALF_ROWS, LANES), F32)
    sa, inv2, y_last, cv_last = lax.fori_loop(0, tt // unroll, steps, (sa_ref[...], inv_ref[...], zeros, zeros))
    out_ref[0, tt - 1] = finish(y_last, cv_last)
    sa_ref[...] = sa
    inv_ref[...] = inv2

    @pl.when(ti == pl.num_programs(1) - 1)
    def _():
        sfin_ref[0] = s_ref[...]


def _from_scan_rows(y):
    g, t = y.shape[:2]
    x = y.reshape(g, t, HALF_ROWS, 2, GROUP_BATCH, N_HEADS).transpose(0, 4, 1, 5, 3, 2)
    return x.reshape(g * GROUP_BATCH, t, D_A)


def _state_to_scan(s):
    g = s.shape[0] // GROUP_BATCH
    y = s.reshape(g, GROUP_BATCH, N_HEADS, 2, HALF_ROWS, HEAD_DIM).transpose(0, 5, 4, 3, 1, 2)
    return y.reshape(g, HEAD_DIM, HALF_ROWS, LANES)


def _state_from_scan(y):
    g = y.shape[0]
    s = y.reshape(g, HEAD_DIM, HALF_ROWS, 2, GROUP_BATCH, N_HEADS).transpose(0, 4, 5, 3, 2, 1)
    return s.reshape(g * GROUP_BATCH, N_HEADS, HEAD_DIM, HEAD_DIM)


def _head_rows(x):
    y = x.reshape(N_HEADS, 2, HALF_ROWS).transpose(2, 1, 0)
    y = jnp.broadcast_to(y[:, :, None, :], (HALF_ROWS, 2, GROUP_BATCH, N_HEADS))
    return y.reshape(HALF_ROWS, LANES)


def _head_keys(x):
    y = jnp.broadcast_to(x.T[:, None, None, :], (HEAD_DIM, 2, GROUP_BATCH, N_HEADS))
    return y.reshape(HEAD_DIM, LANES)


def _pairs_to_groups(x, t):
    nb = x.shape[0]
    g = nb // GROUP_BATCH
    y = x.reshape(g, GROUP_BATCH, t, N_HEADS, LANES).transpose(0, 2, 1, 3, 4)
    return y.reshape(g, t, GROUP_BATCH * N_HEADS, LANES)


def _wkv_scan(wb, kr, kv, s0, p, tt):
    g, t = wb.shape[:2]
    pair_tile = pl.BlockSpec((1, tt, HEAD_DIM, LANES), lambda gi, i: (gi, i, 0, 0))
    next_step = pl.BlockSpec((1, 1, HEAD_DIM, LANES), lambda gi, i: (gi, jnp.minimum((i + 1) * tt, t - 1), 0, 0))
    row_tile = pl.BlockSpec((1, tt, HALF_ROWS, LANES), lambda gi, i: (gi, i, 0, 0))
    state = pl.BlockSpec((1, HEAD_DIM, HALF_ROWS, LANES), lambda gi, i: (gi, 0, 0, 0))
    const = lambda n: pl.BlockSpec((n, LANES), lambda gi, i: (0, 0))
    keys = pltpu.VMEM((SCAN_SLOTS, HEAD_DIM, LANES), F32)
    out, sfin = pl.pallas_call(
        functools.partial(_scan_kernel, tt=tt),
        grid=(g, t // tt),
        in_specs=[pair_tile, pair_tile, pair_tile, next_step,
                  state, const(HALF_ROWS), const(HALF_ROWS), const(HEAD_DIM)],
        out_specs=[row_tile, state],
        out_shape=[jax.ShapeDtypeStruct((g, t, HALF_ROWS, LANES), F32),
                   jax.ShapeDtypeStruct((g, HEAD_DIM, HALF_ROWS, LANES), F32)],
        scratch_shapes=[pltpu.VMEM((HEAD_DIM, HALF_ROWS, LANES), F32),
                        pltpu.VMEM((HALF_ROWS, LANES), F32),
                        pltpu.VMEM((1, LANES), F32),
                        keys, keys, keys, keys, keys,
                        pltpu.VMEM((SCAN_SLOTS, HALF_ROWS, LANES), F32)],
        compiler_params=_cparams(("arbitrary", "arbitrary")),
        name="wkv_scan",
    )(wb, kr, kv, kv, _state_to_scan(s0),
      _head_rows(p['rk_lnx_w']), _head_rows(p['rk_lnx_b']), _head_keys(p['rk_r_k']))
    return _from_scan_rows(out), _state_from_scan(sfin)


def _gelu(x):
    return 0.5 * x * (1.0 + jnp.tanh(0.7978845608028654 * (x + 0.044715 * (x * x * x))))


def _lru_gates(xc, wri_ref, bri_ref, nsl_ref, precise):
    dot = _dot3 if precise else _dot
    gates = _sigmoid(dot(xc, wri_ref[...]) + bri_ref[...])
    gate_r = gates[:, :D_B]
    gate_i = gates[:, D_B:]
    log_a = gate_r * nsl_ref[...]
    a = jnp.exp(log_a)
    th = jnp.tanh(log_a)
    one_minus_a2 = -2.0 * th / (1.0 - th)
    bt = jnp.sqrt(one_minus_a2) * (gate_i * xc)
    return a, bt


def _lru_seq_math(pb, cw_ref, cb_ref, wri_ref, bri_ref, nsl_ref, ng_ref, out_ref, hlast_ref, xprev_ref, h_ref):
    tm = pb.shape[0]
    yb = pb[:, :D_B]
    xb = pb[:, D_B:]
    xprev = xprev_ref[...]
    row8 = lax.broadcasted_iota(jnp.int32, (SUBLANES, D_B), 0)

    def shifted(d):
        rolled = pltpu.roll(xb, d, axis=0)
        top = jnp.where(row8 < d, pltpu.roll(xprev, d, axis=0), rolled[:SUBLANES])
        return jnp.concatenate([top, rolled[SUBLANES:]], axis=0)

    xc = cb_ref[...] + cw_ref[3:4, :] * xb
    for d in range(1, CONV_W):
        xc = xc + cw_ref[3 - d:4 - d, :] * shifted(d)
    xprev_ref[...] = xb[tm - SUBLANES:, :]

    a, x = _lru_gates(xc, wri_ref, bri_ref, nsl_ref, False)
    row = lax.broadcasted_iota(jnp.int32, (tm, D_B), 0)
    d = 1
    while d < tm:
        if d < SUBLANES:
            keep = row >= d
            a_s = jnp.where(keep, pltpu.roll(a, d, axis=0), 1.0)
            x_s = jnp.where(keep, pltpu.roll(x, d, axis=0), 0.0)
            x = a * x_s + x
            a = a * a_s
        else:
            x = jnp.concatenate([x[:d], a[d:] * x[:tm - d] + x[d:]], axis=0)
            a = jnp.concatenate([a[:d], a[d:] * a[:tm - d]], axis=0)
        d *= 2
    h = a * h_ref[...] + x
    h_ref[...] = h[tm - 1:, :]
    hlast_ref[0] = h[tm - 1:, :]
    out_ref[0] = _rms(h * _gelu(yb), ng_ref[...]).astype(out_ref.dtype)


def _lru_params(p):
    eye = jnp.eye(N_BLOCKS_B, dtype=F32)
    bd = lambda w: (eye[:, None, :, None] * w[:, :, None, :]).reshape(D_B, D_B)
    wri = jnp.concatenate([bd(p['lru_w_r']), bd(p['lru_w_i'])], axis=1)
    bri = jnp.concatenate([p['lru_b_r'], p['lru_b_i']]).reshape(1, 2 * D_B)
    nsl = (-LRU_C * jax.nn.softplus(-p['lru_lambda'])).reshape(1, D_B)
    return wri, bri, nsl


def _front_kernel(x_ref, shift_ref, scale_ref, g1_ref, w_ref,
                  mu_ref, w0_ref, wup_ref, a0_ref, gup_ref, kk_ref, ka_ref,
                  cw_ref, cb_ref, wri_ref, bri_ref, nsl_ref, ng_ref,
                  wb_out, kr_out, kv_out, g_out, yb_out, hlast_out, patail_out, xtail_out,
                  carry_ref, stage_ref, xprev_ref, h_ref):
    @pl.when(pl.program_id(1) == 0)
    def _():
        carry_ref[...] = jnp.zeros_like(carry_ref)
        xprev_ref[...] = jnp.zeros_like(xprev_ref)
        h_ref[...] = jnp.zeros_like(h_ref)

    h = (_rms(x_ref[0], g1_ref[...]) * (1.0 + scale_ref[0]) + shift_ref[0]).astype(BF16)
    proj = jnp.dot(h, w_ref[...], preferred_element_type=F32)
    pa = proj[:, :N_COLS_A]
    pb = proj[:, N_COLS_A:]
    tm = pa.shape[0]
    patail_out[0] = pa[tm - SUBLANES:, :]
    xtail_out[0] = pb[tm - SUBLANES:, D_B:]

    rolled = pltpu.roll(pa, 1, axis=0)
    row = lax.broadcasted_iota(jnp.int32, pa.shape, 0)
    prev = jnp.where(row == 0, carry_ref[...], rolled)
    carry_ref[...] = pa[tm - 1:, :]
    _prep_math(pa, prev, mu_ref, w0_ref, wup_ref, a0_ref, gup_ref, kk_ref, ka_ref,
               wb_out, kr_out, kv_out, g_out, stage_ref)
    _lru_seq_math(pb, cw_ref, cb_ref, wri_ref, bri_ref, nsl_ref, ng_ref, yb_out, hlast_out, xprev_ref, h_ref)


def _front(x, shift, scale, mod_map, p, w, tm):
    nb, t, _ = x.shape
    assert nb == GROUP_BATCH
    wri, bri, nsl = _lru_params(p)
    consts = ((p['norm1_g'].reshape(1, D_MODEL), w) + _prep_params(p)
              + (p['lru_conv_w'], p['lru_conv_b'].reshape(1, D_B), wri.astype(BF16), bri, nsl,
                 p['lru_norm_g'].reshape(1, D_B)))
    mod_block = (1,) + shift.shape[1:]
    tile = lambda n: pl.BlockSpec((1, tm, n), lambda b, i: (b, i, 0))
    tail = lambda n: pl.BlockSpec((1, SUBLANES, n), lambda b, i: (b, 0, 0))
    pair_tile = pl.BlockSpec((tm, N_HEADS, LANES), lambda b, i: (i, b, 0))
    pair_shape = jax.ShapeDtypeStruct((t, nb * N_HEADS, LANES), F32)
    return pl.pallas_call(
        _front_kernel,
        grid=(nb, t // tm),
        in_specs=[tile(D_MODEL),
                  pl.BlockSpec(mod_block, functools.partial(mod_map, 0)),
                  pl.BlockSpec(mod_block, functools.partial(mod_map, 1))] + [_const_spec(a) for a in consts],
        out_specs=[pair_tile] * 3 + [tile(D_A), tile(D_B), pl.BlockSpec((1, 1, D_B), lambda b, i: (b, 0, 0)),
                                     tail(N_COLS_A), tail(D_B)],
        out_shape=[pair_shape] * 3 + [jax.ShapeDtypeStruct((nb, t, D_A), F32),
                                      jax.ShapeDtypeStruct((nb, t, D_B), BF16),
                                      jax.ShapeDtypeStruct((nb, 1, D_B), F32),
                                      jax.ShapeDtypeStruct((nb, SUBLANES, N_COLS_A), F32),
                                      jax.ShapeDtypeStruct((nb, SUBLANES, D_B), F32)],
        scratch_shapes=[pltpu.VMEM((1, N_COLS_A), F32), pltpu.VMEM((tm * N_HEADS, LANES), F32),
                        pltpu.VMEM((SUBLANES, D_B), F32), pltpu.VMEM((1, D_B), F32)],
        compiler_params=_cparams(("arbitrary", "arbitrary")),
        name="prompt_front",
    )(x, shift, scale, *consts)


def _lru_step_kernel(pb_ref, conv_ref, h0_ref, cw_ref, cb_ref, wri_ref, bri_ref, nsl_ref, ng_ref,
                     out_ref, hnew_ref):
    pb = pb_ref[...]
    yb = pb[:, :D_B]
    xb = pb[:, D_B:]
    xc = cb_ref[...] + cw_ref[3:4, :] * xb
    for j in range(CONV_W - 1):
        xc = xc + cw_ref[j:j + 1, :] * conv_ref[j]
    a, x = _lru_gates(xc, wri_ref, bri_ref, nsl_ref, True)
    h = a * h0_ref[...] + x
    hnew_ref[...] = h
    out_ref[...] = _rms(h * _gelu(yb), ng_ref[...]).astype(out_ref.dtype)


def _lru_step(pb, conv0, h0, p):
    n = pb.shape[0]
    wri, bri, nsl = _lru_params(p)
    return pl.pallas_call(
        _lru_step_kernel,
        out_shape=[jax.ShapeDtypeStruct((n, D_B), BF16), jax.ShapeDtypeStruct((n, D_B), F32)],
        compiler_params=pltpu.CompilerParams(vmem_limit_bytes=VMEM_LIMIT),
        name="rglru_step",
    )(pb, conv0, h0, p['lru_conv_w'], p['lru_conv_b'].reshape(1, D_B), wri, bri, nsl,
      p['lru_norm_g'].reshape(1, D_B))


def _to_token_tiles(ref, x):
    rows = x.shape[0]
    for c in range(TILE_CHUNKS):
        ref[pl.ds(c, rows, stride=TILE_CHUNKS), :] = x[:, c * LANES:(c + 1) * LANES]


def _from_token_tiles(ref, row0, rows):
    return jnp.concatenate(
        [ref[pl.ds(row0 + c, rows, stride=TILE_CHUNKS), :] for c in range(TILE_CHUNKS)], axis=1)


def _post_kernel(x_ref, wkv_ref, g_ref, yb_ref, gate1_ref, shift2_ref, scale2_ref, n2_ref,
                 wo_ref, rw_ref, rb_ref, tri_ref, x1_ref, h2_ref, ti_ref, rk_ref, tg_ref, cnt_ref):
    ya = (wkv_ref[0] * g_ref[0]).astype(BF16)
    mixed = (jnp.dot(ya, wo_ref[:D_A, :], preferred_element_type=F32)
             + jnp.dot(yb_ref[0], wo_ref[D_A:, :], preferred_element_type=F32))
    x1 = x_ref[0] + gate1_ref[0] * mixed
    x1_ref[0] = x1
    h2 = _rms(x1, n2_ref[...]) * (1.0 + scale2_ref[0]) + shift2_ref[0]
    _to_token_tiles(h2_ref, h2)
    logits = _dot3_nt(rw_ref[...], h2) + rb_ref[...]
    eidx = lax.broadcasted_iota(jnp.int32, logits.shape, 0)
    vals, idxs = [], []
    cur = logits
    for _ in range(TOP_K):
        m = jnp.max(cur, axis=0, keepdims=True)
        i = jnp.min(jnp.where(cur == m, eidx, N_EXPERTS), axis=0, keepdims=True)
        vals.append(m)
        idxs.append(i)
        cur = jnp.where(eidx == i, -jnp.inf, cur)
    ex = [jnp.exp(v - vals[0]) for v in vals]
    den = ex[0] + ex[1] + ex[2] + ex[3]
    sel = [eidx == i for i in idxs]
    onehot = (sel[0] | sel[1] | sel[2] | sel[3]).astype(F32)
    incl = jnp.dot(onehot.astype(BF16), tri_ref[...], preferred_element_type=F32)
    rank = incl - onehot
    cnt_ref[0] = jnp.broadcast_to(jnp.sum(onehot, axis=1, keepdims=True), cnt_ref.shape[1:])
    for k in range(TOP_K):
        ti_ref[0, k:k + 1, :] = idxs[k]
        rk_ref[0, k:k + 1, :] = jnp.sum(jnp.where(sel[k], rank, 0.0), axis=0, keepdims=True).astype(jnp.int32)
        tg_ref[0, k:k + 1, :] = ex[k] / den


def _post(x, wkv, g, yb, gate1, shift2, scale2, mod_map, p, tm):
    nb, t, _ = x.shape
    mod_block = (1,) + gate1.shape[1:]
    tile = lambda n: pl.BlockSpec((1, tm, n), lambda b, i: (b, i, 0))
    full = lambda a: pl.BlockSpec(a.shape, lambda b, i: (0,) * a.ndim)
    mspec = lambda j: pl.BlockSpec(mod_block, functools.partial(mod_map, j))
    n2 = p['norm2_g'].reshape(1, D_MODEL)
    wo = p['w_out'].astype(BF16)
    rw = p['router_w'].T
    rb = p['router_b'].reshape(N_EXPERTS, 1)
    tri = jnp.triu(jnp.ones((tm, tm), BF16))
    nt = t // tm
    n = nb * t
    topk = pl.BlockSpec((1, TOP_K, tm), lambda b, i: (b * nt + i, 0, 0))
    topk_i = jax.ShapeDtypeStruct((nb * nt, TOP_K, tm), jnp.int32)
    x1, h2, ti, rk, tg, cnt = pl.pallas_call(
        _post_kernel,
        grid=(nb, nt),
        in_specs=[tile(D_MODEL), tile(D_A), tile(D_A), tile(D_B), mspec(2), mspec(3), mspec(4),
                  full(n2), full(wo), full(rw), full(rb), full(tri)],
        out_specs=[tile(D_MODEL),
                   pl.BlockSpec((tm * TILE_CHUNKS, LANES), lambda b, i: (b * nt + i, 0)),
                   topk, topk, topk,
                   pl.BlockSpec((1, N_EXPERTS, LANES), lambda b, i: (b * nt + i, 0, 0))],
        out_shape=[jax.ShapeDtypeStruct((nb, t, D_MODEL), F32),
                   jax.ShapeDtypeStruct((n * TILE_CHUNKS, LANES), F32),
                   topk_i, topk_i,
                   jax.ShapeDtypeStruct((nb * nt, TOP_K, tm), F32),
                   jax.ShapeDtypeStruct((nb * nt, N_EXPERTS, LANES), F32)],
        compiler_params=_cparams(("arbitrary", "arbitrary")),
        name="outproj_router",
    )(x, wkv, g, yb, gate1, shift2, scale2, n2, wo, rw, rb, tri)
    return x1, h2, ti, rk, tg, cnt[:, :, 0].astype(jnp.int32)


def _plan(routes):
    cnt = jnp.concatenate([r[2] for r in routes], axis=0)
    total = jnp.sum(cnt, axis=0)
    run_start = jnp.cumsum(cnt, axis=0) - cnt
    padded = (total + MOE_ROWS - 1) // MOE_ROWS * MOE_ROWS
    pad_end = jnp.cumsum(padded)
    pad_start = pad_end - padded
    loc_off = jnp.cumsum(cnt, axis=1) - cnt
    dst0 = pad_start[None, :] + run_start
    n_tokens = sum(r[0].shape[0] * r[0].shape[2] for r in routes)
    n_blocks = -(-n_tokens * TOP_K // MOE_ROWS) + N_EXPERTS
    block_row0 = jnp.arange(n_blocks, dtype=jnp.int32) * MOE_ROWS
    block_e = jnp.minimum(jnp.sum(pad_end[None, :] <= block_row0[:, None], axis=1), N_EXPERTS - 1)
    n_used = (pad_end[-1] // MOE_ROWS).reshape(1)
    first_total = jnp.sum(routes[0][2], axis=0)
    zero_table = jnp.stack([pad_end - (pad_start + first_total), jnp.zeros_like(pad_end), pad_start + first_total])
    as_i32 = lambda a: a.astype(jnp.int32)
    experts = jnp.arange(N_EXPERTS, dtype=jnp.int32)
    groups, t0 = [], 0
    for ti, rk, c in routes:
        nt = ti.shape[0]
        sl = slice(t0, t0 + nt)
        off = jnp.sum(jnp.where(ti[..., None] == experts, loc_off[sl, None, None, :], 0), axis=-1)
        lpos = as_i32((rk + off) * TILE_CHUNKS).reshape(nt, -1)
        table = as_i32(jnp.stack([cnt[sl], loc_off[sl], dst0[sl]], axis=0).reshape(3, -1))
        groups.append((lpos, table))
        t0 += nt
    return groups, as_i32(block_e), as_i32(n_used), as_i32(pad_end), as_i32(zero_table), as_i32(padded), n_blocks


def _tile_rows(row):
    return pl.ds(pl.multiple_of(row * TILE_CHUNKS, TILE_CHUNKS), TILE_CHUNKS)


def _sublane_rows(row8):
    return pl.ds(pl.multiple_of(row8, TILE_CHUNKS), TILE_CHUNKS)


def _piece_rows(row0, n_rows):
    return pl.ds(pl.multiple_of(row0 * TILE_CHUNKS, TILE_CHUNKS), n_rows * TILE_CHUNKS)


def _block_rows(block):
    rows = MOE_ROWS * TILE_CHUNKS
    return pl.ds(pl.multiple_of(block * rows, rows), rows)


def _for_each_piece(table_ref, tile, n_tiles, fn):
    def per_expert(e, carry):
        col = tile * N_EXPERTS + e
        cnt = table_ref[col]
        lo = table_ref[n_tiles * N_EXPERTS + col]
        d0 = table_ref[2 * n_tiles * N_EXPERTS + col]

        def full(c, c2):
            fn(lo + c * MOE_CHUNK, d0 + c * MOE_CHUNK, MOE_CHUNK)
            return c2
        lax.fori_loop(0, cnt // MOE_CHUNK, full, 0)
        bit = MOE_CHUNK // 2
        while bit >= 1:
            done = cnt - cnt % (2 * bit)

            @pl.when((cnt & bit) != 0)
            def _(done=done, bit=bit):
                fn(lo + done, d0 + done, bit)
            bit //= 2
        return carry
    lax.fori_loop(0, N_EXPERTS, per_expert, 0)


def _scatter_kernel(*refs, first):
    idx_ref, loc_ref, zero_ref, sem, isem, zsem = refs[-6:]
    table_ref, pad_end_ref, ztab_ref, h2_ref, lpos_hbm = refs[:5]
    xs_hbm = refs[-7]
    i = pl.program_id(0)
    n_tiles = pl.num_programs(0)
    tm = h2_ref.shape[0] // TILE_CHUNKS
    slot = i % 2

    def idx_copy(tile, s):
        dst = idx_ref.at[pl.ds(s * (TOP_K * tm), TOP_K * tm)]
        return pltpu.make_async_copy(lpos_hbm.at[tile], dst, isem.at[s])

    def drain(s):
        for _ in range(TOP_K):
            pltpu.make_async_copy(h2_ref, xs_hbm.at[pl.ds(0, tm * TILE_CHUNKS), :], sem.at[s]).wait()

    @pl.when(i == 0)
    def _():
        idx_copy(0, 0).start()

    if first:
        @pl.when(i == 0)
        def _():
            zero_ref[...] = jnp.zeros_like(zero_ref)
            n_used = pad_end_ref[N_EXPERTS - 1] // MOE_ROWS
            n_blocks = xs_hbm.shape[0] // (MOE_ROWS * TILE_CHUNKS)

            def zero_piece(lo, d0, n):
                return pltpu.make_async_copy(zero_ref.at[_piece_rows(0, n), :], xs_hbm.at[_piece_rows(d0, n), :], zsem)
            _for_each_piece(ztab_ref, 0, 1, lambda lo, d0, n: zero_piece(lo, d0, n).start())

            def fill_tail(blk, c):
                pltpu.make_async_copy(zero_ref, xs_hbm.at[_block_rows(blk), :], zsem).start()
                return c
            lax.fori_loop(n_used, n_blocks, fill_tail, 0)

            def drain_fill(j, c):
                pltpu.make_async_copy(zero_ref, xs_hbm.at[_block_rows(0), :], zsem).wait()
                return c
            lax.fori_loop(0, n_blocks - n_used, drain_fill, 0)
            _for_each_piece(ztab_ref, 0, 1, lambda lo, d0, n: zero_piece(lo, d0, n).wait())

    def step(s):
        @pl.when(i + 1 < n_tiles)
        def _():
            idx_copy(i + 1, 1 - s).start()

        @pl.when(i >= 2)
        def _():
            drain(s)
        idx_copy(i, s).wait()
        base = s * (TOP_K * tm)

        def place(r, c):
            row = h2_ref[_tile_rows(r), :]
            for k in range(TOP_K):
                loc_ref[s, _sublane_rows(idx_ref[base + k * tm + r]), :] = row
            return c
        lax.fori_loop(0, tm, place, 0, unroll=8)

        def send(lo, d0, n):
            pltpu.make_async_copy(loc_ref.at[s, _piece_rows(lo, n), :], xs_hbm.at[_piece_rows(d0, n), :],
                                  sem.at[s]).start()
        _for_each_piece(table_ref, i, n_tiles, send)

        @pl.when(i == n_tiles - 1)
        def _():
            @pl.when(i >= 1)
            def _():
                drain(1 - s)
            drain(s)

    for s in range(2):
        pl.when(slot == s)(functools.partial(step, s))


def _scatter(h2, lpos, table, pad_end, zero_table, n_rows, tm, xs=None):
    nt = lpos.shape[0]
    first = xs is None
    loc_rows = TOP_K * tm * TILE_CHUNKS
    in_specs = [pl.BlockSpec((tm * TILE_CHUNKS, LANES), lambda i, *_: (i, 0)),
                pl.BlockSpec(memory_space=pl.ANY)]
    args = [table.reshape(-1), pad_end, zero_table.reshape(-1), h2, lpos]
    if not first:
        in_specs.append(pl.BlockSpec(memory_space=pl.ANY))
        args.append(xs)
    grid_spec = pltpu.PrefetchScalarGridSpec(
        num_scalar_prefetch=3,
        grid=(nt,),
        in_specs=in_specs,
        out_specs=pl.BlockSpec(memory_space=pl.ANY),
        scratch_shapes=[pltpu.SMEM((2 * TOP_K * tm,), jnp.int32),
                        pltpu.VMEM((2, loc_rows, LANES), F32),
                        pltpu.VMEM((MOE_ROWS * TILE_CHUNKS, LANES), F32),
                        pltpu.SemaphoreType.DMA((2,)), pltpu.SemaphoreType.DMA((2,)), pltpu.SemaphoreType.DMA],
    )
    return pl.pallas_call(
        functools.partial(_scatter_kernel, first=first),
        grid_spec=grid_spec,
        out_shape=jax.ShapeDtypeStruct((n_rows * TILE_CHUNKS, LANES), F32),
        input_output_aliases={} if first else {5: 0},
        compiler_params=_cparams(("arbitrary",)),
        name="moe_dispatch",
    )(*args)


def _mlp_kernel(be_ref, nu_ref, next_ref, par_ref, xs_ref, bias_ref, wg_hbm, wu_hbm, wd_hbm, out_ref,
                w32_ref, w16_ref, sem):
    i = pl.program_id(0)
    used = i < nu_ref[0]
    e = be_ref[i]

    def weight_copies(expert, s):
        return [pltpu.make_async_copy(w_hbm.at[expert], w32_ref.at[s, j], sem.at[s])
                for j, w_hbm in enumerate((wg_hbm, wu_hbm, wd_hbm))]

    def load_expert(s):
        @pl.when(i == 0)
        def _():
            for c in weight_copies(e, s):
                c.start()
        for c in weight_copies(e, s):
            c.wait()
        nxt = next_ref[e]

        @pl.when(nxt >= 0)
        def _():
            for c in weight_copies(nxt, 1 - s):
                c.start()
        for j in range(3):
            w16_ref[j] = w32_ref[s, j].astype(BF16)

    first_of_expert = used & ((i == 0) | (e != be_ref[jnp.maximum(i - 1, 0)]))
    for s in range(2):
        pl.when(first_of_expert & (par_ref[e] == s))(functools.partial(load_expert, s))

    @pl.when(used)
    def _():
        x = _from_token_tiles(xs_ref, 0, MOE_ROWS).astype(BF16)
        gt = jnp.dot(x, w16_ref[0], preferred_element_type=F32) + bias_ref[0, 0:1, :]
        up = jnp.dot(x, w16_ref[1], preferred_element_type=F32) + bias_ref[0, 1:2, :]
        gt = jnp.minimum(gt, SWIGLU_LIMIT)
        up = jnp.clip(up, -SWIGLU_LIMIT, SWIGLU_LIMIT)
        glu = gt * _sigmoid(gt * SWIGLU_ALPHA)
        mid = ((up + 1.0) * glu).astype(BF16)
        _to_token_tiles(out_ref, jnp.dot(mid, w16_ref[2], preferred_element_type=F32) + bias_ref[0, 2:3, :])

    @pl.when(jnp.logical_not(used))
    def _():
        out_ref[...] = jnp.zeros_like(out_ref)


def _mlp(xs, block_e, n_used, padded, n_blocks, wts):
    wg, bg, wu, bu, wd, bd = wts
    assert D_FF == D_MODEL
    bias = jnp.stack([bg, bu, bd], axis=1)
    nonempty = padded > 0
    experts = jnp.arange(N_EXPERTS, dtype=jnp.int32)
    parity = ((jnp.cumsum(nonempty) - nonempty) % 2).astype(jnp.int32)
    later = nonempty[None, :] & (experts[None, :] > experts[:, None])
    nxt = jnp.min(jnp.where(later, experts[None, :], N_EXPERTS), axis=1)
    nxt = jnp.where(nxt == N_EXPERTS, -1, nxt).astype(jnp.int32)
    rows = pl.BlockSpec((MOE_ROWS * TILE_CHUNKS, LANES), lambda i, *_: (i, 0))
    hbm = pl.BlockSpec(memory_space=pl.ANY)
    grid_spec = pltpu.PrefetchScalarGridSpec(
        num_scalar_prefetch=4,
        grid=(n_blocks,),
        in_specs=[rows, pl.BlockSpec((1, 3, D_FF), lambda i, be, *_: (be[i], 0, 0)), hbm, hbm, hbm],
        out_specs=rows,
        scratch_shapes=[pltpu.VMEM((2, 3, D_MODEL, D_FF), F32),
                        pltpu.VMEM((3, D_MODEL, D_FF), BF16),
                        pltpu.SemaphoreType.DMA((2,))],
    )
    return pl.pallas_call(
        _mlp_kernel,
        grid_spec=grid_spec,
        out_shape=jax.ShapeDtypeStruct(xs.shape, F32),
        compiler_params=_cparams(("arbitrary",)),
        name="moe_experts",
    )(block_e, n_used, nxt, parity, xs, bias, wg, wu, wd)


def _gather_kernel(table_ref, x1_ref, gate2_ref, fg_ref, lpos_hbm, tg_hbm, rows_hbm, y_ref,
                   idx_ref, gsm_ref, loc_ref, ff_ref, sem, isem):
    i = pl.program_id(1) + pl.program_id(0) * pl.num_programs(1)
    n_tiles = pl.num_programs(0) * pl.num_programs(1)
    tm = x1_ref.shape[1]
    slot = i % 2

    def meta_copies(tile, s):
        seg = pl.ds(s * (TOP_K * tm), TOP_K * tm)
        return (pltpu.make_async_copy(lpos_hbm.at[tile], idx_ref.at[seg], isem.at[s]),
                pltpu.make_async_copy(tg_hbm.at[tile], gsm_ref.at[seg], isem.at[s]))

    def fetch_tile(tile, s):
        for c in meta_copies(tile, s):
            c.start()

        def fetch(lo, d0, n):
            pltpu.make_async_copy(rows_hbm.at[_piece_rows(d0, n), :], loc_ref.at[s, _piece_rows(lo, n), :],
                                  sem.at[s]).start()
        _for_each_piece(table_ref, tile, n_tiles, fetch)

    @pl.when(i == 0)
    def _():
        fetch_tile(0, 0)

    def step(s):
        @pl.when(i + 1 < n_tiles)
        def _():
            fetch_tile(i + 1, 1 - s)

        pltpu.make_async_copy(rows_hbm.at[pl.ds(0, TOP_K * tm * TILE_CHUNKS), :], loc_ref.at[s], sem.at[s]).wait()
        for c in meta_copies(i, s):
            c.wait()
        base = s * (TOP_K * tm)

        def mix(r, c):
            acc = gsm_ref[base + r] * loc_ref[s, _sublane_rows(idx_ref[base + r]), :]
            for k in range(1, TOP_K):
                acc = acc + gsm_ref[base + k * tm + r] * loc_ref[s, _sublane_rows(idx_ref[base + k * tm + r]), :]
            ff_ref[_tile_rows(r), :] = acc
            return c
        lax.fori_loop(0, tm, mix, 0, unroll=8)

    for s in range(2):
        pl.when(slot == s)(functools.partial(step, s))

    x2 = x1_ref[0] + gate2_ref[0] * _from_token_tiles(ff_ref, 0, tm)
    y_ref[0] = _rms(x2, fg_ref[...])


def _gather(x1, tg, gate2, mod_map, final_g, lpos, table, rows, tm):
    nb, t, _ = x1.shape
    nt = t // tm
    mod_block = (1,) + gate2.shape[1:]
    loc_rows = TOP_K * tm * TILE_CHUNKS
    grid_spec = pltpu.PrefetchScalarGridSpec(
        num_scalar_prefetch=1,
        grid=(nb, nt),
        in_specs=[pl.BlockSpec((1, tm, D_MODEL), lambda b, i, *_: (b, i, 0)),
                  pl.BlockSpec(mod_block, lambda b, i, *_: mod_map(5, b, i)),
                  pl.BlockSpec((1, D_MODEL), lambda b, i, *_: (0, 0)),
                  pl.BlockSpec(memory_space=pl.ANY),
                  pl.BlockSpec(memory_space=pl.ANY),
                  pl.BlockSpec(memory_space=pl.ANY)],
        out_specs=pl.BlockSpec((1, tm, D_MODEL), lambda b, i, *_: (b, i, 0)),
        scratch_shapes=[pltpu.SMEM((2 * TOP_K * tm,), jnp.int32),
                        pltpu.SMEM((2 * TOP_K * tm,), F32),
                        pltpu.VMEM((2, loc_rows, LANES), F32),
                        pltpu.VMEM((tm * TILE_CHUNKS, LANES), F32),
                        pltpu.SemaphoreType.DMA((2,)), pltpu.SemaphoreType.DMA((2,))],
    )
    return pl.pallas_call(
        _gather_kernel,
        grid_spec=grid_spec,
        out_shape=jax.ShapeDtypeStruct((nb, t, D_MODEL), F32),
        compiler_params=_cparams(("arbitrary", "arbitrary")),
        name="moe_combine",
    )(table.reshape(-1), x1, gate2, final_g.reshape(1, D_MODEL), lpos, tg.reshape(tg.shape[0], -1), rows)


def _moe(route_p, route_s, mod_p, mod_s, final_g, wts):
    routes = [route_p, route_s]
    groups, block_e, n_used, pad_end, zero_table, padded, n_blocks = _plan([(r[2], r[3], r[5]) for r in routes])
    xs = None
    for (x1, h2, ti, rk, tg, cnt), (lpos, table) in zip(routes, groups):
        xs = _scatter(h2, lpos, table, pad_end, zero_table, n_blocks * MOE_ROWS, ti.shape[2], xs)
    rows = _mlp(xs, block_e, n_used, padded, n_blocks, wts)
    outs = []
    for (x1, h2, ti, rk, tg, cnt), (lpos, table), (mod, mod_map) in zip(routes, groups, (mod_p, mod_s)):
        outs.append(_gather(x1, tg, mod, mod_map, final_g, lpos, table, rows, ti.shape[2]))
    return outs


def _forward(x_prompt, x_sample, c_prompt, c_sample, state_wkv, state_shift, state_conv, state_lru, p, final_g):
    bp, tp, _ = x_prompt.shape
    bs = x_sample.shape[0]
    tm = min(512, tp)
    tt = min(128, tp)

    mod = _ada(jnp.concatenate([c_prompt, c_sample], axis=0), p['w_ada'], p['b_ada'])
    mod_p = mod[:bp].reshape(bp * N_MOD, 1, D_MODEL)
    mod_s = mod[bp:].reshape(bs, N_MOD, D_MODEL).transpose(1, 0, 2)
    map_p = lambda j, b, i: (b * N_MOD + j, 0, 0)
    map_s = lambda j, b, i: (j, 0, 0)

    wts = (p['w_gate'], p['b_gate'], p['w_up'], p['b_up'], p['w_down'], p['b_down'])

    wb_p, kr_p, kv_p, g, yb, lru_p, pa_tail, x_tail = _front(x_prompt, mod_p, mod_p, map_p, p,
                                                             p['w_in'].astype(BF16), tm)
    s0 = jnp.zeros((bp, N_HEADS, HEAD_DIM, HEAD_DIM), F32)
    wkv_out, wkv_p = _wkv_scan(wb_p[None], kr_p[None], kv_p[None], s0, p, tt)
    route_p = _post(x_prompt, wkv_out, g, yb, mod_p, mod_p, mod_p, map_p, p, tm)
    shift_p = pa_tail[:, -1, :]
    conv_p = x_tail[:, SUBLANES - (CONV_W - 1):, :]

    xs = x_sample.reshape(1, bs, D_MODEL)
    pa_s, pb_s = _inproj(xs, mod_s, mod_s, map_s, p['norm1_g'], p['w_in'], bs)
    wb_s, kr_s, kv_s, g = _prep(pa_s, state_shift.reshape(1, bs, N_COLS_A), p, bs)
    as_seq = lambda a: _pairs_to_groups(a.reshape(bs, N_HEADS, LANES), 1)
    wkv_out, wkv_s = _wkv_scan(as_seq(wb_s), as_seq(kr_s), as_seq(kv_s), state_wkv, p, 1)
    conv0 = state_conv.transpose(1, 0, 2)
    yb, lru_s = _lru_step(pb_s[0], conv0, state_lru, p)
    route_s = _post(xs, wkv_out.reshape(1, bs, D_A), g, yb.reshape(1, bs, D_B),
                    mod_s, mod_s, mod_s, map_s, p, bs)
    y_prompt, y_sample = _moe(route_p, route_s, (mod_p, map_p), (mod_s, map_s), final_g, wts)
    shift_s = pa_s[0]
    conv_s = jnp.concatenate([state_conv[:, 1:], pb_s[0][:, None, D_B:]], axis=1)

    return (y_prompt, y_sample.reshape(bs, 1, D_MODEL),
            wkv_p[None], shift_p[None], conv_p[None], lru_p.reshape(bp, D_B)[None],
            wkv_s[None], shift_s[None], conv_s[None], lru_s[None])


def kernel(x_prompt, x_sample, c_prompt, c_sample, state_wkv, state_shift, state_conv, state_lru, w_ada, b_ada, norm1_g, norm2_g, w_in, rk_mu, rk_w0, rk_w_up, rk_a0, rk_a_up, rk_g_up, rk_k_k, rk_k_a, rk_r_k, rk_lnx_w, rk_lnx_b, lru_conv_w, lru_conv_b, lru_w_r, lru_b_r, lru_w_i, lru_b_i, lru_lambda, lru_norm_g, w_out, router_w, router_b, w_gate, b_gate, w_up, b_up, w_down, b_down, final_g):
    assert w_ada.shape[0] == 1, "single-layer trunk"
    p = dict(w_ada=w_ada[0], b_ada=b_ada[0], norm1_g=norm1_g[0], norm2_g=norm2_g[0], w_in=w_in[0],
             rk_mu=rk_mu[0], rk_w0=rk_w0[0], rk_w_up=rk_w_up[0], rk_a0=rk_a0[0], rk_a_up=rk_a_up[0],
             rk_g_up=rk_g_up[0], rk_k_k=rk_k_k[0], rk_k_a=rk_k_a[0], rk_r_k=rk_r_k[0],
             rk_lnx_w=rk_lnx_w[0], rk_lnx_b=rk_lnx_b[0], lru_conv_w=lru_conv_w[0],
             lru_conv_b=lru_conv_b[0], lru_w_r=lru_w_r[0], lru_b_r=lru_b_r[0], lru_w_i=lru_w_i[0],
             lru_b_i=lru_b_i[0], lru_lambda=lru_lambda[0], lru_norm_g=lru_norm_g[0], w_out=w_out[0],
             router_w=router_w[0], router_b=router_b[0], w_gate=w_gate[0], b_gate=b_gate[0],
             w_up=w_up[0], b_up=b_up[0], w_down=w_down[0], b_down=b_down[0])
    return _forward(x_prompt, x_sample, c_prompt, c_sample, state_wkv[0], state_shift[0], state_conv[0],
                    state_lru[0], p, final_g)
```

```python
import functools

import jax
import jax.numpy as jnp
from jax import lax
from jax.experimental import pallas as pl
from jax.experimental.pallas import tpu as pltpu

F32 = jnp.float32
BF16 = jnp.bfloat16

D_MODEL = 1024
D_A = 512
HEAD_DIM = 64
N_HEADS = 8
D_B = 512
N_BLOCKS_B = 8
CONV_W = 4
LRU_C = 8.0
R_DECAY = 64
R_AAA = 64
R_GATE = 128
N_COLS_A = 3 * D_A + R_DECAY + R_AAA + R_GATE
N_COLS_B = 2 * D_B
N_EXPERTS = 32
TOP_K = 4
D_FF = 1024
SWIGLU_LIMIT = 7.0
SWIGLU_ALPHA = 1.702
RMS_EPS = 1e-6
LN_X_EPS = 64e-5
N_MOD = 6

LANES = 128
SUBLANES = 8
GROUP_BATCH = 8
HALF_ROWS = HEAD_DIM // 2
MOE_ROWS = 256
MOE_CHUNK = 16
TILE_CHUNKS = D_MODEL // LANES
VMEM_LIMIT = 56 * 1024 * 1024


def _cparams(sem):
    return pltpu.CompilerParams(dimension_semantics=sem, vmem_limit_bytes=VMEM_LIMIT)


def _dot(a, b):
    return jnp.dot(a.astype(BF16), b.astype(BF16), preferred_element_type=F32)


def _split(a):
    hi = a.astype(BF16)
    lo = (a - hi.astype(F32)).astype(BF16)
    return hi, lo


def _dot3(a, b):
    ah, al = _split(a)
    bh, bl = _split(b)
    return (jnp.dot(ah, bh, preferred_element_type=F32)
            + (jnp.dot(al, bh, preferred_element_type=F32) + jnp.dot(ah, bl, preferred_element_type=F32)))


def _dot3_nt(a, b):
    dn = (((1,), (1,)), ((), ()))
    ah, al = _split(a)
    bh, bl = _split(b)
    d = lambda x, y: lax.dot_general(x, y, dn, preferred_element_type=F32)
    return d(ah, bh) + (d(al, bh) + d(ah, bl))


def _softplus(x):
    return jnp.maximum(x, 0.0) + jnp.log1p(jnp.exp(-jnp.abs(x)))


def _sigmoid(x):
    return 1.0 / (1.0 + jnp.exp(-x))


def _rms(x, g):
    ms = jnp.mean(x * x, axis=-1, keepdims=True)
    return x * lax.rsqrt(ms + RMS_EPS) * g


def _ada_kernel(c_ref, w_ref, b_ref, o_ref):
    c = c_ref[...]
    s = c * _sigmoid(c)
    o_ref[...] = _dot3(s, w_ref[...]) + b_ref[...]


def _ada(c, w_ada, b_ada):
    rows = c.shape[0]
    ncol = w_ada.shape[1]
    tn = D_MODEL
    return pl.pallas_call(
        _ada_kernel,
        grid=(ncol // tn,),
        in_specs=[pl.BlockSpec((rows, D_MODEL), lambda j: (0, 0)),
                  pl.BlockSpec((D_MODEL, tn), lambda j: (0, j)),
                  pl.BlockSpec((1, tn), lambda j: (0, j))],
        out_specs=pl.BlockSpec((rows, tn), lambda j: (0, j)),
        out_shape=jax.ShapeDtypeStruct((rows, ncol), F32),
        compiler_params=_cparams(("arbitrary",)),
        name="ada_mod",
    )(c, w_ada, b_ada.reshape(1, ncol))


def _inproj_kernel(x_ref, shift_ref, scale_ref, g_ref, w_ref, pa_ref, pb_ref):
    x = x_ref[0]
    h = _rms(x, g_ref[...]) * (1.0 + scale_ref[0]) + shift_ref[0]
    proj = _dot3(h, w_ref[...])
    pa_ref[0] = proj[:, :N_COLS_A]
    pb_ref[0] = proj[:, N_COLS_A:]


def _inproj(x, shift, scale, mod_map, g, w, tm):
    nb, t, _ = x.shape
    mod_block = (1,) + shift.shape[1:]
    return pl.pallas_call(
        _inproj_kernel,
        grid=(nb, t // tm),
        in_specs=[pl.BlockSpec((1, tm, D_MODEL), lambda b, i: (b, i, 0)),
                  pl.BlockSpec(mod_block, functools.partial(mod_map, 0)),
                  pl.BlockSpec(mod_block, functools.partial(mod_map, 1)),
                  pl.BlockSpec((1, D_MODEL), lambda b, i: (0, 0)),
                  pl.BlockSpec((D_MODEL, N_COLS_A + N_COLS_B), lambda b, i: (0, 0))],
        out_specs=[pl.BlockSpec((1, tm, N_COLS_A), lambda b, i: (b, i, 0)),
                   pl.BlockSpec((1, tm, N_COLS_B), lambda b, i: (b, i, 0))],
        out_shape=[jax.ShapeDtypeStruct((nb, t, N_COLS_A), F32),
                   jax.ShapeDtypeStruct((nb, t, N_COLS_B), F32)],
        compiler_params=_cparams(("arbitrary", "arbitrary")),
        name="norm1_inproj",
    )(x, shift, scale, g.reshape(1, D_MODEL), w)


def _store_head_pairs(ref, stage_ref, x, y):
    rows = x.shape[0]
    flat = ref.shape[0] == 1
    for h in range(N_HEADS):
        sl = slice(h * HEAD_DIM, (h + 1) * HEAD_DIM)
        pair = jnp.concatenate([x[:, sl], y[:, sl]], axis=1)
        if flat:
            ref[0, pl.ds(h, rows, stride=N_HEADS), :] = pair
        else:
            stage_ref[pl.ds(h, rows, stride=N_HEADS), :] = pair
    if not flat:
        ref[...] = stage_ref[...].reshape(rows, N_HEADS, LANES)


def _prep_math(pa, prev, mu_ref, w0_ref, wup_ref, a0_ref, gup_ref, kk_ref, ka_ref,
               wb_out, kr_out, kv_out, g_out, stage_ref):
    z = pa + (prev - pa) * mu_ref[...]
    r = z[:, 0:D_A]
    k = z[:, D_A:2 * D_A]
    v = z[:, 2 * D_A:3 * D_A]
    lo = 3 * D_A
    za = z[:, lo:lo + R_DECAY + R_AAA]
    lane = lax.broadcasted_iota(jnp.int32, za.shape, 1)
    za = jnp.where(lane < R_DECAY, jnp.tanh(za), za)
    lw = _dot3(za, wup_ref[...])
    w_log = -_softplus(-(w0_ref[...] + lw[:, :D_A])) - 0.5
    decay = jnp.exp(-jnp.exp(w_log))
    a = _sigmoid(a0_ref[...] + lw[:, D_A:])
    gd = z[:, lo + R_DECAY + R_AAA:]
    g = _dot3(_sigmoid(gd), gup_ref[...])
    kk = k * kk_ref[...]
    _store_head_pairs(wb_out, stage_ref, decay, kk * a)
    _store_head_pairs(kr_out, stage_ref, k * (1.0 + (a - 1.0) * ka_ref[...]), r)
    _store_head_pairs(kv_out, stage_ref, kk, v)
    g_out[0] = g


def _prep_kernel(pa_ref, prev_ref, *refs):
    _prep_math(pa_ref[0], prev_ref[0], *refs, None)


def _prep_params(p):
    wup = jnp.zeros((R_DECAY + R_AAA, 2 * D_A), F32)
    wup = wup.at[:R_DECAY, :D_A].set(p['rk_w_up']).at[R_DECAY:, D_A:].set(p['rk_a_up'])
    vec = lambda a: a.reshape(1, -1)
    return (vec(p['rk_mu']), vec(p['rk_w0']), wup, vec(p['rk_a0']), p['rk_g_up'],
            vec(p['rk_k_k']), vec(p['rk_k_a']))


def _const_spec(a):
    return pl.BlockSpec(a.shape, lambda b, i: (0,) * a.ndim)


def _prep(pa, prev, p, tm):
    nb, t, _ = pa.shape
    params = _prep_params(p)
    rows = pl.BlockSpec((1, tm, N_COLS_A), lambda b, i: (b, i, 0))
    return pl.pallas_call(
        _prep_kernel,
        grid=(nb, t // tm),
        in_specs=[rows, rows] + [_const_spec(a) for a in params],
        out_specs=[pl.BlockSpec((1, tm * N_HEADS, LANES), lambda b, i: (b, i, 0))] * 3
                  + [pl.BlockSpec((1, tm, D_A), lambda b, i: (b, i, 0))],
        out_shape=[jax.ShapeDtypeStruct((nb, t * N_HEADS, LANES), F32)] * 3
                  + [jax.ShapeDtypeStruct((nb, t, D_A), F32)],
        compiler_params=_cparams(("arbitrary", "arbitrary")),
        name="rwkv_prep",
    )(pa, prev, *params)


SCAN_SLOTS = 4


def _scan_kernel(wb_ref, kr_ref, kv_ref, kvnext_ref, s0_ref, lnw_ref, lnb_ref, rk_ref,
                 out_ref, sfin_ref, s_ref, sa_ref, inv_ref, wd_ref, kkd_ref, bd_ref, kd_ref, rd_ref, vd_ref, *, tt):
    ti = pl.program_id(1)
    upper_half = lax.broadcasted_iota(jnp.int32, (HALF_ROWS, LANES), 1) >= LANES // 2
    unroll = SCAN_SLOTS if tt % SCAN_SLOTS == 0 else 1

    def expand(pair):
        return jnp.concatenate([pair, pair], axis=0).T

    def prepare(slot, s, kk_pair=None):
        t1 = expand(wb_ref[0, s])
        wd_ref[slot] = t1[:HEAD_DIM]
        bd_ref[slot] = t1[HEAD_DIM:]
        t2 = expand(kr_ref[0, s])
        kd_ref[slot] = t2[:HEAD_DIM]
        rd_ref[slot] = t2[HEAD_DIM:]
        t3 = expand(kv_ref[0, s])
        vd_ref[slot] = jnp.where(upper_half, t3[HEAD_DIM + HALF_ROWS:], t3[HEAD_DIM:HEAD_DIM + HALF_ROWS])
        kkd_ref[slot] = t3[:HEAD_DIM] if kk_pair is None else expand(kk_pair)[:HEAD_DIM]

    prepare(0, 0)
    if tt > 1:
        prepare(1, 1)
    else:
        kkd_ref[1] = expand(kvnext_ref[0, 0])[:HEAD_DIM]

    def inv_norm2(kk_rows):
        s2 = jnp.sum(kk_rows * kk_rows, axis=0, keepdims=True)
        return 1.0 / jnp.maximum(s2, 1e-24)

    @pl.when(ti == 0)
    def _():
        s_ref[...] = s0_ref[0]
        acc = jnp.zeros((HALF_ROWS, LANES), F32)
        for j in range(HEAD_DIM):
            acc = acc + s0_ref[0, j] * kkd_ref[0, j:j + 1, :]
        sa_ref[...] = acc
        inv_ref[...] = inv_norm2(kkd_ref[0])

    def finish(y, cv):
        tot = jnp.sum(y, axis=0, keepdims=True)
        tot = tot + pltpu.roll(tot, LANES // 2, axis=1)
        d = y - tot * (1.0 / HEAD_DIM)
        sq = jnp.sum(d * d, axis=0, keepdims=True)
        sq = sq + pltpu.roll(sq, LANES // 2, axis=1)
        yn = d * lax.rsqrt(sq * (1.0 / HEAD_DIM) + LN_X_EPS)
        return yn * lnw_ref[...] + lnb_ref[...] + cv

    def step(t, u, carry):
        sa, inv2, y_prev, cv_prev = carry
        nxt_slot = (u + 1) % SCAN_SLOTS
        out_ref[0, jnp.maximum(t - 1, 0)] = finish(y_prev, cv_prev)
        sae = sa * (-inv2)
        anchor = lax.shift_right_logical(lax.shift_right_logical(lax.bitcast_convert_type(sae, jnp.uint32),
                                                                 jnp.uint32(16)), jnp.uint32(16))
        v = vd_ref[u] + lax.bitcast_convert_type(anchor, F32)
        acc_y = jnp.zeros((HALF_ROWS, LANES), F32)
        acc_s = jnp.zeros((HALF_ROWS, LANES), F32)
        for j in range(HEAD_DIM):
            row = pl.ds(j, 1)
            s_new = s_ref[j] * wd_ref[u, row, :] + sae * bd_ref[u, row, :] + v * kd_ref[u, row, :]
            s_ref[j] = s_new
            acc_y = acc_y + s_new * rd_ref[u, row, :]
            acc_s = acc_s + s_new * kkd_ref[nxt_slot, row, :]
        c = jnp.sum(rd_ref[u] * kd_ref[u] * rk_ref[...], axis=0, keepdims=True)
        if unroll > 1:
            ahead = jnp.minimum(t + 2, tt - 1)
            kk_pair = None
            if (u + 2) % SCAN_SLOTS == 0:
                kk_pair = jnp.where(t + 2 == tt, kvnext_ref[0, 0], kv_ref[0, ahead])
            prepare((u + 2) % SCAN_SLOTS, ahead, kk_pair)
        return acc_s, inv_norm2(kkd_ref[nxt_slot]), acc_y, c * v

    def steps(q, carry):
        for u in range(unroll):
            carry = step(q * unroll + u, u, carry)
        return carry

    zeros = jnp.zeros((HALF_ROWS, LANES), F32)
    sa, inv2, y_last, cv_last = lax.fori_loop(0, tt // unroll, steps, (sa_ref[...], inv_ref[...], zeros, zeros))
    out_ref[0, tt - 1] = finish(y_last, cv_last)
    sa_ref[...] = sa
    inv_ref[...] = inv2

    @pl.when(ti == pl.num_programs(1) - 1)
    def _():
        sfin_ref[0] = s_ref[...]


def _from_scan_rows(y):
    g, t = y.shape[:2]
    x = y.reshape(g, t, HALF_ROWS, 2, GROUP_BATCH, N_HEADS).transpose(0, 4, 1, 5, 3, 2)
    return x.reshape(g * GROUP_BATCH, t, D_A)


def _state_to_scan(s):
    g = s.shape[0] // GROUP_BATCH
    y = s.reshape(g, GROUP_BATCH, N_HEADS, 2, HALF_ROWS, HEAD_DIM).transpose(0, 5, 4, 3, 1, 2)
    return y.reshape(g, HEAD_DIM, HALF_ROWS, LANES)


def _state_from_scan(y):
    g = y.shape[0]
    s = y.reshape(g, HEAD_DIM, HALF_ROWS, 2, GROUP_BATCH, N_HEADS).transpose(0, 4, 5, 3, 2, 1)
    return s.reshape(g * GROUP_BATCH, N_HEADS, HEAD_DIM, HEAD_DIM)


def _head_rows(x):
    y = x.reshape(N_HEADS, 2, HALF_ROWS).transpose(2, 1, 0)
    y = jnp.broadcast_to(y[:, :, None, :], (HALF_ROWS, 2, GROUP_BATCH, N_HEADS))
    return y.reshape(HALF_ROWS, LANES)


def _head_keys(x):
    y = jnp.broadcast_to(x.T[:, None, None, :], (HEAD_DIM, 2, GROUP_BATCH, N_HEADS))
    return y.reshape(HEAD_DIM, LANES)


def _pairs_to_groups(x, t):
    nb = x.shape[0]
    g = nb // GROUP_BATCH
    y = x.reshape(g, GROUP_BATCH, t, N_HEADS, LANES).transpose(0, 2, 1, 3, 4)
    return y.reshape(g, t, GROUP_BATCH * N_HEADS, LANES)


def _wkv_scan(wb, kr, kv, s0, p, tt):
    g, t = wb.shape[:2]
    pair_tile = pl.BlockSpec((1, tt, HEAD_DIM, LANES), lambda gi, i: (gi, i, 0, 0))
    next_step = pl.BlockSpec((1, 1, HEAD_DIM, LANES), lambda gi, i: (gi, jnp.minimum((i + 1) * tt, t - 1), 0, 0))
    row_tile = pl.BlockSpec((1, tt, HALF_ROWS, LANES), lambda gi, i: (gi, i, 0, 0))
    state = pl.BlockSpec((1, HEAD_DIM, HALF_ROWS, LANES), lambda gi, i: (gi, 0, 0, 0))
    const = lambda n: pl.BlockSpec((n, LANES), lambda gi, i: (0, 0))
    keys = pltpu.VMEM((SCAN_SLOTS, HEAD_DIM, LANES), F32)
    out, sfin = pl.pallas_call(
        functools.partial(_scan_kernel, tt=tt),
        grid=(g, t // tt),
        in_specs=[pair_tile, pair_tile, pair_tile, next_step,
                  state, const(HALF_ROWS), const(HALF_ROWS), const(HEAD_DIM)],
        out_specs=[row_tile, state],
        out_shape=[jax.ShapeDtypeStruct((g, t, HALF_ROWS, LANES), F32),
                   jax.ShapeDtypeStruct((g, HEAD_DIM, HALF_ROWS, LANES), F32)],
        scratch_shapes=[pltpu.VMEM((HEAD_DIM, HALF_ROWS, LANES), F32),
                        pltpu.VMEM((HALF_ROWS, LANES), F32),
                        pltpu.VMEM((1, LANES), F32),
                        keys, keys, keys, keys, keys,
                        pltpu.VMEM((SCAN_SLOTS, HALF_ROWS, LANES), F32)],
        compiler_params=_cparams(("arbitrary", "arbitrary")),
        name="wkv_scan",
    )(wb, kr, kv, kv, _state_to_scan(s0),
      _head_rows(p['rk_lnx_w']), _head_rows(p['rk_lnx_b']), _head_keys(p['rk_r_k']))
    return _from_scan_rows(out), _state_from_scan(sfin)


def _gelu(x):
    return 0.5 * x * (1.0 + jnp.tanh(0.7978845608028654 * (x + 0.044715 * (x * x * x))))


def _lru_gates(xc, wri_ref, bri_ref, nsl_ref, precise):
    dot = _dot3 if precise else _dot
    gates = _sigmoid(dot(xc, wri_ref[...]) + bri_ref[...])
    gate_r = gates[:, :D_B]
    gate_i = gates[:, D_B:]
    log_a = gate_r * nsl_ref[...]
    a = jnp.exp(log_a)
    th = jnp.tanh(log_a)
    one_minus_a2 = -2.0 * th / (1.0 - th)
    bt = jnp.sqrt(one_minus_a2) * (gate_i * xc)
    return a, bt


def _lru_seq_math(pb, cw_ref, cb_ref, wri_ref, bri_ref, nsl_ref, ng_ref, out_ref, hlast_ref, xprev_ref, h_ref):
    tm = pb.shape[0]
    yb = pb[:, :D_B]
    xb = pb[:, D_B:]
    xprev = xprev_ref[...]
    row8 = lax.broadcasted_iota(jnp.int32, (SUBLANES, D_B), 0)

    def shifted(d):
        rolled = pltpu.roll(xb, d, axis=0)
        top = jnp.where(row8 < d, pltpu.roll(xprev, d, axis=0), rolled[:SUBLANES])
        return jnp.concatenate([top, rolled[SUBLANES:]], axis=0)

    xc = cb_ref[...] + cw_ref[3:4, :] * xb
    for d in range(1, CONV_W):
        xc = xc + cw_ref[3 - d:4 - d, :] * shifted(d)
    xprev_ref[...] = xb[tm - SUBLANES:, :]

    a, x = _lru_gates(xc, wri_ref, bri_ref, nsl_ref, False)
    row = lax.broadcasted_iota(jnp.int32, (tm, D_B), 0)
    d = 1
    while d < tm:
        if d < SUBLANES:
            keep = row >= d
            a_s = jnp.where(keep, pltpu.roll(a, d, axis=0), 1.0)
            x_s = jnp.where(keep, pltpu.roll(x, d, axis=0), 0.0)
            x = a * x_s + x
            a = a * a_s
        else:
            x = jnp.concatenate([x[:d], a[d:] * x[:tm - d] + x[d:]], axis=0)
            a = jnp.concatenate([a[:d], a[d:] * a[:tm - d]], axis=0)
        d *= 2
    h = a * h_ref[...] + x
    h_ref[...] = h[tm - 1:, :]
    hlast_ref[0] = h[tm - 1:, :]
    out_ref[0] = _rms(h * _gelu(yb), ng_ref[...]).astype(out_ref.dtype)


def _lru_params(p):
    eye = jnp.eye(N_BLOCKS_B, dtype=F32)
    bd = lambda w: (eye[:, None, :, None] * w[:, :, None, :]).reshape(D_B, D_B)
    wri = jnp.concatenate([bd(p['lru_w_r']), bd(p['lru_w_i'])], axis=1)
    bri = jnp.concatenate([p['lru_b_r'], p['lru_b_i']]).reshape(1, 2 * D_B)
    nsl = (-LRU_C * jax.nn.softplus(-p['lru_lambda'])).reshape(1, D_B)
    return wri, bri, nsl


def _front_kernel(x_ref, shift_ref, scale_ref, g1_ref, w_ref,
                  mu_ref, w0_ref, wup_ref, a0_ref, gup_ref, kk_ref, ka_ref,
                  cw_ref, cb_ref, wri_ref, bri_ref, nsl_ref, ng_ref,
                  wb_out, kr_out, kv_out, g_out, yb_out, hlast_out, patail_out, xtail_out,
                  carry_ref, stage_ref, xprev_ref, h_ref):
    @pl.when(pl.program_id(1) == 0)
    def _():
        carry_ref[...] = jnp.zeros_like(carry_ref)
        xprev_ref[...] = jnp.zeros_like(xprev_ref)
        h_ref[...] = jnp.zeros_like(h_ref)

    h = (_rms(x_ref[0], g1_ref[...]) * (1.0 + scale_ref[0]) + shift_ref[0]).astype(BF16)
    proj = jnp.dot(h, w_ref[...], preferred_element_type=F32)
    pa = proj[:, :N_COLS_A]
    pb = proj[:, N_COLS_A:]
    tm = pa.shape[0]
    patail_out[0] = pa[tm - SUBLANES:, :]
    xtail_out[0] = pb[tm - SUBLANES:, D_B:]

    rolled = pltpu.roll(pa, 1, axis=0)
    row = lax.broadcasted_iota(jnp.int32, pa.shape, 0)
    prev = jnp.where(row == 0, carry_ref[...], rolled)
    carry_ref[...] = pa[tm - 1:, :]
    _prep_math(pa, prev, mu_ref, w0_ref, wup_ref, a0_ref, gup_ref, kk_ref, ka_ref,
               wb_out, kr_out, kv_out, g_out, stage_ref)
    _lru_seq_math(pb, cw_ref, cb_ref, wri_ref, bri_ref, nsl_ref, ng_ref, yb_out, hlast_out, xprev_ref, h_ref)


def _front(x, shift, scale, mod_map, p, w, tm):
    nb, t, _ = x.shape
    assert nb == GROUP_BATCH
    wri, bri, nsl = _lru_params(p)
    consts = ((p['norm1_g'].reshape(1, D_MODEL), w) + _prep_params(p)
              + (p['lru_conv_w'], p['lru_conv_b'].reshape(1, D_B), wri.astype(BF16), bri, nsl,
                 p['lru_norm_g'].reshape(1, D_B)))
    mod_block = (1,) + shift.shape[1:]
    tile = lambda n: pl.BlockSpec((1, tm, n), lambda b, i: (b, i, 0))
    tail = lambda n: pl.BlockSpec((1, SUBLANES, n), lambda b, i: (b, 0, 0))
    pair_tile = pl.BlockSpec((tm, N_HEADS, LANES), lambda b, i: (i, b, 0))
    pair_shape = jax.ShapeDtypeStruct((t, nb * N_HEADS, LANES), F32)
    return pl.pallas_call(
        _front_kernel,
        grid=(nb, t // tm),
        in_specs=[tile(D_MODEL),
                  pl.BlockSpec(mod_block, functools.partial(mod_map, 0)),
                  pl.BlockSpec(mod_block, functools.partial(mod_map, 1))] + [_const_spec(a) for a in consts],
        out_specs=[pair_tile] * 3 + [tile(D_A), tile(D_B), pl.BlockSpec((1, 1, D_B), lambda b, i: (b, 0, 0)),
                                     tail(N_COLS_A), tail(D_B)],
        out_shape=[pair_shape] * 3 + [jax.ShapeDtypeStruct((nb, t, D_A), F32),
                                      jax.ShapeDtypeStruct((nb, t, D_B), BF16),
                                      jax.ShapeDtypeStruct((nb, 1, D_B), F32),
                                      jax.ShapeDtypeStruct((nb, SUBLANES, N_COLS_A), F32),
                                      jax.ShapeDtypeStruct((nb, SUBLANES, D_B), F32)],
        scratch_shapes=[pltpu.VMEM((1, N_COLS_A), F32), pltpu.VMEM((tm * N_HEADS, LANES), F32),
                        pltpu.VMEM((SUBLANES, D_B), F32), pltpu.VMEM((1, D_B), F32)],
        compiler_params=_cparams(("arbitrary", "arbitrary")),
        name="prompt_front",
    )(x, shift, scale, *consts)


def _lru_step_kernel(pb_ref, conv_ref, h0_ref, cw_ref, cb_ref, wri_ref, bri_ref, nsl_ref, ng_ref,
                     out_ref, hnew_ref):
    pb = pb_ref[...]
    yb = pb[:, :D_B]
    xb = pb[:, D_B:]
    xc = cb_ref[...] + cw_ref[3:4, :] * xb
    for j in range(CONV_W - 1):
        xc = xc + cw_ref[j:j + 1, :] * conv_ref[j]
    a, x = _lru_gates(xc, wri_ref, bri_ref, nsl_ref, True)
    h = a * h0_ref[...] + x
    hnew_ref[...] = h
    out_ref[...] = _rms(h * _gelu(yb), ng_ref[...]).astype(out_ref.dtype)


def _lru_step(pb, conv0, h0, p):
    n = pb.shape[0]
    wri, bri, nsl = _lru_params(p)
    return pl.pallas_call(
        _lru_step_kernel,
        out_shape=[jax.ShapeDtypeStruct((n, D_B), BF16), jax.ShapeDtypeStruct((n, D_B), F32)],
        compiler_params=pltpu.CompilerParams(vmem_limit_bytes=VMEM_LIMIT),
        name="rglru_step",
    )(pb, conv0, h0, p['lru_conv_w'], p['lru_conv_b'].reshape(1, D_B), wri, bri, nsl,
      p['lru_norm_g'].reshape(1, D_B))


def _to_token_tiles(ref, x):
    rows = x.shape[0]
    for c in range(TILE_CHUNKS):
        ref[pl.ds(c, rows, stride=TILE_CHUNKS), :] = x[:, c * LANES:(c + 1) * LANES]


def _from_token_tiles(ref, row0, rows):
    return jnp.concatenate(
        [ref[pl.ds(row0 + c, rows, stride=TILE_CHUNKS), :] for c in range(TILE_CHUNKS)], axis=1)


def _post_kernel(x_ref, wkv_ref, g_ref, yb_ref, gate1_ref, shift2_ref, scale2_ref, n2_ref,
                 wo_ref, rw_ref, rb_ref, tri_ref, x1_ref, h2_ref, ti_ref, rk_ref, tg_ref, cnt_ref):
    ya = (wkv_ref[0] * g_ref[0]).astype(BF16)
    mixed = (jnp.dot(ya, wo_ref[:D_A, :], preferred_element_type=F32)
             + jnp.dot(yb_ref[0], wo_ref[D_A:, :], preferred_element_type=F32))
    x1 = x_ref[0] + gate1_ref[0] * mixed
    x1_ref[0] = x1
    h2 = _rms(x1, n2_ref[...]) * (1.0 + scale2_ref[0]) + shift2_ref[0]
    _to_token_tiles(h2_ref, h2)
    logits = _dot3_nt(rw_ref[...], h2) + rb_ref[...]
    eidx = lax.broadcasted_iota(jnp.int32, logits.shape, 0)
    vals, idxs = [], []
    cur = logits
    for _ in range(TOP_K):
        m = jnp.max(cur, axis=0, keepdims=True)
        i = jnp.min(jnp.where(cur == m, eidx, N_EXPERTS), axis=0, keepdims=True)
        vals.append(m)
        idxs.append(i)
        cur = jnp.where(eidx == i, -jnp.inf, cur)
    ex = [jnp.exp(v - vals[0]) for v in vals]
    den = ex[0] + ex[1] + ex[2] + ex[3]
    sel = [eidx == i for i in idxs]
    onehot = (sel[0] | sel[1] | sel[2] | sel[3]).astype(F32)
    incl = jnp.dot(onehot.astype(BF16), tri_ref[...], preferred_element_type=F32)
    rank = incl - onehot
    cnt_ref[0] = jnp.broadcast_to(jnp.sum(onehot, axis=1, keepdims=True), cnt_ref.shape[1:])
    for k in range(TOP_K):
        ti_ref[0, k:k + 1, :] = idxs[k]
        rk_ref[0, k:k + 1, :] = jnp.sum(jnp.where(sel[k], rank, 0.0), axis=0, keepdims=True).astype(jnp.int32)
        tg_ref[0, k:k + 1, :] = ex[k] / den


def _post(x, wkv, g, yb, gate1, shift2, scale2, mod_map, p, tm):
    nb, t, _ = x.shape
    mod_block = (1,) + gate1.shape[1:]
    tile = lambda n: pl.BlockSpec((1, tm, n), lambda b, i: (b, i, 0))
    full = lambda a: pl.BlockSpec(a.shape, lambda b, i: (0,) * a.ndim)
    mspec = lambda j: pl.BlockSpec(mod_block, functools.partial(mod_map, j))
    n2 = p['norm2_g'].reshape(1, D_MODEL)
    wo = p['w_out'].astype(BF16)
    rw = p['router_w'].T
    rb = p['router_b'].reshape(N_EXPERTS, 1)
    tri = jnp.triu(jnp.ones((tm, tm), BF16))
    nt = t // tm
    n = nb * t
    topk = pl.BlockSpec((1, TOP_K, tm), lambda b, i: (b * nt + i, 0, 0))
    topk_i = jax.ShapeDtypeStruct((nb * nt, TOP_K, tm), jnp.int32)
    x1, h2, ti, rk, tg, cnt = pl.pallas_call(
        _post_kernel,
        grid=(nb, nt),
        in_specs=[tile(D_MODEL), tile(D_A), tile(D_A), tile(D_B), mspec(2), mspec(3), mspec(4),
                  full(n2), full(wo), full(rw), full(rb), full(tri)],
        out_specs=[tile(D_MODEL),
                   pl.BlockSpec((tm * TILE_CHUNKS, LANES), lambda b, i: (b * nt + i, 0)),
                   topk, topk, topk,
                   pl.BlockSpec((1, N_EXPERTS, LANES), lambda b, i: (b * nt + i, 0, 0))],
        out_shape=[jax.ShapeDtypeStruct((nb, t, D_MODEL), F32),
                   jax.ShapeDtypeStruct((n * TILE_CHUNKS, LANES), F32),
                   topk_i, topk_i,
                   jax.ShapeDtypeStruct((nb * nt, TOP_K, tm), F32),
                   jax.ShapeDtypeStruct((nb * nt, N_EXPERTS, LANES), F32)],
        compiler_params=_cparams(("arbitrary", "arbitrary")),
        name="outproj_router",
    )(x, wkv, g, yb, gate1, shift2, scale2, n2, wo, rw, rb, tri)
    return x1, h2, ti, rk, tg, cnt[:, :, 0].astype(jnp.int32)


def _plan(routes):
    cnt = jnp.concatenate([r[2] for r in routes], axis=0)
    total = jnp.sum(cnt, axis=0)
    run_start = jnp.cumsum(cnt, axis=0) - cnt
    padded = (total + MOE_ROWS - 1) // MOE_ROWS * MOE_ROWS
    pad_end = jnp.cumsum(padded)
    pad_start = pad_end - padded
    loc_off = jnp.cumsum(cnt, axis=1) - cnt
    dst0 = pad_start[None, :] + run_start
    n_tokens = sum(r[0].shape[0] * r[0].shape[2] for r in routes)
    n_blocks = -(-n_tokens * TOP_K // MOE_ROWS) + N_EXPERTS
    block_row0 = jnp.arange(n_blocks, dtype=jnp.int32) * MOE_ROWS
    block_e = jnp.minimum(jnp.sum(pad_end[None, :] <= block_row0[:, None], axis=1), N_EXPERTS - 1)
    n_used = (pad_end[-1] // MOE_ROWS).reshape(1)
    first_total = jnp.sum(routes[0][2], axis=0)
    zero_table = jnp.stack([pad_end - (pad_start + first_total), jnp.zeros_like(pad_end), pad_start + first_total])
    as_i32 = lambda a: a.astype(jnp.int32)
    experts = jnp.arange(N_EXPERTS, dtype=jnp.int32)
    groups, t0 = [], 0
    for ti, rk, c in routes:
        nt = ti.shape[0]
        sl = slice(t0, t0 + nt)
        off = jnp.sum(jnp.where(ti[..., None] == experts, loc_off[sl, None, None, :], 0), axis=-1)
        lpos = as_i32((rk + off) * TILE_CHUNKS).reshape(nt, -1)
        table = as_i32(jnp.stack([cnt[sl], loc_off[sl], dst0[sl]], axis=0).reshape(3, -1))
        groups.append((lpos, table))
        t0 += nt
    return groups, as_i32(block_e), as_i32(n_used), as_i32(pad_end), as_i32(zero_table), as_i32(padded), n_blocks


def _tile_rows(row):
    return pl.ds(pl.multiple_of(row * TILE_CHUNKS, TILE_CHUNKS), TILE_CHUNKS)


def _sublane_rows(row8):
    return pl.ds(pl.multiple_of(row8, TILE_CHUNKS), TILE_CHUNKS)


def _piece_rows(row0, n_rows):
    return pl.ds(pl.multiple_of(row0 * TILE_CHUNKS, TILE_CHUNKS), n_rows * TILE_CHUNKS)


def _block_rows(block):
    rows = MOE_ROWS * TILE_CHUNKS
    return pl.ds(pl.multiple_of(block * rows, rows), rows)


def _for_each_piece(table_ref, tile, n_tiles, fn):
    def per_expert(e, queue):
        col = tile * N_EXPERTS + e
        cnt = table_ref[col]
        lo = table_ref[n_tiles * N_EXPERTS + col]
        d0 = table_ref[2 * n_tiles * N_EXPERTS + col]

        def full(c, c2):
            fn(lo + c * MOE_CHUNK, d0 + c * MOE_CHUNK, MOE_CHUNK, queue)
            return c2
        lax.fori_loop(0, cnt // MOE_CHUNK, full, 0)
        bit = MOE_CHUNK // 2
        while bit >= 1:
            done = cnt - cnt % (2 * bit)

            @pl.when((cnt & bit) != 0)
            def _(done=done, bit=bit):
                fn(lo + done, d0 + done, bit, queue)
            bit //= 2

    def per_pair(p, carry):
        for queue in range(2):
            per_expert(2 * p + queue, queue)
        return carry
    lax.fori_loop(0, N_EXPERTS // 2, per_pair, 0)


def _scatter_kernel(*refs, first):
    idx_ref, loc_ref, zero_ref, sem, isem, zsem = refs[-6:]
    table_ref, pad_end_ref, ztab_ref, h2_ref, lpos_hbm = refs[:5]
    xs_hbm = refs[-7]
    i = pl.program_id(0)
    n_tiles = pl.num_programs(0)
    tm = h2_ref.shape[0] // TILE_CHUNKS
    slot = i % 2

    def idx_copy(tile, s):
        dst = idx_ref.at[pl.ds(s * (TOP_K * tm), TOP_K * tm)]
        return pltpu.make_async_copy(lpos_hbm.at[tile], dst, isem.at[s])

    def drain(s):
        for _ in range(TOP_K):
            pltpu.make_async_copy(h2_ref, xs_hbm.at[pl.ds(0, tm * TILE_CHUNKS), :], sem.at[s]).wait()

    @pl.when(i == 0)
    def _():
        idx_copy(0, 0).start()

    if first:
        @pl.when(i == 0)
        def _():
            zero_ref[...] = jnp.zeros_like(zero_ref)
            n_used = pad_end_ref[N_EXPERTS - 1] // MOE_ROWS
            n_blocks = xs_hbm.shape[0] // (MOE_ROWS * TILE_CHUNKS)

            def zero_piece(lo, d0, n):
                return pltpu.make_async_copy(zero_ref.at[_piece_rows(0, n), :], xs_hbm.at[_piece_rows(d0, n), :], zsem)
            _for_each_piece(ztab_ref, 0, 1, lambda lo, d0, n, q: zero_piece(lo, d0, n).start(priority=q))

            def fill_tail(blk, c):
                pltpu.make_async_copy(zero_ref, xs_hbm.at[_block_rows(blk), :], zsem).start()
                return c
            lax.fori_loop(n_used, n_blocks, fill_tail, 0)

            def drain_fill(j, c):
                pltpu.make_async_copy(zero_ref, xs_hbm.at[_block_rows(0), :], zsem).wait()
                return c
            lax.fori_loop(0, n_blocks - n_used, drain_fill, 0)
            _for_each_piece(ztab_ref, 0, 1, lambda lo, d0, n, q: zero_piece(lo, d0, n).wait())

    def step(s):
        @pl.when(i + 1 < n_tiles)
        def _():
            idx_copy(i + 1, 1 - s).start()

        @pl.when(i >= 2)
        def _():
            drain(s)
        idx_copy(i, s).wait()
        base = s * (TOP_K * tm)

        def place(r, c):
            row = h2_ref[_tile_rows(r), :]
            for k in range(TOP_K):
                loc_ref[s, _sublane_rows(idx_ref[base + k * tm + r]), :] = row
            return c
        lax.fori_loop(0, tm, place, 0, unroll=8)

        def send(lo, d0, n, q):
            pltpu.make_async_copy(loc_ref.at[s, _piece_rows(lo, n), :], xs_hbm.at[_piece_rows(d0, n), :],
                                  sem.at[s]).start(priority=q)
        _for_each_piece(table_ref, i, n_tiles, send)

        @pl.when(i == n_tiles - 1)
        def _():
            @pl.when(i >= 1)
            def _():
                drain(1 - s)
            drain(s)

    for s in range(2):
        pl.when(slot == s)(functools.partial(step, s))


def _scatter(h2, lpos, table, pad_end, zero_table, n_rows, tm, xs=None):
    nt = lpos.shape[0]
    first = xs is None
    loc_rows = TOP_K * tm * TILE_CHUNKS
    in_specs = [pl.BlockSpec((tm * TILE_CHUNKS, LANES), lambda i, *_: (i, 0)),
                pl.BlockSpec(memory_space=pl.ANY)]
    args = [table.reshape(-1), pad_end, zero_table.reshape(-1), h2, lpos]
    if not first:
        in_specs.append(pl.BlockSpec(memory_space=pl.ANY))
        args.append(xs)
    grid_spec = pltpu.PrefetchScalarGridSpec(
        num_scalar_prefetch=3,
        grid=(nt,),
        in_specs=in_specs,
        out_specs=pl.BlockSpec(memory_space=pl.ANY),
        scratch_shapes=[pltpu.SMEM((2 * TOP_K * tm,), jnp.int32),
                        pltpu.VMEM((2, loc_rows, LANES), F32),
                        pltpu.VMEM((MOE_ROWS * TILE_CHUNKS, LANES), F32),
                        pltpu.SemaphoreType.DMA((2,)), pltpu.SemaphoreType.DMA((2,)), pltpu.SemaphoreType.DMA],
    )
    return pl.pallas_call(
        functools.partial(_scatter_kernel, first=first),
        grid_spec=grid_spec,
        out_shape=jax.ShapeDtypeStruct((n_rows * TILE_CHUNKS, LANES), F32),
        input_output_aliases={} if first else {5: 0},
        compiler_params=_cparams(("arbitrary",)),
        name="moe_dispatch",
    )(*args)


def _mlp_kernel(be_ref, nu_ref, next_ref, par_ref, xs_ref, bias_ref, wg_hbm, wu_hbm, wd_hbm, out_ref,
                w32_ref, w16_ref, sem):
    i = pl.program_id(0)
    used = i < nu_ref[0]
    e = be_ref[i]

    def weight_copies(expert, s):
        return [pltpu.make_async_copy(w_hbm.at[expert], w32_ref.at[s, j], sem.at[s])
                for j, w_hbm in enumerate((wg_hbm, wu_hbm, wd_hbm))]

    def load_expert(s):
        @pl.when(i == 0)
        def _():
            for c in weight_copies(e, s):
                c.start()
        for c in weight_copies(e, s):
            c.wait()
        nxt = next_ref[e]

        @pl.when(nxt >= 0)
        def _():
            for c in weight_copies(nxt, 1 - s):
                c.start()
        for j in range(3):
            w16_ref[j] = w32_ref[s, j].astype(BF16)

    first_of_expert = used & ((i == 0) | (e != be_ref[jnp.maximum(i - 1, 0)]))
    for s in range(2):
        pl.when(first_of_expert & (par_ref[e] == s))(functools.partial(load_expert, s))

    @pl.when(used)
    def _():
        x = _from_token_tiles(xs_ref, 0, MOE_ROWS).astype(BF16)
        gt = jnp.dot(x, w16_ref[0], preferred_element_type=F32) + bias_ref[0, 0:1, :]
        up = jnp.dot(x, w16_ref[1], preferred_element_type=F32) + bias_ref[0, 1:2, :]
        gt = jnp.minimum(gt, SWIGLU_LIMIT)
        up = jnp.clip(up, -SWIGLU_LIMIT, SWIGLU_LIMIT)
        glu = gt * _sigmoid(gt * SWIGLU_ALPHA)
        mid = ((up + 1.0) * glu).astype(BF16)
        _to_token_tiles(out_ref, jnp.dot(mid, w16_ref[2], preferred_element_type=F32) + bias_ref[0, 2:3, :])

    @pl.when(jnp.logical_not(used))
    def _():
        out_ref[...] = jnp.zeros_like(out_ref)


def _mlp(xs, block_e, n_used, padded, n_blocks, wts):
    wg, bg, wu, bu, wd, bd = wts
    assert D_FF == D_MODEL
    bias = jnp.stack([bg, bu, bd], axis=1)
    nonempty = padded > 0
    experts = jnp.arange(N_EXPERTS, dtype=jnp.int32)
    parity = ((jnp.cumsum(nonempty) - nonempty) % 2).astype(jnp.int32)
    later = nonempty[None, :] & (experts[None, :] > experts[:, None])
    nxt = jnp.min(jnp.where(later, experts[None, :], N_EXPERTS), axis=1)
    nxt = jnp.where(nxt == N_EXPERTS, -1, nxt).astype(jnp.int32)
    rows = pl.BlockSpec((MOE_ROWS * TILE_CHUNKS, LANES), lambda i, *_: (i, 0))
    hbm = pl.BlockSpec(memory_space=pl.ANY)
    grid_spec = pltpu.PrefetchScalarGridSpec(
        num_scalar_prefetch=4,
        grid=(n_blocks,),
        in_specs=[rows, pl.BlockSpec((1, 3, D_FF), lambda i, be, *_: (be[i], 0, 0)), hbm, hbm, hbm],
        out_specs=rows,
        scratch_shapes=[pltpu.VMEM((2, 3, D_MODEL, D_FF), F32),
                        pltpu.VMEM((3, D_MODEL, D_FF), BF16),
                        pltpu.SemaphoreType.DMA((2,))],
    )
    return pl.pallas_call(
        _mlp_kernel,
        grid_spec=grid_spec,
        out_shape=jax.ShapeDtypeStruct(xs.shape, F32),
        compiler_params=_cparams(("arbitrary",)),
        name="moe_experts",
    )(block_e, n_used, nxt, parity, xs, bias, wg, wu, wd)


def _gather_kernel(table_ref, x1_ref, gate2_ref, fg_ref, lpos_hbm, tg_hbm, rows_hbm, y_ref,
                   idx_ref, gsm_ref, loc_ref, ff_ref, sem, isem):
    i = pl.program_id(1) + pl.program_id(0) * pl.num_programs(1)
    n_tiles = pl.num_programs(0) * pl.num_programs(1)
    tm = x1_ref.shape[1]
    slot = i % 2

    def meta_copies(tile, s):
        seg = pl.ds(s * (TOP_K * tm), TOP_K * tm)
        return (pltpu.make_async_copy(lpos_hbm.at[tile], idx_ref.at[seg], isem.at[s]),
                pltpu.make_async_copy(tg_hbm.at[tile], gsm_ref.at[seg], isem.at[s]))

    def fetch_tile(tile, s):
        for c in meta_copies(tile, s):
            c.start()

        def fetch(lo, d0, n, q):
            pltpu.make_async_copy(rows_hbm.at[_piece_rows(d0, n), :], loc_ref.at[s, _piece_rows(lo, n), :],
                                  sem.at[s]).start(priority=q)
        _for_each_piece(table_ref, tile, n_tiles, fetch)

    @pl.when(i == 0)
    def _():
        fetch_tile(0, 0)

    def step(s):
        @pl.when(i + 1 < n_tiles)
        def _():
            fetch_tile(i + 1, 1 - s)

        pltpu.make_async_copy(rows_hbm.at[pl.ds(0, TOP_K * tm * TILE_CHUNKS), :], loc_ref.at[s], sem.at[s]).wait()
        for c in meta_copies(i, s):
            c.wait()
        base = s * (TOP_K * tm)

        def mix(r, c):
            acc = gsm_ref[base + r] * loc_ref[s, _sublane_rows(idx_ref[base + r]), :]
            for k in range(1, TOP_K):
                acc = acc + gsm_ref[base + k * tm + r] * loc_ref[s, _sublane_rows(idx_ref[base + k * tm + r]), :]
            ff_ref[_tile_rows(r), :] = acc
            return c
        lax.fori_loop(0, tm, mix, 0, unroll=8)

    for s in range(2):
        pl.when(slot == s)(functools.partial(step, s))

    x2 = x1_ref[0] + gate2_ref[0] * _from_token_tiles(ff_ref, 0, tm)
    y_ref[0] = _rms(x2, fg_ref[...])


def _gather(x1, tg, gate2, mod_map, final_g, lpos, table, rows, tm):
    nb, t, _ = x1.shape
    nt = t // tm
    mod_block = (1,) + gate2.shape[1:]
    loc_rows = TOP_K * tm * TILE_CHUNKS
    grid_spec = pltpu.PrefetchScalarGridSpec(
        num_scalar_prefetch=1,
        grid=(nb, nt),
        in_specs=[pl.BlockSpec((1, tm, D_MODEL), lambda b, i, *_: (b, i, 0)),
                  pl.BlockSpec(mod_block, lambda b, i, *_: mod_map(5, b, i)),
                  pl.BlockSpec((1, D_MODEL), lambda b, i, *_: (0, 0)),
                  pl.BlockSpec(memory_space=pl.ANY),
                  pl.BlockSpec(memory_space=pl.ANY),
                  pl.BlockSpec(memory_space=pl.ANY)],
        out_specs=pl.BlockSpec((1, tm, D_MODEL), lambda b, i, *_: (b, i, 0)),
        scratch_shapes=[pltpu.SMEM((2 * TOP_K * tm,), jnp.int32),
                        pltpu.SMEM((2 * TOP_K * tm,), F32),
                        pltpu.VMEM((2, loc_rows, LANES), F32),
                        pltpu.VMEM((tm * TILE_CHUNKS, LANES), F32),
                        pltpu.SemaphoreType.DMA((2,)), pltpu.SemaphoreType.DMA((2,))],
    )
    return pl.pallas_call(
        _gather_kernel,
        grid_spec=grid_spec,
        out_shape=jax.ShapeDtypeStruct((nb, t, D_MODEL), F32),
        compiler_params=_cparams(("arbitrary", "arbitrary")),
        name="moe_combine",
    )(table.reshape(-1), x1, gate2, final_g.reshape(1, D_MODEL), lpos, tg.reshape(tg.shape[0], -1), rows)


def _moe(route_p, route_s, mod_p, mod_s, final_g, wts):
    routes = [route_p, route_s]
    groups, block_e, n_used, pad_end, zero_table, padded, n_blocks = _plan([(r[2], r[3], r[5]) for r in routes])
    xs = None
    for (x1, h2, ti, rk, tg, cnt), (lpos, table) in zip(routes, groups):
        xs = _scatter(h2, lpos, table, pad_end, zero_table, n_blocks * MOE_ROWS, ti.shape[2], xs)
    rows = _mlp(xs, block_e, n_used, padded, n_blocks, wts)
    outs = []
    for (x1, h2, ti, rk, tg, cnt), (lpos, table), (mod, mod_map) in zip(routes, groups, (mod_p, mod_s)):
        outs.append(_gather(x1, tg, mod, mod_map, final_g, lpos, table, rows, ti.shape[2]))
    return outs


def _forward(x_prompt, x_sample, c_prompt, c_sample, state_wkv, state_shift, state_conv, state_lru, p, final_g):
    bp, tp, _ = x_prompt.shape
    bs = x_sample.shape[0]
    tm = min(512, tp)
    tt = min(128, tp)

    mod = _ada(jnp.concatenate([c_prompt, c_sample], axis=0), p['w_ada'], p['b_ada'])
    mod_p = mod[:bp].reshape(bp * N_MOD, 1, D_MODEL)
    mod_s = mod[bp:].reshape(bs, N_MOD, D_MODEL).transpose(1, 0, 2)
    map_p = lambda j, b, i: (b * N_MOD + j, 0, 0)
    map_s = lambda j, b, i: (j, 0, 0)

    wts = (p['w_gate'], p['b_gate'], p['w_up'], p['b_up'], p['w_down'], p['b_down'])

    wb_p, kr_p, kv_p, g, yb, lru_p, pa_tail, x_tail = _front(x_prompt, mod_p, mod_p, map_p, p,
                                                             p['w_in'].astype(BF16), tm)
    s0 = jnp.zeros((bp, N_HEADS, HEAD_DIM, HEAD_DIM), F32)
    wkv_out, wkv_p = _wkv_scan(wb_p[None], kr_p[None], kv_p[None], s0, p, tt)
    route_p = _post(x_prompt, wkv_out, g, yb, mod_p, mod_p, mod_p, map_p, p, tm)
    shift_p = pa_tail[:, -1, :]
    conv_p = x_tail[:, SUBLANES - (CONV_W - 1):, :]

    xs = x_sample.reshape(1, bs, D_MODEL)
    pa_s, pb_s = _inproj(xs, mod_s, mod_s, map_s, p['norm1_g'], p['w_in'], bs)
    wb_s, kr_s, kv_s, g = _prep(pa_s, state_shift.reshape(1, bs, N_COLS_A), p, bs)
    as_seq = lambda a: _pairs_to_groups(a.reshape(bs, N_HEADS, LANES), 1)
    wkv_out, wkv_s = _wkv_scan(as_seq(wb_s), as_seq(kr_s), as_seq(kv_s), state_wkv, p, 1)
    conv0 = state_conv.transpose(1, 0, 2)
    yb, lru_s = _lru_step(pb_s[0], conv0, state_lru, p)
    route_s = _post(xs, wkv_out.reshape(1, bs, D_A), g, yb.reshape(1, bs, D_B),
                    mod_s, mod_s, mod_s, map_s, p, bs)
    y_prompt, y_sample = _moe(route_p, route_s, (mod_p, map_p), (mod_s, map_s), final_g, wts)
    shift_s = pa_s[0]
    conv_s = jnp.concatenate([state_conv[:, 1:], pb_s[0][:, None, D_B:]], axis=1)

    return (y_prompt, y_sample.reshape(bs, 1, D_MODEL),
            wkv_p[None], shift_p[None], conv_p[None], lru_p.reshape(bp, D_B)[None],
            wkv_s[None], shift_s[None], conv_s[None], lru_s[None])


def kernel(x_prompt, x_sample, c_prompt, c_sample, state_wkv, state_shift, state_conv, state_lru, w_ada, b_ada, norm1_g, norm2_g, w_in, rk_mu, rk_w0, rk_w_up, rk_a0, rk_a_up, rk_g_up, rk_k_k, rk_k_a, rk_r_k, rk_lnx_w, rk_lnx_b, lru_conv_w, lru_conv_b, lru_w_r, lru_b_r, lru_w_i, lru_b_i, lru_lambda, lru_norm_g, w_out, router_w, router_b, w_gate, b_gate, w_up, b_up, w_down, b_down, final_g):
    assert w_ada.shape[0] == 1, "single-layer trunk"
    p = dict(w_ada=w_ada[0], b_ada=b_ada[0], norm1_g=norm1_g[0], norm2_g=norm2_g[0], w_in=w_in[0],
             rk_mu=rk_mu[0], rk_w0=rk_w0[0], rk_w_up=rk_w_up[0], rk_a0=rk_a0[0], rk_a_up=rk_a_up[0],
             rk_g_up=rk_g_up[0], rk_k_k=rk_k_k[0], rk_k_a=rk_k_a[0], rk_r_k=rk_r_k[0],
             rk_lnx_w=rk_lnx_w[0], rk_lnx_b=rk_lnx_b[0], lru_conv_w=lru_conv_w[0],
             lru_conv_b=lru_conv_b[0], lru_w_r=lru_w_r[0], lru_b_r=lru_b_r[0], lru_w_i=lru_w_i[0],
             lru_b_i=lru_b_i[0], lru_lambda=lru_lambda[0], lru_norm_g=lru_norm_g[0], w_out=w_out[0],
             router_w=router_w[0], router_b=router_b[0], w_gate=w_gate[0], b_gate=b_gate[0],
             w_up=w_up[0], b_up=b_up[0], w_down=w_down[0], b_down=b_down[0])
    return _forward(x_prompt, x_sample, c_prompt, c_sample, state_wkv[0], state_shift[0], state_conv[0],
                    state_lru[0], p, final_g)
```

```python
import functools

import jax
import jax.numpy as jnp
from jax import lax
from jax.experimental import pallas as pl
from jax.experimental.pallas import tpu as pltpu

F32 = jnp.float32
BF16 = jnp.bfloat16

D_MODEL = 1024
D_A = 512
HEAD_DIM = 64
N_HEADS = 8
D_B = 512
N_BLOCKS_B = 8
CONV_W = 4
LRU_C = 8.0
R_DECAY = 64
R_AAA = 64
R_GATE = 128
N_COLS_A = 3 * D_A + R_DECAY + R_AAA + R_GATE
N_COLS_B = 2 * D_B
N_EXPERTS = 32
TOP_K = 4
D_FF = 1024
SWIGLU_LIMIT = 7.0
SWIGLU_ALPHA = 1.702
RMS_EPS = 1e-6
LN_X_EPS = 64e-5
N_MOD = 6

LANES = 128
SUBLANES = 8
GROUP_BATCH = 8
HALF_ROWS = HEAD_DIM // 2
MOE_ROWS = 256
MOE_CHUNK = 16
TILE_CHUNKS = D_MODEL // LANES
VMEM_LIMIT = 56 * 1024 * 1024


def _cparams(sem):
    return pltpu.CompilerParams(dimension_semantics=sem, vmem_limit_bytes=VMEM_LIMIT)


def _dot(a, b):
    return jnp.dot(a.astype(BF16), b.astype(BF16), preferred_element_type=F32)


def _split(a):
    hi = a.astype(BF16)
    lo = (a - hi.astype(F32)).astype(BF16)
    return hi, lo


def _dot3(a, b):
    ah, al = _split(a)
    bh, bl = _split(b)
    return (jnp.dot(ah, bh, preferred_element_type=F32)
            + (jnp.dot(al, bh, preferred_element_type=F32) + jnp.dot(ah, bl, preferred_element_type=F32)))


def _dot3_nt(a, b):
    dn = (((1,), (1,)), ((), ()))
    ah, al = _split(a)
    bh, bl = _split(b)
    d = lambda x, y: lax.dot_general(x, y, dn, preferred_element_type=F32)
    return d(ah, bh) + (d(al, bh) + d(ah, bl))


def _softplus(x):
    return jnp.maximum(x, 0.0) + jnp.log1p(jnp.exp(-jnp.abs(x)))


def _sigmoid(x):
    return 1.0 / (1.0 + jnp.exp(-x))


def _rms(x, g):
    ms = jnp.mean(x * x, axis=-1, keepdims=True)
    return x * lax.rsqrt(ms + RMS_EPS) * g


def _ada_kernel(c_ref, w_ref, b_ref, o_ref):
    c = c_ref[...]
    s = c * _sigmoid(c)
    o_ref[...] = _dot3(s, w_ref[...]) + b_ref[...]


def _ada(c, w_ada, b_ada):
    rows = c.shape[0]
    ncol = w_ada.shape[1]
    tn = D_MODEL
    return pl.pallas_call(
        _ada_kernel,
        grid=(ncol // tn,),
        in_specs=[pl.BlockSpec((rows, D_MODEL), lambda j: (0, 0)),
                  pl.BlockSpec((D_MODEL, tn), lambda j: (0, j)),
                  pl.BlockSpec((1, tn), lambda j: (0, j))],
        out_specs=pl.BlockSpec((rows, tn), lambda j: (0, j)),
        out_shape=jax.ShapeDtypeStruct((rows, ncol), F32),
        compiler_params=_cparams(("arbitrary",)),
        name="ada_mod",
    )(c, w_ada, b_ada.reshape(1, ncol))


def _inproj_kernel(x_ref, shift_ref, scale_ref, g_ref, w_ref, pa_ref, pb_ref):
    x = x_ref[0]
    h = _rms(x, g_ref[...]) * (1.0 + scale_ref[0]) + shift_ref[0]
    proj = _dot3(h, w_ref[...])
    pa_ref[0] = proj[:, :N_COLS_A]
    pb_ref[0] = proj[:, N_COLS_A:]


def _inproj(x, shift, scale, mod_map, g, w, tm):
    nb, t, _ = x.shape
    mod_block = (1,) + shift.shape[1:]
    return pl.pallas_call(
        _inproj_kernel,
        grid=(nb, t // tm),
        in_specs=[pl.BlockSpec((1, tm, D_MODEL), lambda b, i: (b, i, 0)),
                  pl.BlockSpec(mod_block, functools.partial(mod_map, 0)),
                  pl.BlockSpec(mod_block, functools.partial(mod_map, 1)),
                  pl.BlockSpec((1, D_MODEL), lambda b, i: (0, 0)),
                  pl.BlockSpec((D_MODEL, N_COLS_A + N_COLS_B), lambda b, i: (0, 0))],
        out_specs=[pl.BlockSpec((1, tm, N_COLS_A), lambda b, i: (b, i, 0)),
                   pl.BlockSpec((1, tm, N_COLS_B), lambda b, i: (b, i, 0))],
        out_shape=[jax.ShapeDtypeStruct((nb, t, N_COLS_A), F32),
                   jax.ShapeDtypeStruct((nb, t, N_COLS_B), F32)],
        compiler_params=_cparams(("arbitrary", "arbitrary")),
        name="norm1_inproj",
    )(x, shift, scale, g.reshape(1, D_MODEL), w)


def _store_head_pairs(ref, stage_ref, x, y):
    rows = x.shape[0]
    flat = ref.shape[0] == 1
    for h in range(N_HEADS):
        sl = slice(h * HEAD_DIM, (h + 1) * HEAD_DIM)
        pair = jnp.concatenate([x[:, sl], y[:, sl]], axis=1)
        if flat:
            ref[0, pl.ds(h, rows, stride=N_HEADS), :] = pair
        else:
            stage_ref[pl.ds(h, rows, stride=N_HEADS), :] = pair
    if not flat:
        ref[...] = stage_ref[...].reshape(rows, N_HEADS, LANES)


def _prep_math(pa, prev, mu_ref, w0_ref, wup_ref, a0_ref, gup_ref, kk_ref, ka_ref,
               wb_out, kr_out, kv_out, g_out, stage_ref):
    z = pa + (prev - pa) * mu_ref[...]
    r = z[:, 0:D_A]
    k = z[:, D_A:2 * D_A]
    v = z[:, 2 * D_A:3 * D_A]
    lo = 3 * D_A
    za = z[:, lo:lo + R_DECAY + R_AAA]
    lane = lax.broadcasted_iota(jnp.int32, za.shape, 1)
    za = jnp.where(lane < R_DECAY, jnp.tanh(za), za)
    lw = _dot3(za, wup_ref[...])
    w_log = -_softplus(-(w0_ref[...] + lw[:, :D_A])) - 0.5
    decay = jnp.exp(-jnp.exp(w_log))
    a = _sigmoid(a0_ref[...] + lw[:, D_A:])
    gd = z[:, lo + R_DECAY + R_AAA:]
    g = _dot3(_sigmoid(gd), gup_ref[...])
    kk = k * kk_ref[...]
    _store_head_pairs(wb_out, stage_ref, decay, kk * a)
    _store_head_pairs(kr_out, stage_ref, k * (1.0 + (a - 1.0) * ka_ref[...]), r)
    _store_head_pairs(kv_out, stage_ref, kk, v)
    g_out[0] = g


def _prep_kernel(pa_ref, prev_ref, *refs):
    _prep_math(pa_ref[0], prev_ref[0], *refs, None)


def _prep_params(p):
    wup = jnp.zeros((R_DECAY + R_AAA, 2 * D_A), F32)
    wup = wup.at[:R_DECAY, :D_A].set(p['rk_w_up']).at[R_DECAY:, D_A:].set(p['rk_a_up'])
    vec = lambda a: a.reshape(1, -1)
    return (vec(p['rk_mu']), vec(p['rk_w0']), wup, vec(p['rk_a0']), p['rk_g_up'],
            vec(p['rk_k_k']), vec(p['rk_k_a']))


def _const_spec(a):
    return pl.BlockSpec(a.shape, lambda b, i: (0,) * a.ndim)


def _prep(pa, prev, p, tm):
    nb, t, _ = pa.shape
    params = _prep_params(p)
    rows = pl.BlockSpec((1, tm, N_COLS_A), lambda b, i: (b, i, 0))
    return pl.pallas_call(
        _prep_kernel,
        grid=(nb, t // tm),
        in_specs=[rows, rows] + [_const_spec(a) for a in params],
        out_specs=[pl.BlockSpec((1, tm * N_HEADS, LANES), lambda b, i: (b, i, 0))] * 3
                  + [pl.BlockSpec((1, tm, D_A), lambda b, i: (b, i, 0))],
        out_shape=[jax.ShapeDtypeStruct((nb, t * N_HEADS, LANES), F32)] * 3
                  + [jax.ShapeDtypeStruct((nb, t, D_A), F32)],
        compiler_params=_cparams(("arbitrary", "arbitrary")),
        name="rwkv_prep",
    )(pa, prev, *params)


SCAN_SLOTS = 4


def _scan_kernel(wb_ref, kr_ref, kv_ref, kvnext_ref, s0_ref, lnw_ref, lnb_ref, rk_ref,
                 out_ref, sfin_ref, s_ref, sa_ref, inv_ref, wd_ref, kkd_ref, bd_ref, kd_ref, rd_ref, vd_ref, *, tt):
    ti = pl.program_id(1)
    upper_half = lax.broadcasted_iota(jnp.int32, (HALF_ROWS, LANES), 1) >= LANES // 2
    unroll = SCAN_SLOTS if tt % SCAN_SLOTS == 0 else 1

    def expand(pair):
        return jnp.concatenate([pair, pair], axis=0).T

    def prepare(slot, s, kk_pair=None):
        t1 = expand(wb_ref[0, s])
        wd_ref[slot] = t1[:HEAD_DIM]
        bd_ref[slot] = t1[HEAD_DIM:]
        t2 = expand(kr_ref[0, s])
        kd_ref[slot] = t2[:HEAD_DIM]
        rd_ref[slot] = t2[HEAD_DIM:]
        t3 = expand(kv_ref[0, s])
        vd_ref[slot] = jnp.where(upper_half, t3[HEAD_DIM + HALF_ROWS:], t3[HEAD_DIM:HEAD_DIM + HALF_ROWS])
        kkd_ref[slot] = t3[:HEAD_DIM] if kk_pair is None else expand(kk_pair)[:HEAD_DIM]

    prepare(0, 0)
    if tt > 1:
        prepare(1, 1)
    else:
        kkd_ref[1] = expand(kvnext_ref[0, 0])[:HEAD_DIM]

    def inv_norm2(kk_rows):
        s2 = jnp.sum(kk_rows * kk_rows, axis=0, keepdims=True)
        return 1.0 / jnp.maximum(s2, 1e-24)

    @pl.when(ti == 0)
    def _():
        s_ref[...] = s0_ref[0]
        acc = jnp.zeros((HALF_ROWS, LANES), F32)
        for j in range(HEAD_DIM):
            acc = acc + s0_ref[0, j] * kkd_ref[0, j:j + 1, :]
        sa_ref[...] = acc
        inv_ref[...] = inv_norm2(kkd_ref[0])

    def finish(y, cv):
        tot = jnp.sum(y, axis=0, keepdims=True)
        tot = tot + pltpu.roll(tot, LANES // 2, axis=1)
        d = y - tot * (1.0 / HEAD_DIM)
        sq = jnp.sum(d * d, axis=0, keepdims=True)
        sq = sq + pltpu.roll(sq, LANES // 2, axis=1)
        yn = d * lax.rsqrt(sq * (1.0 / HEAD_DIM) + LN_X_EPS)
        return yn * lnw_ref[...] + lnb_ref[...] + cv

    def step(t, u, carry):
        sa, inv2, y_prev, cv_prev = carry
        nxt_slot = (u + 1) % SCAN_SLOTS
        out_ref[0, jnp.maximum(t - 1, 0)] = finish(y_prev, cv_prev)
        sae = sa * (-inv2)
        anchor = lax.shift_right_logical(lax.shift_right_logical(lax.bitcast_convert_type(sae, jnp.uint32),
                                                                 jnp.uint32(16)), jnp.uint32(16))
        v = vd_ref[u] + lax.bitcast_convert_type(anchor, F32)
        acc_y = jnp.zeros((HALF_ROWS, LANES), F32)
        acc_s = jnp.zeros((HALF_ROWS, LANES), F32)
        for j in range(HEAD_DIM):
            row = pl.ds(j, 1)
            s_new = s_ref[j] * wd_ref[u, row, :] + sae * bd_ref[u, row, :] + v * kd_ref[u, row, :]
            s_ref[j] = s_new
            acc_y = acc_y + s_new * rd_ref[u, row, :]
            acc_s = acc_s + s_new * kkd_ref[nxt_slot, row, :]
        c = jnp.sum(rd_ref[u] * kd_ref[u] * rk_ref[...], axis=0, keepdims=True)
        if unroll > 1:
            ahead = jnp.minimum(t + 2, tt - 1)
            kk_pair = None
            if (u + 2) % SCAN_SLOTS == 0:
                kk_pair = jnp.where(t + 2 == tt, kvnext_ref[0, 0], kv_ref[0, ahead])
            prepare((u + 2) % SCAN_SLOTS, ahead, kk_pair)
        return acc_s, inv_norm2(kkd_ref[nxt_slot]), acc_y, c * v

    def steps(q, carry):
        for u in range(unroll):
            carry = step(q * unroll + u, u, carry)
        return carry

    zeros = jnp.zeros((HALF_ROWS, LANES), F32)
    sa, inv2, y_last, cv_last = lax.fori_loop(0, tt // unroll, steps, (sa_ref[...], inv_ref[...], zeros, zeros))
    out_ref[0, tt - 1] = finish(y_last, cv_last)
    sa_ref[...] = sa
    inv_ref[...] = inv2

    @pl.when(ti == pl.num_programs(1) - 1)
    def _():
        sfin_ref[0] = s_ref[...]


def _from_scan_rows(y):
    g, t = y.shape[:2]
    x = y.reshape(g, t, HALF_ROWS, 2, GROUP_BATCH, N_HEADS).transpose(0, 4, 1, 5, 3, 2)
    return x.reshape(g * GROUP_BATCH, t, D_A)


def _state_to_scan(s):
    g = s.shape[0] // GROUP_BATCH
    y = s.reshape(g, GROUP_BATCH, N_HEADS, 2, HALF_ROWS, HEAD_DIM).transpose(0, 5, 4, 3, 1, 2)
    return y.reshape(g, HEAD_DIM, HALF_ROWS, LANES)


def _state_from_scan(y):
    g = y.shape[0]
    s = y.reshape(g, HEAD_DIM, HALF_ROWS, 2, GROUP_BATCH, N_HEADS).transpose(0, 4, 5, 3, 2, 1)
    return s.reshape(g * GROUP_BATCH, N_HEADS, HEAD_DIM, HEAD_DIM)


def _head_rows(x):
    y = x.reshape(N_HEADS, 2, HALF_ROWS).transpose(2, 1, 0)
    y = jnp.broadcast_to(y[:, :, None, :], (HALF_ROWS, 2, GROUP_BATCH, N_HEADS))
    return y.reshape(HALF_ROWS, LANES)


def _head_keys(x):
    y = jnp.broadcast_to(x.T[:, None, None, :], (HEAD_DIM, 2, GROUP_BATCH, N_HEADS))
    return y.reshape(HEAD_DIM, LANES)


def _pairs_to_groups(x, t):
    nb = x.shape[0]
    g = nb // GROUP_BATCH
    y = x.reshape(g, GROUP_BATCH, t, N_HEADS, LANES).transpose(0, 2, 1, 3, 4)
    return y.reshape(g, t, GROUP_BATCH * N_HEADS, LANES)


def _wkv_scan(wb, kr, kv, s0, p, tt):
    g, t = wb.shape[:2]
    pair_tile = pl.BlockSpec((1, tt, HEAD_DIM, LANES), lambda gi, i: (gi, i, 0, 0))
    next_step = pl.BlockSpec((1, 1, HEAD_DIM, LANES), lambda gi, i: (gi, jnp.minimum((i + 1) * tt, t - 1), 0, 0))
    row_tile = pl.BlockSpec((1, tt, HALF_ROWS, LANES), lambda gi, i: (gi, i, 0, 0))
    state = pl.BlockSpec((1, HEAD_DIM, HALF_ROWS, LANES), lambda gi, i: (gi, 0, 0, 0))
    const = lambda n: pl.BlockSpec((n, LANES), lambda gi, i: (0, 0))
    keys = pltpu.VMEM((SCAN_SLOTS, HEAD_DIM, LANES), F32)
    out, sfin = pl.pallas_call(
        functools.partial(_scan_kernel, tt=tt),
        grid=(g, t // tt),
        in_specs=[pair_tile, pair_tile, pair_tile, next_step,
                  state, const(HALF_ROWS), const(HALF_ROWS), const(HEAD_DIM)],
        out_specs=[row_tile, state],
        out_shape=[jax.ShapeDtypeStruct((g, t, HALF_ROWS, LANES), F32),
                   jax.ShapeDtypeStruct((g, HEAD_DIM, HALF_ROWS, LANES), F32)],
        scratch_shapes=[pltpu.VMEM((HEAD_DIM, HALF_ROWS, LANES), F32),
                        pltpu.VMEM((HALF_ROWS, LANES), F32),
                        pltpu.VMEM((1, LANES), F32),
                        keys, keys, keys, keys, keys,
                        pltpu.VMEM((SCAN_SLOTS, HALF_ROWS, LANES), F32)],
        compiler_params=_cparams(("arbitrary", "arbitrary")),
        name="wkv_scan",
    )(wb, kr, kv, kv, _state_to_scan(s0),
      _head_rows(p['rk_lnx_w']), _head_rows(p['rk_lnx_b']), _head_keys(p['rk_r_k']))
    return _from_scan_rows(out), _state_from_scan(sfin)


def _gelu(x):
    return 0.5 * x * (1.0 + jnp.tanh(0.7978845608028654 * (x + 0.044715 * (x * x * x))))


def _lru_gates(xc, wri_ref, bri_ref, nsl_ref, precise):
    dot = _dot3 if precise else _dot
    gates = _sigmoid(dot(xc, wri_ref[...]) + bri_ref[...])
    gate_r = gates[:, :D_B]
    gate_i = gates[:, D_B:]
    log_a = gate_r * nsl_ref[...]
    a = jnp.exp(log_a)
    th = jnp.tanh(log_a)
    one_minus_a2 = -2.0 * th / (1.0 - th)
    bt = jnp.sqrt(one_minus_a2) * (gate_i * xc)
    return a, bt


def _lru_seq_math(pb, cw_ref, cb_ref, wri_ref, bri_ref, nsl_ref, ng_ref, out_ref, hlast_ref, xprev_ref, h_ref):
    tm = pb.shape[0]
    yb = pb[:, :D_B]
    xb = pb[:, D_B:]
    xprev = xprev_ref[...]
    row8 = lax.broadcasted_iota(jnp.int32, (SUBLANES, D_B), 0)

    def shifted(d):
        rolled = pltpu.roll(xb, d, axis=0)
        top = jnp.where(row8 < d, pltpu.roll(xprev, d, axis=0), rolled[:SUBLANES])
        return jnp.concatenate([top, rolled[SUBLANES:]], axis=0)

    xc = cb_ref[...] + cw_ref[3:4, :] * xb
    for d in range(1, CONV_W):
        xc = xc + cw_ref[3 - d:4 - d, :] * shifted(d)
    xprev_ref[...] = xb[tm - SUBLANES:, :]

    a, x = _lru_gates(xc, wri_ref, bri_ref, nsl_ref, False)
    row = lax.broadcasted_iota(jnp.int32, (tm, D_B), 0)
    d = 1
    while d < tm:
        if d < SUBLANES:
            keep = row >= d
            a_s = jnp.where(keep, pltpu.roll(a, d, axis=0), 1.0)
            x_s = jnp.where(keep, pltpu.roll(x, d, axis=0), 0.0)
            x = a * x_s + x
            a = a * a_s
        else:
            x = jnp.concatenate([x[:d], a[d:] * x[:tm - d] + x[d:]], axis=0)
            a = jnp.concatenate([a[:d], a[d:] * a[:tm - d]], axis=0)
        d *= 2
    h = a * h_ref[...] + x
    h_ref[...] = h[tm - 1:, :]
    hlast_ref[0] = h[tm - 1:, :]
    out_ref[0] = _rms(h * _gelu(yb), ng_ref[...]).astype(out_ref.dtype)


def _lru_params(p):
    eye = jnp.eye(N_BLOCKS_B, dtype=F32)
    bd = lambda w: (eye[:, None, :, None] * w[:, :, None, :]).reshape(D_B, D_B)
    wri = jnp.concatenate([bd(p['lru_w_r']), bd(p['lru_w_i'])], axis=1)
    bri = jnp.concatenate([p['lru_b_r'], p['lru_b_i']]).reshape(1, 2 * D_B)
    nsl = (-LRU_C * jax.nn.softplus(-p['lru_lambda'])).reshape(1, D_B)
    return wri, bri, nsl


def _front_kernel(x_ref, shift_ref, scale_ref, g1_ref, w_ref,
                  mu_ref, w0_ref, wup_ref, a0_ref, gup_ref, kk_ref, ka_ref,
                  cw_ref, cb_ref, wri_ref, bri_ref, nsl_ref, ng_ref,
                  wb_out, kr_out, kv_out, g_out, yb_out, hlast_out, patail_out, xtail_out,
                  carry_ref, stage_ref, xprev_ref, h_ref):
    @pl.when(pl.program_id(1) == 0)
    def _():
        carry_ref[...] = jnp.zeros_like(carry_ref)
        xprev_ref[...] = jnp.zeros_like(xprev_ref)
        h_ref[...] = jnp.zeros_like(h_ref)

    h = (_rms(x_ref[0], g1_ref[...]) * (1.0 + scale_ref[0]) + shift_ref[0]).astype(BF16)
    proj = jnp.dot(h, w_ref[...], preferred_element_type=F32)
    pa = proj[:, :N_COLS_A]
    pb = proj[:, N_COLS_A:]
    tm = pa.shape[0]
    patail_out[0] = pa[tm - SUBLANES:, :]
    xtail_out[0] = pb[tm - SUBLANES:, D_B:]

    rolled = pltpu.roll(pa, 1, axis=0)
    row = lax.broadcasted_iota(jnp.int32, pa.shape, 0)
    prev = jnp.where(row == 0, carry_ref[...], rolled)
    carry_ref[...] = pa[tm - 1:, :]
    _prep_math(pa, prev, mu_ref, w0_ref, wup_ref, a0_ref, gup_ref, kk_ref, ka_ref,
               wb_out, kr_out, kv_out, g_out, stage_ref)
    _lru_seq_math(pb, cw_ref, cb_ref, wri_ref, bri_ref, nsl_ref, ng_ref, yb_out, hlast_out, xprev_ref, h_ref)


def _front(x, shift, scale, mod_map, p, w, tm):
    nb, t, _ = x.shape
    assert nb == GROUP_BATCH
    wri, bri, nsl = _lru_params(p)
    consts = ((p['norm1_g'].reshape(1, D_MODEL), w) + _prep_params(p)
              + (p['lru_conv_w'], p['lru_conv_b'].reshape(1, D_B), wri.astype(BF16), bri, nsl,
                 p['lru_norm_g'].reshape(1, D_B)))
    mod_block = (1,) + shift.shape[1:]
    tile = lambda n: pl.BlockSpec((1, tm, n), lambda b, i: (b, i, 0))
    tail = lambda n: pl.BlockSpec((1, SUBLANES, n), lambda b, i: (b, 0, 0))
    pair_tile = pl.BlockSpec((tm, N_HEADS, LANES), lambda b, i: (i, b, 0))
    pair_shape = jax.ShapeDtypeStruct((t, nb * N_HEADS, LANES), F32)
    return pl.pallas_call(
        _front_kernel,
        grid=(nb, t // tm),
        in_specs=[tile(D_MODEL),
                  pl.BlockSpec(mod_block, functools.partial(mod_map, 0)),
                  pl.BlockSpec(mod_block, functools.partial(mod_map, 1))] + [_const_spec(a) for a in consts],
        out_specs=[pair_tile] * 3 + [tile(D_A), tile(D_B), pl.BlockSpec((1, 1, D_B), lambda b, i: (b, 0, 0)),
                                     tail(N_COLS_A), tail(D_B)],
        out_shape=[pair_shape] * 3 + [jax.ShapeDtypeStruct((nb, t, D_A), F32),
                                      jax.ShapeDtypeStruct((nb, t, D_B), BF16),
                                      jax.ShapeDtypeStruct((nb, 1, D_B), F32),
                                      jax.ShapeDtypeStruct((nb, SUBLANES, N_COLS_A), F32),
                                      jax.ShapeDtypeStruct((nb, SUBLANES, D_B), F32)],
        scratch_shapes=[pltpu.VMEM((1, N_COLS_A), F32), pltpu.VMEM((tm * N_HEADS, LANES), F32),
                        pltpu.VMEM((SUBLANES, D_B), F32), pltpu.VMEM((1, D_B), F32)],
        compiler_params=_cparams(("arbitrary", "arbitrary")),
        name="prompt_front",
    )(x, shift, scale, *consts)


def _lru_step_kernel(pb_ref, conv_ref, h0_ref, cw_ref, cb_ref, wri_ref, bri_ref, nsl_ref, ng_ref,
                     out_ref, hnew_ref):
    pb = pb_ref[...]
    yb = pb[:, :D_B]
    xb = pb[:, D_B:]
    xc = cb_ref[...] + cw_ref[3:4, :] * xb
    for j in range(CONV_W - 1):
        xc = xc + cw_ref[j:j + 1, :] * conv_ref[j]
    a, x = _lru_gates(xc, wri_ref, bri_ref, nsl_ref, True)
    h = a * h0_ref[...] + x
    hnew_ref[...] = h
    out_ref[...] = _rms(h * _gelu(yb), ng_ref[...]).astype(out_ref.dtype)


def _lru_step(pb, conv0, h0, p):
    n = pb.shape[0]
    wri, bri, nsl = _lru_params(p)
    return pl.pallas_call(
        _lru_step_kernel,
        out_shape=[jax.ShapeDtypeStruct((n, D_B), BF16), jax.ShapeDtypeStruct((n, D_B), F32)],
        compiler_params=pltpu.CompilerParams(vmem_limit_bytes=VMEM_LIMIT),
        name="rglru_step",
    )(pb, conv0, h0, p['lru_conv_w'], p['lru_conv_b'].reshape(1, D_B), wri, bri, nsl,
      p['lru_norm_g'].reshape(1, D_B))


def _to_token_tiles(ref, x):
    rows = x.shape[0]
    for c in range(TILE_CHUNKS):
        ref[pl.ds(c, rows, stride=TILE_CHUNKS), :] = x[:, c * LANES:(c + 1) * LANES]


def _from_token_tiles(ref, row0, rows):
    return jnp.concatenate(
        [ref[pl.ds(row0 + c, rows, stride=TILE_CHUNKS), :] for c in range(TILE_CHUNKS)], axis=1)


def _post_kernel(x_ref, wkv_ref, g_ref, yb_ref, gate1_ref, shift2_ref, scale2_ref, n2_ref,
                 wo_ref, rw_ref, rb_ref, tri_ref, x1_ref, h2_ref, ti_ref, rk_ref, tg_ref, cnt_ref):
    ya = (wkv_ref[0] * g_ref[0]).astype(BF16)
    mixed = (jnp.dot(ya, wo_ref[:D_A, :], preferred_element_type=F32)
             + jnp.dot(yb_ref[0], wo_ref[D_A:, :], preferred_element_type=F32))
    x1 = x_ref[0] + gate1_ref[0] * mixed
    x1_ref[0] = x1
    h2 = _rms(x1, n2_ref[...]) * (1.0 + scale2_ref[0]) + shift2_ref[0]
    _to_token_tiles(h2_ref, h2)
    logits = _dot3_nt(rw_ref[...], h2) + rb_ref[...]
    eidx = lax.broadcasted_iota(jnp.int32, logits.shape, 0)
    vals, idxs = [], []
    cur = logits
    for _ in range(TOP_K):
        m = jnp.max(cur, axis=0, keepdims=True)
        i = jnp.min(jnp.where(cur == m, eidx, N_EXPERTS), axis=0, keepdims=True)
        vals.append(m)
        idxs.append(i)
        cur = jnp.where(eidx == i, -jnp.inf, cur)
    ex = [jnp.exp(v - vals[0]) for v in vals]
    den = ex[0] + ex[1] + ex[2] + ex[3]
    sel = [eidx == i for i in idxs]
    onehot = (sel[0] | sel[1] | sel[2] | sel[3]).astype(F32)
    incl = jnp.dot(onehot.astype(BF16), tri_ref[...], preferred_element_type=F32)
    rank = incl - onehot
    cnt_ref[0] = jnp.broadcast_to(jnp.sum(onehot, axis=1, keepdims=True), cnt_ref.shape[1:])
    for k in range(TOP_K):
        ti_ref[0, k:k + 1, :] = idxs[k]
        rk_ref[0, k:k + 1, :] = jnp.sum(jnp.where(sel[k], rank, 0.0), axis=0, keepdims=True).astype(jnp.int32)
        tg_ref[0, k:k + 1, :] = ex[k] / den


def _post(x, wkv, g, yb, gate1, shift2, scale2, mod_map, p, tm):
    nb, t, _ = x.shape
    mod_block = (1,) + gate1.shape[1:]
    tile = lambda n: pl.BlockSpec((1, tm, n), lambda b, i: (b, i, 0))
    full = lambda a: pl.BlockSpec(a.shape, lambda b, i: (0,) * a.ndim)
    mspec = lambda j: pl.BlockSpec(mod_block, functools.partial(mod_map, j))
    n2 = p['norm2_g'].reshape(1, D_MODEL)
    wo = p['w_out'].astype(BF16)
    rw = p['router_w'].T
    rb = p['router_b'].reshape(N_EXPERTS, 1)
    tri = jnp.triu(jnp.ones((tm, tm), BF16))
    nt = t // tm
    n = nb * t
    topk = pl.BlockSpec((1, TOP_K, tm), lambda b, i: (b * nt + i, 0, 0))
    topk_i = jax.ShapeDtypeStruct((nb * nt, TOP_K, tm), jnp.int32)
    x1, h2, ti, rk, tg, cnt = pl.pallas_call(
        _post_kernel,
        grid=(nb, nt),
        in_specs=[tile(D_MODEL), tile(D_A), tile(D_A), tile(D_B), mspec(2), mspec(3), mspec(4),
                  full(n2), full(wo), full(rw), full(rb), full(tri)],
        out_specs=[tile(D_MODEL),
                   pl.BlockSpec((tm * TILE_CHUNKS, LANES), lambda b, i: (b * nt + i, 0)),
                   topk, topk, topk,
                   pl.BlockSpec((1, N_EXPERTS, LANES), lambda b, i: (b * nt + i, 0, 0))],
        out_shape=[jax.ShapeDtypeStruct((nb, t, D_MODEL), F32),
                   jax.ShapeDtypeStruct((n * TILE_CHUNKS, LANES), F32),
                   topk_i, topk_i,
                   jax.ShapeDtypeStruct((nb * nt, TOP_K, tm), F32),
                   jax.ShapeDtypeStruct((nb * nt, N_EXPERTS, LANES), F32)],
        compiler_params=_cparams(("arbitrary", "arbitrary")),
        name="outproj_router",
    )(x, wkv, g, yb, gate1, shift2, scale2, n2, wo, rw, rb, tri)
    return x1, h2, ti, rk, tg, cnt[:, :, 0].astype(jnp.int32)


def _plan(routes):
    cnt = jnp.concatenate([r[2] for r in routes], axis=0)
    total = jnp.sum(cnt, axis=0)
    run_start = jnp.cumsum(cnt, axis=0) - cnt
    padded = (total + MOE_ROWS - 1) // MOE_ROWS * MOE_ROWS
    pad_end = jnp.cumsum(padded)
    pad_start = pad_end - padded
    loc_off = jnp.cumsum(cnt, axis=1) - cnt
    dst0 = pad_start[None, :] + run_start
    n_tokens = sum(r[0].shape[0] * r[0].shape[2] for r in routes)
    n_blocks = -(-n_tokens * TOP_K // MOE_ROWS) + N_EXPERTS
    block_row0 = jnp.arange(n_blocks, dtype=jnp.int32) * MOE_ROWS
    block_e = jnp.minimum(jnp.sum(pad_end[None, :] <= block_row0[:, None], axis=1), N_EXPERTS - 1)
    n_used = (pad_end[-1] // MOE_ROWS).reshape(1)
    first_total = jnp.sum(routes[0][2], axis=0)
    zero_table = jnp.stack([pad_end - (pad_start + first_total), jnp.zeros_like(pad_end), pad_start + first_total])
    as_i32 = lambda a: a.astype(jnp.int32)
    experts = jnp.arange(N_EXPERTS, dtype=jnp.int32)
    groups, t0 = [], 0
    for ti, rk, c in routes:
        nt = ti.shape[0]
        sl = slice(t0, t0 + nt)
        off = jnp.sum(jnp.where(ti[..., None] == experts, loc_off[sl, None, None, :], 0), axis=-1)
        lpos = as_i32((rk + off) * TILE_CHUNKS).reshape(nt, -1)
        table = as_i32(jnp.stack([cnt[sl], loc_off[sl], dst0[sl]], axis=0).reshape(3, -1))
        groups.append((lpos, table))
        t0 += nt
    return groups, as_i32(block_e), as_i32(n_used), as_i32(pad_end), as_i32(zero_table), as_i32(padded), n_blocks


def _tile_rows(row):
    return pl.ds(pl.multiple_of(row * TILE_CHUNKS, TILE_CHUNKS), TILE_CHUNKS)


def _sublane_rows(row8):
    return pl.ds(pl.multiple_of(row8, TILE_CHUNKS), TILE_CHUNKS)


def _piece_rows(row0, n_rows):
    return pl.ds(pl.multiple_of(row0 * TILE_CHUNKS, TILE_CHUNKS), n_rows * TILE_CHUNKS)


def _block_rows(block):
    rows = MOE_ROWS * TILE_CHUNKS
    return pl.ds(pl.multiple_of(block * rows, rows), rows)


def _for_each_piece(table_ref, tile, n_tiles, fn):
    def per_expert(e, queue):
        col = tile * N_EXPERTS + e
        cnt = table_ref[col]
        lo = table_ref[n_tiles * N_EXPERTS + col]
        d0 = table_ref[2 * n_tiles * N_EXPERTS + col]

        def full(c, c2):
            fn(lo + c * MOE_CHUNK, d0 + c * MOE_CHUNK, MOE_CHUNK, queue)
            return c2
        lax.fori_loop(0, cnt // MOE_CHUNK, full, 0)
        bit = MOE_CHUNK // 2
        while bit >= 1:
            done = cnt - cnt % (2 * bit)

            @pl.when((cnt & bit) != 0)
            def _(done=done, bit=bit):
                fn(lo + done, d0 + done, bit, queue)
            bit //= 2

    def per_pair(p, carry):
        for queue in range(2):
            per_expert(2 * p + queue, queue)
        return carry
    lax.fori_loop(0, N_EXPERTS // 2, per_pair, 0)


def _scatter_kernel(*refs, first):
    idx_ref, loc_ref, zero_ref, sem, isem, zsem = refs[-6:]
    table_ref, pad_end_ref, ztab_ref, h2_ref, lpos_hbm = refs[:5]
    xs_hbm = refs[-7]
    i = pl.program_id(0)
    n_tiles = pl.num_programs(0)
    tm = h2_ref.shape[0] // TILE_CHUNKS
    slot = i % 2

    def idx_copy(tile, s):
        dst = idx_ref.at[pl.ds(s * (TOP_K * tm), TOP_K * tm)]
        return pltpu.make_async_copy(lpos_hbm.at[tile], dst, isem.at[s])

    def drain(s):
        for _ in range(TOP_K):
            pltpu.make_async_copy(h2_ref, xs_hbm.at[pl.ds(0, tm * TILE_CHUNKS), :], sem.at[s]).wait()

    @pl.when(i == 0)
    def _():
        idx_copy(0, 0).start()

    if first:
        @pl.when(i == 0)
        def _():
            zero_ref[...] = jnp.zeros_like(zero_ref)
            n_used = pad_end_ref[N_EXPERTS - 1] // MOE_ROWS
            n_blocks = xs_hbm.shape[0] // (MOE_ROWS * TILE_CHUNKS)

            def zero_piece(lo, d0, n):
                return pltpu.make_async_copy(zero_ref.at[_piece_rows(0, n), :], xs_hbm.at[_piece_rows(d0, n), :], zsem)
            _for_each_piece(ztab_ref, 0, 1, lambda lo, d0, n, q: zero_piece(lo, d0, n).start(priority=q))

            def fill_tail(blk, c):
                pltpu.make_async_copy(zero_ref, xs_hbm.at[_block_rows(blk), :], zsem).start()
                return c
            lax.fori_loop(n_used, n_blocks, fill_tail, 0)

            def drain_fill(j, c):
                pltpu.make_async_copy(zero_ref, xs_hbm.at[_block_rows(0), :], zsem).wait()
                return c
            lax.fori_loop(0, n_blocks - n_used, drain_fill, 0)
            _for_each_piece(ztab_ref, 0, 1, lambda lo, d0, n, q: zero_piece(lo, d0, n).wait())

    def step(s):
        @pl.when(i + 1 < n_tiles)
        def _():
            idx_copy(i + 1, 1 - s).start()

        @pl.when(i >= 2)
        def _():
            drain(s)
        idx_copy(i, s).wait()
        base = s * (TOP_K * tm)

        def place(r, c):
            row = h2_ref[_tile_rows(r), :]
            for k in range(TOP_K):
                loc_ref[s, _sublane_rows(idx_ref[base + k * tm + r]), :] = row
            return c
        lax.fori_loop(0, tm, place, 0, unroll=8)

        def send(lo, d0, n, q):
            pltpu.make_async_copy(loc_ref.at[s, _piece_rows(lo, n), :], xs_hbm.at[_piece_rows(d0, n), :],
                                  sem.at[s]).start(priority=q)
        _for_each_piece(table_ref, i, n_tiles, send)

        @pl.when(i == n_tiles - 1)
        def _():
            @pl.when(i >= 1)
            def _():
                drain(1 - s)
            drain(s)

    for s in range(2):
        pl.when(slot == s)(functools.partial(step, s))


def _scatter(h2, lpos, table, pad_end, zero_table, n_rows, tm, xs=None):
    nt = lpos.shape[0]
    first = xs is None
    loc_rows = TOP_K * tm * TILE_CHUNKS
    in_specs = [pl.BlockSpec((tm * TILE_CHUNKS, LANES), lambda i, *_: (i, 0)),
                pl.BlockSpec(memory_space=pl.ANY)]
    args = [table.reshape(-1), pad_end, zero_table.reshape(-1), h2, lpos]
    if not first:
        in_specs.append(pl.BlockSpec(memory_space=pl.ANY))
        args.append(xs)
    grid_spec = pltpu.PrefetchScalarGridSpec(
        num_scalar_prefetch=3,
        grid=(nt,),
        in_specs=in_specs,
        out_specs=pl.BlockSpec(memory_space=pl.ANY),
        scratch_shapes=[pltpu.SMEM((2 * TOP_K * tm,), jnp.int32),
                        pltpu.VMEM((2, loc_rows, LANES), F32),
                        pltpu.VMEM((MOE_ROWS * TILE_CHUNKS, LANES), F32),
                        pltpu.SemaphoreType.DMA((2,)), pltpu.SemaphoreType.DMA((2,)), pltpu.SemaphoreType.DMA],
    )
    return pl.pallas_call(
        functools.partial(_scatter_kernel, first=first),
        grid_spec=grid_spec,
        out_shape=jax.ShapeDtypeStruct((n_rows * TILE_CHUNKS, LANES), F32),
        input_output_aliases={} if first else {5: 0},
        compiler_params=_cparams(("arbitrary",)),
        name="moe_dispatch",
    )(*args)


def _mlp_kernel(be_ref, nu_ref, next_ref, par_ref, xs_ref, bias_ref, wg_hbm, wu_hbm, wd_hbm, out_ref,
                w32_ref, w16_ref, sem):
    i = pl.program_id(0)
    used = i < nu_ref[0]
    e = be_ref[i]

    def weight_copies(expert, s):
        return [pltpu.make_async_copy(w_hbm.at[expert], w32_ref.at[s, j], sem.at[s])
                for j, w_hbm in enumerate((wg_hbm, wu_hbm, wd_hbm))]

    def load_expert(s):
        @pl.when(i == 0)
        def _():
            for c in weight_copies(e, s):
                c.start()
        for c in weight_copies(e, s):
            c.wait()
        nxt = next_ref[e]

        @pl.when(nxt >= 0)
        def _():
            for c in weight_copies(nxt, 1 - s):
                c.start(priority=1)
        for j in range(3):
            w16_ref[j] = w32_ref[s, j].astype(BF16)

    first_of_expert = used & ((i == 0) | (e != be_ref[jnp.maximum(i - 1, 0)]))
    for s in range(2):
        pl.when(first_of_expert & (par_ref[e] == s))(functools.partial(load_expert, s))

    @pl.when(used)
    def _():
        x = _from_token_tiles(xs_ref, 0, MOE_ROWS).astype(BF16)
        gt = jnp.dot(x, w16_ref[0], preferred_element_type=F32) + bias_ref[0, 0:1, :]
        up = jnp.dot(x, w16_ref[1], preferred_element_type=F32) + bias_ref[0, 1:2, :]
        gt = jnp.minimum(gt, SWIGLU_LIMIT)
        up = jnp.clip(up, -SWIGLU_LIMIT, SWIGLU_LIMIT)
        glu = gt * _sigmoid(gt * SWIGLU_ALPHA)
        mid = ((up + 1.0) * glu).astype(BF16)
        _to_token_tiles(out_ref, jnp.dot(mid, w16_ref[2], preferred_element_type=F32) + bias_ref[0, 2:3, :])

    @pl.when(jnp.logical_not(used))
    def _():
        out_ref[...] = jnp.zeros_like(out_ref)


def _mlp(xs, block_e, n_used, padded, n_blocks, wts):
    wg, bg, wu, bu, wd, bd = wts
    assert D_FF == D_MODEL
    bias = jnp.stack([bg, bu, bd], axis=1)
    nonempty = padded > 0
    experts = jnp.arange(N_EXPERTS, dtype=jnp.int32)
    parity = ((jnp.cumsum(nonempty) - nonempty) % 2).astype(jnp.int32)
    later = nonempty[None, :] & (experts[None, :] > experts[:, None])
    nxt = jnp.min(jnp.where(later, experts[None, :], N_EXPERTS), axis=1)
    nxt = jnp.where(nxt == N_EXPERTS, -1, nxt).astype(jnp.int32)
    rows = pl.BlockSpec((MOE_ROWS * TILE_CHUNKS, LANES), lambda i, *_: (i, 0))
    hbm = pl.BlockSpec(memory_space=pl.ANY)
    grid_spec = pltpu.PrefetchScalarGridSpec(
        num_scalar_prefetch=4,
        grid=(n_blocks,),
        in_specs=[rows, pl.BlockSpec((1, 3, D_FF), lambda i, be, *_: (be[i], 0, 0)), hbm, hbm, hbm],
        out_specs=rows,
        scratch_shapes=[pltpu.VMEM((2, 3, D_MODEL, D_FF), F32),
                        pltpu.VMEM((3, D_MODEL, D_FF), BF16),
                        pltpu.SemaphoreType.DMA((2,))],
    )
    return pl.pallas_call(
        _mlp_kernel,
        grid_spec=grid_spec,
        out_shape=jax.ShapeDtypeStruct(xs.shape, F32),
        compiler_params=_cparams(("arbitrary",)),
        name="moe_experts",
    )(block_e, n_used, nxt, parity, xs, bias, wg, wu, wd)


def _gather_kernel(table_ref, x1_ref, gate2_ref, fg_ref, lpos_hbm, tg_hbm, rows_hbm, y_ref,
                   idx_ref, gsm_ref, loc_ref, ff_ref, sem, isem):
    i = pl.program_id(1) + pl.program_id(0) * pl.num_programs(1)
    n_tiles = pl.num_programs(0) * pl.num_programs(1)
    tm = x1_ref.shape[1]
    slot = i % 2

    def meta_copies(tile, s):
        seg = pl.ds(s * (TOP_K * tm), TOP_K * tm)
        return (pltpu.make_async_copy(lpos_hbm.at[tile], idx_ref.at[seg], isem.at[s]),
                pltpu.make_async_copy(tg_hbm.at[tile], gsm_ref.at[seg], isem.at[s]))

    def fetch_tile(tile, s):
        for c in meta_copies(tile, s):
            c.start()

        def fetch(lo, d0, n, q):
            pltpu.make_async_copy(rows_hbm.at[_piece_rows(d0, n), :], loc_ref.at[s, _piece_rows(lo, n), :],
                                  sem.at[s]).start(priority=q)
        _for_each_piece(table_ref, tile, n_tiles, fetch)

    @pl.when(i == 0)
    def _():
        fetch_tile(0, 0)

    def step(s):
        @pl.when(i + 1 < n_tiles)
        def _():
            fetch_tile(i + 1, 1 - s)

        pltpu.make_async_copy(rows_hbm.at[pl.ds(0, TOP_K * tm * TILE_CHUNKS), :], loc_ref.at[s], sem.at[s]).wait()
        for c in meta_copies(i, s):
            c.wait()
        base = s * (TOP_K * tm)

        def mix(r, c):
            acc = gsm_ref[base + r] * loc_ref[s, _sublane_rows(idx_ref[base + r]), :]
            for k in range(1, TOP_K):
                acc = acc + gsm_ref[base + k * tm + r] * loc_ref[s, _sublane_rows(idx_ref[base + k * tm + r]), :]
            ff_ref[_tile_rows(r), :] = acc
            return c
        lax.fori_loop(0, tm, mix, 0, unroll=8)

    for s in range(2):
        pl.when(slot == s)(functools.partial(step, s))

    x2 = x1_ref[0] + gate2_ref[0] * _from_token_tiles(ff_ref, 0, tm)
    y_ref[0] = _rms(x2, fg_ref[...])


def _gather(x1, tg, gate2, mod_map, final_g, lpos, table, rows, tm):
    nb, t, _ = x1.shape
    nt = t // tm
    mod_block = (1,) + gate2.shape[1:]
    loc_rows = TOP_K * tm * TILE_CHUNKS
    grid_spec = pltpu.PrefetchScalarGridSpec(
        num_scalar_prefetch=1,
        grid=(nb, nt),
        in_specs=[pl.BlockSpec((1, tm, D_MODEL), lambda b, i, *_: (b, i, 0)),
                  pl.BlockSpec(mod_block, lambda b, i, *_: mod_map(5, b, i)),
                  pl.BlockSpec((1, D_MODEL), lambda b, i, *_: (0, 0)),
                  pl.BlockSpec(memory_space=pl.ANY),
                  pl.BlockSpec(memory_space=pl.ANY),
                  pl.BlockSpec(memory_space=pl.ANY)],
        out_specs=pl.BlockSpec((1, tm, D_MODEL), lambda b, i, *_: (b, i, 0)),
        scratch_shapes=[pltpu.SMEM((2 * TOP_K * tm,), jnp.int32),
                        pltpu.SMEM((2 * TOP_K * tm,), F32),
                        pltpu.VMEM((2, loc_rows, LANES), F32),
                        pltpu.VMEM((tm * TILE_CHUNKS, LANES), F32),
                        pltpu.SemaphoreType.DMA((2,)), pltpu.SemaphoreType.DMA((2,))],
    )
    return pl.pallas_call(
        _gather_kernel,
        grid_spec=grid_spec,
        out_shape=jax.ShapeDtypeStruct((nb, t, D_MODEL), F32),
        compiler_params=_cparams(("arbitrary", "arbitrary")),
        name="moe_combine",
    )(table.reshape(-1), x1, gate2, final_g.reshape(1, D_MODEL), lpos, tg.reshape(tg.shape[0], -1), rows)


def _moe(route_p, route_s, mod_p, mod_s, final_g, wts):
    routes = [route_p, route_s]
    groups, block_e, n_used, pad_end, zero_table, padded, n_blocks = _plan([(r[2], r[3], r[5]) for r in routes])
    xs = None
    for (x1, h2, ti, rk, tg, cnt), (lpos, table) in zip(routes, groups):
        xs = _scatter(h2, lpos, table, pad_end, zero_table, n_blocks * MOE_ROWS, ti.shape[2], xs)
    rows = _mlp(xs, block_e, n_used, padded, n_blocks, wts)
    outs = []
    for (x1, h2, ti, rk, tg, cnt), (lpos, table), (mod, mod_map) in zip(routes, groups, (mod_p, mod_s)):
        outs.append(_gather(x1, tg, mod, mod_map, final_g, lpos, table, rows, ti.shape[2]))
    return outs


def _forward(x_prompt, x_sample, c_prompt, c_sample, state_wkv, state_shift, state_conv, state_lru, p, final_g):
    bp, tp, _ = x_prompt.shape
    bs = x_sample.shape[0]
    tm = min(512, tp)
    tt = min(128, tp)

    mod = _ada(jnp.concatenate([c_prompt, c_sample], axis=0), p['w_ada'], p['b_ada'])
    mod_p = mod[:bp].reshape(bp * N_MOD, 1, D_MODEL)
    mod_s = mod[bp:].reshape(bs, N_MOD, D_MODEL).transpose(1, 0, 2)
    map_p = lambda j, b, i: (b * N_MOD + j, 0, 0)
    map_s = lambda j, b, i: (j, 0, 0)

    wts = (p['w_gate'], p['b_gate'], p['w_up'], p['b_up'], p['w_down'], p['b_down'])

    wb_p, kr_p, kv_p, g, yb, lru_p, pa_tail, x_tail = _front(x_prompt, mod_p, mod_p, map_p, p,
                                                             p['w_in'].astype(BF16), tm)
    s0 = jnp.zeros((bp, N_HEADS, HEAD_DIM, HEAD_DIM), F32)
    wkv_out, wkv_p = _wkv_scan(wb_p[None], kr_p[None], kv_p[None], s0, p, tt)
    route_p = _post(x_prompt, wkv_out, g, yb, mod_p, mod_p, mod_p, map_p, p, tm)
    shift_p = pa_tail[:, -1, :]
    conv_p = x_tail[:, SUBLANES - (CONV_W - 1):, :]

    xs = x_sample.reshape(1, bs, D_MODEL)
    pa_s, pb_s = _inproj(xs, mod_s, mod_s, map_s, p['norm1_g'], p['w_in'], bs)
    wb_s, kr_s, kv_s, g = _prep(pa_s, state_shift.reshape(1, bs, N_COLS_A), p, bs)
    as_seq = lambda a: _pairs_to_groups(a.reshape(bs, N_HEADS, LANES), 1)
    wkv_out, wkv_s = _wkv_scan(as_seq(wb_s), as_seq(kr_s), as_seq(kv_s), state_wkv, p, 1)
    conv0 = state_conv.transpose(1, 0, 2)
    yb, lru_s = _lru_step(pb_s[0], conv0, state_lru, p)
    route_s = _post(xs, wkv_out.reshape(1, bs, D_A), g, yb.reshape(1, bs, D_B),
                    mod_s, mod_s, mod_s, map_s, p, bs)
    y_prompt, y_sample = _moe(route_p, route_s, (mod_p, map_p), (mod_s, map_s), final_g, wts)
    shift_s = pa_s[0]
    conv_s = jnp.concatenate([state_conv[:, 1:], pb_s[0][:, None, D_B:]], axis=1)

    return (y_prompt, y_sample.reshape(bs, 1, D_MODEL),
            wkv_p[None], shift_p[None], conv_p[None], lru_p.reshape(bp, D_B)[None],
            wkv_s[None], shift_s[None], conv_s[None], lru_s[None])


def kernel(x_prompt, x_sample, c_prompt, c_sample, state_wkv, state_shift, state_conv, state_lru, w_ada, b_ada, norm1_g, norm2_g, w_in, rk_mu, rk_w0, rk_w_up, rk_a0, rk_a_up, rk_g_up, rk_k_k, rk_k_a, rk_r_k, rk_lnx_w, rk_lnx_b, lru_conv_w, lru_conv_b, lru_w_r, lru_b_r, lru_w_i, lru_b_i, lru_lambda, lru_norm_g, w_out, router_w, router_b, w_gate, b_gate, w_up, b_up, w_down, b_down, final_g):
    assert w_ada.shape[0] == 1, "single-layer trunk"
    p = dict(w_ada=w_ada[0], b_ada=b_ada[0], norm1_g=norm1_g[0], norm2_g=norm2_g[0], w_in=w_in[0],
             rk_mu=rk_mu[0], rk_w0=rk_w0[0], rk_w_up=rk_w_up[0], rk_a0=rk_a0[0], rk_a_up=rk_a_up[0],
             rk_g_up=rk_g_up[0], rk_k_k=rk_k_k[0], rk_k_a=rk_k_a[0], rk_r_k=rk_r_k[0],
             rk_lnx_w=rk_lnx_w[0], rk_lnx_b=rk_lnx_b[0], lru_conv_w=lru_conv_w[0],
             lru_conv_b=lru_conv_b[0], lru_w_r=lru_w_r[0], lru_b_r=lru_b_r[0], lru_w_i=lru_w_i[0],
             lru_b_i=lru_b_i[0], lru_lambda=lru_lambda[0], lru_norm_g=lru_norm_g[0], w_out=w_out[0],
             router_w=router_w[0], router_b=router_b[0], w_gate=w_gate[0], b_gate=b_gate[0],
             w_up=w_up[0], b_up=b_up[0], w_down=w_down[0], b_down=b_down[0])
    return _forward(x_prompt, x_sample, c_prompt, c_sample, state_wkv[0], state_shift[0], state_conv[0],
                    state_lru[0], p, final_g)
```

```python
import functools

import jax
import jax.numpy as jnp
from jax import lax
from jax.experimental import pallas as pl
from jax.experimental.pallas import tpu as pltpu

F32 = jnp.float32
BF16 = jnp.bfloat16

D_MODEL = 1024
D_A = 512
HEAD_DIM = 64
N_HEADS = 8
D_B = 512
N_BLOCKS_B = 8
CONV_W = 4
LRU_C = 8.0
R_DECAY = 64
R_AAA = 64
R_GATE = 128
N_COLS_A = 3 * D_A + R_DECAY + R_AAA + R_GATE
N_COLS_B = 2 * D_B
N_EXPERTS = 32
TOP_K = 4
D_FF = 1024
SWIGLU_LIMIT = 7.0
SWIGLU_ALPHA = 1.702
RMS_EPS = 1e-6
LN_X_EPS = 64e-5
N_MOD = 6

LANES = 128
SUBLANES = 8
GROUP_BATCH = 8
HALF_ROWS = HEAD_DIM // 2
MOE_ROWS = 256
MOE_CHUNK = 16
TILE_CHUNKS = D_MODEL // LANES
VMEM_LIMIT = 56 * 1024 * 1024


def _cparams(sem):
    return pltpu.CompilerParams(dimension_semantics=sem, vmem_limit_bytes=VMEM_LIMIT)


def _dot(a, b):
    return jnp.dot(a.astype(BF16), b.astype(BF16), preferred_element_type=F32)


def _split(a):
    hi = a.astype(BF16)
    lo = (a - hi.astype(F32)).astype(BF16)
    return hi, lo


def _dot3(a, b):
    ah, al = _split(a)
    bh, bl = _split(b)
    return (jnp.dot(ah, bh, preferred_element_type=F32)
            + (jnp.dot(al, bh, preferred_element_type=F32) + jnp.dot(ah, bl, preferred_element_type=F32)))


def _dot3_nt(a, b):
    dn = (((1,), (1,)), ((), ()))
    ah, al = _split(a)
    bh, bl = _split(b)
    d = lambda x, y: lax.dot_general(x, y, dn, preferred_element_type=F32)
    return d(ah, bh) + (d(al, bh) + d(ah, bl))


def _softplus(x):
    return jnp.maximum(x, 0.0) + jnp.log1p(jnp.exp(-jnp.abs(x)))


def _sigmoid(x):
    return 1.0 / (1.0 + jnp.exp(-x))


def _rms(x, g):
    ms = jnp.mean(x * x, axis=-1, keepdims=True)
    return x * lax.rsqrt(ms + RMS_EPS) * g


def _ada_kernel(c_ref, w_ref, b_ref, o_ref):
    c = c_ref[...]
    s = c * _sigmoid(c)
    o_ref[...] = _dot3(s, w_ref[...]) + b_ref[...]


def _ada(c, w_ada, b_ada):
    rows = c.shape[0]
    ncol = w_ada.shape[1]
    tn = D_MODEL
    return pl.pallas_call(
        _ada_kernel,
        grid=(ncol // tn,),
        in_specs=[pl.BlockSpec((rows, D_MODEL), lambda j: (0, 0)),
                  pl.BlockSpec((D_MODEL, tn), lambda j: (0, j)),
                  pl.BlockSpec((1, tn), lambda j: (0, j))],
        out_specs=pl.BlockSpec((rows, tn), lambda j: (0, j)),
        out_shape=jax.ShapeDtypeStruct((rows, ncol), F32),
        compiler_params=_cparams(("arbitrary",)),
        name="ada_mod",
    )(c, w_ada, b_ada.reshape(1, ncol))


def _inproj_kernel(x_ref, shift_ref, scale_ref, g_ref, w_ref, pa_ref, pb_ref):
    x = x_ref[0]
    h = _rms(x, g_ref[...]) * (1.0 + scale_ref[0]) + shift_ref[0]
    proj = _dot3(h, w_ref[...])
    pa_ref[0] = proj[:, :N_COLS_A]
    pb_ref[0] = proj[:, N_COLS_A:]


def _inproj(x, shift, scale, mod_map, g, w, tm):
    nb, t, _ = x.shape
    mod_block = (1,) + shift.shape[1:]
    return pl.pallas_call(
        _inproj_kernel,
        grid=(nb, t // tm),
        in_specs=[pl.BlockSpec((1, tm, D_MODEL), lambda b, i: (b, i, 0)),
                  pl.BlockSpec(mod_block, functools.partial(mod_map, 0)),
                  pl.BlockSpec(mod_block, functools.partial(mod_map, 1)),
                  pl.BlockSpec((1, D_MODEL), lambda b, i: (0, 0)),
                  pl.BlockSpec((D_MODEL, N_COLS_A + N_COLS_B), lambda b, i: (0, 0))],
        out_specs=[pl.BlockSpec((1, tm, N_COLS_A), lambda b, i: (b, i, 0)),
                   pl.BlockSpec((1, tm, N_COLS_B), lambda b, i: (b, i, 0))],
        out_shape=[jax.ShapeDtypeStruct((nb, t, N_COLS_A), F32),
                   jax.ShapeDtypeStruct((nb, t, N_COLS_B), F32)],
        compiler_params=_cparams(("arbitrary", "arbitrary")),
        name="norm1_inproj",
    )(x, shift, scale, g.reshape(1, D_MODEL), w)


def _store_head_pairs(ref, stage_ref, x, y):
    rows = x.shape[0]
    flat = ref.shape[0] == 1
    for h in range(N_HEADS):
        sl = slice(h * HEAD_DIM, (h + 1) * HEAD_DIM)
        pair = jnp.concatenate([x[:, sl], y[:, sl]], axis=1)
        if flat:
            ref[0, pl.ds(h, rows, stride=N_HEADS), :] = pair
        else:
            stage_ref[pl.ds(h, rows, stride=N_HEADS), :] = pair
    if not flat:
        ref[...] = stage_ref[...].reshape(rows, N_HEADS, LANES)


def _prep_math(pa, prev, mu_ref, w0_ref, wup_ref, a0_ref, gup_ref, kk_ref, ka_ref,
               wb_out, kr_out, kv_out, g_out, stage_ref):
    z = pa + (prev - pa) * mu_ref[...]
    r = z[:, 0:D_A]
    k = z[:, D_A:2 * D_A]
    v = z[:, 2 * D_A:3 * D_A]
    lo = 3 * D_A
    za = z[:, lo:lo + R_DECAY + R_AAA]
    lane = lax.broadcasted_iota(jnp.int32, za.shape, 1)
    za = jnp.where(lane < R_DECAY, jnp.tanh(za), za)
    lw = _dot3(za, wup_ref[...])
    w_log = -_softplus(-(w0_ref[...] + lw[:, :D_A])) - 0.5
    decay = jnp.exp(-jnp.exp(w_log))
    a = _sigmoid(a0_ref[...] + lw[:, D_A:])
    gd = z[:, lo + R_DECAY + R_AAA:]
    g = _dot3(_sigmoid(gd), gup_ref[...])
    kk = k * kk_ref[...]
    _store_head_pairs(wb_out, stage_ref, decay, kk * a)
    _store_head_pairs(kr_out, stage_ref, k * (1.0 + (a - 1.0) * ka_ref[...]), r)
    _store_head_pairs(kv_out, stage_ref, kk, v)
    g_out[0] = g


def _prep_kernel(pa_ref, prev_ref, *refs):
    _prep_math(pa_ref[0], prev_ref[0], *refs, None)


def _prep_params(p):
    wup = jnp.zeros((R_DECAY + R_AAA, 2 * D_A), F32)
    wup = wup.at[:R_DECAY, :D_A].set(p['rk_w_up']).at[R_DECAY:, D_A:].set(p['rk_a_up'])
    vec = lambda a: a.reshape(1, -1)
    return (vec(p['rk_mu']), vec(p['rk_w0']), wup, vec(p['rk_a0']), p['rk_g_up'],
            vec(p['rk_k_k']), vec(p['rk_k_a']))


def _const_spec(a):
    return pl.BlockSpec(a.shape, lambda b, i: (0,) * a.ndim)


def _prep(pa, prev, p, tm):
    nb, t, _ = pa.shape
    params = _prep_params(p)
    rows = pl.BlockSpec((1, tm, N_COLS_A), lambda b, i: (b, i, 0))
    return pl.pallas_call(
        _prep_kernel,
        grid=(nb, t // tm),
        in_specs=[rows, rows] + [_const_spec(a) for a in params],
        out_specs=[pl.BlockSpec((1, tm * N_HEADS, LANES), lambda b, i: (b, i, 0))] * 3
                  + [pl.BlockSpec((1, tm, D_A), lambda b, i: (b, i, 0))],
        out_shape=[jax.ShapeDtypeStruct((nb, t * N_HEADS, LANES), F32)] * 3
                  + [jax.ShapeDtypeStruct((nb, t, D_A), F32)],
        compiler_params=_cparams(("arbitrary", "arbitrary")),
        name="rwkv_prep",
    )(pa, prev, *params)


SCAN_SLOTS = 4


def _scan_kernel(wb_ref, kr_ref, kv_ref, kvnext_ref, s0_ref, lnw_ref, lnb_ref, rk_ref,
                 out_ref, sfin_ref, s_ref, sa_ref, inv_ref, wd_ref, kkd_ref, bd_ref, kd_ref, rd_ref, vd_ref, *, tt):
    ti = pl.program_id(1)
    upper_half = lax.broadcasted_iota(jnp.int32, (HALF_ROWS, LANES), 1) >= LANES // 2
    unroll = SCAN_SLOTS if tt % SCAN_SLOTS == 0 else 1

    def expand(pair):
        return jnp.concatenate([pair, pair], axis=0).T

    def prepare(slot, s, kk_pair=None):
        t1 = expand(wb_ref[0, s])
        wd_ref[slot] = t1[:HEAD_DIM]
        bd_ref[slot] = t1[HEAD_DIM:]
        t2 = expand(kr_ref[0, s])
        kd_ref[slot] = t2[:HEAD_DIM]
        rd_ref[slot] = t2[HEAD_DIM:]
        t3 = expand(kv_ref[0, s])
        vd_ref[slot] = jnp.where(upper_half, t3[HEAD_DIM + HALF_ROWS:], t3[HEAD_DIM:HEAD_DIM + HALF_ROWS])
        kkd_ref[slot] = t3[:HEAD_DIM] if kk_pair is None else expand(kk_pair)[:HEAD_DIM]

    prepare(0, 0)
    if tt > 1:
        prepare(1, 1)
    else:
        kkd_ref[1] = expand(kvnext_ref[0, 0])[:HEAD_DIM]

    def inv_norm2(kk_rows):
        s2 = jnp.sum(kk_rows * kk_rows, axis=0, keepdims=True)
        return 1.0 / jnp.maximum(s2, 1e-24)

    @pl.when(ti == 0)
    def _():
        s_ref[...] = s0_ref[0]
        acc = jnp.zeros((HALF_ROWS, LANES), F32)
        for j in range(HEAD_DIM):
            acc = acc + s0_ref[0, j] * kkd_ref[0, j:j + 1, :]
        sa_ref[...] = acc
        inv_ref[...] = inv_norm2(kkd_ref[0])

    def finish(y, cv):
        tot = jnp.sum(y, axis=0, keepdims=True)
        tot = tot + pltpu.roll(tot, LANES // 2, axis=1)
        d = y - tot * (1.0 / HEAD_DIM)
        sq = jnp.sum(d * d, axis=0, keepdims=True)
        sq = sq + pltpu.roll(sq, LANES // 2, axis=1)
        yn = d * lax.rsqrt(sq * (1.0 / HEAD_DIM) + LN_X_EPS)
        return yn * lnw_ref[...] + lnb_ref[...] + cv

    def step(t, u, carry):
        sa, inv2, y_prev, cv_prev = carry
        nxt_slot = (u + 1) % SCAN_SLOTS
        out_ref[0, jnp.maximum(t - 1, 0)] = finish(y_prev, cv_prev)
        sae = sa * (-inv2)
        anchor = lax.shift_right_logical(lax.shift_right_logical(lax.bitcast_convert_type(sae, jnp.uint32),
                                                                 jnp.uint32(16)), jnp.uint32(16))
        v = vd_ref[u] + lax.bitcast_convert_type(anchor, F32)
        acc_y = jnp.zeros((HALF_ROWS, LANES), F32)
        acc_s = jnp.zeros((HALF_ROWS, LANES), F32)
        for j in range(HEAD_DIM):
            row = pl.ds(j, 1)
            s_new = s_ref[j] * wd_ref[u, row, :] + sae * bd_ref[u, row, :] + v * kd_ref[u, row, :]
            s_ref[j] = s_new
            acc_y = acc_y + s_new * rd_ref[u, row, :]
            acc_s = acc_s + s_new * kkd_ref[nxt_slot, row, :]
        c = jnp.sum(rd_ref[u] * kd_ref[u] * rk_ref[...], axis=0, keepdims=True)
        if unroll > 1:
            ahead = jnp.minimum(t + 2, tt - 1)
            kk_pair = None
            if (u + 2) % SCAN_SLOTS == 0:
                kk_pair = jnp.where(t + 2 == tt, kvnext_ref[0, 0], kv_ref[0, ahead])
            prepare((u + 2) % SCAN_SLOTS, ahead, kk_pair)
        return acc_s, inv_norm2(kkd_ref[nxt_slot]), acc_y, c * v

    def steps(q, carry):
        for u in range(unroll):
            carry = step(q * unroll + u, u, carry)
        return carry

    zeros = jnp.zeros((HALF_ROWS, LANES), F32)
    sa, inv2, y_last, cv_last = lax.fori_loop(0, tt // unroll, steps, (sa_ref[...], inv_ref[...], zeros, zeros))
    out_ref[0, tt - 1] = finish(y_last, cv_last)
    sa_ref[...] = sa
    inv_ref[...] = inv2

    @pl.when(ti == pl.num_programs(1) - 1)
    def _():
        sfin_ref[0] = s_ref[...]


def _from_scan_rows(y):
    g, t = y.shape[:2]
    x = y.reshape(g, t, HALF_ROWS, 2, GROUP_BATCH, N_HEADS).transpose(0, 4, 1, 5, 3, 2)
    return x.reshape(g * GROUP_BATCH, t, D_A)


def _state_to_scan(s):
    g = s.shape[0] // GROUP_BATCH
    y = s.reshape(g, GROUP_BATCH, N_HEADS, 2, HALF_ROWS, HEAD_DIM).transpose(0, 5, 4, 3, 1, 2)
    return y.reshape(g, HEAD_DIM, HALF_ROWS, LANES)


def _state_from_scan(y):
    g = y.shape[0]
    s = y.reshape(g, HEAD_DIM, HALF_ROWS, 2, GROUP_BATCH, N_HEADS).transpose(0, 4, 5, 3, 2, 1)
    return s.reshape(g * GROUP_BATCH, N_HEADS, HEAD_DIM, HEAD_DIM)


def _head_rows(x):
    y = x.reshape(N_HEADS, 2, HALF_ROWS).transpose(2, 1, 0)
    y = jnp.broadcast_to(y[:, :, None, :], (HALF_ROWS, 2, GROUP_BATCH, N_HEADS))
    return y.reshape(HALF_ROWS, LANES)


def _head_keys(x):
    y = jnp.broadcast_to(x.T[:, None, None, :], (HEAD_DIM, 2, GROUP_BATCH, N_HEADS))
    return y.reshape(HEAD_DIM, LANES)


def _pairs_to_groups(x, t):
    nb = x.shape[0]
    g = nb // GROUP_BATCH
    y = x.reshape(g, GROUP_BATCH, t, N_HEADS, LANES).transpose(0, 2, 1, 3, 4)
    return y.reshape(g, t, GROUP_BATCH * N_HEADS, LANES)


def _wkv_scan(wb, kr, kv, s0, p, tt):
    g, t = wb.shape[:2]
    pair_tile = pl.BlockSpec((1, tt, HEAD_DIM, LANES), lambda gi, i: (gi, i, 0, 0))
    next_step = pl.BlockSpec((1, 1, HEAD_DIM, LANES), lambda gi, i: (gi, jnp.minimum((i + 1) * tt, t - 1), 0, 0))
    row_tile = pl.BlockSpec((1, tt, HALF_ROWS, LANES), lambda gi, i: (gi, i, 0, 0))
    state = pl.BlockSpec((1, HEAD_DIM, HALF_ROWS, LANES), lambda gi, i: (gi, 0, 0, 0))
    const = lambda n: pl.BlockSpec((n, LANES), lambda gi, i: (0, 0))
    keys = pltpu.VMEM((SCAN_SLOTS, HEAD_DIM, LANES), F32)
    out, sfin = pl.pallas_call(
        functools.partial(_scan_kernel, tt=tt),
        grid=(g, t // tt),
        in_specs=[pair_tile, pair_tile, pair_tile, next_step,
                  state, const(HALF_ROWS), const(HALF_ROWS), const(HEAD_DIM)],
        out_specs=[row_tile, state],
        out_shape=[jax.ShapeDtypeStruct((g, t, HALF_ROWS, LANES), F32),
                   jax.ShapeDtypeStruct((g, HEAD_DIM, HALF_ROWS, LANES), F32)],
        scratch_shapes=[pltpu.VMEM((HEAD_DIM, HALF_ROWS, LANES), F32),
                        pltpu.VMEM((HALF_ROWS, LANES), F32),
                        pltpu.VMEM((1, LANES), F32),
                        keys, keys, keys, keys, keys,
                        pltpu.VMEM((SCAN_SLOTS, HALF_ROWS, LANES), F32)],
        compiler_params=_cparams(("arbitrary", "arbitrary")),
        name="wkv_scan",
    )(wb, kr, kv, kv, _state_to_scan(s0),
      _head_rows(p['rk_lnx_w']), _head_rows(p['rk_lnx_b']), _head_keys(p['rk_r_k']))
    return _from_scan_rows(out), _state_from_scan(sfin)


def _gelu(x):
    return 0.5 * x * (1.0 + jnp.tanh(0.7978845608028654 * (x + 0.044715 * (x * x * x))))


def _lru_gates(xc, wri_ref, bri_ref, nsl_ref, precise):
    dot = _dot3 if precise else _dot
    gates = _sigmoid(dot(xc, wri_ref[...]) + bri_ref[...])
    gate_r = gates[:, :D_B]
    gate_i = gates[:, D_B:]
    log_a = gate_r * nsl_ref[...]
    a = jnp.exp(log_a)
    th = jnp.tanh(log_a)
    one_minus_a2 = -2.0 * th / (1.0 - th)
    bt = jnp.sqrt(one_minus_a2) * (gate_i * xc)
    return a, bt


def _lru_seq_math(pb, cw_ref, cb_ref, wri_ref, bri_ref, nsl_ref, ng_ref, out_ref, hlast_ref, xprev_ref, h_ref):
    tm = pb.shape[0]
    yb = pb[:, :D_B]
    xb = pb[:, D_B:]
    xprev = xprev_ref[...]
    row8 = lax.broadcasted_iota(jnp.int32, (SUBLANES, D_B), 0)

    def shifted(d):
        rolled = pltpu.roll(xb, d, axis=0)
        top = jnp.where(row8 < d, pltpu.roll(xprev, d, axis=0), rolled[:SUBLANES])
        return jnp.concatenate([top, rolled[SUBLANES:]], axis=0)

    xc = cb_ref[...] + cw_ref[3:4, :] * xb
    for d in range(1, CONV_W):
        xc = xc + cw_ref[3 - d:4 - d, :] * shifted(d)
    xprev_ref[...] = xb[tm - SUBLANES:, :]

    a, x = _lru_gates(xc, wri_ref, bri_ref, nsl_ref, False)
    row = lax.broadcasted_iota(jnp.int32, (tm, D_B), 0)
    d = 1
    while d < tm:
        if d < SUBLANES:
            keep = row >= d
            a_s = jnp.where(keep, pltpu.roll(a, d, axis=0), 1.0)
            x_s = jnp.where(keep, pltpu.roll(x, d, axis=0), 0.0)
            x = a * x_s + x
            a = a * a_s
        else:
            x = jnp.concatenate([x[:d], a[d:] * x[:tm - d] + x[d:]], axis=0)
            a = jnp.concatenate([a[:d], a[d:] * a[:tm - d]], axis=0)
        d *= 2
    h = a * h_ref[...] + x
    h_ref[...] = h[tm - 1:, :]
    hlast_ref[0] = h[tm - 1:, :]
    out_ref[0] = _rms(h * _gelu(yb), ng_ref[...]).astype(out_ref.dtype)


def _lru_params(p):
    eye = jnp.eye(N_BLOCKS_B, dtype=F32)
    bd = lambda w: (eye[:, None, :, None] * w[:, :, None, :]).reshape(D_B, D_B)
    wri = jnp.concatenate([bd(p['lru_w_r']), bd(p['lru_w_i'])], axis=1)
    bri = jnp.concatenate([p['lru_b_r'], p['lru_b_i']]).reshape(1, 2 * D_B)
    nsl = (-LRU_C * jax.nn.softplus(-p['lru_lambda'])).reshape(1, D_B)
    return wri, bri, nsl


def _front_kernel(x_ref, shift_ref, scale_ref, g1_ref, w_ref,
                  mu_ref, w0_ref, wup_ref, a0_ref, gup_ref, kk_ref, ka_ref,
                  cw_ref, cb_ref, wri_ref, bri_ref, nsl_ref, ng_ref,
                  wb_out, kr_out, kv_out, g_out, yb_out, hlast_out, patail_out, xtail_out,
                  carry_ref, stage_ref, xprev_ref, h_ref):
    @pl.when(pl.program_id(1) == 0)
    def _():
        carry_ref[...] = jnp.zeros_like(carry_ref)
        xprev_ref[...] = jnp.zeros_like(xprev_ref)
        h_ref[...] = jnp.zeros_like(h_ref)

    h = (_rms(x_ref[0], g1_ref[...]) * (1.0 + scale_ref[0]) + shift_ref[0]).astype(BF16)
    proj = jnp.dot(h, w_ref[...], preferred_element_type=F32)
    pa = proj[:, :N_COLS_A]
    pb = proj[:, N_COLS_A:]
    tm = pa.shape[0]
    patail_out[0] = pa[tm - SUBLANES:, :]
    xtail_out[0] = pb[tm - SUBLANES:, D_B:]

    rolled = pltpu.roll(pa, 1, axis=0)
    row = lax.broadcasted_iota(jnp.int32, pa.shape, 0)
    prev = jnp.where(row == 0, carry_ref[...], rolled)
    carry_ref[...] = pa[tm - 1:, :]
    _prep_math(pa, prev, mu_ref, w0_ref, wup_ref, a0_ref, gup_ref, kk_ref, ka_ref,
               wb_out, kr_out, kv_out, g_out, stage_ref)
    _lru_seq_math(pb, cw_ref, cb_ref, wri_ref, bri_ref, nsl_ref, ng_ref, yb_out, hlast_out, xprev_ref, h_ref)


def _front(x, shift, scale, mod_map, p, w, tm):
    nb, t, _ = x.shape
    assert nb == GROUP_BATCH
    wri, bri, nsl = _lru_params(p)
    consts = ((p['norm1_g'].reshape(1, D_MODEL), w) + _prep_params(p)
              + (p['lru_conv_w'], p['lru_conv_b'].reshape(1, D_B), wri.astype(BF16), bri, nsl,
                 p['lru_norm_g'].reshape(1, D_B)))
    mod_block = (1,) + shift.shape[1:]
    tile = lambda n: pl.BlockSpec((1, tm, n), lambda b, i: (b, i, 0))
    tail = lambda n: pl.BlockSpec((1, SUBLANES, n), lambda b, i: (b, 0, 0))
    pair_tile = pl.BlockSpec((tm, N_HEADS, LANES), lambda b, i: (i, b, 0))
    pair_shape = jax.ShapeDtypeStruct((t, nb * N_HEADS, LANES), F32)
    return pl.pallas_call(
        _front_kernel,
        grid=(nb, t // tm),
        in_specs=[tile(D_MODEL),
                  pl.BlockSpec(mod_block, functools.partial(mod_map, 0)),
                  pl.BlockSpec(mod_block, functools.partial(mod_map, 1))] + [_const_spec(a) for a in consts],
        out_specs=[pair_tile] * 3 + [tile(D_A), tile(D_B), pl.BlockSpec((1, 1, D_B), lambda b, i: (b, 0, 0)),
                                     tail(N_COLS_A), tail(D_B)],
        out_shape=[pair_shape] * 3 + [jax.ShapeDtypeStruct((nb, t, D_A), F32),
                                      jax.ShapeDtypeStruct((nb, t, D_B), BF16),
                                      jax.ShapeDtypeStruct((nb, 1, D_B), F32),
                                      jax.ShapeDtypeStruct((nb, SUBLANES, N_COLS_A), F32),
                                      jax.ShapeDtypeStruct((nb, SUBLANES, D_B), F32)],
        scratch_shapes=[pltpu.VMEM((1, N_COLS_A), F32), pltpu.VMEM((tm * N_HEADS, LANES), F32),
                        pltpu.VMEM((SUBLANES, D_B), F32), pltpu.VMEM((1, D_B), F32)],
        compiler_params=_cparams(("arbitrary", "arbitrary")),
        name="prompt_front",
    )(x, shift, scale, *consts)


def _lru_step_kernel(pb_ref, conv_ref, h0_ref, cw_ref, cb_ref, wri_ref, bri_ref, nsl_ref, ng_ref,
                     out_ref, hnew_ref):
    pb = pb_ref[...]
    yb = pb[:, :D_B]
    xb = pb[:, D_B:]
    xc = cb_ref[...] + cw_ref[3:4, :] * xb
    for j in range(CONV_W - 1):
        xc = xc + cw_ref[j:j + 1, :] * conv_ref[j]
    a, x = _lru_gates(xc, wri_ref, bri_ref, nsl_ref, True)
    h = a * h0_ref[...] + x
    hnew_ref[...] = h
    out_ref[...] = _rms(h * _gelu(yb), ng_ref[...]).astype(out_ref.dtype)


def _lru_step(pb, conv0, h0, p):
    n = pb.shape[0]
    wri, bri, nsl = _lru_params(p)
    return pl.pallas_call(
        _lru_step_kernel,
        out_shape=[jax.ShapeDtypeStruct((n, D_B), BF16), jax.ShapeDtypeStruct((n, D_B), F32)],
        compiler_params=pltpu.CompilerParams(vmem_limit_bytes=VMEM_LIMIT),
        name="rglru_step",
    )(pb, conv0, h0, p['lru_conv_w'], p['lru_conv_b'].reshape(1, D_B), wri, bri, nsl,
      p['lru_norm_g'].reshape(1, D_B))


def _to_token_tiles(ref, x):
    rows = x.shape[0]
    for c in range(TILE_CHUNKS):
        ref[pl.ds(c, rows, stride=TILE_CHUNKS), :] = x[:, c * LANES:(c + 1) * LANES]


def _from_token_tiles(ref, row0, rows):
    return jnp.concatenate(
        [ref[pl.ds(row0 + c, rows, stride=TILE_CHUNKS), :] for c in range(TILE_CHUNKS)], axis=1)


def _post_kernel(x_ref, wkv_ref, g_ref, yb_ref, gate1_ref, shift2_ref, scale2_ref, n2_ref,
                 wo_ref, rw_ref, rb_ref, tri_ref, x1_ref, h2_ref, ti_ref, rk_ref, tg_ref, cnt_ref):
    ya = (wkv_ref[0] * g_ref[0]).astype(BF16)
    mixed = (jnp.dot(ya, wo_ref[:D_A, :], preferred_element_type=F32)
             + jnp.dot(yb_ref[0], wo_ref[D_A:, :], preferred_element_type=F32))
    x1 = x_ref[0] + gate1_ref[0] * mixed
    x1_ref[0] = x1
    h2 = _rms(x1, n2_ref[...]) * (1.0 + scale2_ref[0]) + shift2_ref[0]
    _to_token_tiles(h2_ref, h2)
    logits = _dot3_nt(rw_ref[...], h2) + rb_ref[...]
    eidx = lax.broadcasted_iota(jnp.int32, logits.shape, 0)
    vals, idxs = [], []
    cur = logits
    for _ in range(TOP_K):
        m = jnp.max(cur, axis=0, keepdims=True)
        i = jnp.min(jnp.where(cur == m, eidx, N_EXPERTS), axis=0, keepdims=True)
        vals.append(m)
        idxs.append(i)
        cur = jnp.where(eidx == i, -jnp.inf, cur)
    ex = [jnp.exp(v - vals[0]) for v in vals]
    den = ex[0] + ex[1] + ex[2] + ex[3]
    sel = [eidx == i for i in idxs]
    onehot = (sel[0] | sel[1] | sel[2] | sel[3]).astype(F32)
    incl = jnp.dot(onehot.astype(BF16), tri_ref[...], preferred_element_type=F32)
    rank = incl - onehot
    cnt_ref[0] = jnp.broadcast_to(jnp.sum(onehot, axis=1, keepdims=True), cnt_ref.shape[1:])
    for k in range(TOP_K):
        ti_ref[0, k:k + 1, :] = idxs[k]
        rk_ref[0, k:k + 1, :] = jnp.sum(jnp.where(sel[k], rank, 0.0), axis=0, keepdims=True).astype(jnp.int32)
        tg_ref[0, k:k + 1, :] = ex[k] / den


def _post(x, wkv, g, yb, gate1, shift2, scale2, mod_map, p, tm):
    nb, t, _ = x.shape
    mod_block = (1,) + gate1.shape[1:]
    tile = lambda n: pl.BlockSpec((1, tm, n), lambda b, i: (b, i, 0))
    full = lambda a: pl.BlockSpec(a.shape, lambda b, i: (0,) * a.ndim)
    mspec = lambda j: pl.BlockSpec(mod_block, functools.partial(mod_map, j))
    n2 = p['norm2_g'].reshape(1, D_MODEL)
    wo = p['w_out'].astype(BF16)
    rw = p['router_w'].T
    rb = p['router_b'].reshape(N_EXPERTS, 1)
    tri = jnp.triu(jnp.ones((tm, tm), BF16))
    nt = t // tm
    n = nb * t
    topk = pl.BlockSpec((1, TOP_K, tm), lambda b, i: (b * nt + i, 0, 0))
    topk_i = jax.ShapeDtypeStruct((nb * nt, TOP_K, tm), jnp.int32)
    x1, h2, ti, rk, tg, cnt = pl.pallas_call(
        _post_kernel,
        grid=(nb, nt),
        in_specs=[tile(D_MODEL), tile(D_A), tile(D_A), tile(D_B), mspec(2), mspec(3), mspec(4),
                  full(n2), full(wo), full(rw), full(rb), full(tri)],
        out_specs=[tile(D_MODEL),
                   pl.BlockSpec((tm * TILE_CHUNKS, LANES), lambda b, i: (b * nt + i, 0)),
                   topk, topk, topk,
                   pl.BlockSpec((1, N_EXPERTS, LANES), lambda b, i: (b * nt + i, 0, 0))],
        out_shape=[jax.ShapeDtypeStruct((nb, t, D_MODEL), F32),
                   jax.ShapeDtypeStruct((n * TILE_CHUNKS, LANES), F32),
                   topk_i, topk_i,
                   jax.ShapeDtypeStruct((nb * nt, TOP_K, tm), F32),
                   jax.ShapeDtypeStruct((nb * nt, N_EXPERTS, LANES), F32)],
        compiler_params=_cparams(("arbitrary", "arbitrary")),
        name="outproj_router",
    )(x, wkv, g, yb, gate1, shift2, scale2, n2, wo, rw, rb, tri)
    return x1, h2, ti, rk, tg, cnt[:, :, 0].astype(jnp.int32)


def _plan(routes):
    cnt = jnp.concatenate([r[2] for r in routes], axis=0)
    total = jnp.sum(cnt, axis=0)
    run_start = jnp.cumsum(cnt, axis=0) - cnt
    padded = (total + MOE_ROWS - 1) // MOE_ROWS * MOE_ROWS
    pad_end = jnp.cumsum(padded)
    pad_start = pad_end - padded
    loc_off = jnp.cumsum(cnt, axis=1) - cnt
    dst0 = pad_start[None, :] + run_start
    n_tokens = sum(r[0].shape[0] * r[0].shape[2] for r in routes)
    n_blocks = -(-n_tokens * TOP_K // MOE_ROWS) + N_EXPERTS
    block_row0 = jnp.arange(n_blocks, dtype=jnp.int32) * MOE_ROWS
    block_e = jnp.minimum(jnp.sum(pad_end[None, :] <= block_row0[:, None], axis=1), N_EXPERTS - 1)
    n_used = (pad_end[-1] // MOE_ROWS).reshape(1)
    first_total = jnp.sum(routes[0][2], axis=0)
    zero_table = jnp.stack([pad_end - (pad_start + first_total), jnp.zeros_like(pad_end), pad_start + first_total])
    as_i32 = lambda a: a.astype(jnp.int32)
    experts = jnp.arange(N_EXPERTS, dtype=jnp.int32)
    groups, t0 = [], 0
    for ti, rk, c in routes:
        nt = ti.shape[0]
        sl = slice(t0, t0 + nt)
        off = jnp.sum(jnp.where(ti[..., None] == experts, loc_off[sl, None, None, :], 0), axis=-1)
        lpos = as_i32((rk + off) * TILE_CHUNKS).reshape(nt, -1)
        table = as_i32(jnp.stack([cnt[sl], loc_off[sl], dst0[sl]], axis=0).reshape(3, -1))
        groups.append((lpos, table))
        t0 += nt
    return groups, as_i32(block_e), as_i32(n_used), as_i32(pad_end), as_i32(zero_table), as_i32(padded), n_blocks


def _tile_rows(row):
    return pl.ds(pl.multiple_of(row * TILE_CHUNKS, TILE_CHUNKS), TILE_CHUNKS)


def _sublane_rows(row8):
    return pl.ds(pl.multiple_of(row8, TILE_CHUNKS), TILE_CHUNKS)


def _piece_rows(row0, n_rows):
    return pl.ds(pl.multiple_of(row0 * TILE_CHUNKS, TILE_CHUNKS), n_rows * TILE_CHUNKS)


def _block_rows(block):
    rows = MOE_ROWS * TILE_CHUNKS
    return pl.ds(pl.multiple_of(block * rows, rows), rows)


def _for_each_piece(table_ref, tile, n_tiles, fn):
    def per_expert(e, queue):
        col = tile * N_EXPERTS + e
        cnt = table_ref[col]
        lo = table_ref[n_tiles * N_EXPERTS + col]
        d0 = table_ref[2 * n_tiles * N_EXPERTS + col]

        def full(c, c2):
            fn(lo + c * MOE_CHUNK, d0 + c * MOE_CHUNK, MOE_CHUNK, queue)
            return c2
        lax.fori_loop(0, cnt // MOE_CHUNK, full, 0)
        bit = MOE_CHUNK // 2
        while bit >= 1:
            done = cnt - cnt % (2 * bit)

            @pl.when((cnt & bit) != 0)
            def _(done=done, bit=bit):
                fn(lo + done, d0 + done, bit, queue)
            bit //= 2

    def per_pair(p, carry):
        for queue in range(2):
            per_expert(2 * p + queue, queue)
        return carry
    lax.fori_loop(0, N_EXPERTS // 2, per_pair, 0)


def _scatter_kernel(*refs, first):
    idx_ref, loc_ref, zero_ref, sem, isem, zsem = refs[-6:]
    table_ref, pad_end_ref, ztab_ref, h2_ref, lpos_hbm = refs[:5]
    xs_hbm = refs[-7]
    i = pl.program_id(0)
    n_tiles = pl.num_programs(0)
    tm = h2_ref.shape[0] // TILE_CHUNKS
    slot = i % 2

    def idx_copy(tile, s):
        dst = idx_ref.at[pl.ds(s * (TOP_K * tm), TOP_K * tm)]
        return pltpu.make_async_copy(lpos_hbm.at[tile], dst, isem.at[s])

    def drain(s):
        for _ in range(TOP_K):
            pltpu.make_async_copy(h2_ref, xs_hbm.at[pl.ds(0, tm * TILE_CHUNKS), :], sem.at[s]).wait()

    @pl.when(i == 0)
    def _():
        idx_copy(0, 0).start()

    if first:
        @pl.when(i == 0)
        def _():
            zero_ref[...] = jnp.zeros_like(zero_ref)
            n_used = pad_end_ref[N_EXPERTS - 1] // MOE_ROWS
            n_blocks = xs_hbm.shape[0] // (MOE_ROWS * TILE_CHUNKS)

            def zero_piece(lo, d0, n):
                return pltpu.make_async_copy(zero_ref.at[_piece_rows(0, n), :], xs_hbm.at[_piece_rows(d0, n), :], zsem)
            _for_each_piece(ztab_ref, 0, 1, lambda lo, d0, n, q: zero_piece(lo, d0, n).start(priority=q))

            def fill_tail(blk, c):
                pltpu.make_async_copy(zero_ref, xs_hbm.at[_block_rows(blk), :], zsem).start()
                return c
            lax.fori_loop(n_used, n_blocks, fill_tail, 0)

            def drain_fill(j, c):
                pltpu.make_async_copy(zero_ref, xs_hbm.at[_block_rows(0), :], zsem).wait()
                return c
            lax.fori_loop(0, n_blocks - n_used, drain_fill, 0)
            _for_each_piece(ztab_ref, 0, 1, lambda lo, d0, n, q: zero_piece(lo, d0, n).wait())

    def step(s):
        @pl.when(i + 1 < n_tiles)
        def _():
            idx_copy(i + 1, 1 - s).start()

        @pl.when(i >= 2)
        def _():
            drain(s)
        idx_copy(i, s).wait()
        base = s * (TOP_K * tm)

        def place(r, c):
            row = h2_ref[_tile_rows(r), :]
            for k in range(TOP_K):
                loc_ref[s, _sublane_rows(idx_ref[base + k * tm + r]), :] = row
            return c
        lax.fori_loop(0, tm, place, 0, unroll=8)

        def send(lo, d0, n, q):
            pltpu.make_async_copy(loc_ref.at[s, _piece_rows(lo, n), :], xs_hbm.at[_piece_rows(d0, n), :],
                                  sem.at[s]).start(priority=q)
        _for_each_piece(table_ref, i, n_tiles, send)

        @pl.when(i == n_tiles - 1)
        def _():
            @pl.when(i >= 1)
            def _():
                drain(1 - s)
            drain(s)

    for s in range(2):
        pl.when(slot == s)(functools.partial(step, s))


def _scatter(h2, lpos, table, pad_end, zero_table, n_rows, tm, xs=None):
    nt = lpos.shape[0]
    first = xs is None
    loc_rows = TOP_K * tm * TILE_CHUNKS
    in_specs = [pl.BlockSpec((tm * TILE_CHUNKS, LANES), lambda i, *_: (i, 0)),
                pl.BlockSpec(memory_space=pl.ANY)]
    args = [table.reshape(-1), pad_end, zero_table.reshape(-1), h2, lpos]
    if not first:
        in_specs.append(pl.BlockSpec(memory_space=pl.ANY))
        args.append(xs)
    grid_spec = pltpu.PrefetchScalarGridSpec(
        num_scalar_prefetch=3,
        grid=(nt,),
        in_specs=in_specs,
        out_specs=pl.BlockSpec(memory_space=pl.ANY),
        scratch_shapes=[pltpu.SMEM((2 * TOP_K * tm,), jnp.int32),
                        pltpu.VMEM((2, loc_rows, LANES), F32),
                        pltpu.VMEM((MOE_ROWS * TILE_CHUNKS, LANES), F32),
                        pltpu.SemaphoreType.DMA((2,)), pltpu.SemaphoreType.DMA((2,)), pltpu.SemaphoreType.DMA],
    )
    return pl.pallas_call(
        functools.partial(_scatter_kernel, first=first),
        grid_spec=grid_spec,
        out_shape=jax.ShapeDtypeStruct((n_rows * TILE_CHUNKS, LANES), F32),
        input_output_aliases={} if first else {5: 0},
        compiler_params=_cparams(("arbitrary",)),
        name="moe_dispatch",
    )(*args)


def _mlp_kernel(be_ref, nu_ref, next_ref, par_ref, xs_ref, bias_ref, wg_hbm, wu_hbm, wd_hbm, out_ref,
                w32_ref, w16_ref, sem):
    i = pl.program_id(0)
    used = i < nu_ref[0]
    e = be_ref[i]

    def weight_copies(expert, s):
        return [pltpu.make_async_copy(w_hbm.at[expert], w32_ref.at[s, j], sem.at[s])
                for j, w_hbm in enumerate((wg_hbm, wu_hbm, wd_hbm))]

    def load_expert(s):
        @pl.when(i == 0)
        def _():
            for c in weight_copies(e, s):
                c.start()
        for c in weight_copies(e, s):
            c.wait()
        nxt = next_ref[e]

        @pl.when(nxt >= 0)
        def _():
            for c in weight_copies(nxt, 1 - s):
                c.start(priority=1)
        for j in range(3):
            w16_ref[j] = w32_ref[s, j].astype(BF16)

    first_of_expert = used & ((i == 0) | (e != be_ref[jnp.maximum(i - 1, 0)]))
    for s in range(2):
        pl.when(first_of_expert & (par_ref[e] == s))(functools.partial(load_expert, s))

    @pl.when(used)
    def _():
        x = _from_token_tiles(xs_ref, 0, MOE_ROWS).astype(BF16)
        gt = jnp.dot(x, w16_ref[0], preferred_element_type=F32) + bias_ref[0, 0:1, :]
        up = jnp.dot(x, w16_ref[1], preferred_element_type=F32) + bias_ref[0, 1:2, :]
        gt = jnp.minimum(gt, SWIGLU_LIMIT)
        up = jnp.clip(up, -SWIGLU_LIMIT, SWIGLU_LIMIT)
        glu = gt * _sigmoid(gt * SWIGLU_ALPHA)
        mid = ((up + 1.0) * glu).astype(BF16)
        _to_token_tiles(out_ref, jnp.dot(mid, w16_ref[2], preferred_element_type=F32) + bias_ref[0, 2:3, :])

    @pl.when(jnp.logical_not(used))
    def _():
        out_ref[...] = jnp.zeros_like(out_ref)


def _mlp(xs, block_e, n_used, padded, n_blocks, wts):
    wg, bg, wu, bu, wd, bd = wts
    assert D_FF == D_MODEL
    bias = jnp.stack([bg, bu, bd], axis=1)
    nonempty = padded > 0
    experts = jnp.arange(N_EXPERTS, dtype=jnp.int32)
    parity = ((jnp.cumsum(nonempty) - nonempty) % 2).astype(jnp.int32)
    later = nonempty[None, :] & (experts[None, :] > experts[:, None])
    nxt = jnp.min(jnp.where(later, experts[None, :], N_EXPERTS), axis=1)
    nxt = jnp.where(nxt == N_EXPERTS, -1, nxt).astype(jnp.int32)
    rows = pl.BlockSpec((MOE_ROWS * TILE_CHUNKS, LANES), lambda i, *_: (i, 0))
    hbm = pl.BlockSpec(memory_space=pl.ANY)
    grid_spec = pltpu.PrefetchScalarGridSpec(
        num_scalar_prefetch=4,
        grid=(n_blocks,),
        in_specs=[rows, pl.BlockSpec((1, 3, D_FF), lambda i, be, *_: (be[i], 0, 0)), hbm, hbm, hbm],
        out_specs=rows,
        scratch_shapes=[pltpu.VMEM((2, 3, D_MODEL, D_FF), F32),
                        pltpu.VMEM((3, D_MODEL, D_FF), BF16),
                        pltpu.SemaphoreType.DMA((2,))],
    )
    return pl.pallas_call(
        _mlp_kernel,
        grid_spec=grid_spec,
        out_shape=jax.ShapeDtypeStruct(xs.shape, F32),
        compiler_params=_cparams(("arbitrary",)),
        name="moe_experts",
    )(block_e, n_used, nxt, parity, xs, bias, wg, wu, wd)


def _gather_kernel(table_ref, x1_ref, gate2_ref, fg_ref, lpos_hbm, tg_hbm, rows_hbm, y_ref,
                   idx_ref, gsm_ref, loc_ref, ff_ref, sem, isem):
    i = pl.program_id(1) + pl.program_id(0) * pl.num_programs(1)
    n_tiles = pl.num_programs(0) * pl.num_programs(1)
    tm = x1_ref.shape[1]
    slot = i % 2

    def meta_copies(tile, s):
        seg = pl.ds(s * (TOP_K * tm), TOP_K * tm)
        return (pltpu.make_async_copy(lpos_hbm.at[tile], idx_ref.at[seg], isem.at[s]),
                pltpu.make_async_copy(tg_hbm.at[tile], gsm_ref.at[seg], isem.at[s]))

    def fetch_tile(tile, s):
        for c in meta_copies(tile, s):
            c.start()

        def fetch(lo, d0, n, q):
            pltpu.make_async_copy(rows_hbm.at[_piece_rows(d0, n), :], loc_ref.at[s, _piece_rows(lo, n), :],
                                  sem.at[s]).start(priority=1)
        _for_each_piece(table_ref, tile, n_tiles, fetch)

    @pl.when(i == 0)
    def _():
        fetch_tile(0, 0)

    def step(s):
        @pl.when(i + 1 < n_tiles)
        def _():
            fetch_tile(i + 1, 1 - s)

        pltpu.make_async_copy(rows_hbm.at[pl.ds(0, TOP_K * tm * TILE_CHUNKS), :], loc_ref.at[s], sem.at[s]).wait()
        for c in meta_copies(i, s):
            c.wait()
        base = s * (TOP_K * tm)

        def mix(r, c):
            acc = gsm_ref[base + r] * loc_ref[s, _sublane_rows(idx_ref[base + r]), :]
            for k in range(1, TOP_K):
                acc = acc + gsm_ref[base + k * tm + r] * loc_ref[s, _sublane_rows(idx_ref[base + k * tm + r]), :]
            ff_ref[_tile_rows(r), :] = acc
            return c
        lax.fori_loop(0, tm, mix, 0, unroll=8)

    for s in range(2):
        pl.when(slot == s)(functools.partial(step, s))

    x2 = x1_ref[0] + gate2_ref[0] * _from_token_tiles(ff_ref, 0, tm)
    y_ref[0] = _rms(x2, fg_ref[...])


def _gather(x1, tg, gate2, mod_map, final_g, lpos, table, rows, tm):
    nb, t, _ = x1.shape
    nt = t // tm
    mod_block = (1,) + gate2.shape[1:]
    loc_rows = TOP_K * tm * TILE_CHUNKS
    grid_spec = pltpu.PrefetchScalarGridSpec(
        num_scalar_prefetch=1,
        grid=(nb, nt),
        in_specs=[pl.BlockSpec((1, tm, D_MODEL), lambda b, i, *_: (b, i, 0)),
                  pl.BlockSpec(mod_block, lambda b, i, *_: mod_map(5, b, i)),
                  pl.BlockSpec((1, D_MODEL), lambda b, i, *_: (0, 0)),
                  pl.BlockSpec(memory_space=pl.ANY),
                  pl.BlockSpec(memory_space=pl.ANY),
                  pl.BlockSpec(memory_space=pl.ANY)],
        out_specs=pl.BlockSpec((1, tm, D_MODEL), lambda b, i, *_: (b, i, 0)),
        scratch_shapes=[pltpu.SMEM((2 * TOP_K * tm,), jnp.int32),
                        pltpu.SMEM((2 * TOP_K * tm,), F32),
                        pltpu.VMEM((2, loc_rows, LANES), F32),
                        pltpu.VMEM((tm * TILE_CHUNKS, LANES), F32),
                        pltpu.SemaphoreType.DMA((2,)), pltpu.SemaphoreType.DMA((2,))],
    )
    return pl.pallas_call(
        _gather_kernel,
        grid_spec=grid_spec,
        out_shape=jax.ShapeDtypeStruct((nb, t, D_MODEL), F32),
        compiler_params=_cparams(("arbitrary", "arbitrary")),
        name="moe_combine",
    )(table.reshape(-1), x1, gate2, final_g.reshape(1, D_MODEL), lpos, tg.reshape(tg.shape[0], -1), rows)


def _moe(route_p, route_s, mod_p, mod_s, final_g, wts):
    routes = [route_p, route_s]
    groups, block_e, n_used, pad_end, zero_table, padded, n_blocks = _plan([(r[2], r[3], r[5]) for r in routes])
    xs = None
    for (x1, h2, ti, rk, tg, cnt), (lpos, table) in zip(routes, groups):
        xs = _scatter(h2, lpos, table, pad_end, zero_table, n_blocks * MOE_ROWS, ti.shape[2], xs)
    rows = _mlp(xs, block_e, n_used, padded, n_blocks, wts)
    outs = []
    for (x1, h2, ti, rk, tg, cnt), (lpos, table), (mod, mod_map) in zip(routes, groups, (mod_p, mod_s)):
        outs.append(_gather(x1, tg, mod, mod_map, final_g, lpos, table, rows, ti.shape[2]))
    return outs


def _forward(x_prompt, x_sample, c_prompt, c_sample, state_wkv, state_shift, state_conv, state_lru, p, final_g):
    bp, tp, _ = x_prompt.shape
    bs = x_sample.shape[0]
    tm = min(512, tp)
    tt = min(128, tp)

    mod = _ada(jnp.concatenate([c_prompt, c_sample], axis=0), p['w_ada'], p['b_ada'])
    mod_p = mod[:bp].reshape(bp * N_MOD, 1, D_MODEL)
    mod_s = mod[bp:].reshape(bs, N_MOD, D_MODEL).transpose(1, 0, 2)
    map_p = lambda j, b, i: (b * N_MOD + j, 0, 0)
    map_s = lambda j, b, i: (j, 0, 0)

    wts = (p['w_gate'], p['b_gate'], p['w_up'], p['b_up'], p['w_down'], p['b_down'])

    wb_p, kr_p, kv_p, g, yb, lru_p, pa_tail, x_tail = _front(x_prompt, mod_p, mod_p, map_p, p,
                                                             p['w_in'].astype(BF16), tm)
    s0 = jnp.zeros((bp, N_HEADS, HEAD_DIM, HEAD_DIM), F32)
    wkv_out, wkv_p = _wkv_scan(wb_p[None], kr_p[None], kv_p[None], s0, p, tt)
    route_p = _post(x_prompt, wkv_out, g, yb, mod_p, mod_p, mod_p, map_p, p, tm)
    shift_p = pa_tail[:, -1, :]
    conv_p = x_tail[:, SUBLANES - (CONV_W - 1):, :]

    xs = x_sample.reshape(1, bs, D_MODEL)
    pa_s, pb_s = _inproj(xs, mod_s, mod_s, map_s, p['norm1_g'], p['w_in'], bs)
    wb_s, kr_s, kv_s, g = _prep(pa_s, state_shift.reshape(1, bs, N_COLS_A), p, bs)
    as_seq = lambda a: _pairs_to_groups(a.reshape(bs, N_HEADS, LANES), 1)
    wkv_out, wkv_s = _wkv_scan(as_seq(wb_s), as_seq(kr_s), as_seq(kv_s), state_wkv, p, 1)
    conv0 = state_conv.transpose(1, 0, 2)
    yb, lru_s = _lru_step(pb_s[0], conv0, state_lru, p)
    route_s = _post(xs, wkv_out.reshape(1, bs, D_A), g, yb.reshape(1, bs, D_B),
                    mod_s, mod_s, mod_s, map_s, p, bs)
    y_prompt, y_sample = _moe(route_p, route_s, (mod_p, map_p), (mod_s, map_s), final_g, wts)
    shift_s = pa_s[0]
    conv_s = jnp.concatenate([state_conv[:, 1:], pb_s[0][:, None, D_B:]], axis=1)

    return (y_prompt, y_sample.reshape(bs, 1, D_MODEL),
            wkv_p[None], shift_p[None], conv_p[None], lru_p.reshape(bp, D_B)[None],
            wkv_s[None], shift_s[None], conv_s[None], lru_s[None])


def kernel(x_prompt, x_sample, c_prompt, c_sample, state_wkv, state_shift, state_conv, state_lru, w_ada, b_ada, norm1_g, norm2_g, w_in, rk_mu, rk_w0, rk_w_up, rk_a0, rk_a_up, rk_g_up, rk_k_k, rk_k_a, rk_r_k, rk_lnx_w, rk_lnx_b, lru_conv_w, lru_conv_b, lru_w_r, lru_b_r, lru_w_i, lru_b_i, lru_lambda, lru_norm_g, w_out, router_w, router_b, w_gate, b_gate, w_up, b_up, w_down, b_down, final_g):
    assert w_ada.shape[0] == 1, "single-layer trunk"
    p = dict(w_ada=w_ada[0], b_ada=b_ada[0], norm1_g=norm1_g[0], norm2_g=norm2_g[0], w_in=w_in[0],
             rk_mu=rk_mu[0], rk_w0=rk_w0[0], rk_w_up=rk_w_up[0], rk_a0=rk_a0[0], rk_a_up=rk_a_up[0],
             rk_g_up=rk_g_up[0], rk_k_k=rk_k_k[0], rk_k_a=rk_k_a[0], rk_r_k=rk_r_k[0],
             rk_lnx_w=rk_lnx_w[0], rk_lnx_b=rk_lnx_b[0], lru_conv_w=lru_conv_w[0],
             lru_conv_b=lru_conv_b[0], lru_w_r=lru_w_r[0], lru_b_r=lru_b_r[0], lru_w_i=lru_w_i[0],
             lru_b_i=lru_b_i[0], lru_lambda=lru_lambda[0], lru_norm_g=lru_norm_g[0], w_out=w_out[0],
             router_w=router_w[0], router_b=router_b[0], w_gate=w_gate[0], b_gate=b_gate[0],
             w_up=w_up[0], b_up=b_up[0], w_down=w_down[0], b_down=b_down[0])
    return _forward(x_prompt, x_sample, c_prompt, c_sample, state_wkv[0], state_shift[0], state_conv[0],
                    state_lru[0], p, final_g)
```
